```python
import jax, jax.numpy as jnp
from jax import lax
import numpy as np

D_MODEL = 1024
BATCH = 32
SEQ = 256
DEPTH = 4
DEC_BATCH = 4
DEC_SEQ = 1024
PAST_LEN = 256

GRID_W = 64
N_MIXERS = 2
N_MLA_LAYERS = (DEPTH + 1) // 2
N_NAT_LAYERS = DEPTH // 2
MLA_HEADS = 16
MLA_NOPE_DIM = 64
MLA_ROPE_DIM = 32
MLA_V_DIM = 64
MLA_Q_LORA = 384
MLA_KV_LORA = 256
MLA_IN_DIM = MLA_Q_LORA + MLA_KV_LORA + MLA_ROPE_DIM
MLA_SCALE = (MLA_NOPE_DIM + MLA_ROPE_DIM) ** -0.5
ROPE_THETA = 10000.0
NAT_HEADS = 16
NAT_HEAD_DIM = D_MODEL // NAT_HEADS
NAT_SCALE = NAT_HEAD_DIM ** -0.5
WIN_H = 8
WIN_W = 16
Q_COLS = WIN_W
K_COLS = 2 * WIN_W
N_COL_BLOCKS = GRID_W // Q_COLS
N_EXPERTS = 16
N_GROUPS = 4
EXPERTS_PER_GROUP = N_EXPERTS // N_GROUPS
TOPK_GROUPS = 1
TOP_K = 2
D_EXPERT = 256
Q_BLOCK = 128
NORM_EPS = 1e-6
NEG_INF = -1e30

kernel_name = 'hybrid_mla_natten_grouped_moe_diffusion_step'


def rmsnorm(x, g):
    xf = x.astype(jnp.float32)
    xf = xf * lax.rsqrt(jnp.mean(xf * xf, axis=-1, keepdims=True) + NORM_EPS)
    return (xf * g.astype(jnp.float32)).astype(x.dtype)


def ada_modulation(cond, w, b):
    mod = jax.nn.silu(cond) @ w + b
    return [m[:, None, :] for m in jnp.split(mod, 6, axis=-1)]


def modulate(h, shift, scale):
    return h * (1.0 + scale) + shift


def axial_rope_angles(n_tokens):
    t = jnp.arange(n_tokens)
    row = (t // GRID_W).astype(jnp.float32)
    col = (t % GRID_W).astype(jnp.float32)
    n_freq = MLA_ROPE_DIM // 4
    inv_freq = ROPE_THETA ** (-jnp.arange(n_freq, dtype=jnp.float32) / n_freq)
    return row[:, None] * inv_freq, col[:, None] * inv_freq


def rotate_pairs(x, ang):
    x1, x2 = jnp.split(x, 2, axis=-1)
    cos = jnp.cos(ang).astype(x.dtype)
    sin = jnp.sin(ang).astype(x.dtype)
    return jnp.concatenate([x1 * cos - x2 * sin, x1 * sin + x2 * cos], axis=-1)


def apply_axial_rope(x, ang_row, ang_col):
    half = MLA_ROPE_DIM // 2
    return jnp.concatenate([rotate_pairs(x[..., :half], ang_row),
                            rotate_pairs(x[..., half:], ang_col)], axis=-1)


def blocked_attention(q, k, v, scale):
    B, S, H, Dk = q.shape
    nb = S // Q_BLOCK
    qb = jnp.moveaxis(q.reshape(B, nb, Q_BLOCK, H, Dk), 1, 0)

    def one_block(q_blk):
        s = jnp.einsum('bqhd,bkhd->bhqk', q_blk, k).astype(jnp.float32) * scale
        p = jax.nn.softmax(s, axis=-1).astype(v.dtype)
        return jnp.einsum('bhqk,bkhd->bqhd', p, v)

    out = lax.map(one_block, qb)
    return jnp.moveaxis(out, 0, 1).reshape(B, S, H, v.shape[-1])


def mla_compress(h, w_in, q_norm, w_uq, kv_norm):
    B, S, _ = h.shape
    lat = h @ w_in
    c_q = lat[..., :MLA_Q_LORA]
    c_kv = lat[..., MLA_Q_LORA:MLA_Q_LORA + MLA_KV_LORA]
    k_rope = lat[..., MLA_Q_LORA + MLA_KV_LORA:]
    q = (rmsnorm(c_q, q_norm) @ w_uq).reshape(B, S, MLA_HEADS, MLA_NOPE_DIM + MLA_ROPE_DIM)
    return q, rmsnorm(c_kv, kv_norm), k_rope


def mla_attend(q, c_kv, k_rope, w_ukv, w_o):
    B, S = q.shape[0], q.shape[1]
    T = c_kv.shape[1]
    kv = (c_kv @ w_ukv).reshape(B, T, MLA_HEADS, MLA_NOPE_DIM + MLA_V_DIM)
    k_nope, v = kv[..., :MLA_NOPE_DIM], kv[..., MLA_NOPE_DIM:]
    k = jnp.concatenate([k_nope, jnp.broadcast_to(k_rope[:, :, None, :], (B, T, MLA_HEADS, MLA_ROPE_DIM))], axis=-1)
    o = blocked_attention(q, k, v, MLA_SCALE)
    return o.reshape(B, S, MLA_HEADS * MLA_V_DIM) @ w_o


def nat_qkv(h, w_qkv):
    B, S, _ = h.shape
    q, k, v = jnp.split(h @ w_qkv, 3, axis=-1)
    shp = (B, S, NAT_HEADS, NAT_HEAD_DIM)
    return q.reshape(shp), k.reshape(shp), v.reshape(shp)


def neighbourhood_attention(q, k, v, k_ctx, v_ctx, rpb):
    B, S, H, Dh = q.shape
    rows = S // GRID_W
    kh = min(WIN_H, rows)
    r = np.arange(rows)
    row_idx = np.clip(r - kh // 2, 0, rows - kh)[:, None] + np.arange(kh)
    cb = np.arange(N_COL_BLOCKS)
    col_idx = np.clip(cb * Q_COLS - WIN_W // 2, 0, GRID_W - K_COLS)[:, None] + np.arange(K_COLS)
    q_col = cb[:, None] * Q_COLS + np.arange(Q_COLS)
    q_cs = np.clip(q_col - WIN_W // 2, 0, GRID_W - WIN_W)
    valid = (col_idx[:, None, :] >= q_cs[:, :, None]) & (col_idx[:, None, :] < q_cs[:, :, None] + WIN_W)
    d_row = row_idx - r[:, None] + (WIN_H - 1)
    d_col = np.clip(col_idx[:, None, :] - q_col[:, :, None] + (WIN_W - 1), 0, 2 * WIN_W - 2)
    bias = rpb.astype(jnp.float32)[:, d_row[:, None, None, :, None], d_col[None, :, :, None, :]]
    bias = jnp.where(valid[None, None, :, :, None, :], bias, NEG_INF)
    bias = bias.reshape(H, rows, N_COL_BLOCKS, Q_COLS, kh * K_COLS)

    qg = q.reshape(B, rows, N_COL_BLOCKS, Q_COLS, H, Dh)
    kg = k.reshape(B, rows, GRID_W, H, Dh)[:, row_idx][:, :, :, col_idx]
    vg = v.reshape(B, rows, GRID_W, H, Dh)[:, row_idx][:, :, :, col_idx]
    s_loc = jnp.einsum('brnqhd,brankhd->bhrnqak', qg, kg).astype(jnp.float32)
    s_loc = s_loc.reshape(B, H, rows, N_COL_BLOCKS, Q_COLS, kh * K_COLS) * NAT_SCALE + bias
    s_ctx = jnp.einsum('brnqhd,bchd->bhrnqc', qg, k_ctx).astype(jnp.float32) * NAT_SCALE
    p = jax.nn.softmax(jnp.concatenate([s_loc, s_ctx], axis=-1), axis=-1).astype(v.dtype)
    n_loc = kh * K_COLS
    p_loc = p[..., :n_loc].reshape(B, H, rows, N_COL_BLOCKS, Q_COLS, kh, K_COLS)
    p_ctx = p[..., n_loc:]
    o = (jnp.einsum('bhrnqak,brankhd->brnqhd', p_loc, vg)
         + jnp.einsum('bhrnqc,bchd->brnqhd', p_ctx, v_ctx))
    return o.reshape(B, S, H * Dh)


def grouped_moe(h, w_router, router_bias, w_gate, w_up, w_down):
    shp = h.shape
    t = h.reshape(-1, shp[-1])
    T = t.shape[0]
    scores = jax.nn.sigmoid((t @ w_router).astype(jnp.float32))
    biased = scores + router_bias.astype(jnp.float32)
    grp = biased.reshape(T, N_GROUPS, EXPERTS_PER_GROUP)
    group_score = jnp.sum(lax.top_k(grp, 2)[0], axis=-1)
    _, top_group = lax.top_k(group_score, TOPK_GROUPS)
    group_mask = jnp.sum(jax.nn.one_hot(top_group, N_GROUPS, dtype=jnp.float32), axis=1)
    expert_mask = jnp.repeat(group_mask, EXPERTS_PER_GROUP, axis=-1) > 0
    _, top_e = lax.top_k(jnp.where(expert_mask, biased, NEG_INF), TOP_K)
    w_sel = jnp.take_along_axis(scores, top_e, axis=-1)
    w_sel = w_sel / jnp.sum(w_sel, axis=-1, keepdims=True)
    gates = jnp.sum(jax.nn.one_hot(top_e, N_EXPERTS, dtype=jnp.float32) * w_sel[..., None], axis=1)
    act = jax.nn.silu(jnp.einsum('td,edf->tef', t, w_gate)) * jnp.einsum('td,edf->tef', t, w_up)
    act = act * gates.astype(act.dtype)[..., None]
    return jnp.einsum('tef,efd->td', act, w_down).reshape(shp)


def setup_inputs(seed: int = 0) -> dict:
    key = jax.random.key(seed)
    ks = jax.random.split(key, 27)
    nrm = jax.random.normal
    f32 = jnp.float32
    D = D_MODEL
    return {
        'x_prompt': nrm(ks[0], (BATCH, SEQ, D), f32),
        'x_sample': nrm(ks[1], (DEC_BATCH, DEC_SEQ, D), f32),
        'cache_mla_ckv': nrm(ks[2], (DEC_BATCH, N_MLA_LAYERS, PAST_LEN, MLA_KV_LORA), f32),
        'cache_mla_krope': nrm(ks[3], (DEC_BATCH, N_MLA_LAYERS, PAST_LEN, MLA_ROPE_DIM), f32),
        'cache_nat_k': nrm(ks[4], (DEC_BATCH, N_NAT_LAYERS, PAST_LEN, NAT_HEADS, NAT_HEAD_DIM), f32),
        'cache_nat_v': nrm(ks[5], (DEC_BATCH, N_NAT_LAYERS, PAST_LEN, NAT_HEADS, NAT_HEAD_DIM), f32),
        'c': nrm(ks[6], (DEC_BATCH, D), f32),
        'c_ctx': nrm(ks[7], (D,), f32),
        'w_ada': nrm(ks[8], (DEPTH, D, 6 * D), f32) * (0.5 * D ** -0.5),
        'b_ada': nrm(ks[9], (DEPTH, 6 * D), f32) * 0.02,
        'norm_mix': 1.0 + 0.02 * nrm(ks[10], (DEPTH, D), f32),
        'norm_ffn': 1.0 + 0.02 * nrm(ks[11], (DEPTH, D), f32),
        'norm_final': 1.0 + 0.02 * nrm(ks[12], (D,), f32),
        'mla_w_in': nrm(ks[13], (N_MLA_LAYERS, D, MLA_IN_DIM), f32) * D ** -0.5,
        'mla_q_norm': 1.0 + 0.02 * nrm(ks[14], (N_MLA_LAYERS, MLA_Q_LORA), f32),
        'mla_w_uq': nrm(ks[15], (N_MLA_LAYERS, MLA_Q_LORA, MLA_HEADS * (MLA_NOPE_DIM + MLA_ROPE_DIM)), f32) * MLA_Q_LORA ** -0.5,
        'mla_kv_norm': 1.0 + 0.02 * nrm(ks[16], (N_MLA_LAYERS, MLA_KV_LORA), f32),
        'mla_w_ukv': nrm(ks[17], (N_MLA_LAYERS, MLA_KV_LORA, MLA_HEADS * (MLA_NOPE_DIM + MLA_V_DIM)), f32) * MLA_KV_LORA ** -0.5,
        'mla_w_o': nrm(ks[18], (N_MLA_LAYERS, MLA_HEADS * MLA_V_DIM, D), f32) * (MLA_HEADS * MLA_V_DIM) ** -0.5,
        'nat_w_qkv': nrm(ks[19], (N_NAT_LAYERS, D, 3 * D), f32) * D ** -0.5,
        'nat_rpb': nrm(ks[20], (N_NAT_LAYERS, NAT_HEADS, 2 * WIN_H - 1, 2 * WIN_W - 1), f32) * 0.1,
        'nat_w_o': nrm(ks[21], (N_NAT_LAYERS, D, D), f32) * D ** -0.5,
        'w_router': nrm(ks[22], (D, N_EXPERTS), f32) * D ** -0.5,
        'router_bias': nrm(ks[23], (N_EXPERTS,), f32) * 0.01,
        'moe_w_gate': nrm(ks[24], (DEPTH, N_EXPERTS, D, D_EXPERT), f32) * D ** -0.5,
        'moe_w_up': nrm(ks[25], (DEPTH, N_EXPERTS, D, D_EXPERT), f32) * D ** -0.5,
        'moe_w_down': nrm(ks[26], (DEPTH, N_EXPERTS, D_EXPERT, D), f32) * D_EXPERT ** -0.5,
    }


def reference(x_prompt, x_sample, cache_mla_ckv, cache_mla_krope, cache_nat_k, cache_nat_v,
              c, c_ctx, w_ada, b_ada, norm_mix, norm_ffn, norm_final,
              mla_w_in, mla_q_norm, mla_w_uq, mla_kv_norm, mla_w_ukv, mla_w_o,
              nat_w_qkv, nat_rpb, nat_w_o, w_router, router_bias,
              moe_w_gate, moe_w_up, moe_w_down):
    B, S, D = x_prompt.shape
    n_lat = x_sample.shape[1]
    ang_row, ang_col = axial_rope_angles(n_lat)
    xc, xs = x_prompt, x_sample
    new_ckv, new_krope, new_k, new_v = [], [], [], []
    for layer in range(DEPTH):
        sh_a_c, sc_a_c, g_a_c, sh_f_c, sc_f_c, g_f_c = ada_modulation(c_ctx[None, :], w_ada[layer], b_ada[layer])
        sh_a_s, sc_a_s, g_a_s, sh_f_s, sc_f_s, g_f_s = ada_modulation(c, w_ada[layer], b_ada[layer])
        hc = modulate(rmsnorm(xc, norm_mix[layer]), sh_a_c, sc_a_c)
        hs = modulate(rmsnorm(xs, norm_mix[layer]), sh_a_s, sc_a_s)
        j = layer // N_MIXERS
        if layer % N_MIXERS == 0:
            q_c, ckv_c, kr_c = mla_compress(hc, mla_w_in[j], mla_q_norm[j], mla_w_uq[j], mla_kv_norm[j])
            out_c = mla_attend(q_c, ckv_c, kr_c, mla_w_ukv[j], mla_w_o[j])
            new_ckv.append(ckv_c)
            new_krope.append(kr_c)
            q_s, ckv_s, kr_s = mla_compress(hs, mla_w_in[j], mla_q_norm[j], mla_w_uq[j], mla_kv_norm[j])
            q_s = jnp.concatenate([q_s[..., :MLA_NOPE_DIM],
                                   apply_axial_rope(q_s[..., MLA_NOPE_DIM:], ang_row[:, None, :], ang_col[:, None, :])], axis=-1)
            kr_s = apply_axial_rope(kr_s, ang_row, ang_col)
            out_s = mla_attend(q_s,
                               jnp.concatenate([ckv_s, cache_mla_ckv[:, j]], axis=1),
                               jnp.concatenate([kr_s, cache_mla_krope[:, j]], axis=1),
                               mla_w_ukv[j], mla_w_o[j])
        else:
            q_c, k_c, v_c = nat_qkv(hc, nat_w_qkv[j])
            out_c = blocked_attention(q_c, k_c, v_c, NAT_SCALE).reshape(B, S, D) @ nat_w_o[j]
            new_k.append(k_c)
            new_v.append(v_c)
            q_s, k_s, v_s = nat_qkv(hs, nat_w_qkv[j])
            out_s = neighbourhood_attention(q_s, k_s, v_s, cache_nat_k[:, j], cache_nat_v[:, j], nat_rpb[j]) @ nat_w_o[j]
        xc = xc + g_a_c * out_c
        xs = xs + g_a_s * out_s
        hc = modulate(rmsnorm(xc, norm_ffn[layer]), sh_f_c, sc_f_c)
        hs = modulate(rmsnorm(xs, norm_ffn[layer]), sh_f_s, sc_f_s)
        xc = xc + g_f_c * grouped_moe(hc, w_router, router_bias, moe_w_gate[layer], moe_w_up[layer], moe_w_down[layer])
        xs = xs + g_f_s * grouped_moe(hs, w_router, router_bias, moe_w_gate[layer], moe_w_up[layer], moe_w_down[layer])
    y_prompt = rmsnorm(xc, norm_final)
    y_sample = rmsnorm(xs, norm_final)
    return (y_prompt, y_sample, jnp.stack(new_ckv, axis=1), jnp.stack(new_krope, axis=1),
            jnp.stack(new_k, axis=1), jnp.stack(new_v, axis=1))
```

```python
import functools

import numpy as np
import jax
import jax.numpy as jnp
from jax import lax
from jax.experimental import pallas as pl
from jax.experimental.pallas import tpu as pltpu

F32 = jnp.float32
BF16 = jnp.bfloat16

D_MODEL = 1024
DEPTH = 4
GRID_W = 64
LANES = 128
MLA_HEADS = 16
MLA_NOPE = 64
MLA_ROPE = 32
MLA_V = 64
MLA_Q_LORA = 384
MLA_KV_LORA = 256
MLA_SCALE = (MLA_NOPE + MLA_ROPE) ** -0.5
ROPE_THETA = 10000.0
NAT_HEADS = 16
NAT_DH = 64
NAT_SCALE = NAT_DH ** -0.5
WIN_H = 8
WIN_W = 16
NAT_QROWS = 4
NAT_KROWS = NAT_QROWS + WIN_H - 1
N_EXPERTS = 16
N_GROUPS = 4
EPG = N_EXPERTS // N_GROUPS
D_EXPERT = 256
NORM_EPS = 1e-6
NEG_INF = -1e30

HEAD_PAIRS = MLA_HEADS // 2
VMEM_LIMIT = 56 * 1024 * 1024


def _cparams(*sem):
    return pltpu.CompilerParams(dimension_semantics=sem, vmem_limit_bytes=VMEM_LIMIT)


def _sigmoid(x):
    return 1.0 / (1.0 + jnp.exp(-x))


def _rms(x, g):
    ms = jnp.mean(x * x, axis=-1, keepdims=True)
    return x * lax.rsqrt(ms + NORM_EPS) * g


def _dot(a, b):
    return jnp.dot(a, b, preferred_element_type=F32)


def _dot_nt(a, b):
    return lax.dot_general(a, b, (((1,), (1,)), ((), ())), preferred_element_type=F32)


def _lane_mask(width, idx, dtype):
    lane = lax.broadcasted_iota(jnp.int32, (1, LANES), 1)
    return ((lane >= idx * width) & (lane < (idx + 1) * width)).astype(dtype)


def _ada_kernel(cond_ref, w_ref, b_ref, o_ref):
    c = cond_ref[...]
    s = (c * _sigmoid(c)).astype(BF16)
    o_ref[0] = _dot(s, w_ref[0].astype(BF16)) + b_ref[0]


def _ada_modulation(cond8, w_ada, b_ada):
    n_chunk = 6
    return pl.pallas_call(
        _ada_kernel,
        grid=(DEPTH, n_chunk),
        in_specs=[
            pl.BlockSpec((8, D_MODEL), lambda l, j: (0, 0)),
            pl.BlockSpec((1, D_MODEL, D_MODEL), lambda l, j: (l, 0, j)),
            pl.BlockSpec((1, 1, D_MODEL), lambda l, j: (l, 0, j)),
        ],
        out_specs=pl.BlockSpec((1, 8, D_MODEL), lambda l, j: (l, 0, j)),
        out_shape=jax.ShapeDtypeStruct((DEPTH, 8, 6 * D_MODEL), F32),
        compiler_params=_cparams("parallel", "parallel"),
        name="ada_modulation",
    )(cond8, w_ada, b_ada.reshape(DEPTH, 1, 6 * D_MODEL))


def _mod_spec(rows_per_cond):
    if rows_per_cond is None:
        return pl.BlockSpec((1, 6, D_MODEL), lambda i, *_: (0, 0, 0))
    return pl.BlockSpec((1, 6, D_MODEL), lambda i, *_: (1 + i // rows_per_cond, 0, 0))


def _premix_mla_kernel(*refs, rope):
    if rope:
        (x_ref, mod_ref, g_ref, w_in_ref, qg_ref, w_uq_ref, kvg_ref,
         cosq_ref, sinq_ref, cosk_ref, sink_ref,
         qn_ref, qr_ref, ckv_ref, kr_ref, kr4_ref) = refs
    else:
        (x_ref, mod_ref, g_ref, w_in_ref, qg_ref, w_uq_ref, kvg_ref,
         qn_ref, qr_ref, ckv_ref, kr_ref, kr4_ref) = refs
    h = _rms(x_ref[...], g_ref[...]) * (1.0 + mod_ref[0, 1:2, :]) + mod_ref[0, 0:1, :]
    lat = _dot(h.astype(BF16), w_in_ref[...])
    c_q = lat[:, :MLA_Q_LORA]
    c_kv = lat[:, MLA_Q_LORA:MLA_Q_LORA + MLA_KV_LORA]
    o = MLA_Q_LORA + MLA_KV_LORA
    kr4 = lat[:, o:o + LANES]
    q = _dot(_rms(c_q, qg_ref[...]).astype(BF16), w_uq_ref[...])
    n_nope = MLA_HEADS * MLA_NOPE
    n_rope = MLA_HEADS * MLA_ROPE
    qr = q[:, n_nope:n_nope + n_rope]
    if rope:
        qr = qr * cosq_ref[...] + q[:, n_nope + n_rope:] * sinq_ref[...]
        kr4 = kr4 * cosk_ref[...] + lat[:, o + LANES:o + 2 * LANES] * sink_ref[...]
    qn_ref[...] = q[:, :n_nope].astype(BF16)
    qr_ref[...] = qr.astype(BF16)
    ckv_ref[...] = _rms(c_kv, kvg_ref[...])
    kr_ref[...] = kr4[:, :MLA_ROPE]
    kr4_ref[...] = kr4.astype(BF16)


def _premix_mla(x, mod, g, w_in, qg, w_uq, kvg, rope_tabs, rows_per_cond, tm):
    n = x.shape[0]
    rope = rope_tabs is not None
    row = lambda i: (i, 0)
    full = lambda i: (0, 0)
    in_specs = [
        pl.BlockSpec((tm, D_MODEL), row),
        _mod_spec(None if rows_per_cond is None else rows_per_cond // tm),
        pl.BlockSpec((1, D_MODEL), full),
        pl.BlockSpec(w_in.shape, full),
        pl.BlockSpec((1, MLA_Q_LORA), full),
        pl.BlockSpec(w_uq.shape, full),
        pl.BlockSpec((1, MLA_KV_LORA), full),
    ]
    args = [x, mod, g, w_in, qg, w_uq, kvg]
    if rope:
        nblk = rope_tabs[0].shape[0] // tm
        pos = lambda i: (i % nblk, 0)
        for t in rope_tabs:
            in_specs.append(pl.BlockSpec((tm, t.shape[1]), pos))
            args.append(t)
    widths = (MLA_HEADS * MLA_NOPE, MLA_HEADS * MLA_ROPE, MLA_KV_LORA, MLA_ROPE, LANES)
    dtypes = (BF16, BF16, F32, F32, BF16)
    return pl.pallas_call(
        functools.partial(_premix_mla_kernel, rope=rope),
        grid=(n // tm,),
        in_specs=in_specs,
        out_specs=[pl.BlockSpec((tm, w), row) for w in widths],
        out_shape=[jax.ShapeDtypeStruct((n, w), d) for w, d in zip(widths, dtypes)],
        compiler_params=_cparams("parallel"),
        name="premix_mla_rope" if rope else "premix_mla",
    )(*args)


def _softmax_pv(s_list, v_list):
    m = functools.reduce(jnp.maximum, [jnp.max(s, axis=-1, keepdims=True) for s in s_list])
    e_list = [jnp.exp(s - m) for s in s_list]
    l = functools.reduce(lambda a, b: a + b, [jnp.sum(e, axis=-1, keepdims=True) for e in e_list])
    o = functools.reduce(lambda a, b: a + b,
                         [_dot(e.astype(BF16), v) for e, v in zip(e_list, v_list)])
    return o / l


def _mla_attn_kernel(*refs, cached):
    if cached:
        (qn_ref, qr_ref, ckv_ref, kr4_ref, cckv_ref, ckr4_ref, w_uk_ref, w_uv_ref,
         o_ref, kn_s, v_s, kr_s) = refs
    else:
        (qn_ref, qr_ref, ckv_ref, kr4_ref, w_uk_ref, w_uv_ref, o_ref, kn_s, v_s, kr_s) = refs
    t_own = ckv_ref.shape[0]

    @pl.when(pl.program_id(1) == 0)
    def _():
        c = ckv_ref[...].astype(BF16)
        kn_s[0:t_own, :] = _dot(c, w_uk_ref[...]).astype(BF16)
        v_s[0:t_own, :] = _dot(c, w_uv_ref[...]).astype(BF16)
        kr_s[0:t_own, :] = kr4_ref[...]
        if cached:
            cc = cckv_ref[0].astype(BF16)
            kn_s[t_own:, :] = _dot(cc, w_uk_ref[...]).astype(BF16)
            v_s[t_own:, :] = _dot(cc, w_uv_ref[...]).astype(BF16)
            kr_s[t_own:, :] = ckr4_ref[0].astype(BF16)

    kr4 = kr_s[...]
    for p in range(HEAD_PAIRS):
        sl = slice(p * LANES, (p + 1) * LANES)
        qn = qn_ref[:, sl]
        qr = qr_ref[:, (p // 2) * LANES:(p // 2 + 1) * LANES]
        k_cat = jnp.concatenate([kn_s[:, sl], kr4], axis=-1)
        v = v_s[:, sl]
        o = None
        for i in range(2):
            h = 2 * p + i
            m64 = _lane_mask(MLA_NOPE, i, BF16)
            q_cat = jnp.concatenate([qn * m64, qr * _lane_mask(MLA_ROPE, h % 4, BF16)], axis=-1)
            s = _dot_nt(q_cat, k_cat) * MLA_SCALE
            oh = _softmax_pv([s], [v * m64])
            o = oh if o is None else o + oh
        o_ref[:, sl] = o.astype(BF16)


def _mla_attention(qn, qr, ckv, kr4, cache_ckv, cache_kr4, w_uk, w_uv, n_batch, tq):
    n = qn.shape[0]
    s_own = n // n_batch
    nq = s_own // tq
    cached = cache_ckv is not None
    t_all = s_own + (cache_ckv.shape[1] if cached else 0)
    qrow = lambda b, j: (b * nq + j, 0)
    own = lambda b, j: (b, 0)
    full = lambda b, j: (0, 0)
    in_specs = [
        pl.BlockSpec((tq, qn.shape[1]), qrow),
        pl.BlockSpec((tq, qr.shape[1]), qrow),
        pl.BlockSpec((s_own, MLA_KV_LORA), own),
        pl.BlockSpec((s_own, LANES), own),
    ]
    args = [qn, qr, ckv, kr4]
    if cached:
        in_specs += [pl.BlockSpec((1,) + cache_ckv.shape[1:], lambda b, j: (b, 0, 0)),
                     pl.BlockSpec((1,) + cache_kr4.shape[1:], lambda b, j: (b, 0, 0))]
        args += [cache_ckv, cache_kr4]
    in_specs += [pl.BlockSpec(w_uk.shape, full), pl.BlockSpec(w_uv.shape, full)]
    args += [w_uk, w_uv]
    return pl.pallas_call(
        functools.partial(_mla_attn_kernel, cached=cached),
        grid=(n_batch, nq),
        in_specs=in_specs,
        out_specs=pl.BlockSpec((tq, D_MODEL), qrow),
        out_shape=jax.ShapeDtypeStruct((n, D_MODEL), BF16),
        scratch_shapes=[pltpu.VMEM((t_all, D_MODEL), BF16), pltpu.VMEM((t_all, D_MODEL), BF16),
                        pltpu.VMEM((t_all, LANES), BF16)],
        compiler_params=_cparams("parallel", "arbitrary"),
        name="mla_attention_cached" if cached else "mla_attention",
    )(*args)


def _premix_nat_kernel(x_ref, mod_ref, g_ref, w_ref, q_ref, k_ref, v_ref):
    h = _rms(x_ref[...], g_ref[...]) * (1.0 + mod_ref[0, 1:2, :]) + mod_ref[0, 0:1, :]
    qkv = _dot(h.astype(BF16), w_ref[...])
    q_ref[...] = qkv[:, :D_MODEL].astype(q_ref.dtype)
    k_ref[...] = qkv[:, D_MODEL:2 * D_MODEL].astype(k_ref.dtype)
    v_ref[...] = qkv[:, 2 * D_MODEL:].astype(v_ref.dtype)


def _premix_nat(x, mod, g, w_qkv, rows_per_cond, tm, kv_dtype):
    n = x.shape[0]
    row = lambda i: (i, 0)
    full = lambda i: (0, 0)
    return pl.pallas_call(
        _premix_nat_kernel,
        grid=(n // tm,),
        in_specs=[
            pl.BlockSpec((tm, D_MODEL), row),
            _mod_spec(None if rows_per_cond is None else rows_per_cond // tm),
            pl.BlockSpec((1, D_MODEL), full),
            pl.BlockSpec(w_qkv.shape, full),
        ],
        out_specs=[pl.BlockSpec((tm, D_MODEL), row)] * 3,
        out_shape=[jax.ShapeDtypeStruct((n, D_MODEL), BF16),
                   jax.ShapeDtypeStruct((n, D_MODEL), kv_dtype),
                   jax.ShapeDtypeStruct((n, D_MODEL), kv_dtype)],
        compiler_params=_cparams("parallel"),
        name="premix_nat",
    )(x, mod, g, w_qkv)


def _dense_attn_kernel(q_ref, k_ref, v_ref, o_ref):
    for p in range(HEAD_PAIRS):
        sl = slice(p * LANES, (p + 1) * LANES)
        q = q_ref[:, sl]
        k = k_ref[:, sl].astype(BF16)
        v = v_ref[:, sl].astype(BF16)
        o = None
        for i in range(2):
            m64 = _lane_mask(NAT_DH, i, BF16)
            s = _dot_nt(q * m64, k) * NAT_SCALE
            oh = _softmax_pv([s], [v * m64])
            o = oh if o is None else o + oh
        o_ref[:, sl] = o.astype(BF16)


def _dense_attention(q, k, v, n_batch):
    n = q.shape[0]
    s = n // n_batch
    blk = pl.BlockSpec((s, D_MODEL), lambda b: (b, 0))
    return pl.pallas_call(
        _dense_attn_kernel,
        grid=(n_batch,),
        in_specs=[blk, blk, blk],
        out_specs=blk,
        out_shape=jax.ShapeDtypeStruct((n, D_MODEL), BF16),
        compiler_params=_cparams("parallel"),
        name="dense_attention",
    )(q, k, v)


_NAT_QBLK = NAT_QROWS * GRID_W
_NAT_KBLK = NAT_KROWS * GRID_W


def _nat_block_plan(rows):
    plan = []
    for r0 in range(0, rows, NAT_QROWS):
        ks = min(max(r0 - WIN_H // 2, 0), rows - NAT_KROWS)
        var = 0 if r0 == 0 else (2 if r0 + NAT_QROWS == rows else 1)
        plan.append((ks, var))
    return plan


def _nat_bias_tables(rpb, rows):
    plan = _nat_block_plan(rows)
    firsts = {}
    for bi, (ks, var) in enumerate(plan):
        firsts.setdefault(var, (bi * NAT_QROWS, ks))
    c = np.arange(GRID_W)
    cs = np.clip(c - WIN_W // 2, 0, GRID_W - WIN_W)
    valid_col = (c[None, :] >= cs[:, None]) & (c[None, :] < cs[:, None] + WIN_W)
    d_col = np.clip(c[None, :] - c[:, None] + (WIN_W - 1), 0, 2 * WIN_W - 2)
    tabs = []
    for var in range(3):
        r0, ks = firsts[var]
        r = r0 + np.arange(NAT_QROWS)
        kr = ks + np.arange(NAT_KROWS)
        rs = np.clip(r - WIN_H // 2, 0, rows - WIN_H)
        valid_row = (kr[None, :] >= rs[:, None]) & (kr[None, :] < rs[:, None] + WIN_H)
        d_row = np.clip(kr[None, :] - r[:, None] + (WIN_H - 1), 0, 2 * WIN_H - 2)
        b = rpb[:, d_row[:, None, :, None], d_col[None, :, None, :]]
        valid = valid_row[:, None, :, None] & valid_col[None, :, None, :]
        b = jnp.where(valid[None], b, NEG_INF)
        tabs.append(b.reshape(rpb.shape[0], _NAT_QBLK, _NAT_KBLK))
    return jnp.stack(tabs, axis=0)


def _nat_attn_kernel(q_ref, k_ref, v_ref, kc_ref, vc_ref, bias_ref, o_ref, *, plan):
    kc = kc_ref[0].astype(BF16)
    vc = vc_ref[0].astype(BF16)
    masks = [_lane_mask(NAT_DH, i, BF16) for i in range(2)]
    vcm = [vc * m for m in masks]
    for bi, (ks, var) in enumerate(plan):
        q = q_ref[bi * _NAT_QBLK:(bi + 1) * _NAT_QBLK, :]
        k = k_ref[ks * GRID_W:ks * GRID_W + _NAT_KBLK, :]
        v = v_ref[ks * GRID_W:ks * GRID_W + _NAT_KBLK, :]
        o = None
        for i in range(2):
            qm = q * masks[i]
            s_loc = _dot_nt(qm, k) * NAT_SCALE + bias_ref[var, i]
            s_ctx = _dot_nt(qm, kc) * NAT_SCALE
            oh = _softmax_pv([s_loc, s_ctx], [v * masks[i], vcm[i]])
            o = oh if o is None else o + oh
        o_ref[bi * _NAT_QBLK:(bi + 1) * _NAT_QBLK, :] = o.astype(BF16)


def _nat_attention(q, k, v, cache_k, cache_v, bias, n_batch):
    n = q.shape[0]
    s = n // n_batch
    plan = _nat_block_plan(s // GRID_W)
    own = pl.BlockSpec((s, LANES), lambda p, b: (b, p))
    cache = pl.BlockSpec((1, cache_k.shape[1], LANES), lambda p, b: (b, 0, p))
    return pl.pallas_call(
        functools.partial(_nat_attn_kernel, plan=plan),
        grid=(HEAD_PAIRS, n_batch),
        in_specs=[own, own, own, cache, cache,
                  pl.BlockSpec((3, 2, _NAT_QBLK, _NAT_KBLK), lambda p, b: (0, p, 0, 0))],
        out_specs=own,
        out_shape=jax.ShapeDtypeStruct((n, D_MODEL), BF16),
        compiler_params=_cparams("parallel", "parallel"),
        name="nat_attention",
    )(q, k, v, cache_k, cache_v, bias)


def _proj_res_kernel(x_ref, a_ref, w_ref, mod_ref, o_ref):
    o_ref[...] = x_ref[...] + mod_ref[0, 2:3, :] * _dot(a_ref[...], w_ref[...])


def _proj_residual(x, a, w, mod, rows_per_cond, tm):
    n = x.shape[0]
    row = lambda i: (i, 0)
    return pl.pallas_call(
        _proj_res_kernel,
        grid=(n // tm,),
        in_specs=[
            pl.BlockSpec((tm, D_MODEL), row),
            pl.BlockSpec((tm, D_MODEL), row),
            pl.BlockSpec(w.shape, lambda i: (0, 0)),
            _mod_spec(None if rows_per_cond is None else rows_per_cond // tm),
        ],
        out_specs=pl.BlockSpec((tm, D_MODEL), row),
        out_shape=jax.ShapeDtypeStruct((n, D_MODEL), F32),
        compiler_params=_cparams("parallel"),
        name="proj_residual",
    )(x, a, w, mod)


def _top2_sum(a, b, c, d):
    hi1, lo1 = jnp.maximum(a, b), jnp.minimum(a, b)
    hi2, lo2 = jnp.maximum(c, d), jnp.minimum(c, d)
    return jnp.maximum(hi1, hi2) + jnp.maximum(jnp.minimum(hi1, hi2), jnp.maximum(lo1, lo2))


def _route(logits, bias):
    scores = _sigmoid(logits)
    biased = scores + bias
    sc = [scores[:, e:e + 1] for e in range(N_EXPERTS)]
    bs = [biased[:, e:e + 1] for e in range(N_EXPERTS)]
    gscore = [_top2_sum(*bs[EPG * g:EPG * (g + 1)]) for g in range(N_GROUPS)]
    best, gidx = gscore[0], jnp.zeros_like(gscore[0], dtype=jnp.int32)
    for g in range(1, N_GROUPS):
        better = gscore[g] > best
        gidx = jnp.where(better, g, gidx)
        best = jnp.where(better, gscore[g], best)
    cb = [functools.reduce(lambda a, b: a + b,
                           [jnp.where(gidx == g, bs[EPG * g + i], 0.0) for g in range(N_GROUPS)])
          for i in range(EPG)]
    cs = [functools.reduce(lambda a, b: a + b,
                           [jnp.where(gidx == g, sc[EPG * g + i], 0.0) for g in range(N_GROUPS)])
          for i in range(EPG)]
    b1, i1 = cb[0], jnp.zeros_like(gidx)
    for i in range(1, EPG):
        better = cb[i] > b1
        i1 = jnp.where(better, i, i1)
        b1 = jnp.where(better, cb[i], b1)
    b2, i2 = jnp.full_like(b1, -jnp.inf), jnp.full_like(i1, -1)
    for i in range(EPG):
        better = (i1 != i) & (cb[i] > b2)
        i2 = jnp.where(better, i, i2)
        b2 = jnp.where(better, cb[i], b2)
    sel = [(i1 == i) | (i2 == i) for i in range(EPG)]
    w = [jnp.where(sel[i], cs[i], 0.0) for i in range(EPG)]
    tot = w[0] + w[1] + w[2] + w[3]
    return [[jnp.where(gidx == g, w[i] / tot, 0.0) for i in range(EPG)] for g in range(N_GROUPS)]


def _moe_kernel(x_ref, mod_ref, g_ref, wr_ref, rb_ref, wg_ref, wu_ref, wd_ref, o_ref,
                h_s, gate_s, acc_s):
    grp = pl.program_id(1)

    @pl.when(grp == 0)
    def _():
        h = _rms(x_ref[...], g_ref[...]) * (1.0 + mod_ref[0, 4:5, :]) + mod_ref[0, 3:4, :]
        h_s[...] = h.astype(BF16)
        logits = jnp.dot(h, wr_ref[...], precision=lax.Precision.HIGHEST, preferred_element_type=F32)
        gates = _route(logits, rb_ref[...])
        lane = lax.broadcasted_iota(jnp.int32, (1, LANES), 1)
        for g in range(N_GROUPS):
            gate_s[g] = functools.reduce(
                lambda a, b: a + b, [jnp.where(lane == i, gates[g][i], 0.0) for i in range(EPG)])
        acc_s[...] = jnp.zeros_like(acc_s)

    h = h_s[...]
    gate = gate_s[grp]
    acts = []
    for i in range(EPG):
        a = _dot(h, wg_ref[i])
        u = _dot(h, wu_ref[i])
        acts.append(((a * _sigmoid(a)) * u * gate[:, i:i + 1]).astype(BF16))
    act = jnp.concatenate(acts, axis=-1)
    acc_s[...] += _dot(act, wd_ref[...].reshape(EPG * D_EXPERT, D_MODEL))

    @pl.when(grp == N_GROUPS - 1)
    def _():
        o_ref[...] = x_ref[...] + mod_ref[0, 5:6, :] * acc_s[...]


def _moe(x, mod, g, w_router, router_bias, wg, wu, wd, layer, rows_per_cond, tm):
    n = x.shape[0]
    row = lambda i, e: (i, 0)
    full = lambda i, e: (0, 0)
    return pl.pallas_call(
        _moe_kernel,
        grid=(n // tm, N_GROUPS),
        in_specs=[
            pl.BlockSpec((tm, D_MODEL), row),
            _mod_spec(None if rows_per_cond is None else rows_per_cond // tm),
            pl.BlockSpec((1, D_MODEL), full),
            pl.BlockSpec(w_router.shape, full),
            pl.BlockSpec(router_bias.shape, full),
            pl.BlockSpec((None, EPG, D_MODEL, D_EXPERT), lambda i, e: (layer, e, 0, 0)),
            pl.BlockSpec((None, EPG, D_MODEL, D_EXPERT), lambda i, e: (layer, e, 0, 0)),
            pl.BlockSpec((None, EPG, D_EXPERT, D_MODEL), lambda i, e: (layer, e, 0, 0)),
        ],
        out_specs=pl.BlockSpec((tm, D_MODEL), row),
        out_shape=jax.ShapeDtypeStruct((n, D_MODEL), F32),
        scratch_shapes=[pltpu.VMEM((tm, D_MODEL), BF16),
                        pltpu.VMEM((N_GROUPS, tm, LANES), F32),
                        pltpu.VMEM((tm, D_MODEL), F32)],
        compiler_params=_cparams("parallel", "arbitrary"),
        name="grouped_moe",
    )(x, mod, g, w_router, router_bias, wg, wu, wd)


def _final_norm_kernel(x_ref, g_ref, o_ref):
    o_ref[...] = _rms(x_ref[...], g_ref[...])


def _final_norm(x, g, tm):
    n = x.shape[0]
    return pl.pallas_call(
        _final_norm_kernel,
        grid=(n // tm,),
        in_specs=[pl.BlockSpec((tm, D_MODEL), lambda i: (i, 0)),
                  pl.BlockSpec((1, D_MODEL), lambda i: (0, 0))],
        out_specs=pl.BlockSpec((tm, D_MODEL), lambda i: (i, 0)),
        out_shape=jax.ShapeDtypeStruct((n, D_MODEL), F32),
        compiler_params=_cparams("parallel"),
        name="final_norm",
    )(x, g)


def _rope_tables(n_lat):
    t = np.arange(n_lat)
    n_freq = MLA_ROPE // 4
    inv_freq = jnp.asarray(ROPE_THETA, F32) ** (-jnp.arange(n_freq, dtype=F32) / n_freq)
    ar = jnp.asarray(t // GRID_W, F32)[:, None] * inv_freq
    ac = jnp.asarray(t % GRID_W, F32)[:, None] * inv_freq
    cos = jnp.concatenate([jnp.cos(ar), jnp.cos(ar), jnp.cos(ac), jnp.cos(ac)], axis=-1)
    sin = jnp.concatenate([-jnp.sin(ar), jnp.sin(ar), -jnp.sin(ac), jnp.sin(ac)], axis=-1)
    return (jnp.tile(cos, (1, MLA_HEADS)), jnp.tile(sin, (1, MLA_HEADS)),
            jnp.tile(cos, (1, LANES // MLA_ROPE)), jnp.tile(sin, (1, LANES // MLA_ROPE)))


_ROPE_SWAP = np.concatenate([np.arange(8, 16), np.arange(0, 8), np.arange(24, 32), np.arange(16, 24)])


def _mla_weights(w_in, w_uq, w_ukv):
    o = MLA_Q_LORA + MLA_KV_LORA
    kr = w_in[:, o:]
    rep = LANES // MLA_ROPE
    w_in_x = jnp.concatenate([w_in[:, :o], jnp.tile(kr, (1, rep)), jnp.tile(kr[:, _ROPE_SWAP], (1, rep))],
                             axis=-1).astype(BF16)
    uq = w_uq.reshape(MLA_Q_LORA, MLA_HEADS, MLA_NOPE + MLA_ROPE)
    q_nope = uq[:, :, :MLA_NOPE].reshape(MLA_Q_LORA, -1)
    q_rope = uq[:, :, MLA_NOPE:]
    w_uq_x = jnp.concatenate([q_nope, q_rope.reshape(MLA_Q_LORA, -1),
                              q_rope[:, :, _ROPE_SWAP].reshape(MLA_Q_LORA, -1)], axis=-1).astype(BF16)
    ukv = w_ukv.reshape(MLA_KV_LORA, MLA_HEADS, MLA_NOPE + MLA_V)
    w_uk = ukv[:, :, :MLA_NOPE].reshape(MLA_KV_LORA, -1).astype(BF16)
    w_uv = ukv[:, :, MLA_NOPE:].reshape(MLA_KV_LORA, -1).astype(BF16)
    return w_in_x, w_uq_x, w_uk, w_uv


def kernel(x_prompt, x_sample, cache_mla_ckv, cache_mla_krope, cache_nat_k, cache_nat_v, c, c_ctx,
           w_ada, b_ada, norm_mix, norm_ffn, norm_final, mla_w_in, mla_q_norm, mla_w_uq, mla_kv_norm,
           mla_w_ukv, mla_w_o, nat_w_qkv, nat_rpb, nat_w_o, w_router, router_bias,
           moe_w_gate, moe_w_up, moe_w_down):
    B, S, D = x_prompt.shape
    Bd, Sd, _ = x_sample.shape
    assert D == D_MODEL and Bd + 1 <= 8
    tm_c, tm_s = 512, 512
    tm_moe = 1024

    xc = x_prompt.reshape(B * S, D)
    xs = x_sample.reshape(Bd * Sd, D)
    cond8 = jnp.concatenate([c_ctx[None, :], c, jnp.zeros((8 - 1 - Bd, D), F32)], axis=0)
    mod_all = _ada_modulation(cond8, w_ada, b_ada).reshape(DEPTH, 8, 6, D)

    rope_tabs = _rope_tables(Sd)
    wr = jnp.pad(w_router, ((0, 0), (0, LANES - N_EXPERTS)))
    rb = jnp.pad(router_bias, (0, LANES - N_EXPERTS)).reshape(1, LANES)
    wg = moe_w_gate.astype(BF16)
    wu = moe_w_up.astype(BF16)
    wd = moe_w_down.astype(BF16)

    new_ckv, new_krope, new_k, new_v = [], [], [], []
    for layer in range(DEPTH):
        mod = mod_all[layer]
        g_mix = norm_mix[layer][None, :]
        j = layer // 2
        if layer % 2 == 0:
            w_in_x, w_uq_x, w_uk, w_uv = _mla_weights(mla_w_in[j], mla_w_uq[j], mla_w_ukv[j])
            w_in_c = w_in_x[:, :MLA_Q_LORA + MLA_KV_LORA + LANES]
            w_uq_c = w_uq_x[:, :MLA_HEADS * (MLA_NOPE + MLA_ROPE)]
            qg = mla_q_norm[j][None, :]
            kvg = mla_kv_norm[j][None, :]
            w_o = mla_w_o[j].astype(BF16)
            qn, qr, ckv, kr, kr4 = _premix_mla(xc, mod, g_mix, w_in_c, qg, w_uq_c, kvg, None, None, tm_c)
            new_ckv.append(ckv.reshape(B, S, MLA_KV_LORA))
            new_krope.append(kr.reshape(B, S, MLA_ROPE))
            att_c = _mla_attention(qn, qr, ckv, kr4, None, None, w_uk, w_uv, B, S)
            qn, qr, ckv, kr, kr4 = _premix_mla(xs, mod, g_mix, w_in_x, qg, w_uq_x, kvg, rope_tabs, Sd, tm_s)
            cache_kr4 = jnp.tile(cache_mla_krope[:, j], (1, 1, LANES // MLA_ROPE))
            att_s = _mla_attention(qn, qr, ckv, kr4, cache_mla_ckv[:, j], cache_kr4, w_uk, w_uv, Bd, 256)
        else:
            w_qkv = nat_w_qkv[j].astype(BF16)
            w_o = nat_w_o[j].astype(BF16)
            q, k, v = _premix_nat(xc, mod, g_mix, w_qkv, None, tm_c, F32)
            new_k.append(k.reshape(B, S, NAT_HEADS, NAT_DH))
            new_v.append(v.reshape(B, S, NAT_HEADS, NAT_DH))
            att_c = _dense_attention(q, k, v, B)
            q, k, v = _premix_nat(xs, mod, g_mix, w_qkv, Sd, tm_s, BF16)
            bias = _nat_bias_tables(nat_rpb[j], Sd // GRID_W)
            att_s = _nat_attention(q, k, v,
                                   cache_nat_k[:, j].reshape(Bd, -1, D), cache_nat_v[:, j].reshape(Bd, -1, D),
                                   bias, Bd)
        xc = _proj_residual(xc, att_c, w_o, mod, None, tm_c)
        xs = _proj_residual(xs, att_s, w_o, mod, Sd, tm_s)
        g_ffn = norm_ffn[layer][None, :]
        xc = _moe(xc, mod, g_ffn, wr, rb, wg, wu, wd, layer, None, tm_moe)
        xs = _moe(xs, mod, g_ffn, wr, rb, wg, wu, wd, layer, Sd, tm_moe)

    gf = norm_final[None, :]
    y_prompt = _final_norm(xc, gf, tm_c).reshape(B, S, D)
    y_sample = _final_norm(xs, gf, tm_s).reshape(Bd, Sd, D)
    return (y_prompt, y_sample, jnp.stack(new_ckv, axis=1), jnp.stack(new_krope, axis=1),
            jnp.stack(new_k, axis=1), jnp.stack(new_v, axis=1))
```

```python
import functools

import numpy as np
import jax
import jax.numpy as jnp
from jax import lax
from jax.experimental import pallas as pl
from jax.experimental.pallas import tpu as pltpu

F32 = jnp.float32
BF16 = jnp.bfloat16

D_MODEL = 1024
DEPTH = 4
GRID_W = 64
LANES = 128
MLA_HEADS = 16
MLA_NOPE = 64
MLA_ROPE = 32
MLA_V = 64
MLA_Q_LORA = 384
MLA_KV_LORA = 256
MLA_SCALE = (MLA_NOPE + MLA_ROPE) ** -0.5
ROPE_THETA = 10000.0
NAT_HEADS = 16
NAT_DH = 64
NAT_SCALE = NAT_DH ** -0.5
WIN_H = 8
WIN_W = 16
NAT_QROWS = 4
NAT_KROWS = NAT_QROWS + WIN_H
N_EXPERTS = 16
N_GROUPS = 4
EPG = N_EXPERTS // N_GROUPS
D_EXPERT = 256
NORM_EPS = 1e-6
NEG_INF = -1e30

HEAD_PAIRS = MLA_HEADS // 2
VMEM_LIMIT = 56 * 1024 * 1024


def _cparams(*sem):
    return pltpu.CompilerParams(dimension_semantics=sem, vmem_limit_bytes=VMEM_LIMIT)


def _sigmoid(x):
    return 1.0 / (1.0 + jnp.exp(-x))


def _rms(x, g):
    ms = jnp.mean(x * x, axis=-1, keepdims=True)
    return x * lax.rsqrt(ms + NORM_EPS) * g


def _dot(a, b):
    return jnp.dot(a, b, preferred_element_type=F32)


def _dot_nt(a, b):
    return lax.dot_general(a, b, (((1,), (1,)), ((), ())), preferred_element_type=F32)


def _lane_mask(width, idx, dtype):
    lane = lax.broadcasted_iota(jnp.int32, (1, LANES), 1)
    return ((lane >= idx * width) & (lane < (idx + 1) * width)).astype(dtype)


def _ada_kernel(cond_ref, w_ref, b_ref, o_ref):
    c = cond_ref[...]
    s = (c * _sigmoid(c)).astype(BF16)
    o_ref[0] = _dot(s, w_ref[0].astype(BF16)) + b_ref[0]


def _ada_modulation(cond8, w_ada, b_ada):
    n_chunk = 6
    return pl.pallas_call(
        _ada_kernel,
        grid=(DEPTH, n_chunk),
        in_specs=[
            pl.BlockSpec((8, D_MODEL), lambda l, j: (0, 0)),
            pl.BlockSpec((1, D_MODEL, D_MODEL), lambda l, j: (l, 0, j)),
            pl.BlockSpec((1, 1, D_MODEL), lambda l, j: (l, 0, j)),
        ],
        out_specs=pl.BlockSpec((1, 8, D_MODEL), lambda l, j: (l, 0, j)),
        out_shape=jax.ShapeDtypeStruct((DEPTH, 8, 6 * D_MODEL), F32),
        compiler_params=_cparams("parallel", "parallel"),
        name="ada_modulation",
    )(cond8, w_ada, b_ada.reshape(DEPTH, 1, 6 * D_MODEL))


def _mod_spec(rows_per_cond):
    if rows_per_cond is None:
        return pl.BlockSpec((1, 6, D_MODEL), lambda i, *_: (0, 0, 0))
    return pl.BlockSpec((1, 6, D_MODEL), lambda i, *_: (1 + i // rows_per_cond, 0, 0))


def _premix_mla_kernel(*refs, rope):
    if rope:
        (x_ref, mod_ref, g_ref, w_in_ref, qg_ref, w_uq_ref, kvg_ref,
         cosq_ref, sinq_ref, cosk_ref, sink_ref,
         qn_ref, qr_ref, ckv_ref, kr_ref, kr4_ref) = refs
    else:
        (x_ref, mod_ref, g_ref, w_in_ref, qg_ref, w_uq_ref, kvg_ref,
         qn_ref, qr_ref, ckv_ref, kr_ref, kr4_ref) = refs
    h = _rms(x_ref[...], g_ref[...]) * (1.0 + mod_ref[0, 1:2, :]) + mod_ref[0, 0:1, :]
    lat = _dot(h.astype(BF16), w_in_ref[...])
    c_q = lat[:, :MLA_Q_LORA]
    c_kv = lat[:, MLA_Q_LORA:MLA_Q_LORA + MLA_KV_LORA]
    o = MLA_Q_LORA + MLA_KV_LORA
    kr4 = lat[:, o:o + LANES]
    q = _dot(_rms(c_q, qg_ref[...]).astype(BF16), w_uq_ref[...])
    n_nope = MLA_HEADS * MLA_NOPE
    n_rope = MLA_HEADS * MLA_ROPE
    qr = q[:, n_nope:n_nope + n_rope]
    if rope:
        qr = qr * cosq_ref[...] + q[:, n_nope + n_rope:] * sinq_ref[...]
        kr4 = kr4 * cosk_ref[...] + lat[:, o + LANES:o + 2 * LANES] * sink_ref[...]
    qn_ref[...] = q[:, :n_nope].astype(BF16)
    qr_ref[...] = qr.astype(BF16)
    ckv_ref[...] = _rms(c_kv, kvg_ref[...])
    kr_ref[...] = kr4[:, :MLA_ROPE]
    kr4_ref[...] = kr4.astype(BF16)


def _premix_mla(x, mod, g, w_in, qg, w_uq, kvg, rope_tabs, rows_per_cond, tm):
    n = x.shape[0]
    rope = rope_tabs is not None
    row = lambda i: (i, 0)
    full = lambda i: (0, 0)
    in_specs = [
        pl.BlockSpec((tm, D_MODEL), row),
        _mod_spec(None if rows_per_cond is None else rows_per_cond // tm),
        pl.BlockSpec((1, D_MODEL), full),
        pl.BlockSpec(w_in.shape, full),
        pl.BlockSpec((1, MLA_Q_LORA), full),
        pl.BlockSpec(w_uq.shape, full),
        pl.BlockSpec((1, MLA_KV_LORA), full),
    ]
    args = [x, mod, g, w_in, qg, w_uq, kvg]
    if rope:
        nblk = rope_tabs[0].shape[0] // tm
        pos = lambda i: (i % nblk, 0)
        for t in rope_tabs:
            in_specs.append(pl.BlockSpec((tm, t.shape[1]), pos))
            args.append(t)
    widths = (MLA_HEADS * MLA_NOPE, MLA_HEADS * MLA_ROPE, MLA_KV_LORA, MLA_ROPE, LANES)
    dtypes = (BF16, BF16, F32, F32, BF16)
    return pl.pallas_call(
        functools.partial(_premix_mla_kernel, rope=rope),
        grid=(n // tm,),
        in_specs=in_specs,
        out_specs=[pl.BlockSpec((tm, w), row) for w in widths],
        out_shape=[jax.ShapeDtypeStruct((n, w), d) for w, d in zip(widths, dtypes)],
        compiler_params=_cparams("parallel"),
        name="premix_mla_rope" if rope else "premix_mla",
    )(*args)


def _softmax_pv(s_list, v_list):
    m = functools.reduce(jnp.maximum, [jnp.max(s, axis=-1, keepdims=True) for s in s_list])
    e_list = [jnp.exp(s - m) for s in s_list]
    l = functools.reduce(lambda a, b: a + b, [jnp.sum(e, axis=-1, keepdims=True) for e in e_list])
    o = functools.reduce(lambda a, b: a + b,
                         [_dot(e.astype(BF16), v) for e, v in zip(e_list, v_list)])
    return o / l


def _mla_attn_kernel(*refs, cached):
    if cached:
        (qn_ref, qr_ref, ckv_ref, kr4_ref, cckv_ref, ckr4_ref, w_uk_ref, w_uv_ref,
         o_ref, kn_s, v_s, kr_s) = refs
    else:
        (qn_ref, qr_ref, ckv_ref, kr4_ref, w_uk_ref, w_uv_ref, o_ref, kn_s, v_s, kr_s) = refs
    t_own = ckv_ref.shape[0]

    @pl.when(pl.program_id(1) == 0)
    def _():
        c = ckv_ref[...].astype(BF16)
        kn_s[0:t_own, :] = _dot(c, w_uk_ref[...]).astype(BF16)
        v_s[0:t_own, :] = _dot(c, w_uv_ref[...]).astype(BF16)
        kr_s[0:t_own, :] = kr4_ref[...]
        if cached:
            cc = cckv_ref[0].astype(BF16)
            kn_s[t_own:, :] = _dot(cc, w_uk_ref[...]).astype(BF16)
            v_s[t_own:, :] = _dot(cc, w_uv_ref[...]).astype(BF16)
            kr_s[t_own:, :] = ckr4_ref[0].astype(BF16)

    kr4 = kr_s[...]
    for p in range(HEAD_PAIRS):
        sl = slice(p * LANES, (p + 1) * LANES)
        qn = qn_ref[:, sl]
        qr = qr_ref[:, (p // 2) * LANES:(p // 2 + 1) * LANES]
        k_cat = jnp.concatenate([kn_s[:, sl], kr4], axis=-1)
        v = v_s[:, sl]
        o = None
        for i in range(2):
            h = 2 * p + i
            m64 = _lane_mask(MLA_NOPE, i, BF16)
            q_cat = jnp.concatenate([qn * m64, qr * _lane_mask(MLA_ROPE, h % 4, BF16)], axis=-1)
            s = _dot_nt(q_cat, k_cat) * MLA_SCALE
            oh = _softmax_pv([s], [v * m64])
            o = oh if o is None else o + oh
        o_ref[:, sl] = o.astype(BF16)


def _mla_attention(qn, qr, ckv, kr4, cache_ckv, cache_kr4, w_uk, w_uv, n_batch, tq):
    n = qn.shape[0]
    s_own = n // n_batch
    nq = s_own // tq
    cached = cache_ckv is not None
    t_all = s_own + (cache_ckv.shape[1] if cached else 0)
    qrow = lambda b, j: (b * nq + j, 0)
    own = lambda b, j: (b, 0)
    full = lambda b, j: (0, 0)
    in_specs = [
        pl.BlockSpec((tq, qn.shape[1]), qrow),
        pl.BlockSpec((tq, qr.shape[1]), qrow),
        pl.BlockSpec((s_own, MLA_KV_LORA), own),
        pl.BlockSpec((s_own, LANES), own),
    ]
    args = [qn, qr, ckv, kr4]
    if cached:
        in_specs += [pl.BlockSpec((1,) + cache_ckv.shape[1:], lambda b, j: (b, 0, 0)),
                     pl.BlockSpec((1,) + cache_kr4.shape[1:], lambda b, j: (b, 0, 0))]
        args += [cache_ckv, cache_kr4]
    in_specs += [pl.BlockSpec(w_uk.shape, full), pl.BlockSpec(w_uv.shape, full)]
    args += [w_uk, w_uv]
    return pl.pallas_call(
        functools.partial(_mla_attn_kernel, cached=cached),
        grid=(n_batch, nq),
        in_specs=in_specs,
        out_specs=pl.BlockSpec((tq, D_MODEL), qrow),
        out_shape=jax.ShapeDtypeStruct((n, D_MODEL), BF16),
        scratch_shapes=[pltpu.VMEM((t_all, D_MODEL), BF16), pltpu.VMEM((t_all, D_MODEL), BF16),
                        pltpu.VMEM((t_all, LANES), BF16)],
        compiler_params=_cparams("parallel", "arbitrary"),
        name="mla_attention_cached" if cached else "mla_attention",
    )(*args)


def _premix_nat_kernel(x_ref, mod_ref, g_ref, w_ref, q_ref, k_ref, v_ref):
    h = _rms(x_ref[...], g_ref[...]) * (1.0 + mod_ref[0, 1:2, :]) + mod_ref[0, 0:1, :]
    qkv = _dot(h.astype(BF16), w_ref[...])
    q_ref[...] = qkv[:, :D_MODEL].astype(q_ref.dtype)
    k_ref[...] = qkv[:, D_MODEL:2 * D_MODEL].astype(k_ref.dtype)
    v_ref[...] = qkv[:, 2 * D_MODEL:].astype(v_ref.dtype)


def _premix_nat(x, mod, g, w_qkv, rows_per_cond, tm, kv_dtype):
    n = x.shape[0]
    row = lambda i: (i, 0)
    full = lambda i: (0, 0)
    return pl.pallas_call(
        _premix_nat_kernel,
        grid=(n // tm,),
        in_specs=[
            pl.BlockSpec((tm, D_MODEL), row),
            _mod_spec(None if rows_per_cond is None else rows_per_cond // tm),
            pl.BlockSpec((1, D_MODEL), full),
            pl.BlockSpec(w_qkv.shape, full),
        ],
        out_specs=[pl.BlockSpec((tm, D_MODEL), row)] * 3,
        out_shape=[jax.ShapeDtypeStruct((n, D_MODEL), BF16),
                   jax.ShapeDtypeStruct((n, D_MODEL), kv_dtype),
                   jax.ShapeDtypeStruct((n, D_MODEL), kv_dtype)],
        compiler_params=_cparams("parallel"),
        name="premix_nat",
    )(x, mod, g, w_qkv)


def _dense_attn_kernel(q_ref, k_ref, v_ref, o_ref):
    for p in range(HEAD_PAIRS):
        sl = slice(p * LANES, (p + 1) * LANES)
        q = q_ref[:, sl]
        k = k_ref[:, sl].astype(BF16)
        v = v_ref[:, sl].astype(BF16)
        o = None
        for i in range(2):
            m64 = _lane_mask(NAT_DH, i, BF16)
            s = _dot_nt(q * m64, k) * NAT_SCALE
            oh = _softmax_pv([s], [v * m64])
            o = oh if o is None else o + oh
        o_ref[:, sl] = o.astype(BF16)


def _dense_attention(q, k, v, n_batch):
    n = q.shape[0]
    s = n // n_batch
    blk = pl.BlockSpec((s, D_MODEL), lambda b: (b, 0))
    return pl.pallas_call(
        _dense_attn_kernel,
        grid=(n_batch,),
        in_specs=[blk, blk, blk],
        out_specs=blk,
        out_shape=jax.ShapeDtypeStruct((n, D_MODEL), BF16),
        compiler_params=_cparams("parallel"),
        name="dense_attention",
    )(q, k, v)


_NAT_QBLK = NAT_QROWS * GRID_W
_NAT_KBLK = NAT_KROWS * GRID_W


def _nat_block_plan(rows):
    assert rows % NAT_QROWS == 0 and rows >= NAT_KROWS
    plan, variants = [], []
    for r0 in range(0, rows, NAT_QROWS):
        ks = min(max(r0 - WIN_H // 2, 0), rows - NAT_KROWS)
        r = r0 + np.arange(NAT_QROWS)
        kr = ks + np.arange(NAT_KROWS)
        rs = np.clip(r - WIN_H // 2, 0, rows - WIN_H)
        valid_row = (kr[None, :] >= rs[:, None]) & (kr[None, :] < rs[:, None] + WIN_H)
        d0 = ks - r + (WIN_H - 1)
        for vi, (d0_v, valid_v) in enumerate(variants):
            if np.array_equal(d0, d0_v) and np.array_equal(valid_row, valid_v):
                break
        else:
            vi = len(variants)
            variants.append((d0, valid_row))
        plan.append((ks, vi))
    return plan, variants


def _nat_bias_tables(rpb, rows):
    n_heads, n_dr, n_dc = rpb.shape
    _, variants = _nat_block_plan(rows)
    c = np.arange(GRID_W)
    cs = np.clip(c - WIN_W // 2, 0, GRID_W - WIN_W)
    valid_col = (c[None, :] >= cs[:, None]) & (c[None, :] < cs[:, None] + WIN_W)
    d_col = c[None, :] - c[:, None] + (WIN_W - 1)
    sel = (d_col[None] == np.arange(n_dc)[:, None, None]) & valid_col[None]
    toep = jnp.einsum('hdj,jck->hdck', rpb, jnp.asarray(sel, F32), precision=lax.Precision.HIGHEST)
    toep = jnp.where(jnp.asarray(valid_col)[None, None], toep, NEG_INF)
    pad = NAT_KROWS + NAT_QROWS
    toep = jnp.pad(toep, ((0, 0), (pad, pad), (0, 0), (0, 0)), constant_values=NEG_INF)
    tabs = []
    for d0, valid_row in variants:
        rows_b = jnp.stack([toep[:, pad + d:pad + d + NAT_KROWS] for d in d0], axis=1)
        rows_b = jnp.where(jnp.asarray(valid_row)[None, :, :, None, None], rows_b, NEG_INF)
        tabs.append(rows_b.transpose(0, 1, 3, 2, 4).reshape(n_heads, _NAT_QBLK, _NAT_KBLK))
    return jnp.stack(tabs, axis=0)


def _nat_attn_kernel(q_ref, k_ref, v_ref, kc_ref, vc_ref, bias_ref, o_ref, *, plan):
    kc = kc_ref[0].astype(BF16)
    vc = vc_ref[0].astype(BF16)
    masks = [_lane_mask(NAT_DH, i, BF16) for i in range(2)]
    vcm = [vc * m for m in masks]
    for bi, (ks, var) in enumerate(plan):
        q = q_ref[bi * _NAT_QBLK:(bi + 1) * _NAT_QBLK, :]
        k = k_ref[ks * GRID_W:ks * GRID_W + _NAT_KBLK, :]
        v = v_ref[ks * GRID_W:ks * GRID_W + _NAT_KBLK, :]
        o = None
        for i in range(2):
            qm = q * masks[i]
            s_loc = _dot_nt(qm, k) * NAT_SCALE + bias_ref[var, i]
            s_ctx = _dot_nt(qm, kc) * NAT_SCALE
            oh = _softmax_pv([s_loc, s_ctx], [v * masks[i], vcm[i]])
            o = oh if o is None else o + oh
        o_ref[bi * _NAT_QBLK:(bi + 1) * _NAT_QBLK, :] = o.astype(BF16)


def _nat_attention(q, k, v, cache_k, cache_v, bias, n_batch):
    n = q.shape[0]
    s = n // n_batch
    plan, variants = _nat_block_plan(s // GRID_W)
    own = pl.BlockSpec((s, LANES), lambda p, b: (b, p))
    cache = pl.BlockSpec((1, cache_k.shape[1], LANES), lambda p, b: (b, 0, p))
    return pl.pallas_call(
        functools.partial(_nat_attn_kernel, plan=plan),
        grid=(HEAD_PAIRS, n_batch),
        in_specs=[own, own, own, cache, cache,
                  pl.BlockSpec((len(variants), 2, _NAT_QBLK, _NAT_KBLK), lambda p, b: (0, p, 0, 0))],
        out_specs=own,
        out_shape=jax.ShapeDtypeStruct((n, D_MODEL), BF16),
        compiler_params=_cparams("parallel", "parallel"),
        name="nat_attention",
    )(q, k, v, cache_k, cache_v, bias)


def _proj_res_kernel(x_ref, a_ref, w_ref, mod_ref, o_ref):
    o_ref[...] = x_ref[...] + mod_ref[0, 2:3, :] * _dot(a_ref[...], w_ref[...])


def _proj_residual(x, a, w, mod, rows_per_cond, tm):
    n = x.shape[0]
    row = lambda i: (i, 0)
    return pl.pallas_call(
        _proj_res_kernel,
        grid=(n // tm,),
        in_specs=[
            pl.BlockSpec((tm, D_MODEL), row),
            pl.BlockSpec((tm, D_MODEL), row),
            pl.BlockSpec(w.shape, lambda i: (0, 0)),
            _mod_spec(None if rows_per_cond is None else rows_per_cond // tm),
        ],
        out_specs=pl.BlockSpec((tm, D_MODEL), row),
        out_shape=jax.ShapeDtypeStruct((n, D_MODEL), F32),
        compiler_params=_cparams("parallel"),
        name="proj_residual",
    )(x, a, w, mod)


def _top2_sum(a, b, c, d):
    hi1, lo1 = jnp.maximum(a, b), jnp.minimum(a, b)
    hi2, lo2 = jnp.maximum(c, d), jnp.minimum(c, d)
    return jnp.maximum(hi1, hi2) + jnp.maximum(jnp.minimum(hi1, hi2), jnp.maximum(lo1, lo2))


def _route(logits, bias):
    scores = _sigmoid(logits)
    biased = scores + bias
    sc = [scores[:, e:e + 1] for e in range(N_EXPERTS)]
    bs = [biased[:, e:e + 1] for e in range(N_EXPERTS)]
    gscore = [_top2_sum(*bs[EPG * g:EPG * (g + 1)]) for g in range(N_GROUPS)]
    best, gidx = gscore[0], jnp.zeros_like(gscore[0], dtype=jnp.int32)
    for g in range(1, N_GROUPS):
        better = gscore[g] > best
        gidx = jnp.where(better, g, gidx)
        best = jnp.where(better, gscore[g], best)
    cb = [functools.reduce(lambda a, b: a + b,
                           [jnp.where(gidx == g, bs[EPG * g + i], 0.0) for g in range(N_GROUPS)])
          for i in range(EPG)]
    cs = [functools.reduce(lambda a, b: a + b,
                           [jnp.where(gidx == g, sc[EPG * g + i], 0.0) for g in range(N_GROUPS)])
          for i in range(EPG)]
    b1, i1 = cb[0], jnp.zeros_like(gidx)
    for i in range(1, EPG):
        better = cb[i] > b1
        i1 = jnp.where(better, i, i1)
        b1 = jnp.where(better, cb[i], b1)
    b2, i2 = jnp.full_like(b1, -jnp.inf), jnp.full_like(i1, -1)
    for i in range(EPG):
        better = (i1 != i) & (cb[i] > b2)
        i2 = jnp.where(better, i, i2)
        b2 = jnp.where(better, cb[i], b2)
    sel = [(i1 == i) | (i2 == i) for i in range(EPG)]
    w = [jnp.where(sel[i], cs[i], 0.0) for i in range(EPG)]
    tot = w[0] + w[1] + w[2] + w[3]
    return [[jnp.where(gidx == g, w[i] / tot, 0.0) for i in range(EPG)] for g in range(N_GROUPS)]


def _moe_kernel(x_ref, mod_ref, g_ref, wr_ref, rb_ref, wg_ref, wu_ref, wd_ref, o_ref,
                h_s, gate_s, acc_s):
    grp = pl.program_id(1)

    @pl.when(grp == 0)
    def _():
        h = _rms(x_ref[...], g_ref[...]) * (1.0 + mod_ref[0, 4:5, :]) + mod_ref[0, 3:4, :]
        h_s[...] = h.astype(BF16)
        logits = jnp.dot(h, wr_ref[...], precision=lax.Precision.HIGHEST, preferred_element_type=F32)
        gates = _route(logits, rb_ref[...])
        lane = lax.broadcasted_iota(jnp.int32, (1, LANES), 1)
        for g in range(N_GROUPS):
            gate_s[g] = functools.reduce(
                lambda a, b: a + b, [jnp.where(lane == i, gates[g][i], 0.0) for i in range(EPG)])
        acc_s[...] = jnp.zeros_like(acc_s)

    h = h_s[...]
    gate = gate_s[grp]
    acts = []
    for i in range(EPG):
        a = _dot(h, wg_ref[i])
        u = _dot(h, wu_ref[i])
        acts.append(((a * _sigmoid(a)) * u * gate[:, i:i + 1]).astype(BF16))
    act = jnp.concatenate(acts, axis=-1)
    acc_s[...] += _dot(act, wd_ref[...].reshape(EPG * D_EXPERT, D_MODEL))

    @pl.when(grp == N_GROUPS - 1)
    def _():
        o_ref[...] = x_ref[...] + mod_ref[0, 5:6, :] * acc_s[...]


def _moe(x, mod, g, w_router, router_bias, wg, wu, wd, layer, rows_per_cond, tm):
    n = x.shape[0]
    row = lambda i, e: (i, 0)
    full = lambda i, e: (0, 0)
    return pl.pallas_call(
        _moe_kernel,
        grid=(n // tm, N_GROUPS),
        in_specs=[
            pl.BlockSpec((tm, D_MODEL), row),
            _mod_spec(None if rows_per_cond is None else rows_per_cond // tm),
            pl.BlockSpec((1, D_MODEL), full),
            pl.BlockSpec(w_router.shape, full),
            pl.BlockSpec(router_bias.shape, full),
            pl.BlockSpec((None, EPG, D_MODEL, D_EXPERT), lambda i, e: (layer, e, 0, 0)),
            pl.BlockSpec((None, EPG, D_MODEL, D_EXPERT), lambda i, e: (layer, e, 0, 0)),
            pl.BlockSpec((None, EPG, D_EXPERT, D_MODEL), lambda i, e: (layer, e, 0, 0)),
        ],
        out_specs=pl.BlockSpec((tm, D_MODEL), row),
        out_shape=jax.ShapeDtypeStruct((n, D_MODEL), F32),
        scratch_shapes=[pltpu.VMEM((tm, D_MODEL), BF16),
                        pltpu.VMEM((N_GROUPS, tm, LANES), F32),
                        pltpu.VMEM((tm, D_MODEL), F32)],
        compiler_params=_cparams("parallel", "arbitrary"),
        name="grouped_moe",
    )(x, mod, g, w_router, router_bias, wg, wu, wd)


def _final_norm_kernel(x_ref, g_ref, o_ref):
    o_ref[...] = _rms(x_ref[...], g_ref[...])


def _final_norm(x, g, tm):
    n = x.shape[0]
    return pl.pallas_call(
        _final_norm_kernel,
        grid=(n // tm,),
        in_specs=[pl.BlockSpec((tm, D_MODEL), lambda i: (i, 0)),
                  pl.BlockSpec((1, D_MODEL), lambda i: (0, 0))],
        out_specs=pl.BlockSpec((tm, D_MODEL), lambda i: (i, 0)),
        out_shape=jax.ShapeDtypeStruct((n, D_MODEL), F32),
        compiler_params=_cparams("parallel"),
        name="final_norm",
    )(x, g)


def _rope_tables(n_lat):
    t = np.arange(n_lat)
    n_freq = MLA_ROPE // 4
    inv_freq = jnp.asarray(ROPE_THETA, F32) ** (-jnp.arange(n_freq, dtype=F32) / n_freq)
    ar = jnp.asarray(t // GRID_W, F32)[:, None] * inv_freq
    ac = jnp.asarray(t % GRID_W, F32)[:, None] * inv_freq
    cos = jnp.concatenate([jnp.cos(ar), jnp.cos(ar), jnp.cos(ac), jnp.cos(ac)], axis=-1)
    sin = jnp.concatenate([-jnp.sin(ar), jnp.sin(ar), -jnp.sin(ac), jnp.sin(ac)], axis=-1)
    return (jnp.tile(cos, (1, MLA_HEADS)), jnp.tile(sin, (1, MLA_HEADS)),
            jnp.tile(cos, (1, LANES // MLA_ROPE)), jnp.tile(sin, (1, LANES // MLA_ROPE)))


_ROPE_SWAP = np.concatenate([np.arange(8, 16), np.arange(0, 8), np.arange(24, 32), np.arange(16, 24)])


def _mla_weights(w_in, w_uq, w_ukv):
    o = MLA_Q_LORA + MLA_KV_LORA
    kr = w_in[:, o:]
    rep = LANES // MLA_ROPE
    w_in_x = jnp.concatenate([w_in[:, :o], jnp.tile(kr, (1, rep)), jnp.tile(kr[:, _ROPE_SWAP], (1, rep))],
                             axis=-1).astype(BF16)
    uq = w_uq.reshape(MLA_Q_LORA, MLA_HEADS, MLA_NOPE + MLA_ROPE)
    q_nope = uq[:, :, :MLA_NOPE].reshape(MLA_Q_LORA, -1)
    q_rope = uq[:, :, MLA_NOPE:]
    w_uq_x = jnp.concatenate([q_nope, q_rope.reshape(MLA_Q_LORA, -1),
                              q_rope[:, :, _ROPE_SWAP].reshape(MLA_Q_LORA, -1)], axis=-1).astype(BF16)
    ukv = w_ukv.reshape(MLA_KV_LORA, MLA_HEADS, MLA_NOPE + MLA_V)
    w_uk = ukv[:, :, :MLA_NOPE].reshape(MLA_KV_LORA, -1).astype(BF16)
    w_uv = ukv[:, :, MLA_NOPE:].reshape(MLA_KV_LORA, -1).astype(BF16)
    return w_in_x, w_uq_x, w_uk, w_uv


def kernel(x_prompt, x_sample, cache_mla_ckv, cache_mla_krope, cache_nat_k, cache_nat_v, c, c_ctx,
           w_ada, b_ada, norm_mix, norm_ffn, norm_final, mla_w_in, mla_q_norm, mla_w_uq, mla_kv_norm,
           mla_w_ukv, mla_w_o, nat_w_qkv, nat_rpb, nat_w_o, w_router, router_bias,
           moe_w_gate, moe_w_up, moe_w_down):
    B, S, D = x_prompt.shape
    Bd, Sd, _ = x_sample.shape
    assert D == D_MODEL and Bd + 1 <= 8
    tm_c, tm_s = 512, 512
    tm_moe = 1024

    xc = x_prompt.reshape(B * S, D)
    xs = x_sample.reshape(Bd * Sd, D)
    cond8 = jnp.concatenate([c_ctx[None, :], c, jnp.zeros((8 - 1 - Bd, D), F32)], axis=0)
    mod_all = _ada_modulation(cond8, w_ada, b_ada).reshape(DEPTH, 8, 6, D)

    rope_tabs = _rope_tables(Sd)
    wr = jnp.pad(w_router, ((0, 0), (0, LANES - N_EXPERTS)))
    rb = jnp.pad(router_bias, (0, LANES - N_EXPERTS)).reshape(1, LANES)
    wg = moe_w_gate.astype(BF16)
    wu = moe_w_up.astype(BF16)
    wd = moe_w_down.astype(BF16)

    new_ckv, new_krope, new_k, new_v = [], [], [], []
    for layer in range(DEPTH):
        mod = mod_all[layer]
        g_mix = norm_mix[layer][None, :]
        j = layer // 2
        if layer % 2 == 0:
            w_in_x, w_uq_x, w_uk, w_uv = _mla_weights(mla_w_in[j], mla_w_uq[j], mla_w_ukv[j])
            w_in_c = w_in_x[:, :MLA_Q_LORA + MLA_KV_LORA + LANES]
            w_uq_c = w_uq_x[:, :MLA_HEADS * (MLA_NOPE + MLA_ROPE)]
            qg = mla_q_norm[j][None, :]
            kvg = mla_kv_norm[j][None, :]
            w_o = mla_w_o[j].astype(BF16)
            qn, qr, ckv, kr, kr4 = _premix_mla(xc, mod, g_mix, w_in_c, qg, w_uq_c, kvg, None, None, tm_c)
            new_ckv.append(ckv.reshape(B, S, MLA_KV_LORA))
            new_krope.append(kr.reshape(B, S, MLA_ROPE))
            att_c = _mla_attention(qn, qr, ckv, kr4, None, None, w_uk, w_uv, B, S)
            qn, qr, ckv, kr, kr4 = _premix_mla(xs, mod, g_mix, w_in_x, qg, w_uq_x, kvg, rope_tabs, Sd, tm_s)
            cache_kr4 = jnp.tile(cache_mla_krope[:, j], (1, 1, LANES // MLA_ROPE))
            att_s = _mla_attention(qn, qr, ckv, kr4, cache_mla_ckv[:, j], cache_kr4, w_uk, w_uv, Bd, 256)
        else:
            w_qkv = nat_w_qkv[j].astype(BF16)
            w_o = nat_w_o[j].astype(BF16)
            q, k, v = _premix_nat(xc, mod, g_mix, w_qkv, None, tm_c, F32)
            new_k.append(k.reshape(B, S, NAT_HEADS, NAT_DH))
            new_v.append(v.reshape(B, S, NAT_HEADS, NAT_DH))
            att_c = _dense_attention(q, k, v, B)
            q, k, v = _premix_nat(xs, mod, g_mix, w_qkv, Sd, tm_s, BF16)
            bias = _nat_bias_tables(nat_rpb[j], Sd // GRID_W)
            att_s = _nat_attention(q, k, v,
                                   cache_nat_k[:, j].reshape(Bd, -1, D), cache_nat_v[:, j].reshape(Bd, -1, D),
                                   bias, Bd)
        xc = _proj_residual(xc, att_c, w_o, mod, None, tm_c)
        xs = _proj_residual(xs, att_s, w_o, mod, Sd, tm_s)
        g_ffn = norm_ffn[layer][None, :]
        xc = _moe(xc, mod, g_ffn, wr, rb, wg, wu, wd, layer, None, tm_moe)
        xs = _moe(xs, mod, g_ffn, wr, rb, wg, wu, wd, layer, Sd, tm_moe)

    gf = norm_final[None, :]
    y_prompt = _final_norm(xc, gf, tm_c).reshape(B, S, D)
    y_sample = _final_norm(xs, gf, tm_s).reshape(Bd, Sd, D)
    return (y_prompt, y_sample, jnp.stack(new_ckv, axis=1), jnp.stack(new_krope, axis=1),
            jnp.stack(new_k, axis=1), jnp.stack(new_v, axis=1))
```

```python
import functools

import numpy as np
import jax
import jax.numpy as jnp
from jax import lax
from jax.experimental import pallas as pl
from jax.experimental.pallas import tpu as pltpu

F32 = jnp.float32
BF16 = jnp.bfloat16

D_MODEL = 1024
DEPTH = 4
GRID_W = 64
LANES = 128
MLA_HEADS = 16
MLA_NOPE = 64
MLA_ROPE = 32
MLA_V = 64
MLA_Q_LORA = 384
MLA_KV_LORA = 256
MLA_SCALE = (MLA_NOPE + MLA_ROPE) ** -0.5
ROPE_THETA = 10000.0
NAT_HEADS = 16
NAT_DH = 64
NAT_SCALE = NAT_DH ** -0.5
WIN_H = 8
WIN_W = 16
NAT_QROWS = 4
NAT_KROWS = NAT_QROWS + WIN_H
N_EXPERTS = 16
N_GROUPS = 4
EPG = N_EXPERTS // N_GROUPS
D_EXPERT = 256
NORM_EPS = 1e-6
NEG_INF = -1e30

HEAD_PAIRS = MLA_HEADS // 2
VMEM_LIMIT = 56 * 1024 * 1024


def _cparams(*sem):
    return pltpu.CompilerParams(dimension_semantics=sem, vmem_limit_bytes=VMEM_LIMIT)


def _sigmoid(x):
    return 1.0 / (1.0 + jnp.exp(-x))


def _rms(x, g):
    ms = jnp.mean(x * x, axis=-1, keepdims=True)
    return x * lax.rsqrt(ms + NORM_EPS) * g


def _dot(a, b):
    return jnp.dot(a, b, preferred_element_type=F32)


def _dot_nt(a, b):
    return lax.dot_general(a, b, (((1,), (1,)), ((), ())), preferred_element_type=F32)


def _lane_mask(width, idx, dtype):
    lane = lax.broadcasted_iota(jnp.int32, (1, LANES), 1)
    return ((lane >= idx * width) & (lane < (idx + 1) * width)).astype(dtype)


def _ada_kernel(cond_ref, w_ref, b_ref, o_ref):
    c = cond_ref[...]
    s = (c * _sigmoid(c)).astype(BF16)
    o_ref[0] = _dot(s, w_ref[0].astype(BF16)) + b_ref[0]


def _ada_modulation(cond8, w_ada, b_ada):
    n_chunk = 6
    return pl.pallas_call(
        _ada_kernel,
        grid=(DEPTH, n_chunk),
        in_specs=[
            pl.BlockSpec((8, D_MODEL), lambda l, j: (0, 0)),
            pl.BlockSpec((1, D_MODEL, D_MODEL), lambda l, j: (l, 0, j)),
            pl.BlockSpec((1, 1, D_MODEL), lambda l, j: (l, 0, j)),
        ],
        out_specs=pl.BlockSpec((1, 8, D_MODEL), lambda l, j: (l, 0, j)),
        out_shape=jax.ShapeDtypeStruct((DEPTH, 8, 6 * D_MODEL), F32),
        compiler_params=_cparams("parallel", "parallel"),
        name="ada_modulation",
    )(cond8, w_ada, b_ada.reshape(DEPTH, 1, 6 * D_MODEL))


def _mod_spec(rows_per_cond):
    if rows_per_cond is None:
        return pl.BlockSpec((1, 6, D_MODEL), lambda i, *_: (0, 0, 0))
    return pl.BlockSpec((1, 6, D_MODEL), lambda i, *_: (1 + i // rows_per_cond, 0, 0))


def _premix_mla_kernel(*refs, rope):
    if rope:
        (x_ref, mod_ref, g_ref, w_in_ref, qg_ref, w_uq_ref, kvg_ref,
         cosq_ref, sinq_ref, cosk_ref, sink_ref,
         qn_ref, qr_ref, ckv_ref, kr_ref, kr4_ref) = refs
    else:
        (x_ref, mod_ref, g_ref, w_in_ref, qg_ref, w_uq_ref, kvg_ref,
         qn_ref, qr_ref, ckv_ref, kr_ref, kr4_ref) = refs
    h = _rms(x_ref[...], g_ref[...]) * (1.0 + mod_ref[0, 1:2, :]) + mod_ref[0, 0:1, :]
    lat = _dot(h.astype(BF16), w_in_ref[...])
    c_q = lat[:, :MLA_Q_LORA]
    c_kv = lat[:, MLA_Q_LORA:MLA_Q_LORA + MLA_KV_LORA]
    o = MLA_Q_LORA + MLA_KV_LORA
    kr4 = lat[:, o:o + LANES]
    q = _dot(_rms(c_q, qg_ref[...]).astype(BF16), w_uq_ref[...])
    n_nope = MLA_HEADS * MLA_NOPE
    n_rope = MLA_HEADS * MLA_ROPE
    qr = q[:, n_nope:n_nope + n_rope]
    if rope:
        qr = qr * cosq_ref[...] + q[:, n_nope + n_rope:] * sinq_ref[...]
        kr4 = kr4 * cosk_ref[...] + lat[:, o + LANES:o + 2 * LANES] * sink_ref[...]
    qn_ref[...] = q[:, :n_nope].astype(BF16)
    qr_ref[...] = qr.astype(BF16)
    ckv_ref[...] = _rms(c_kv, kvg_ref[...])
    kr_ref[...] = kr4[:, :MLA_ROPE]
    kr4_ref[...] = kr4.astype(BF16)


def _premix_mla(x, mod, g, w_in, qg, w_uq, kvg, rope_tabs, rows_per_cond, tm):
    n = x.shape[0]
    rope = rope_tabs is not None
    row = lambda i: (i, 0)
    full = lambda i: (0, 0)
    in_specs = [
        pl.BlockSpec((tm, D_MODEL), row),
        _mod_spec(None if rows_per_cond is None else rows_per_cond // tm),
        pl.BlockSpec((1, D_MODEL), full),
        pl.BlockSpec(w_in.shape, full),
        pl.BlockSpec((1, MLA_Q_LORA), full),
        pl.BlockSpec(w_uq.shape, full),
        pl.BlockSpec((1, MLA_KV_LORA), full),
    ]
    args = [x, mod, g, w_in, qg, w_uq, kvg]
    if rope:
        nblk = rope_tabs[0].shape[0] // tm
        pos = lambda i: (i % nblk, 0)
        for t in rope_tabs:
            in_specs.append(pl.BlockSpec((tm, t.shape[1]), pos))
            args.append(t)
    widths = (MLA_HEADS * MLA_NOPE, MLA_HEADS * MLA_ROPE, MLA_KV_LORA, MLA_ROPE, LANES)
    dtypes = (BF16, BF16, F32, F32, BF16)
    return pl.pallas_call(
        functools.partial(_premix_mla_kernel, rope=rope),
        grid=(n // tm,),
        in_specs=in_specs,
        out_specs=[pl.BlockSpec((tm, w), row) for w in widths],
        out_shape=[jax.ShapeDtypeStruct((n, w), d) for w, d in zip(widths, dtypes)],
        compiler_params=_cparams("parallel"),
        name="premix_mla_rope" if rope else "premix_mla",
    )(*args)


def _softmax_pv(s_list, v_list):
    m = functools.reduce(jnp.maximum, [jnp.max(s, axis=-1, keepdims=True) for s in s_list])
    e_list = [jnp.exp(s - m) for s in s_list]
    l = functools.reduce(lambda a, b: a + b, [jnp.sum(e, axis=-1, keepdims=True) for e in e_list])
    o = functools.reduce(lambda a, b: a + b,
                         [_dot(e.astype(BF16), v) for e, v in zip(e_list, v_list)])
    return o / l


def _mla_attn_kernel(*refs, cached):
    if cached:
        (qn_ref, qr_ref, ckv_ref, kr4_ref, cckv_ref, ckr4_ref, w_uk_ref, w_uv_ref,
         o_ref, kn_s, v_s, kr_s) = refs
    else:
        (qn_ref, qr_ref, ckv_ref, kr4_ref, w_uk_ref, w_uv_ref, o_ref, kn_s, v_s, kr_s) = refs
    t_own = ckv_ref.shape[0]

    @pl.when(pl.program_id(1) == 0)
    def _():
        c = ckv_ref[...].astype(BF16)
        kn_s[0:t_own, :] = _dot(c, w_uk_ref[...]).astype(BF16)
        v_s[0:t_own, :] = _dot(c, w_uv_ref[...]).astype(BF16)
        kr_s[0:t_own, :] = kr4_ref[...]
        if cached:
            cc = cckv_ref[0].astype(BF16)
            kn_s[t_own:, :] = _dot(cc, w_uk_ref[...]).astype(BF16)
            v_s[t_own:, :] = _dot(cc, w_uv_ref[...]).astype(BF16)
            kr_s[t_own:, :] = ckr4_ref[0].astype(BF16)

    kr4 = kr_s[...]
    for p in range(HEAD_PAIRS):
        sl = slice(p * LANES, (p + 1) * LANES)
        qn = qn_ref[:, sl]
        qr = qr_ref[:, (p // 2) * LANES:(p // 2 + 1) * LANES]
        k_cat = jnp.concatenate([kn_s[:, sl], kr4], axis=-1)
        v = v_s[:, sl]
        o = None
        for i in range(2):
            h = 2 * p + i
            m64 = _lane_mask(MLA_NOPE, i, BF16)
            q_cat = jnp.concatenate([qn * m64, qr * _lane_mask(MLA_ROPE, h % 4, BF16)], axis=-1)
            s = _dot_nt(q_cat, k_cat) * MLA_SCALE
            oh = _softmax_pv([s], [v * m64])
            o = oh if o is None else o + oh
        o_ref[:, sl] = o.astype(BF16)


def _mla_attention(qn, qr, ckv, kr4, cache_ckv, cache_kr4, w_uk, w_uv, n_batch, tq):
    n = qn.shape[0]
    s_own = n // n_batch
    nq = s_own // tq
    cached = cache_ckv is not None
    t_all = s_own + (cache_ckv.shape[1] if cached else 0)
    qrow = lambda b, j: (b * nq + j, 0)
    own = lambda b, j: (b, 0)
    full = lambda b, j: (0, 0)
    in_specs = [
        pl.BlockSpec((tq, qn.shape[1]), qrow),
        pl.BlockSpec((tq, qr.shape[1]), qrow),
        pl.BlockSpec((s_own, MLA_KV_LORA), own),
        pl.BlockSpec((s_own, LANES), own),
    ]
    args = [qn, qr, ckv, kr4]
    if cached:
        in_specs += [pl.BlockSpec((1,) + cache_ckv.shape[1:], lambda b, j: (b, 0, 0)),
                     pl.BlockSpec((1,) + cache_kr4.shape[1:], lambda b, j: (b, 0, 0))]
        args += [cache_ckv, cache_kr4]
    in_specs += [pl.BlockSpec(w_uk.shape, full), pl.BlockSpec(w_uv.shape, full)]
    args += [w_uk, w_uv]
    return pl.pallas_call(
        functools.partial(_mla_attn_kernel, cached=cached),
        grid=(n_batch, nq),
        in_specs=in_specs,
        out_specs=pl.BlockSpec((tq, D_MODEL), qrow),
        out_shape=jax.ShapeDtypeStruct((n, D_MODEL), BF16),
        scratch_shapes=[pltpu.VMEM((t_all, D_MODEL), BF16), pltpu.VMEM((t_all, D_MODEL), BF16),
                        pltpu.VMEM((t_all, LANES), BF16)],
        compiler_params=_cparams("parallel", "arbitrary"),
        name="mla_attention_cached" if cached else "mla_attention",
    )(*args)


def _premix_nat_kernel(x_ref, mod_ref, g_ref, w_ref, q_ref, k_ref, v_ref):
    h = _rms(x_ref[...], g_ref[...]) * (1.0 + mod_ref[0, 1:2, :]) + mod_ref[0, 0:1, :]
    qkv = _dot(h.astype(BF16), w_ref[...])
    q_ref[...] = qkv[:, :D_MODEL].astype(q_ref.dtype)
    k_ref[...] = qkv[:, D_MODEL:2 * D_MODEL].astype(k_ref.dtype)
    v_ref[...] = qkv[:, 2 * D_MODEL:].astype(v_ref.dtype)


def _premix_nat(x, mod, g, w_qkv, rows_per_cond, tm, kv_dtype):
    n = x.shape[0]
    row = lambda i: (i, 0)
    full = lambda i: (0, 0)
    return pl.pallas_call(
        _premix_nat_kernel,
        grid=(n // tm,),
        in_specs=[
            pl.BlockSpec((tm, D_MODEL), row),
            _mod_spec(None if rows_per_cond is None else rows_per_cond // tm),
            pl.BlockSpec((1, D_MODEL), full),
            pl.BlockSpec(w_qkv.shape, full),
        ],
        out_specs=[pl.BlockSpec((tm, D_MODEL), row)] * 3,
        out_shape=[jax.ShapeDtypeStruct((n, D_MODEL), BF16),
                   jax.ShapeDtypeStruct((n, D_MODEL), kv_dtype),
                   jax.ShapeDtypeStruct((n, D_MODEL), kv_dtype)],
        compiler_params=_cparams("parallel"),
        name="premix_nat",
    )(x, mod, g, w_qkv)


def _dense_attn_kernel(q_ref, k_ref, v_ref, o_ref):
    for p in range(HEAD_PAIRS):
        sl = slice(p * LANES, (p + 1) * LANES)
        q = q_ref[:, sl]
        k = k_ref[:, sl].astype(BF16)
        v = v_ref[:, sl].astype(BF16)
        o = None
        for i in range(2):
            m64 = _lane_mask(NAT_DH, i, BF16)
            s = _dot_nt(q * m64, k) * NAT_SCALE
            oh = _softmax_pv([s], [v * m64])
            o = oh if o is None else o + oh
        o_ref[:, sl] = o.astype(BF16)


def _dense_attention(q, k, v, n_batch):
    n = q.shape[0]
    s = n // n_batch
    blk = pl.BlockSpec((s, D_MODEL), lambda b: (b, 0))
    return pl.pallas_call(
        _dense_attn_kernel,
        grid=(n_batch,),
        in_specs=[blk, blk, blk],
        out_specs=blk,
        out_shape=jax.ShapeDtypeStruct((n, D_MODEL), BF16),
        compiler_params=_cparams("parallel"),
        name="dense_attention",
    )(q, k, v)


_NAT_QBLK = NAT_QROWS * GRID_W
_NAT_KBLK = NAT_KROWS * GRID_W


def _nat_block_plan(rows):
    assert rows % NAT_QROWS == 0 and rows >= NAT_KROWS
    plan, variants = [], []
    for r0 in range(0, rows, NAT_QROWS):
        ks = min(max(r0 - WIN_H // 2, 0), rows - NAT_KROWS)
        r = r0 + np.arange(NAT_QROWS)
        kr = ks + np.arange(NAT_KROWS)
        rs = np.clip(r - WIN_H // 2, 0, rows - WIN_H)
        valid_row = (kr[None, :] >= rs[:, None]) & (kr[None, :] < rs[:, None] + WIN_H)
        d0 = ks - r + (WIN_H - 1)
        for vi, (d0_v, valid_v) in enumerate(variants):
            if np.array_equal(d0, d0_v) and np.array_equal(valid_row, valid_v):
                break
        else:
            vi = len(variants)
            variants.append((d0, valid_row))
        plan.append((ks, vi))
    return plan, variants


def _nat_bias_tables(rpb, rows):
    n_heads, n_dr, n_dc = rpb.shape
    _, variants = _nat_block_plan(rows)
    c = np.arange(GRID_W)
    cs = np.clip(c - WIN_W // 2, 0, GRID_W - WIN_W)
    valid_col = (c[None, :] >= cs[:, None]) & (c[None, :] < cs[:, None] + WIN_W)
    d_col = c[None, :] - c[:, None] + (WIN_W - 1)
    sel = (d_col[None] == np.arange(n_dc)[:, None, None]) & valid_col[None]
    toep = jnp.einsum('hdj,jck->hdck', rpb, jnp.asarray(sel, F32), precision=lax.Precision.HIGHEST)
    toep = jnp.where(jnp.asarray(valid_col)[None, None], toep, NEG_INF)
    pad = NAT_KROWS + NAT_QROWS
    toep = jnp.pad(toep, ((0, 0), (pad, pad), (0, 0), (0, 0)), constant_values=NEG_INF)
    tabs = []
    for d0, valid_row in variants:
        rows_b = jnp.stack([toep[:, pad + d:pad + d + NAT_KROWS] for d in d0], axis=1)
        rows_b = jnp.where(jnp.asarray(valid_row)[None, :, :, None, None], rows_b, NEG_INF)
        tabs.append(rows_b.transpose(0, 1, 3, 2, 4).reshape(n_heads, _NAT_QBLK, _NAT_KBLK))
    return jnp.stack(tabs, axis=0)


def _nat_attn_kernel(q_ref, k_ref, v_ref, kc_ref, vc_ref, bias_ref, o_ref, *, plan):
    kc = kc_ref[0].astype(BF16)
    vc = vc_ref[0].astype(BF16)
    masks = [_lane_mask(NAT_DH, i, BF16) for i in range(2)]
    vcm = [vc * m for m in masks]
    for bi, (ks, var) in enumerate(plan):
        q = q_ref[bi * _NAT_QBLK:(bi + 1) * _NAT_QBLK, :]
        k = k_ref[ks * GRID_W:ks * GRID_W + _NAT_KBLK, :]
        v = v_ref[ks * GRID_W:ks * GRID_W + _NAT_KBLK, :]
        o = None
        for i in range(2):
            qm = q * masks[i]
            s_loc = _dot_nt(qm, k) * NAT_SCALE + bias_ref[var, i]
            s_ctx = _dot_nt(qm, kc) * NAT_SCALE
            oh = _softmax_pv([s_loc, s_ctx], [v * masks[i], vcm[i]])
            o = oh if o is None else o + oh
        o_ref[bi * _NAT_QBLK:(bi + 1) * _NAT_QBLK, :] = o.astype(BF16)


def _nat_attention(q, k, v, cache_k, cache_v, bias, n_batch):
    n = q.shape[0]
    s = n // n_batch
    plan, variants = _nat_block_plan(s // GRID_W)
    own = pl.BlockSpec((s, LANES), lambda p, b: (b, p))
    cache = pl.BlockSpec((1, cache_k.shape[1], LANES), lambda p, b: (b, 0, p))
    return pl.pallas_call(
        functools.partial(_nat_attn_kernel, plan=plan),
        grid=(HEAD_PAIRS, n_batch),
        in_specs=[own, own, own, cache, cache,
                  pl.BlockSpec((len(variants), 2, _NAT_QBLK, _NAT_KBLK), lambda p, b: (0, p, 0, 0))],
        out_specs=own,
        out_shape=jax.ShapeDtypeStruct((n, D_MODEL), BF16),
        compiler_params=_cparams("parallel", "parallel"),
        name="nat_attention",
    )(q, k, v, cache_k, cache_v, bias)


def _proj_res_kernel(x_ref, a_ref, w_ref, mod_ref, o_ref):
    o_ref[...] = x_ref[...] + mod_ref[0, 2:3, :] * _dot(a_ref[...], w_ref[...])


def _proj_residual(x, a, w, mod, rows_per_cond, tm):
    n = x.shape[0]
    row = lambda i: (i, 0)
    return pl.pallas_call(
        _proj_res_kernel,
        grid=(n // tm,),
        in_specs=[
            pl.BlockSpec((tm, D_MODEL), row),
            pl.BlockSpec((tm, D_MODEL), row),
            pl.BlockSpec(w.shape, lambda i: (0, 0)),
            _mod_spec(None if rows_per_cond is None else rows_per_cond // tm),
        ],
        out_specs=pl.BlockSpec((tm, D_MODEL), row),
        out_shape=jax.ShapeDtypeStruct((n, D_MODEL), F32),
        compiler_params=_cparams("parallel"),
        name="proj_residual",
    )(x, a, w, mod)


def _top2_sum(a, b, c, d):
    hi1, lo1 = jnp.maximum(a, b), jnp.minimum(a, b)
    hi2, lo2 = jnp.maximum(c, d), jnp.minimum(c, d)
    return jnp.maximum(hi1, hi2) + jnp.maximum(jnp.minimum(hi1, hi2), jnp.maximum(lo1, lo2))


def _route(logits, bias):
    scores = _sigmoid(logits)
    biased = scores + bias
    sc = [scores[:, e:e + 1] for e in range(N_EXPERTS)]
    bs = [biased[:, e:e + 1] for e in range(N_EXPERTS)]
    gscore = [_top2_sum(*bs[EPG * g:EPG * (g + 1)]) for g in range(N_GROUPS)]
    best, gidx = gscore[0], jnp.zeros_like(gscore[0], dtype=jnp.int32)
    for g in range(1, N_GROUPS):
        better = gscore[g] > best
        gidx = jnp.where(better, g, gidx)
        best = jnp.where(better, gscore[g], best)
    cb = [functools.reduce(lambda a, b: a + b,
                           [jnp.where(gidx == g, bs[EPG * g + i], 0.0) for g in range(N_GROUPS)])
          for i in range(EPG)]
    cs = [functools.reduce(lambda a, b: a + b,
                           [jnp.where(gidx == g, sc[EPG * g + i], 0.0) for g in range(N_GROUPS)])
          for i in range(EPG)]
    b1, i1 = cb[0], jnp.zeros_like(gidx)
    for i in range(1, EPG):
        better = cb[i] > b1
        i1 = jnp.where(better, i, i1)
        b1 = jnp.where(better, cb[i], b1)
    b2, i2 = jnp.full_like(b1, -jnp.inf), jnp.full_like(i1, -1)
    for i in range(EPG):
        better = (i1 != i) & (cb[i] > b2)
        i2 = jnp.where(better, i, i2)
        b2 = jnp.where(better, cb[i], b2)
    sel = [(i1 == i) | (i2 == i) for i in range(EPG)]
    w = [jnp.where(sel[i], cs[i], 0.0) for i in range(EPG)]
    tot = w[0] + w[1] + w[2] + w[3]
    return gidx, [w[i] / tot for i in range(EPG)]


MOE_SRC = 256
MOE_DST = 256
_LOG2_DST = MOE_DST.bit_length() - 1
assert MOE_DST == 1 << _LOG2_DST
_INFO_GID = EPG
_INFO_RANK = EPG + 1


def _moe_route_kernel(x_ref, mod_ref, g_ref, wr_ref, rb_ref, h_ref, info_ref, infot_ref, cum_ref, carry_s):
    @pl.when(pl.program_id(0) == 0)
    def _():
        carry_s[...] = jnp.zeros_like(carry_s)

    h = _rms(x_ref[...], g_ref[...]) * (1.0 + mod_ref[0, 4:5, :]) + mod_ref[0, 3:4, :]
    h_ref[...] = h.astype(BF16)
    logits = jnp.dot(h, wr_ref[...], precision=lax.Precision.HIGHEST, preferred_element_type=F32)
    gidx, gates = _route(logits, rb_ref[...])
    tm = h.shape[0]
    lane = lax.broadcasted_iota(jnp.int32, (1, LANES), 1)
    onehot = (lane == gidx).astype(F32)
    tri = (lax.broadcasted_iota(jnp.int32, (tm, tm), 0)
           >= lax.broadcasted_iota(jnp.int32, (tm, tm), 1)).astype(BF16)
    prefix = _dot(tri, onehot.astype(BF16))
    carry = carry_s[...]
    rank = jnp.sum((prefix - 1.0 + carry) * onehot, axis=-1, keepdims=True)
    info = functools.reduce(lambda a, b: a + b,
                            [jnp.where(lane == i, gates[i], 0.0) for i in range(EPG)])
    info = (info + jnp.where(lane == _INFO_GID, gidx.astype(F32), 0.0)
            + jnp.where(lane == _INFO_RANK, rank, 0.0))
    info_ref[...] = info
    infot_ref[0] = info.T[:8]
    carry = carry + prefix[tm - 1:tm, :]
    carry_s[...] = carry
    cum_ref[0] = jnp.broadcast_to(carry, (8, LANES)).astype(jnp.int32)


def _dest_tile(d, cum_ref):
    n_src = cum_ref.shape[0]
    tot = [cum_ref[n_src - 1, g] for g in range(N_GROUPS)]
    ends, acc = [], 0
    for t in tot:
        acc = acc + lax.shift_right_logical(t + (MOE_DST - 1), _LOG2_DST)
        ends.append(acc)
    g = ((d >= ends[0]).astype(jnp.int32) + (d >= ends[1]).astype(jnp.int32)
         + (d >= ends[2]).astype(jnp.int32))
    first = jnp.where(g == 0, 0, jnp.where(g == 1, ends[0], jnp.where(g == 2, ends[1], ends[2])))
    tot_g = jnp.where(g == 0, tot[0], jnp.where(g == 1, tot[1], jnp.where(g == 2, tot[2], tot[3])))
    k0 = (d - first) * MOE_DST
    n_valid = jnp.clip(tot_g - k0, 0, MOE_DST)
    return g, k0, n_valid


def _tiles_overlap(cum_ref, s, g, k0, n_valid):
    hi = cum_ref[s, g]
    lo = jnp.where(s > 0, cum_ref[jnp.maximum(s - 1, 0), g], 0)
    return (n_valid > 0) & (hi > k0) & (lo < k0 + n_valid)


def _moe_ffn_kernel(cum_ref, h_ref, info_ref, infot_ref, wg_ref, wu_ref, wd_ref, y_ref, xs_s, gs_s):
    d = pl.program_id(0)
    g, k0, n_valid = _dest_tile(d, cum_ref)
    n_src = cum_ref.shape[0]

    @pl.when(n_valid == 0)
    def _():
        y_ref[...] = jnp.zeros_like(y_ref)

    @pl.when(n_valid > 0)
    def _():
        xs_s[...] = jnp.zeros_like(xs_s)
        gs_s[...] = jnp.zeros_like(gs_s)
        row = lax.broadcasted_iota(jnp.int32, (MOE_DST, 1), 0).astype(F32)
        gf = g.astype(F32)
        k0f = k0.astype(F32)

        def gather(s, carry):
            @pl.when(_tiles_overlap(cum_ref, s, g, k0, n_valid))
            def _():
                it = infot_ref[s]
                rel = jnp.where(it[_INFO_GID:_INFO_GID + 1] == gf,
                                it[_INFO_RANK:_INFO_RANK + 1] - k0f, -1.0)
                perm = (row == rel).astype(BF16)
                rows = pl.ds(pl.multiple_of(s * MOE_SRC, MOE_SRC), MOE_SRC)
                xs_s[...] += _dot(perm, h_ref[rows, :])
                inf = info_ref[rows, :]
                inf_hi = inf.astype(BF16)
                inf_lo = (inf - inf_hi.astype(F32)).astype(BF16)
                gs_s[...] += _dot(perm, inf_hi) + _dot(perm, inf_lo)
            return carry

        lax.fori_loop(0, n_src, gather, 0)
        xs = xs_s[...].astype(BF16)
        gate = gs_s[...]
        acts = []
        for i in range(EPG):
            a = _dot(xs, wg_ref[i])
            u = _dot(xs, wu_ref[i])
            acts.append(((a * _sigmoid(a)) * u * gate[:, i:i + 1]).astype(BF16))
        act = jnp.concatenate(acts, axis=-1)
        y_ref[...] = _dot(act, wd_ref[...].reshape(EPG * D_EXPERT, D_MODEL)).astype(BF16)


def _moe_unsort_kernel(cum_ref, x_ref, mod_ref, info_ref, y_ref, o_ref, acc_s):
    s = pl.program_id(0)
    n_dst = y_ref.shape[0] // MOE_DST
    inf = info_ref[...]
    gid = inf[:, _INFO_GID:_INFO_GID + 1]
    rank = inf[:, _INFO_RANK:_INFO_RANK + 1]
    col = lax.broadcasted_iota(jnp.int32, (1, MOE_DST), 1).astype(F32)
    acc_s[...] = jnp.zeros_like(acc_s)

    def scatter(d, carry):
        g, k0, n_valid = _dest_tile(d, cum_ref)

        @pl.when(_tiles_overlap(cum_ref, s, g, k0, n_valid))
        def _():
            rel = jnp.where(gid == g.astype(F32), rank - k0.astype(F32), -1.0)
            perm = (rel == col).astype(BF16)
            rows = pl.ds(pl.multiple_of(d * MOE_DST, MOE_DST), MOE_DST)
            acc_s[...] += _dot(perm, y_ref[rows, :])
        return carry

    lax.fori_loop(0, n_dst, scatter, 0)
    o_ref[...] = x_ref[...] + mod_ref[0, 5:6, :] * acc_s[...]


def _moe(x, mod, g, w_router, router_bias, wg, wu, wd, layer, rows_per_cond):
    n = x.shape[0]
    n_src = n // MOE_SRC
    n_dst = n // MOE_DST + N_GROUPS
    row = lambda i: (i, 0)
    full = lambda i: (0, 0)
    h, info, infot, cum = pl.pallas_call(
        _moe_route_kernel,
        grid=(n_src,),
        in_specs=[
            pl.BlockSpec((MOE_SRC, D_MODEL), row),
            _mod_spec(None if rows_per_cond is None else rows_per_cond // MOE_SRC),
            pl.BlockSpec((1, D_MODEL), full),
            pl.BlockSpec(w_router.shape, full),
            pl.BlockSpec(router_bias.shape, full),
        ],
        out_specs=[pl.BlockSpec((MOE_SRC, D_MODEL), row),
                   pl.BlockSpec((MOE_SRC, LANES), row),
                   pl.BlockSpec((1, 8, MOE_SRC), lambda i: (i, 0, 0)),
                   pl.BlockSpec((1, 8, LANES), lambda i: (i, 0, 0))],
        out_shape=[jax.ShapeDtypeStruct((n, D_MODEL), BF16),
                   jax.ShapeDtypeStruct((n, LANES), F32),
                   jax.ShapeDtypeStruct((n_src, 8, MOE_SRC), F32),
                   jax.ShapeDtypeStruct((n_src, 8, LANES), jnp.int32)],
        scratch_shapes=[pltpu.VMEM((1, LANES), F32)],
        compiler_params=_cparams("arbitrary"),
        name="moe_route",
    )(x, mod, g, w_router, router_bias)
    cum = cum[:, 0, :N_GROUPS]

    resident = lambda a: pl.BlockSpec(a.shape, lambda d, c: (0,) * a.ndim, pipeline_mode=pl.Buffered(1))
    expert = lambda shape: pl.BlockSpec(shape, lambda d, c: (layer, _dest_tile(d, c)[0], 0, 0))
    y = pl.pallas_call(
        _moe_ffn_kernel,
        grid_spec=pltpu.PrefetchScalarGridSpec(
            num_scalar_prefetch=1,
            grid=(n_dst,),
            in_specs=[resident(h), resident(info), resident(infot),
                      expert((None, EPG, D_MODEL, D_EXPERT)),
                      expert((None, EPG, D_MODEL, D_EXPERT)),
                      expert((None, EPG, D_EXPERT, D_MODEL))],
            out_specs=pl.BlockSpec((MOE_DST, D_MODEL), lambda d, c: (d, 0)),
            scratch_shapes=[pltpu.VMEM((MOE_DST, D_MODEL), F32), pltpu.VMEM((MOE_DST, LANES), F32)],
        ),
        out_shape=jax.ShapeDtypeStruct((n_dst * MOE_DST, D_MODEL), BF16),
        compiler_params=_cparams("arbitrary"),
        name="moe_experts",
    )(cum, h, info, infot, wg, wu, wd)

    return pl.pallas_call(
        _moe_unsort_kernel,
        grid_spec=pltpu.PrefetchScalarGridSpec(
            num_scalar_prefetch=1,
            grid=(n_src,),
            in_specs=[pl.BlockSpec((MOE_SRC, D_MODEL), lambda i, c: (i, 0)),
                      _mod_spec(None if rows_per_cond is None else rows_per_cond // MOE_SRC),
                      pl.BlockSpec((MOE_SRC, LANES), lambda i, c: (i, 0)),
                      resident(y)],
            out_specs=pl.BlockSpec((MOE_SRC, D_MODEL), lambda i, c: (i, 0)),
            scratch_shapes=[pltpu.VMEM((MOE_SRC, D_MODEL), F32)],
        ),
        out_shape=jax.ShapeDtypeStruct((n, D_MODEL), F32),
        compiler_params=_cparams("parallel"),
        name="moe_unsort",
    )(cum, x, mod, info, y)


def _final_norm_kernel(x_ref, g_ref, o_ref):
    o_ref[...] = _rms(x_ref[...], g_ref[...])


def _final_norm(x, g, tm):
    n = x.shape[0]
    return pl.pallas_call(
        _final_norm_kernel,
        grid=(n // tm,),
        in_specs=[pl.BlockSpec((tm, D_MODEL), lambda i: (i, 0)),
                  pl.BlockSpec((1, D_MODEL), lambda i: (0, 0))],
        out_specs=pl.BlockSpec((tm, D_MODEL), lambda i: (i, 0)),
        out_shape=jax.ShapeDtypeStruct((n, D_MODEL), F32),
        compiler_params=_cparams("parallel"),
        name="final_norm",
    )(x, g)


def _rope_tables(n_lat):
    t = np.arange(n_lat)
    n_freq = MLA_ROPE // 4
    inv_freq = jnp.asarray(ROPE_THETA, F32) ** (-jnp.arange(n_freq, dtype=F32) / n_freq)
    ar = jnp.asarray(t // GRID_W, F32)[:, None] * inv_freq
    ac = jnp.asarray(t % GRID_W, F32)[:, None] * inv_freq
    cos = jnp.concatenate([jnp.cos(ar), jnp.cos(ar), jnp.cos(ac), jnp.cos(ac)], axis=-1)
    sin = jnp.concatenate([-jnp.sin(ar), jnp.sin(ar), -jnp.sin(ac), jnp.sin(ac)], axis=-1)
    return (jnp.tile(cos, (1, MLA_HEADS)), jnp.tile(sin, (1, MLA_HEADS)),
            jnp.tile(cos, (1, LANES // MLA_ROPE)), jnp.tile(sin, (1, LANES // MLA_ROPE)))


_ROPE_SWAP = np.concatenate([np.arange(8, 16), np.arange(0, 8), np.arange(24, 32), np.arange(16, 24)])


def _mla_weights(w_in, w_uq, w_ukv):
    o = MLA_Q_LORA + MLA_KV_LORA
    kr = w_in[:, o:]
    rep = LANES // MLA_ROPE
    w_in_x = jnp.concatenate([w_in[:, :o], jnp.tile(kr, (1, rep)), jnp.tile(kr[:, _ROPE_SWAP], (1, rep))],
                             axis=-1).astype(BF16)
    uq = w_uq.reshape(MLA_Q_LORA, MLA_HEADS, MLA_NOPE + MLA_ROPE)
    q_nope = uq[:, :, :MLA_NOPE].reshape(MLA_Q_LORA, -1)
    q_rope = uq[:, :, MLA_NOPE:]
    w_uq_x = jnp.concatenate([q_nope, q_rope.reshape(MLA_Q_LORA, -1),
                              q_rope[:, :, _ROPE_SWAP].reshape(MLA_Q_LORA, -1)], axis=-1).astype(BF16)
    ukv = w_ukv.reshape(MLA_KV_LORA, MLA_HEADS, MLA_NOPE + MLA_V)
    w_uk = ukv[:, :, :MLA_NOPE].reshape(MLA_KV_LORA, -1).astype(BF16)
    w_uv = ukv[:, :, MLA_NOPE:].reshape(MLA_KV_LORA, -1).astype(BF16)
    return w_in_x, w_uq_x, w_uk, w_uv


def kernel(x_prompt, x_sample, cache_mla_ckv, cache_mla_krope, cache_nat_k, cache_nat_v, c, c_ctx,
           w_ada, b_ada, norm_mix, norm_ffn, norm_final, mla_w_in, mla_q_norm, mla_w_uq, mla_kv_norm,
           mla_w_ukv, mla_w_o, nat_w_qkv, nat_rpb, nat_w_o, w_router, router_bias,
           moe_w_gate, moe_w_up, moe_w_down):
    B, S, D = x_prompt.shape
    Bd, Sd, _ = x_sample.shape
    assert D == D_MODEL and Bd + 1 <= 8
    tm_c, tm_s = 512, 512

    xc = x_prompt.reshape(B * S, D)
    xs = x_sample.reshape(Bd * Sd, D)
    cond8 = jnp.concatenate([c_ctx[None, :], c, jnp.zeros((8 - 1 - Bd, D), F32)], axis=0)
    mod_all = _ada_modulation(cond8, w_ada, b_ada).reshape(DEPTH, 8, 6, D)

    rope_tabs = _rope_tables(Sd)
    wr = jnp.pad(w_router, ((0, 0), (0, LANES - N_EXPERTS)))
    rb = jnp.pad(router_bias, (0, LANES - N_EXPERTS)).reshape(1, LANES)
    wg = moe_w_gate.astype(BF16)
    wu = moe_w_up.astype(BF16)
    wd = moe_w_down.astype(BF16)

    new_ckv, new_krope, new_k, new_v = [], [], [], []
    for layer in range(DEPTH):
        mod = mod_all[layer]
        g_mix = norm_mix[layer][None, :]
        j = layer // 2
        if layer % 2 == 0:
            w_in_x, w_uq_x, w_uk, w_uv = _mla_weights(mla_w_in[j], mla_w_uq[j], mla_w_ukv[j])
            w_in_c = w_in_x[:, :MLA_Q_LORA + MLA_KV_LORA + LANES]
            w_uq_c = w_uq_x[:, :MLA_HEADS * (MLA_NOPE + MLA_ROPE)]
            qg = mla_q_norm[j][None, :]
            kvg = mla_kv_norm[j][None, :]
            w_o = mla_w_o[j].astype(BF16)
            qn, qr, ckv, kr, kr4 = _premix_mla(xc, mod, g_mix, w_in_c, qg, w_uq_c, kvg, None, None, tm_c)
            new_ckv.append(ckv.reshape(B, S, MLA_KV_LORA))
            new_krope.append(kr.reshape(B, S, MLA_ROPE))
            att_c = _mla_attention(qn, qr, ckv, kr4, None, None, w_uk, w_uv, B, S)
            qn, qr, ckv, kr, kr4 = _premix_mla(xs, mod, g_mix, w_in_x, qg, w_uq_x, kvg, rope_tabs, Sd, tm_s)
            cache_kr4 = jnp.tile(cache_mla_krope[:, j], (1, 1, LANES // MLA_ROPE))
            att_s = _mla_attention(qn, qr, ckv, kr4, cache_mla_ckv[:, j], cache_kr4, w_uk, w_uv, Bd, 256)
        else:
            w_qkv = nat_w_qkv[j].astype(BF16)
            w_o = nat_w_o[j].astype(BF16)
            q, k, v = _premix_nat(xc, mod, g_mix, w_qkv, None, tm_c, F32)
            new_k.append(k.reshape(B, S, NAT_HEADS, NAT_DH))
            new_v.append(v.reshape(B, S, NAT_HEADS, NAT_DH))
            att_c = _dense_attention(q, k, v, B)
            q, k, v = _premix_nat(xs, mod, g_mix, w_qkv, Sd, tm_s, BF16)
            bias = _nat_bias_tables(nat_rpb[j], Sd // GRID_W)
            att_s = _nat_attention(q, k, v,
                                   cache_nat_k[:, j].reshape(Bd, -1, D), cache_nat_v[:, j].reshape(Bd, -1, D),
                                   bias, Bd)
        xc = _proj_residual(xc, att_c, w_o, mod, None, tm_c)
        xs = _proj_residual(xs, att_s, w_o, mod, Sd, tm_s)
        g_ffn = norm_ffn[layer][None, :]
        xc = _moe(xc, mod, g_ffn, wr, rb, wg, wu, wd, layer, None)
        xs = _moe(xs, mod, g_ffn, wr, rb, wg, wu, wd, layer, Sd)

    gf = norm_final[None, :]
    y_prompt = _final_norm(xc, gf, tm_c).reshape(B, S, D)
    y_sample = _final_norm(xs, gf, tm_s).reshape(Bd, Sd, D)
    return (y_prompt, y_sample, jnp.stack(new_ckv, axis=1), jnp.stack(new_krope, axis=1),
            jnp.stack(new_k, axis=1), jnp.stack(new_v, axis=1))
```

```python
import functools

import numpy as np
import jax
import jax.numpy as jnp
from jax import lax
from jax.experimental import pallas as pl
from jax.experimental.pallas import tpu as pltpu

F32 = jnp.float32
BF16 = jnp.bfloat16

D_MODEL = 1024
DEPTH = 4
GRID_W = 64
LANES = 128
MLA_HEADS = 16
MLA_NOPE = 64
MLA_ROPE = 32
MLA_V = 64
MLA_Q_LORA = 384
MLA_KV_LORA = 256
MLA_SCALE = (MLA_NOPE + MLA_ROPE) ** -0.5
ROPE_THETA = 10000.0
NAT_HEADS = 16
NAT_DH = 64
NAT_SCALE = NAT_DH ** -0.5
WIN_H = 8
WIN_W = 16
NAT_QROWS = 4
NAT_KROWS = NAT_QROWS + WIN_H
N_EXPERTS = 16
N_GROUPS = 4
EPG = N_EXPERTS // N_GROUPS
D_EXPERT = 256
NORM_EPS = 1e-6
NEG_INF = -1e30

HEAD_PAIRS = MLA_HEADS // 2
VMEM_LIMIT = 56 * 1024 * 1024


def _cparams(*sem):
    return pltpu.CompilerParams(dimension_semantics=sem, vmem_limit_bytes=VMEM_LIMIT)


def _sigmoid(x):
    return 1.0 / (1.0 + jnp.exp(-x))


def _rms(x, g):
    ms = jnp.mean(x * x, axis=-1, keepdims=True)
    return x * lax.rsqrt(ms + NORM_EPS) * g


def _dot(a, b):
    return jnp.dot(a, b, preferred_element_type=F32)


def _dot_nt(a, b):
    return lax.dot_general(a, b, (((1,), (1,)), ((), ())), preferred_element_type=F32)


def _lane_mask(width, idx, dtype):
    lane = lax.broadcasted_iota(jnp.int32, (1, LANES), 1)
    return ((lane >= idx * width) & (lane < (idx + 1) * width)).astype(dtype)


def _ada_kernel(cond_ref, w_ref, b_ref, o_ref):
    c = cond_ref[...]
    s = (c * _sigmoid(c)).astype(BF16)
    o_ref[0] = _dot(s, w_ref[0].astype(BF16)) + b_ref[0]


def _ada_modulation(cond8, w_ada, b_ada):
    n_chunk = 6
    return pl.pallas_call(
        _ada_kernel,
        grid=(DEPTH, n_chunk),
        in_specs=[
            pl.BlockSpec((8, D_MODEL), lambda l, j: (0, 0)),
            pl.BlockSpec((1, D_MODEL, D_MODEL), lambda l, j: (l, 0, j)),
            pl.BlockSpec((1, 1, D_MODEL), lambda l, j: (l, 0, j)),
        ],
        out_specs=pl.BlockSpec((1, 8, D_MODEL), lambda l, j: (l, 0, j)),
        out_shape=jax.ShapeDtypeStruct((DEPTH, 8, 6 * D_MODEL), F32),
        compiler_params=_cparams("parallel", "parallel"),
        name="ada_modulation",
    )(cond8, w_ada, b_ada.reshape(DEPTH, 1, 6 * D_MODEL))


def _mod_spec(rows_per_cond):
    if rows_per_cond is None:
        return pl.BlockSpec((1, 6, D_MODEL), lambda i, *_: (0, 0, 0))
    return pl.BlockSpec((1, 6, D_MODEL), lambda i, *_: (1 + i // rows_per_cond, 0, 0))


def _premix_mla_kernel(*refs, rope):
    if rope:
        (x_ref, mod_ref, g_ref, w_in_ref, qg_ref, w_uq_ref, kvg_ref,
         cosq_ref, sinq_ref, cosk_ref, sink_ref,
         qn_ref, qr_ref, ckv_ref, kr_ref, kr4_ref) = refs
    else:
        (x_ref, mod_ref, g_ref, w_in_ref, qg_ref, w_uq_ref, kvg_ref,
         qn_ref, qr_ref, ckv_ref, kr_ref, kr4_ref) = refs
    h = _rms(x_ref[...], g_ref[...]) * (1.0 + mod_ref[0, 1:2, :]) + mod_ref[0, 0:1, :]
    lat = _dot(h.astype(BF16), w_in_ref[...])
    c_q = lat[:, :MLA_Q_LORA]
    c_kv = lat[:, MLA_Q_LORA:MLA_Q_LORA + MLA_KV_LORA]
    o = MLA_Q_LORA + MLA_KV_LORA
    kr4 = lat[:, o:o + LANES]
    q = _dot(_rms(c_q, qg_ref[...]).astype(BF16), w_uq_ref[...])
    n_nope = MLA_HEADS * MLA_NOPE
    n_rope = MLA_HEADS * MLA_ROPE
    qr = q[:, n_nope:n_nope + n_rope]
    if rope:
        qr = qr * cosq_ref[...] + q[:, n_nope + n_rope:] * sinq_ref[...]
        kr4 = kr4 * cosk_ref[...] + lat[:, o + LANES:o + 2 * LANES] * sink_ref[...]
    qn_ref[...] = q[:, :n_nope].astype(BF16)
    qr_ref[...] = qr.astype(BF16)
    ckv_ref[...] = _rms(c_kv, kvg_ref[...])
    kr_ref[...] = kr4[:, :MLA_ROPE]
    kr4_ref[...] = kr4.astype(BF16)


def _premix_mla(x, mod, g, w_in, qg, w_uq, kvg, rope_tabs, rows_per_cond, tm):
    n = x.shape[0]
    rope = rope_tabs is not None
    row = lambda i: (i, 0)
    full = lambda i: (0, 0)
    in_specs = [
        pl.BlockSpec((tm, D_MODEL), row),
        _mod_spec(None if rows_per_cond is None else rows_per_cond // tm),
        pl.BlockSpec((1, D_MODEL), full),
        pl.BlockSpec(w_in.shape, full),
        pl.BlockSpec((1, MLA_Q_LORA), full),
        pl.BlockSpec(w_uq.shape, full),
        pl.BlockSpec((1, MLA_KV_LORA), full),
    ]
    args = [x, mod, g, w_in, qg, w_uq, kvg]
    if rope:
        nblk = rope_tabs[0].shape[0] // tm
        pos = lambda i: (i % nblk, 0)
        for t in rope_tabs:
            in_specs.append(pl.BlockSpec((tm, t.shape[1]), pos))
            args.append(t)
    widths = (MLA_HEADS * MLA_NOPE, MLA_HEADS * MLA_ROPE, MLA_KV_LORA, MLA_ROPE, LANES)
    dtypes = (BF16, BF16, F32, F32, BF16)
    return pl.pallas_call(
        functools.partial(_premix_mla_kernel, rope=rope),
        grid=(n // tm,),
        in_specs=in_specs,
        out_specs=[pl.BlockSpec((tm, w), row) for w in widths],
        out_shape=[jax.ShapeDtypeStruct((n, w), d) for w, d in zip(widths, dtypes)],
        compiler_params=_cparams("parallel"),
        name="premix_mla_rope" if rope else "premix_mla",
    )(*args)


def _softmax_pv(s_list, v_list):
    m = functools.reduce(jnp.maximum, [jnp.max(s, axis=-1, keepdims=True) for s in s_list])
    e_list = [jnp.exp(s - m) for s in s_list]
    l = functools.reduce(lambda a, b: a + b, [jnp.sum(e, axis=-1, keepdims=True) for e in e_list])
    o = functools.reduce(lambda a, b: a + b,
                         [_dot(e.astype(BF16), v) for e, v in zip(e_list, v_list)])
    return o / l


def _mla_attn_kernel(*refs, cached):
    if cached:
        (qn_ref, qr_ref, ckv_ref, kr4_ref, cckv_ref, ckr4_ref, w_uk_ref, w_uv_ref,
         o_ref, kn_s, v_s, kr_s) = refs
    else:
        (qn_ref, qr_ref, ckv_ref, kr4_ref, w_uk_ref, w_uv_ref, o_ref, kn_s, v_s, kr_s) = refs
    t_own = ckv_ref.shape[0]

    @pl.when(pl.program_id(1) == 0)
    def _():
        c = ckv_ref[...].astype(BF16)
        kn_s[0:t_own, :] = _dot(c, w_uk_ref[...]).astype(BF16)
        v_s[0:t_own, :] = _dot(c, w_uv_ref[...]).astype(BF16)
        kr_s[0:t_own, :] = kr4_ref[...]
        if cached:
            cc = cckv_ref[0].astype(BF16)
            kn_s[t_own:, :] = _dot(cc, w_uk_ref[...]).astype(BF16)
            v_s[t_own:, :] = _dot(cc, w_uv_ref[...]).astype(BF16)
            kr_s[t_own:, :] = ckr4_ref[0].astype(BF16)

    kr4 = kr_s[...]
    for p in range(HEAD_PAIRS):
        sl = slice(p * LANES, (p + 1) * LANES)
        qn = qn_ref[:, sl]
        qr = qr_ref[:, (p // 2) * LANES:(p // 2 + 1) * LANES]
        k_cat = jnp.concatenate([kn_s[:, sl], kr4], axis=-1)
        v = v_s[:, sl]
        o = None
        for i in range(2):
            h = 2 * p + i
            m64 = _lane_mask(MLA_NOPE, i, BF16)
            q_cat = jnp.concatenate([qn * m64, qr * _lane_mask(MLA_ROPE, h % 4, BF16)], axis=-1)
            s = _dot_nt(q_cat, k_cat) * MLA_SCALE
            oh = _softmax_pv([s], [v * m64])
            o = oh if o is None else o + oh
        o_ref[:, sl] = o.astype(BF16)


def _mla_attention(qn, qr, ckv, kr4, cache_ckv, cache_kr4, w_uk, w_uv, n_batch, tq):
    n = qn.shape[0]
    s_own = n // n_batch
    nq = s_own // tq
    cached = cache_ckv is not None
    t_all = s_own + (cache_ckv.shape[1] if cached else 0)
    qrow = lambda b, j: (b * nq + j, 0)
    own = lambda b, j: (b, 0)
    full = lambda b, j: (0, 0)
    in_specs = [
        pl.BlockSpec((tq, qn.shape[1]), qrow),
        pl.BlockSpec((tq, qr.shape[1]), qrow),
        pl.BlockSpec((s_own, MLA_KV_LORA), own),
        pl.BlockSpec((s_own, LANES), own),
    ]
    args = [qn, qr, ckv, kr4]
    if cached:
        in_specs += [pl.BlockSpec((1,) + cache_ckv.shape[1:], lambda b, j: (b, 0, 0)),
                     pl.BlockSpec((1,) + cache_kr4.shape[1:], lambda b, j: (b, 0, 0))]
        args += [cache_ckv, cache_kr4]
    in_specs += [pl.BlockSpec(w_uk.shape, full), pl.BlockSpec(w_uv.shape, full)]
    args += [w_uk, w_uv]
    return pl.pallas_call(
        functools.partial(_mla_attn_kernel, cached=cached),
        grid=(n_batch, nq),
        in_specs=in_specs,
        out_specs=pl.BlockSpec((tq, D_MODEL), qrow),
        out_shape=jax.ShapeDtypeStruct((n, D_MODEL), BF16),
        scratch_shapes=[pltpu.VMEM((t_all, D_MODEL), BF16), pltpu.VMEM((t_all, D_MODEL), BF16),
                        pltpu.VMEM((t_all, LANES), BF16)],
        compiler_params=_cparams("parallel", "arbitrary"),
        name="mla_attention_cached" if cached else "mla_attention",
    )(*args)


def _premix_nat_kernel(x_ref, mod_ref, g_ref, w_ref, q_ref, k_ref, v_ref):
    h = _rms(x_ref[...], g_ref[...]) * (1.0 + mod_ref[0, 1:2, :]) + mod_ref[0, 0:1, :]
    qkv = _dot(h.astype(BF16), w_ref[...])
    q_ref[...] = qkv[:, :D_MODEL].astype(q_ref.dtype)
    k_ref[...] = qkv[:, D_MODEL:2 * D_MODEL].astype(k_ref.dtype)
    v_ref[...] = qkv[:, 2 * D_MODEL:].astype(v_ref.dtype)


def _premix_nat(x, mod, g, w_qkv, rows_per_cond, tm, kv_dtype):
    n = x.shape[0]
    row = lambda i: (i, 0)
    full = lambda i: (0, 0)
    return pl.pallas_call(
        _premix_nat_kernel,
        grid=(n // tm,),
        in_specs=[
            pl.BlockSpec((tm, D_MODEL), row),
            _mod_spec(None if rows_per_cond is None else rows_per_cond // tm),
            pl.BlockSpec((1, D_MODEL), full),
            pl.BlockSpec(w_qkv.shape, full),
        ],
        out_specs=[pl.BlockSpec((tm, D_MODEL), row)] * 3,
        out_shape=[jax.ShapeDtypeStruct((n, D_MODEL), BF16),
                   jax.ShapeDtypeStruct((n, D_MODEL), kv_dtype),
                   jax.ShapeDtypeStruct((n, D_MODEL), kv_dtype)],
        compiler_params=_cparams("parallel"),
        name="premix_nat",
    )(x, mod, g, w_qkv)


def _dense_attn_kernel(q_ref, k_ref, v_ref, o_ref):
    for p in range(HEAD_PAIRS):
        sl = slice(p * LANES, (p + 1) * LANES)
        q = q_ref[:, sl]
        k = k_ref[:, sl].astype(BF16)
        v = v_ref[:, sl].astype(BF16)
        o = None
        for i in range(2):
            m64 = _lane_mask(NAT_DH, i, BF16)
            s = _dot_nt(q * m64, k) * NAT_SCALE
            oh = _softmax_pv([s], [v * m64])
            o = oh if o is None else o + oh
        o_ref[:, sl] = o.astype(BF16)


def _dense_attention(q, k, v, n_batch):
    n = q.shape[0]
    s = n // n_batch
    blk = pl.BlockSpec((s, D_MODEL), lambda b: (b, 0))
    return pl.pallas_call(
        _dense_attn_kernel,
        grid=(n_batch,),
        in_specs=[blk, blk, blk],
        out_specs=blk,
        out_shape=jax.ShapeDtypeStruct((n, D_MODEL), BF16),
        compiler_params=_cparams("parallel"),
        name="dense_attention",
    )(q, k, v)


_NAT_QBLK = NAT_QROWS * GRID_W
_NAT_KBLK = NAT_KROWS * GRID_W


def _nat_block_plan(rows):
    assert rows % NAT_QROWS == 0 and rows >= NAT_KROWS
    plan, variants = [], []
    for r0 in range(0, rows, NAT_QROWS):
        ks = min(max(r0 - WIN_H // 2, 0), rows - NAT_KROWS)
        r = r0 + np.arange(NAT_QROWS)
        kr = ks + np.arange(NAT_KROWS)
        rs = np.clip(r - WIN_H // 2, 0, rows - WIN_H)
        valid_row = (kr[None, :] >= rs[:, None]) & (kr[None, :] < rs[:, None] + WIN_H)
        d0 = ks - r + (WIN_H - 1)
        for vi, (d0_v, valid_v) in enumerate(variants):
            if np.array_equal(d0, d0_v) and np.array_equal(valid_row, valid_v):
                break
        else:
            vi = len(variants)
            variants.append((d0, valid_row))
        plan.append((ks, vi))
    return plan, variants


def _nat_bias_tables(rpb, rows):
    n_heads, n_dr, n_dc = rpb.shape
    _, variants = _nat_block_plan(rows)
    c = np.arange(GRID_W)
    cs = np.clip(c - WIN_W // 2, 0, GRID_W - WIN_W)
    valid_col = (c[None, :] >= cs[:, None]) & (c[None, :] < cs[:, None] + WIN_W)
    d_col = c[None, :] - c[:, None] + (WIN_W - 1)
    sel = (d_col[None] == np.arange(n_dc)[:, None, None]) & valid_col[None]
    toep = jnp.einsum('hdj,jck->hdck', rpb, jnp.asarray(sel, F32), precision=lax.Precision.HIGHEST)
    toep = jnp.where(jnp.asarray(valid_col)[None, None], toep, NEG_INF)
    pad = NAT_KROWS + NAT_QROWS
    toep = jnp.pad(toep, ((0, 0), (pad, pad), (0, 0), (0, 0)), constant_values=NEG_INF)
    tabs = []
    for d0, valid_row in variants:
        rows_b = jnp.stack([toep[:, pad + d:pad + d + NAT_KROWS] for d in d0], axis=1)
        rows_b = jnp.where(jnp.asarray(valid_row)[None, :, :, None, None], rows_b, NEG_INF)
        tabs.append(rows_b.transpose(0, 1, 3, 2, 4).reshape(n_heads, _NAT_QBLK, _NAT_KBLK))
    return jnp.stack(tabs, axis=0)


def _nat_attn_kernel(q_ref, k_ref, v_ref, kc_ref, vc_ref, bias_ref, o_ref, *, plan):
    kc = kc_ref[0].astype(BF16)
    vc = vc_ref[0].astype(BF16)
    masks = [_lane_mask(NAT_DH, i, BF16) for i in range(2)]
    vcm = [vc * m for m in masks]
    for bi, (ks, var) in enumerate(plan):
        q = q_ref[bi * _NAT_QBLK:(bi + 1) * _NAT_QBLK, :]
        k = k_ref[ks * GRID_W:ks * GRID_W + _NAT_KBLK, :]
        v = v_ref[ks * GRID_W:ks * GRID_W + _NAT_KBLK, :]
        o = None
        for i in range(2):
            qm = q * masks[i]
            s_loc = _dot_nt(qm, k) * NAT_SCALE + bias_ref[var, i]
            s_ctx = _dot_nt(qm, kc) * NAT_SCALE
            oh = _softmax_pv([s_loc, s_ctx], [v * masks[i], vcm[i]])
            o = oh if o is None else o + oh
        o_ref[bi * _NAT_QBLK:(bi + 1) * _NAT_QBLK, :] = o.astype(BF16)


def _nat_attention(q, k, v, cache_k, cache_v, bias, n_batch):
    n = q.shape[0]
    s = n // n_batch
    plan, variants = _nat_block_plan(s // GRID_W)
    own = pl.BlockSpec((s, LANES), lambda p, b: (b, p))
    cache = pl.BlockSpec((1, cache_k.shape[1], LANES), lambda p, b: (b, 0, p))
    return pl.pallas_call(
        functools.partial(_nat_attn_kernel, plan=plan),
        grid=(HEAD_PAIRS, n_batch),
        in_specs=[own, own, own, cache, cache,
                  pl.BlockSpec((len(variants), 2, _NAT_QBLK, _NAT_KBLK), lambda p, b: (0, p, 0, 0))],
        out_specs=own,
        out_shape=jax.ShapeDtypeStruct((n, D_MODEL), BF16),
        compiler_params=_cparams("parallel", "parallel"),
        name="nat_attention",
    )(q, k, v, cache_k, cache_v, bias)


def _proj_res_kernel(x_ref, a_ref, w_ref, mod_ref, o_ref):
    o_ref[...] = x_ref[...] + mod_ref[0, 2:3, :] * _dot(a_ref[...], w_ref[...])


def _proj_residual(x, a, w, mod, rows_per_cond, tm):
    n = x.shape[0]
    row = lambda i: (i, 0)
    return pl.pallas_call(
        _proj_res_kernel,
        grid=(n // tm,),
        in_specs=[
            pl.BlockSpec((tm, D_MODEL), row),
            pl.BlockSpec((tm, D_MODEL), row),
            pl.BlockSpec(w.shape, lambda i: (0, 0)),
            _mod_spec(None if rows_per_cond is None else rows_per_cond // tm),
        ],
        out_specs=pl.BlockSpec((tm, D_MODEL), row),
        out_shape=jax.ShapeDtypeStruct((n, D_MODEL), F32),
        compiler_params=_cparams("parallel"),
        name="proj_residual",
    )(x, a, w, mod)


def _top2_sum(a, b, c, d):
    hi1, lo1 = jnp.maximum(a, b), jnp.minimum(a, b)
    hi2, lo2 = jnp.maximum(c, d), jnp.minimum(c, d)
    return jnp.maximum(hi1, hi2) + jnp.maximum(jnp.minimum(hi1, hi2), jnp.maximum(lo1, lo2))


def _route(scores, biased):
    sc = [scores[e:e + 1, :] for e in range(N_EXPERTS)]
    bs = [biased[e:e + 1, :] for e in range(N_EXPERTS)]
    gscore = [_top2_sum(*bs[EPG * g:EPG * (g + 1)]) for g in range(N_GROUPS)]
    best, gidx = gscore[0], jnp.zeros_like(gscore[0], dtype=jnp.int32)
    for g in range(1, N_GROUPS):
        better = gscore[g] > best
        gidx = jnp.where(better, g, gidx)
        best = jnp.where(better, gscore[g], best)
    cb = [functools.reduce(lambda a, b: a + b,
                           [jnp.where(gidx == g, bs[EPG * g + i], 0.0) for g in range(N_GROUPS)])
          for i in range(EPG)]
    cs = [functools.reduce(lambda a, b: a + b,
                           [jnp.where(gidx == g, sc[EPG * g + i], 0.0) for g in range(N_GROUPS)])
          for i in range(EPG)]
    b1, i1 = cb[0], jnp.zeros_like(gidx)
    for i in range(1, EPG):
        better = cb[i] > b1
        i1 = jnp.where(better, i, i1)
        b1 = jnp.where(better, cb[i], b1)
    b2, i2 = jnp.full_like(b1, -jnp.inf), jnp.full_like(i1, -1)
    for i in range(EPG):
        better = (i1 != i) & (cb[i] > b2)
        i2 = jnp.where(better, i, i2)
        b2 = jnp.where(better, cb[i], b2)
    sel = [(i1 == i) | (i2 == i) for i in range(EPG)]
    w = [jnp.where(sel[i], cs[i], 0.0) for i in range(EPG)]
    tot = w[0] + w[1] + w[2] + w[3]
    return gidx, [w[i] / tot for i in range(EPG)]


MOE_SRC = 256
MOE_DST = 256
MOE_CHUNK = 4
_LOG2_DST = MOE_DST.bit_length() - 1
assert MOE_DST == 1 << _LOG2_DST and MOE_SRC <= MOE_DST
_INFO_GID = EPG
_INFO_RANK = EPG + 1


def _moe_route_kernel(x_ref, mod_ref, g_ref, wr_ref, rb_ref, h_ref, info_ref, infot_ref, cum_ref, carry_s):
    @pl.when(pl.program_id(0) == 0)
    def _():
        carry_s[...] = jnp.zeros_like(carry_s)

    h = _rms(x_ref[...], g_ref[...]) * (1.0 + mod_ref[0, 4:5, :]) + mod_ref[0, 3:4, :]
    h_hi = h.astype(BF16)
    h_ref[...] = h_hi
    h_lo = (h - h_hi.astype(F32)).astype(BF16)
    logits = _dot(h_hi, wr_ref[0]) + (_dot(h_lo, wr_ref[0]) + _dot(h_hi, wr_ref[1]))
    scores = _sigmoid(logits)
    gidx, gates = _route(scores.T[:N_EXPERTS], (scores + rb_ref[...]).T[:N_EXPERTS])
    tm = h.shape[0]
    sub = lax.broadcasted_iota(jnp.int32, (8, 1), 0)
    onehot = (sub == gidx).astype(F32)
    tri = (lax.broadcasted_iota(jnp.int32, (tm, tm), 0)
           <= lax.broadcasted_iota(jnp.int32, (tm, tm), 1)).astype(BF16)
    prefix = _dot(onehot.astype(BF16), tri)
    carry = carry_s[:, 0:1]
    rank = jnp.sum((prefix - 1.0 + carry) * onehot, axis=0, keepdims=True)
    infot = jnp.concatenate(gates + [gidx.astype(F32), rank, jnp.zeros((2, tm), F32)], axis=0)
    infot_ref[0] = infot
    info_ref[...] = jnp.concatenate([infot, jnp.zeros((LANES - 8, tm), F32)], axis=0).T
    carry = jnp.broadcast_to(carry + prefix[:, tm - 1:tm], (8, LANES))
    carry_s[...] = carry
    cum_ref[0] = carry.astype(jnp.int32)


def _group_tiles(cum_ref):
    n_src = cum_ref.shape[0]
    tot = [cum_ref[n_src - 1, g] for g in range(N_GROUPS)]
    ends, acc = [], 0
    for t in tot:
        acc = acc + lax.shift_right_logical(t + (MOE_DST - 1), _LOG2_DST)
        ends.append(acc)
    return tot, ends


def _dest_tile(d, cum_ref):
    tot, ends = _group_tiles(cum_ref)
    g = ((d >= ends[0]).astype(jnp.int32) + (d >= ends[1]).astype(jnp.int32)
         + (d >= ends[2]).astype(jnp.int32))
    first = jnp.where(g == 0, 0, jnp.where(g == 1, ends[0], jnp.where(g == 2, ends[1], ends[2])))
    tot_g = jnp.where(g == 0, tot[0], jnp.where(g == 1, tot[1], jnp.where(g == 2, tot[2], tot[3])))
    k0 = (d - first) * MOE_DST
    n_valid = jnp.clip(tot_g - k0, 0, MOE_DST)
    return g, k0, n_valid


def _counts_before(cum_ref, s, g):
    return jnp.where(s > 0, cum_ref[jnp.maximum(s - 1, 0), g], 0)


def _moe_ffn_kernel(cum_ref, h_ref, info_ref, infot_ref, wg_ref, wu_ref, wd_ref, y_ref, xs_s, gs_s, ptr_s):
    d = pl.program_id(0)
    g, k0, n_valid = _dest_tile(d, cum_ref)
    n_src = cum_ref.shape[0]

    @pl.when(n_valid == 0)
    def _():
        y_ref[...] = jnp.zeros_like(y_ref)

    @pl.when(n_valid > 0)
    def _():
        need = k0 + n_valid
        start = jnp.where(k0 == 0, 0, ptr_s[0])
        s_lo = lax.while_loop(lambda s: cum_ref[s, g] <= k0, lambda s: s + 1, start)
        ptr_s[0] = s_lo
        xs_s[...] = jnp.zeros_like(xs_s)
        gs_s[...] = jnp.zeros_like(gs_s)
        row = lax.broadcasted_iota(jnp.int32, (MOE_DST, 1), 0).astype(F32)
        gf = g.astype(F32)
        k0f = k0.astype(F32)

        def more(s0):
            return (_counts_before(cum_ref, s0, g) < need) & (s0 < n_src)

        def gather(s0):
            base = jnp.minimum(s0, n_src - MOE_CHUNK)
            perms = []
            for j in range(MOE_CHUNK):
                it = infot_ref[base + j]
                shift = k0f + jnp.where(base + j >= s0, 0.0, 1e9)
                rel = jnp.where(it[_INFO_GID:_INFO_GID + 1] == gf,
                                it[_INFO_RANK:_INFO_RANK + 1] - shift, -1.0)
                perms.append((row == rel).astype(BF16))
            perm = jnp.concatenate(perms, axis=-1)
            rows = pl.ds(pl.multiple_of(base * MOE_SRC, MOE_SRC), MOE_CHUNK * MOE_SRC)
            xs_s[...] += _dot(perm, h_ref[rows, :])
            inf = info_ref[rows, :]
            inf_hi = inf.astype(BF16)
            inf_lo = (inf - inf_hi.astype(F32)).astype(BF16)
            gs_s[...] += _dot(perm, inf_hi) + _dot(perm, inf_lo)
            return base + MOE_CHUNK

        lax.while_loop(more, gather, s_lo)
        xs = xs_s[...].astype(BF16)
        gate = gs_s[...]
        acts = []
        for i in range(EPG):
            a = _dot(xs, wg_ref[i])
            u = _dot(xs, wu_ref[i])
            acts.append(((a * _sigmoid(a)) * u * gate[:, i:i + 1]).astype(BF16))
        act = jnp.concatenate(acts, axis=-1)
        y_ref[...] = _dot(act, wd_ref[...].reshape(EPG * D_EXPERT, D_MODEL)).astype(BF16)


def _moe_unsort_kernel(cum_ref, x_ref, mod_ref, info_ref, y_ref, o_ref):
    s = pl.program_id(0)
    n_dst = y_ref.shape[0] // MOE_DST
    _, ends = _group_tiles(cum_ref)
    firsts = [0] + ends[:-1]
    inf = info_ref[...]
    gid = inf[:, _INFO_GID:_INFO_GID + 1]
    rank = inf[:, _INFO_RANK:_INFO_RANK + 1]
    col = lax.broadcasted_iota(jnp.int32, (1, MOE_DST), 1).astype(F32)
    acc = None
    for g in range(N_GROUPS):
        k_lo = lax.shift_right_logical(_counts_before(cum_ref, s, g), _LOG2_DST)
        for j in range(2):
            k = k_lo + j
            dt = jnp.minimum(firsts[g] + k, n_dst - 1)
            rel = jnp.where(gid == float(g), rank - (k * MOE_DST).astype(F32), -1.0)
            perm = (rel == col).astype(BF16)
            rows = pl.ds(pl.multiple_of(dt * MOE_DST, MOE_DST), MOE_DST)
            part = _dot(perm, y_ref[rows, :])
            acc = part if acc is None else acc + part
    o_ref[...] = x_ref[...] + mod_ref[0, 5:6, :] * acc


def _moe(x, mod, g, w_router, router_bias, wg, wu, wd, layer, rows_per_cond):
    n = x.shape[0]
    n_src = n // MOE_SRC
    n_dst = n // MOE_DST + N_GROUPS
    row = lambda i: (i, 0)
    full = lambda i: (0, 0)
    h, info, infot, cum = pl.pallas_call(
        _moe_route_kernel,
        grid=(n_src,),
        in_specs=[
            pl.BlockSpec((MOE_SRC, D_MODEL), row),
            _mod_spec(None if rows_per_cond is None else rows_per_cond // MOE_SRC),
            pl.BlockSpec((1, D_MODEL), full),
            pl.BlockSpec(w_router.shape, lambda i: (0, 0, 0)),
            pl.BlockSpec(router_bias.shape, full),
        ],
        out_specs=[pl.BlockSpec((MOE_SRC, D_MODEL), row),
                   pl.BlockSpec((MOE_SRC, LANES), row),
                   pl.BlockSpec((1, 8, MOE_SRC), lambda i: (i, 0, 0)),
                   pl.BlockSpec((1, 8, LANES), lambda i: (i, 0, 0))],
        out_shape=[jax.ShapeDtypeStruct((n, D_MODEL), BF16),
                   jax.ShapeDtypeStruct((n, LANES), F32),
                   jax.ShapeDtypeStruct((n_src, 8, MOE_SRC), F32),
                   jax.ShapeDtypeStruct((n_src, 8, LANES), jnp.int32)],
        scratch_shapes=[pltpu.VMEM((8, LANES), F32)],
        compiler_params=_cparams("arbitrary"),
        name="moe_route",
    )(x, mod, g, w_router, router_bias)
    cum = cum[:, :N_GROUPS, 0]

    resident = lambda a: pl.BlockSpec(a.shape, lambda d, c: (0,) * a.ndim, pipeline_mode=pl.Buffered(1))
    expert = lambda shape: pl.BlockSpec(shape, lambda d, c: (layer, _dest_tile(d, c)[0], 0, 0))
    y = pl.pallas_call(
        _moe_ffn_kernel,
        grid_spec=pltpu.PrefetchScalarGridSpec(
            num_scalar_prefetch=1,
            grid=(n_dst,),
            in_specs=[resident(h), resident(info), resident(infot),
                      expert((None, EPG, D_MODEL, D_EXPERT)),
                      expert((None, EPG, D_MODEL, D_EXPERT)),
                      expert((None, EPG, D_EXPERT, D_MODEL))],
            out_specs=pl.BlockSpec((MOE_DST, D_MODEL), lambda d, c: (d, 0)),
            scratch_shapes=[pltpu.VMEM((MOE_DST, D_MODEL), F32), pltpu.VMEM((MOE_DST, LANES), F32),
                            pltpu.SMEM((1,), jnp.int32)],
        ),
        out_shape=jax.ShapeDtypeStruct((n_dst * MOE_DST, D_MODEL), BF16),
        compiler_params=_cparams("arbitrary"),
        name="moe_experts",
    )(cum, h, info, infot, wg, wu, wd)

    return pl.pallas_call(
        _moe_unsort_kernel,
        grid_spec=pltpu.PrefetchScalarGridSpec(
            num_scalar_prefetch=1,
            grid=(n_src,),
            in_specs=[pl.BlockSpec((MOE_SRC, D_MODEL), lambda i, c: (i, 0)),
                      _mod_spec(None if rows_per_cond is None else rows_per_cond // MOE_SRC),
                      pl.BlockSpec((MOE_SRC, LANES), lambda i, c: (i, 0)),
                      resident(y)],
            out_specs=pl.BlockSpec((MOE_SRC, D_MODEL), lambda i, c: (i, 0)),
        ),
        out_shape=jax.ShapeDtypeStruct((n, D_MODEL), F32),
        compiler_params=_cparams("parallel"),
        name="moe_unsort",
    )(cum, x, mod, info, y)


def _final_norm_kernel(x_ref, g_ref, o_ref):
    o_ref[...] = _rms(x_ref[...], g_ref[...])


def _final_norm(x, g, tm):
    n = x.shape[0]
    return pl.pallas_call(
        _final_norm_kernel,
        grid=(n // tm,),
        in_specs=[pl.BlockSpec((tm, D_MODEL), lambda i: (i, 0)),
                  pl.BlockSpec((1, D_MODEL), lambda i: (0, 0))],
        out_specs=pl.BlockSpec((tm, D_MODEL), lambda i: (i, 0)),
        out_shape=jax.ShapeDtypeStruct((n, D_MODEL), F32),
        compiler_params=_cparams("parallel"),
        name="final_norm",
    )(x, g)


def _rope_tables(n_lat):
    t = np.arange(n_lat)
    n_freq = MLA_ROPE // 4
    inv_freq = jnp.asarray(ROPE_THETA, F32) ** (-jnp.arange(n_freq, dtype=F32) / n_freq)
    ar = jnp.asarray(t // GRID_W, F32)[:, None] * inv_freq
    ac = jnp.asarray(t % GRID_W, F32)[:, None] * inv_freq
    cos = jnp.concatenate([jnp.cos(ar), jnp.cos(ar), jnp.cos(ac), jnp.cos(ac)], axis=-1)
    sin = jnp.concatenate([-jnp.sin(ar), jnp.sin(ar), -jnp.sin(ac), jnp.sin(ac)], axis=-1)
    return (jnp.tile(cos, (1, MLA_HEADS)), jnp.tile(sin, (1, MLA_HEADS)),
            jnp.tile(cos, (1, LANES // MLA_ROPE)), jnp.tile(sin, (1, LANES // MLA_ROPE)))


_ROPE_SWAP = np.concatenate([np.arange(8, 16), np.arange(0, 8), np.arange(24, 32), np.arange(16, 24)])


def _mla_weights(w_in, w_uq, w_ukv):
    o = MLA_Q_LORA + MLA_KV_LORA
    kr = w_in[:, o:]
    rep = LANES // MLA_ROPE
    w_in_x = jnp.concatenate([w_in[:, :o], jnp.tile(kr, (1, rep)), jnp.tile(kr[:, _ROPE_SWAP], (1, rep))],
                             axis=-1).astype(BF16)
    uq = w_uq.reshape(MLA_Q_LORA, MLA_HEADS, MLA_NOPE + MLA_ROPE)
    q_nope = uq[:, :, :MLA_NOPE].reshape(MLA_Q_LORA, -1)
    q_rope = uq[:, :, MLA_NOPE:]
    w_uq_x = jnp.concatenate([q_nope, q_rope.reshape(MLA_Q_LORA, -1),
                              q_rope[:, :, _ROPE_SWAP].reshape(MLA_Q_LORA, -1)], axis=-1).astype(BF16)
    ukv = w_ukv.reshape(MLA_KV_LORA, MLA_HEADS, MLA_NOPE + MLA_V)
    w_uk = ukv[:, :, :MLA_NOPE].reshape(MLA_KV_LORA, -1).astype(BF16)
    w_uv = ukv[:, :, MLA_NOPE:].reshape(MLA_KV_LORA, -1).astype(BF16)
    return w_in_x, w_uq_x, w_uk, w_uv


def kernel(x_prompt, x_sample, cache_mla_ckv, cache_mla_krope, cache_nat_k, cache_nat_v, c, c_ctx,
           w_ada, b_ada, norm_mix, norm_ffn, norm_final, mla_w_in, mla_q_norm, mla_w_uq, mla_kv_norm,
           mla_w_ukv, mla_w_o, nat_w_qkv, nat_rpb, nat_w_o, w_router, router_bias,
           moe_w_gate, moe_w_up, moe_w_down):
    B, S, D = x_prompt.shape
    Bd, Sd, _ = x_sample.shape
    assert D == D_MODEL and Bd + 1 <= 8
    tm_c, tm_s = 512, 512

    xc = x_prompt.reshape(B * S, D)
    xs = x_sample.reshape(Bd * Sd, D)
    cond8 = jnp.concatenate([c_ctx[None, :], c, jnp.zeros((8 - 1 - Bd, D), F32)], axis=0)
    mod_all = _ada_modulation(cond8, w_ada, b_ada).reshape(DEPTH, 8, 6, D)

    rope_tabs = _rope_tables(Sd)
    wr = jnp.pad(w_router, ((0, 0), (0, LANES - N_EXPERTS)))
    wr_hi = wr.astype(BF16)
    wr = jnp.stack([wr_hi, (wr - wr_hi.astype(F32)).astype(BF16)], axis=0)
    rb = jnp.pad(router_bias, (0, LANES - N_EXPERTS)).reshape(1, LANES)
    wg = moe_w_gate.astype(BF16)
    wu = moe_w_up.astype(BF16)
    wd = moe_w_down.astype(BF16)

    new_ckv, new_krope, new_k, new_v = [], [], [], []
    for layer in range(DEPTH):
        mod = mod_all[layer]
        g_mix = norm_mix[layer][None, :]
        j = layer // 2
        if layer % 2 == 0:
            w_in_x, w_uq_x, w_uk, w_uv = _mla_weights(mla_w_in[j], mla_w_uq[j], mla_w_ukv[j])
            w_in_c = w_in_x[:, :MLA_Q_LORA + MLA_KV_LORA + LANES]
            w_uq_c = w_uq_x[:, :MLA_HEADS * (MLA_NOPE + MLA_ROPE)]
            qg = mla_q_norm[j][None, :]
            kvg = mla_kv_norm[j][None, :]
            w_o = mla_w_o[j].astype(BF16)
            qn, qr, ckv, kr, kr4 = _premix_mla(xc, mod, g_mix, w_in_c, qg, w_uq_c, kvg, None, None, tm_c)
            new_ckv.append(ckv.reshape(B, S, MLA_KV_LORA))
            new_krope.append(kr.reshape(B, S, MLA_ROPE))
            att_c = _mla_attention(qn, qr, ckv, kr4, None, None, w_uk, w_uv, B, S)
            qn, qr, ckv, kr, kr4 = _premix_mla(xs, mod, g_mix, w_in_x, qg, w_uq_x, kvg, rope_tabs, Sd, tm_s)
            cache_kr4 = jnp.tile(cache_mla_krope[:, j], (1, 1, LANES // MLA_ROPE))
            att_s = _mla_attention(qn, qr, ckv, kr4, cache_mla_ckv[:, j], cache_kr4, w_uk, w_uv, Bd, 256)
        else:
            w_qkv = nat_w_qkv[j].astype(BF16)
            w_o = nat_w_o[j].astype(BF16)
            q, k, v = _premix_nat(xc, mod, g_mix, w_qkv, None, tm_c, F32)
            new_k.append(k.reshape(B, S, NAT_HEADS, NAT_DH))
            new_v.append(v.reshape(B, S, NAT_HEADS, NAT_DH))
            att_c = _dense_attention(q, k, v, B)
            q, k, v = _premix_nat(xs, mod, g_mix, w_qkv, Sd, tm_s, BF16)
            bias = _nat_bias_tables(nat_rpb[j], Sd // GRID_W)
            att_s = _nat_attention(q, k, v,
                                   cache_nat_k[:, j].reshape(Bd, -1, D), cache_nat_v[:, j].reshape(Bd, -1, D),
                                   bias, Bd)
        xc = _proj_residual(xc, att_c, w_o, mod, None, tm_c)
        xs = _proj_residual(xs, att_s, w_o, mod, Sd, tm_s)
        g_ffn = norm_ffn[layer][None, :]
        xc = _moe(xc, mod, g_ffn, wr, rb, wg, wu, wd, layer, None)
        xs = _moe(xs, mod, g_ffn, wr, rb, wg, wu, wd, layer, Sd)

    gf = norm_final[None, :]
    y_prompt = _final_norm(xc, gf, tm_c).reshape(B, S, D)
    y_sample = _final_norm(xs, gf, tm_s).reshape(Bd, Sd, D)
    return (y_prompt, y_sample, jnp.stack(new_ckv, axis=1), jnp.stack(new_krope, axis=1),
            jnp.stack(new_k, axis=1), jnp.stack(new_v, axis=1))
```

```python
import functools

import numpy as np
import jax
import jax.numpy as jnp
from jax import lax
from jax.experimental import pallas as pl
from jax.experimental.pallas import tpu as pltpu

F32 = jnp.float32
BF16 = jnp.bfloat16

D_MODEL = 1024
DEPTH = 4
GRID_W = 64
LANES = 128
MLA_HEADS = 16
MLA_NOPE = 64
MLA_ROPE = 32
MLA_V = 64
MLA_Q_LORA = 384
MLA_KV_LORA = 256
MLA_SCALE = (MLA_NOPE + MLA_ROPE) ** -0.5
ROPE_THETA = 10000.0
NAT_HEADS = 16
NAT_DH = 64
NAT_SCALE = NAT_DH ** -0.5
WIN_H = 8
WIN_W = 16
NAT_QROWS = 4
NAT_KROWS = NAT_QROWS + WIN_H
N_EXPERTS = 16
N_GROUPS = 4
EPG = N_EXPERTS // N_GROUPS
D_EXPERT = 256
NORM_EPS = 1e-6
NEG_INF = -1e30

HEAD_PAIRS = MLA_HEADS // 2
VMEM_LIMIT = 56 * 1024 * 1024


def _cparams(*sem):
    return pltpu.CompilerParams(dimension_semantics=sem, vmem_limit_bytes=VMEM_LIMIT)


def _sigmoid(x):
    return 1.0 / (1.0 + jnp.exp(-x))


def _rms(x, g):
    ms = jnp.mean(x * x, axis=-1, keepdims=True)
    return x * lax.rsqrt(ms + NORM_EPS) * g


def _dot(a, b):
    return jnp.dot(a, b, preferred_element_type=F32)


def _dot_nt(a, b):
    return lax.dot_general(a, b, (((1,), (1,)), ((), ())), preferred_element_type=F32)


def _lane_mask(width, idx, dtype):
    lane = lax.broadcasted_iota(jnp.int32, (1, LANES), 1)
    return ((lane >= idx * width) & (lane < (idx + 1) * width)).astype(dtype)


def _ada_kernel(cond_ref, w_ref, b_ref, o_ref):
    c = cond_ref[...]
    s = (c * _sigmoid(c)).astype(BF16)
    o_ref[0] = _dot(s, w_ref[0].astype(BF16)) + b_ref[0]


def _ada_modulation(cond8, w_ada, b_ada):
    n_chunk = 6
    return pl.pallas_call(
        _ada_kernel,
        grid=(DEPTH, n_chunk),
        in_specs=[
            pl.BlockSpec((8, D_MODEL), lambda l, j: (0, 0)),
            pl.BlockSpec((1, D_MODEL, D_MODEL), lambda l, j: (l, 0, j)),
            pl.BlockSpec((1, 1, D_MODEL), lambda l, j: (l, 0, j)),
        ],
        out_specs=pl.BlockSpec((1, 8, D_MODEL), lambda l, j: (l, 0, j)),
        out_shape=jax.ShapeDtypeStruct((DEPTH, 8, 6 * D_MODEL), F32),
        compiler_params=_cparams("parallel", "parallel"),
        name="ada_modulation",
    )(cond8, w_ada, b_ada.reshape(DEPTH, 1, 6 * D_MODEL))


def _mod_spec(rows_per_cond):
    if rows_per_cond is None:
        return pl.BlockSpec((1, 6, D_MODEL), lambda i, *_: (0, 0, 0))
    return pl.BlockSpec((1, 6, D_MODEL), lambda i, *_: (1 + i // rows_per_cond, 0, 0))


def _premix_mla_kernel(*refs, rope):
    if rope:
        (x_ref, mod_ref, g_ref, w_in_ref, qg_ref, w_uq_ref, kvg_ref,
         cosq_ref, sinq_ref, cosk_ref, sink_ref,
         qn_ref, qr_ref, ckv_ref, kr_ref, kr4_ref) = refs
    else:
        (x_ref, mod_ref, g_ref, w_in_ref, qg_ref, w_uq_ref, kvg_ref,
         qn_ref, qr_ref, ckv_ref, kr_ref, kr4_ref) = refs
    h = _rms(x_ref[...], g_ref[...]) * (1.0 + mod_ref[0, 1:2, :]) + mod_ref[0, 0:1, :]
    lat = _dot(h.astype(BF16), w_in_ref[...])
    c_q = lat[:, :MLA_Q_LORA]
    c_kv = lat[:, MLA_Q_LORA:MLA_Q_LORA + MLA_KV_LORA]
    o = MLA_Q_LORA + MLA_KV_LORA
    kr4 = lat[:, o:o + LANES]
    q = _dot(_rms(c_q, qg_ref[...]).astype(BF16), w_uq_ref[...])
    n_nope = MLA_HEADS * MLA_NOPE
    n_rope = MLA_HEADS * MLA_ROPE
    qr = q[:, n_nope:n_nope + n_rope]
    if rope:
        qr = qr * cosq_ref[...] + q[:, n_nope + n_rope:] * sinq_ref[...]
        kr4 = kr4 * cosk_ref[...] + lat[:, o + LANES:o + 2 * LANES] * sink_ref[...]
    qn_ref[...] = q[:, :n_nope].astype(BF16)
    qr_ref[...] = qr.astype(BF16)
    ckv_ref[...] = _rms(c_kv, kvg_ref[...])
    kr_ref[...] = kr4[:, :MLA_ROPE]
    kr4_ref[...] = kr4.astype(BF16)


def _premix_mla(x, mod, g, w_in, qg, w_uq, kvg, rope_tabs, rows_per_cond, tm):
    n = x.shape[0]
    rope = rope_tabs is not None
    row = lambda i: (i, 0)
    full = lambda i: (0, 0)
    in_specs = [
        pl.BlockSpec((tm, D_MODEL), row),
        _mod_spec(None if rows_per_cond is None else rows_per_cond // tm),
        pl.BlockSpec((1, D_MODEL), full),
        pl.BlockSpec(w_in.shape, full),
        pl.BlockSpec((1, MLA_Q_LORA), full),
        pl.BlockSpec(w_uq.shape, full),
        pl.BlockSpec((1, MLA_KV_LORA), full),
    ]
    args = [x, mod, g, w_in, qg, w_uq, kvg]
    if rope:
        nblk = rope_tabs[0].shape[0] // tm
        pos = lambda i: (i % nblk, 0)
        for t in rope_tabs:
            in_specs.append(pl.BlockSpec((tm, t.shape[1]), pos))
            args.append(t)
    widths = (MLA_HEADS * MLA_NOPE, MLA_HEADS * MLA_ROPE, MLA_KV_LORA, MLA_ROPE, LANES)
    dtypes = (BF16, BF16, F32, F32, BF16)
    return pl.pallas_call(
        functools.partial(_premix_mla_kernel, rope=rope),
        grid=(n // tm,),
        in_specs=in_specs,
        out_specs=[pl.BlockSpec((tm, w), row) for w in widths],
        out_shape=[jax.ShapeDtypeStruct((n, w), d) for w, d in zip(widths, dtypes)],
        compiler_params=_cparams("parallel"),
        name="premix_mla_rope" if rope else "premix_mla",
    )(*args)


def _softmax_pv(s_list, v_list):
    m = functools.reduce(jnp.maximum, [jnp.max(s, axis=-1, keepdims=True) for s in s_list])
    e_list = [jnp.exp(s - m) for s in s_list]
    l = functools.reduce(lambda a, b: a + b, [jnp.sum(e, axis=-1, keepdims=True) for e in e_list])
    o = functools.reduce(lambda a, b: a + b,
                         [_dot(e.astype(BF16), v) for e, v in zip(e_list, v_list)])
    return o / l


def _mla_attn_kernel(*refs, cached):
    if cached:
        (qn_ref, qr_ref, ckv_ref, kr4_ref, cckv_ref, ckr4_ref, w_uk_ref, w_uv_ref,
         o_ref, kn_s, v_s, kr_s) = refs
    else:
        (qn_ref, qr_ref, ckv_ref, kr4_ref, w_uk_ref, w_uv_ref, o_ref, kn_s, v_s, kr_s) = refs
    t_own = ckv_ref.shape[0]

    @pl.when(pl.program_id(1) == 0)
    def _():
        c = ckv_ref[...].astype(BF16)
        kn_s[0:t_own, :] = _dot(c, w_uk_ref[...]).astype(BF16)
        v_s[0:t_own, :] = _dot(c, w_uv_ref[...]).astype(BF16)
        kr_s[0:t_own, :] = kr4_ref[...]
        if cached:
            cc = cckv_ref[0].astype(BF16)
            kn_s[t_own:, :] = _dot(cc, w_uk_ref[...]).astype(BF16)
            v_s[t_own:, :] = _dot(cc, w_uv_ref[...]).astype(BF16)
            kr_s[t_own:, :] = ckr4_ref[0].astype(BF16)

    kr4 = kr_s[...]
    for p in range(HEAD_PAIRS):
        sl = slice(p * LANES, (p + 1) * LANES)
        qn = qn_ref[:, sl]
        qr = qr_ref[:, (p // 2) * LANES:(p // 2 + 1) * LANES]
        k_cat = jnp.concatenate([kn_s[:, sl], kr4], axis=-1)
        v = v_s[:, sl]
        o = None
        for i in range(2):
            h = 2 * p + i
            m64 = _lane_mask(MLA_NOPE, i, BF16)
            q_cat = jnp.concatenate([qn * m64, qr * _lane_mask(MLA_ROPE, h % 4, BF16)], axis=-1)
            s = _dot_nt(q_cat, k_cat) * MLA_SCALE
            oh = _softmax_pv([s], [v * m64])
            o = oh if o is None else o + oh
        o_ref[:, sl] = o.astype(BF16)


def _mla_attention(qn, qr, ckv, kr4, cache_ckv, cache_kr4, w_uk, w_uv, n_batch, tq):
    n = qn.shape[0]
    s_own = n // n_batch
    nq = s_own // tq
    cached = cache_ckv is not None
    t_all = s_own + (cache_ckv.shape[1] if cached else 0)
    qrow = lambda b, j: (b * nq + j, 0)
    own = lambda b, j: (b, 0)
    full = lambda b, j: (0, 0)
    in_specs = [
        pl.BlockSpec((tq, qn.shape[1]), qrow),
        pl.BlockSpec((tq, qr.shape[1]), qrow),
        pl.BlockSpec((s_own, MLA_KV_LORA), own),
        pl.BlockSpec((s_own, LANES), own),
    ]
    args = [qn, qr, ckv, kr4]
    if cached:
        in_specs += [pl.BlockSpec((1,) + cache_ckv.shape[1:], lambda b, j: (b, 0, 0)),
                     pl.BlockSpec((1,) + cache_kr4.shape[1:], lambda b, j: (b, 0, 0))]
        args += [cache_ckv, cache_kr4]
    in_specs += [pl.BlockSpec(w_uk.shape, full), pl.BlockSpec(w_uv.shape, full)]
    args += [w_uk, w_uv]
    return pl.pallas_call(
        functools.partial(_mla_attn_kernel, cached=cached),
        grid=(n_batch, nq),
        in_specs=in_specs,
        out_specs=pl.BlockSpec((tq, D_MODEL), qrow),
        out_shape=jax.ShapeDtypeStruct((n, D_MODEL), BF16),
        scratch_shapes=[pltpu.VMEM((t_all, D_MODEL), BF16), pltpu.VMEM((t_all, D_MODEL), BF16),
                        pltpu.VMEM((t_all, LANES), BF16)],
        compiler_params=_cparams("parallel", "arbitrary"),
        name="mla_attention_cached" if cached else "mla_attention",
    )(*args)


def _premix_nat_kernel(x_ref, mod_ref, g_ref, w_ref, q_ref, k_ref, v_ref):
    h = _rms(x_ref[...], g_ref[...]) * (1.0 + mod_ref[0, 1:2, :]) + mod_ref[0, 0:1, :]
    qkv = _dot(h.astype(BF16), w_ref[...])
    q_ref[...] = qkv[:, :D_MODEL].astype(q_ref.dtype)
    k_ref[...] = qkv[:, D_MODEL:2 * D_MODEL].astype(k_ref.dtype)
    v_ref[...] = qkv[:, 2 * D_MODEL:].astype(v_ref.dtype)


def _premix_nat(x, mod, g, w_qkv, rows_per_cond, tm, kv_dtype):
    n = x.shape[0]
    row = lambda i: (i, 0)
    full = lambda i: (0, 0)
    return pl.pallas_call(
        _premix_nat_kernel,
        grid=(n // tm,),
        in_specs=[
            pl.BlockSpec((tm, D_MODEL), row),
            _mod_spec(None if rows_per_cond is None else rows_per_cond // tm),
            pl.BlockSpec((1, D_MODEL), full),
            pl.BlockSpec(w_qkv.shape, full),
        ],
        out_specs=[pl.BlockSpec((tm, D_MODEL), row)] * 3,
        out_shape=[jax.ShapeDtypeStruct((n, D_MODEL), BF16),
                   jax.ShapeDtypeStruct((n, D_MODEL), kv_dtype),
                   jax.ShapeDtypeStruct((n, D_MODEL), kv_dtype)],
        compiler_params=_cparams("parallel"),
        name="premix_nat",
    )(x, mod, g, w_qkv)


def _dense_attn_kernel(q_ref, k_ref, v_ref, o_ref):
    for p in range(HEAD_PAIRS):
        sl = slice(p * LANES, (p + 1) * LANES)
        q = q_ref[:, sl]
        k = k_ref[:, sl].astype(BF16)
        v = v_ref[:, sl].astype(BF16)
        o = None
        for i in range(2):
            m64 = _lane_mask(NAT_DH, i, BF16)
            s = _dot_nt(q * m64, k) * NAT_SCALE
            oh = _softmax_pv([s], [v * m64])
            o = oh if o is None else o + oh
        o_ref[:, sl] = o.astype(BF16)


def _dense_attention(q, k, v, n_batch):
    n = q.shape[0]
    s = n // n_batch
    blk = pl.BlockSpec((s, D_MODEL), lambda b: (b, 0))
    return pl.pallas_call(
        _dense_attn_kernel,
        grid=(n_batch,),
        in_specs=[blk, blk, blk],
        out_specs=blk,
        out_shape=jax.ShapeDtypeStruct((n, D_MODEL), BF16),
        compiler_params=_cparams("parallel"),
        name="dense_attention",
    )(q, k, v)


_NAT_QBLK = NAT_QROWS * GRID_W
_NAT_KBLK = NAT_KROWS * GRID_W


def _nat_block_plan(rows):
    assert rows % NAT_QROWS == 0 and rows >= NAT_KROWS and NAT_KROWS % 2 == 0
    plan, variants = [], []
    for r0 in range(0, rows, NAT_QROWS):
        ks = min(max(r0 - WIN_H // 2, 0), rows - NAT_KROWS)
        r = r0 + np.arange(NAT_QROWS)
        kr = ks + np.arange(NAT_KROWS)
        rs = np.clip(r - WIN_H // 2, 0, rows - WIN_H)
        valid_row = (kr[None, :] >= rs[:, None]) & (kr[None, :] < rs[:, None] + WIN_H)
        d0 = ks - r + (WIN_H - 1)
        for vi, (d0_v, valid_v) in enumerate(variants):
            if np.array_equal(d0, d0_v) and np.array_equal(valid_row, valid_v):
                break
        else:
            vi = len(variants)
            variants.append((d0, valid_row))
        plan.append((ks, vi))
    return plan, variants


def _nat_bias_row_range(variants):
    lo = min(int(d0.min()) for d0, _ in variants)
    hi = max(int(d0.max()) for d0, _ in variants) + NAT_KROWS
    return lo, hi


def _nat_bias_pairs(rpb, rows):
    n_heads, n_dr, n_dc = rpb.shape
    _, variants = _nat_block_plan(rows)
    lo, hi = _nat_bias_row_range(variants)
    c = np.arange(GRID_W)
    cs = np.clip(c - WIN_W // 2, 0, GRID_W - WIN_W)
    valid_col = (c[None, :] >= cs[:, None]) & (c[None, :] < cs[:, None] + WIN_W)
    d_col = c[None, :] - c[:, None] + (WIN_W - 1)
    sel = (d_col[None] == np.arange(n_dc)[:, None, None]) & valid_col[None]
    toep = jnp.einsum('hdj,jck->hdck', rpb, jnp.asarray(sel, F32), precision=lax.Precision.HIGHEST)
    toep = jnp.where(jnp.asarray(valid_col)[None, None], toep, NEG_INF)
    toep = jnp.pad(toep, ((0, 0), (max(-lo, 0), max(hi + 1 - n_dr, 0)), (0, 0), (0, 0)),
                   constant_values=NEG_INF)
    toep = toep[:, max(lo, 0):]
    return jnp.concatenate([toep[:, :-1], toep[:, 1:]], axis=-1)


def _nat_block_bias(tp_ref, head, d0, valid_row, row_lo):
    neg = jnp.full((GRID_W, 2 * GRID_W), NEG_INF, F32)
    left = lax.broadcasted_iota(jnp.int32, (1, 2 * GRID_W), 1) < GRID_W
    rows_out = []
    for dr in range(NAT_QROWS):
        pieces = []
        for a in range(0, NAT_KROWS, 2):
            ok0, ok1 = bool(valid_row[dr, a]), bool(valid_row[dr, a + 1])
            if not (ok0 or ok1):
                pieces.append(neg)
                continue
            piece = tp_ref[head, int(d0[dr]) + a - row_lo]
            if ok0 and not ok1:
                piece = jnp.where(left, piece, NEG_INF)
            elif ok1 and not ok0:
                piece = jnp.where(left, NEG_INF, piece)
            pieces.append(piece)
        rows_out.append(jnp.concatenate(pieces, axis=-1))
    return jnp.concatenate(rows_out, axis=0)


def _nat_attn_kernel(q_ref, k_ref, v_ref, kc_ref, vc_ref, tp_ref, o_ref, *, plan, variants, row_lo):
    kc = kc_ref[0].astype(BF16)
    vc = vc_ref[0].astype(BF16)
    masks = [_lane_mask(NAT_DH, i, BF16) for i in range(2)]
    vcm = [vc * m for m in masks]
    for bi, (ks, var) in enumerate(plan):
        q = q_ref[bi * _NAT_QBLK:(bi + 1) * _NAT_QBLK, :]
        k = k_ref[ks * GRID_W:ks * GRID_W + _NAT_KBLK, :]
        v = v_ref[ks * GRID_W:ks * GRID_W + _NAT_KBLK, :]
        o = None
        for i in range(2):
            qm = q * masks[i]
            s_loc = _dot_nt(qm, k) * NAT_SCALE + _nat_block_bias(tp_ref, i, *variants[var], row_lo)
            s_ctx = _dot_nt(qm, kc) * NAT_SCALE
            oh = _softmax_pv([s_loc, s_ctx], [v * masks[i], vcm[i]])
            o = oh if o is None else o + oh
        o_ref[bi * _NAT_QBLK:(bi + 1) * _NAT_QBLK, :] = o.astype(BF16)


def _nat_attention(q, k, v, cache_k, cache_v, bias, n_batch):
    n = q.shape[0]
    s = n // n_batch
    plan, variants = _nat_block_plan(s // GRID_W)
    row_lo, _ = _nat_bias_row_range(variants)
    own = pl.BlockSpec((s, LANES), lambda p, b: (b, p))
    cache = pl.BlockSpec((1, cache_k.shape[1], LANES), lambda p, b: (b, 0, p))
    return pl.pallas_call(
        functools.partial(_nat_attn_kernel, plan=plan, variants=variants, row_lo=row_lo),
        grid=(HEAD_PAIRS, n_batch),
        in_specs=[own, own, own, cache, cache,
                  pl.BlockSpec((2,) + bias.shape[1:], lambda p, b: (p, 0, 0, 0))],
        out_specs=own,
        out_shape=jax.ShapeDtypeStruct((n, D_MODEL), BF16),
        compiler_params=_cparams("parallel", "parallel"),
        name="nat_attention",
    )(q, k, v, cache_k, cache_v, bias)


def _proj_res_kernel(x_ref, a_ref, w_ref, mod_ref, o_ref):
    o_ref[...] = x_ref[...] + mod_ref[0, 2:3, :] * _dot(a_ref[...], w_ref[...])


def _proj_residual(x, a, w, mod, rows_per_cond, tm):
    n = x.shape[0]
    row = lambda i: (i, 0)
    return pl.pallas_call(
        _proj_res_kernel,
        grid=(n // tm,),
        in_specs=[
            pl.BlockSpec((tm, D_MODEL), row),
            pl.BlockSpec((tm, D_MODEL), row),
            pl.BlockSpec(w.shape, lambda i: (0, 0)),
            _mod_spec(None if rows_per_cond is None else rows_per_cond // tm),
        ],
        out_specs=pl.BlockSpec((tm, D_MODEL), row),
        out_shape=jax.ShapeDtypeStruct((n, D_MODEL), F32),
        compiler_params=_cparams("parallel"),
        name="proj_residual",
    )(x, a, w, mod)


def _top2_sum(a, b, c, d):
    hi1, lo1 = jnp.maximum(a, b), jnp.minimum(a, b)
    hi2, lo2 = jnp.maximum(c, d), jnp.minimum(c, d)
    return jnp.maximum(hi1, hi2) + jnp.maximum(jnp.minimum(hi1, hi2), jnp.maximum(lo1, lo2))


def _route(scores, biased):
    sc = [scores[e:e + 1, :] for e in range(N_EXPERTS)]
    bs = [biased[e:e + 1, :] for e in range(N_EXPERTS)]
    gscore = [_top2_sum(*bs[EPG * g:EPG * (g + 1)]) for g in range(N_GROUPS)]
    best, gidx = gscore[0], jnp.zeros_like(gscore[0], dtype=jnp.int32)
    for g in range(1, N_GROUPS):
        better = gscore[g] > best
        gidx = jnp.where(better, g, gidx)
        best = jnp.where(better, gscore[g], best)
    cb = [functools.reduce(lambda a, b: a + b,
                           [jnp.where(gidx == g, bs[EPG * g + i], 0.0) for g in range(N_GROUPS)])
          for i in range(EPG)]
    cs = [functools.reduce(lambda a, b: a + b,
                           [jnp.where(gidx == g, sc[EPG * g + i], 0.0) for g in range(N_GROUPS)])
          for i in range(EPG)]
    b1, i1 = cb[0], jnp.zeros_like(gidx)
    for i in range(1, EPG):
        better = cb[i] > b1
        i1 = jnp.where(better, i, i1)
        b1 = jnp.where(better, cb[i], b1)
    b2, i2 = jnp.full_like(b1, -jnp.inf), jnp.full_like(i1, -1)
    for i in range(EPG):
        better = (i1 != i) & (cb[i] > b2)
        i2 = jnp.where(better, i, i2)
        b2 = jnp.where(better, cb[i], b2)
    sel = [(i1 == i) | (i2 == i) for i in range(EPG)]
    w = [jnp.where(sel[i], cs[i], 0.0) for i in range(EPG)]
    tot = w[0] + w[1] + w[2] + w[3]
    return gidx, [w[i] / tot for i in range(EPG)]


MOE_SRC = 256
MOE_DST = 256
MOE_CHUNK = 4
_LOG2_DST = MOE_DST.bit_length() - 1
assert MOE_DST == 1 << _LOG2_DST and MOE_SRC <= MOE_DST
_INFO_GID = EPG
_INFO_RANK = EPG + 1


def _moe_route_kernel(x_ref, mod_ref, g_ref, wr_ref, rb_ref, h_ref, info_ref, infot_ref, cum_ref, carry_s):
    @pl.when(pl.program_id(0) == 0)
    def _():
        carry_s[...] = jnp.zeros_like(carry_s)

    h = _rms(x_ref[...], g_ref[...]) * (1.0 + mod_ref[0, 4:5, :]) + mod_ref[0, 3:4, :]
    h_hi = h.astype(BF16)
    h_ref[...] = h_hi
    h_lo = (h - h_hi.astype(F32)).astype(BF16)
    logits = _dot(h_hi, wr_ref[0]) + (_dot(h_lo, wr_ref[0]) + _dot(h_hi, wr_ref[1]))
    scores = _sigmoid(logits)
    gidx, gates = _route(scores.T[:N_EXPERTS], (scores + rb_ref[...]).T[:N_EXPERTS])
    tm = h.shape[0]
    sub = lax.broadcasted_iota(jnp.int32, (8, 1), 0)
    onehot = (sub == gidx).astype(F32)
    tri = (lax.broadcasted_iota(jnp.int32, (tm, tm), 0)
           <= lax.broadcasted_iota(jnp.int32, (tm, tm), 1)).astype(BF16)
    prefix = _dot(onehot.astype(BF16), tri)
    carry = carry_s[:, 0:1]
    rank = jnp.sum((prefix - 1.0 + carry) * onehot, axis=0, keepdims=True)
    infot = jnp.concatenate(gates + [gidx.astype(F32), rank, jnp.zeros((2, tm), F32)], axis=0)
    infot_ref[0] = infot
    info_ref[...] = jnp.concatenate([infot, jnp.zeros((LANES - 8, tm), F32)], axis=0).T
    carry = jnp.broadcast_to(carry + prefix[:, tm - 1:tm], (8, LANES))
    carry_s[...] = carry
    cum_ref[0] = carry.astype(jnp.int32)


def _group_tiles(cum_ref):
    n_src = cum_ref.shape[0]
    tot = [cum_ref[n_src - 1, g] for g in range(N_GROUPS)]
    ends, acc = [], 0
    for t in tot:
        acc = acc + lax.shift_right_logical(t + (MOE_DST - 1), _LOG2_DST)
        ends.append(acc)
    return tot, ends


def _dest_tile(d, cum_ref):
    tot, ends = _group_tiles(cum_ref)
    g = ((d >= ends[0]).astype(jnp.int32) + (d >= ends[1]).astype(jnp.int32)
         + (d >= ends[2]).astype(jnp.int32))
    first = jnp.where(g == 0, 0, jnp.where(g == 1, ends[0], jnp.where(g == 2, ends[1], ends[2])))
    tot_g = jnp.where(g == 0, tot[0], jnp.where(g == 1, tot[1], jnp.where(g == 2, tot[2], tot[3])))
    k0 = (d - first) * MOE_DST
    n_valid = jnp.clip(tot_g - k0, 0, MOE_DST)
    return g, k0, n_valid


def _counts_before(cum_ref, s, g):
    return jnp.where(s > 0, cum_ref[jnp.maximum(s - 1, 0), g], 0)


def _moe_ffn_kernel(cum_ref, h_ref, info_ref, infot_ref, wg_ref, wu_ref, wd_ref, y_ref, xs_s, gs_s, ptr_s):
    d = pl.program_id(0)
    g, k0, n_valid = _dest_tile(d, cum_ref)
    n_src = cum_ref.shape[0]

    @pl.when(d == 0)
    def _():
        ptr_s[0] = 0

    @pl.when(n_valid == 0)
    def _():
        y_ref[...] = jnp.zeros_like(y_ref)

    @pl.when(n_valid > 0)
    def _():
        need = k0 + n_valid
        start = jnp.where(k0 == 0, 0, ptr_s[0])
        s_lo = lax.while_loop(lambda s: cum_ref[s, g] <= k0, lambda s: s + 1, start)
        ptr_s[0] = s_lo
        xs_s[...] = jnp.zeros_like(xs_s)
        gs_s[...] = jnp.zeros_like(gs_s)
        row = lax.broadcasted_iota(jnp.int32, (MOE_DST, 1), 0).astype(F32)
        gf = g.astype(F32)
        k0f = k0.astype(F32)

        def more(s0):
            return (_counts_before(cum_ref, s0, g) < need) & (s0 < n_src)

        def gather(s0):
            base = jnp.minimum(s0, n_src - MOE_CHUNK)
            perms = []
            for j in range(MOE_CHUNK):
                it = infot_ref[base + j]
                shift = k0f + jnp.where(base + j >= s0, 0.0, 1e9)
                rel = jnp.where(it[_INFO_GID:_INFO_GID + 1] == gf,
                                it[_INFO_RANK:_INFO_RANK + 1] - shift, -1.0)
                perms.append((row == rel).astype(BF16))
            perm = jnp.concatenate(perms, axis=-1)
            rows = pl.ds(pl.multiple_of(base * MOE_SRC, MOE_SRC), MOE_CHUNK * MOE_SRC)
            xs_s[...] += _dot(perm, h_ref[rows, :])
            inf = info_ref[rows, :]
            inf_hi = inf.astype(BF16)
            inf_lo = (inf - inf_hi.astype(F32)).astype(BF16)
            gs_s[...] += _dot(perm, inf_hi) + _dot(perm, inf_lo)
            return base + MOE_CHUNK

        lax.while_loop(more, gather, s_lo)
        xs = xs_s[...].astype(BF16)
        gate = gs_s[...]
        acts = []
        for i in range(EPG):
            a = _dot(xs, wg_ref[i])
            u = _dot(xs, wu_ref[i])
            acts.append(((a * _sigmoid(a)) * u * gate[:, i:i + 1]).astype(BF16))
        act = jnp.concatenate(acts, axis=-1)
        y_ref[...] = _dot(act, wd_ref[...].reshape(EPG * D_EXPERT, D_MODEL)).astype(BF16)


def _moe_unsort_kernel(cum_ref, x_ref, mod_ref, info_ref, y_ref, o_ref):
    s = pl.program_id(0)
    n_dst = y_ref.shape[0] // MOE_DST
    _, ends = _group_tiles(cum_ref)
    firsts = [0] + ends[:-1]
    inf = info_ref[...]
    gid = inf[:, _INFO_GID:_INFO_GID + 1]
    rank = inf[:, _INFO_RANK:_INFO_RANK + 1]
    col = lax.broadcasted_iota(jnp.int32, (1, MOE_DST), 1).astype(F32)
    acc = None
    for g in range(N_GROUPS):
        k_lo = lax.shift_right_logical(_counts_before(cum_ref, s, g), _LOG2_DST)
        for j in range(2):
            k = k_lo + j
            dt = jnp.minimum(firsts[g] + k, n_dst - 1)
            rel = jnp.where(gid == float(g), rank - (k * MOE_DST).astype(F32), -1.0)
            perm = (rel == col).astype(BF16)
            rows = pl.ds(pl.multiple_of(dt * MOE_DST, MOE_DST), MOE_DST)
            part = _dot(perm, y_ref[rows, :])
            acc = part if acc is None else acc + part
    o_ref[...] = x_ref[...] + mod_ref[0, 5:6, :] * acc


def _moe(x, mod, g, w_router, router_bias, wg, wu, wd, layer, rows_per_cond):
    n = x.shape[0]
    n_src = n // MOE_SRC
    n_dst = n // MOE_DST + N_GROUPS
    row = lambda i: (i, 0)
    full = lambda i: (0, 0)
    h, info, infot, cum = pl.pallas_call(
        _moe_route_kernel,
        grid=(n_src,),
        in_specs=[
            pl.BlockSpec((MOE_SRC, D_MODEL), row),
            _mod_spec(None if rows_per_cond is None else rows_per_cond // MOE_SRC),
            pl.BlockSpec((1, D_MODEL), full),
            pl.BlockSpec(w_router.shape, lambda i: (0, 0, 0)),
            pl.BlockSpec(router_bias.shape, full),
        ],
        out_specs=[pl.BlockSpec((MOE_SRC, D_MODEL), row),
                   pl.BlockSpec((MOE_SRC, LANES), row),
                   pl.BlockSpec((1, 8, MOE_SRC), lambda i: (i, 0, 0)),
                   pl.BlockSpec((1, 8, LANES), lambda i: (i, 0, 0))],
        out_shape=[jax.ShapeDtypeStruct((n, D_MODEL), BF16),
                   jax.ShapeDtypeStruct((n, LANES), F32),
                   jax.ShapeDtypeStruct((n_src, 8, MOE_SRC), F32),
                   jax.ShapeDtypeStruct((n_src, 8, LANES), jnp.int32)],
        scratch_shapes=[pltpu.VMEM((8, LANES), F32)],
        compiler_params=_cparams("arbitrary"),
        name="moe_route",
    )(x, mod, g, w_router, router_bias)
    cum = cum[:, :N_GROUPS, 0]

    resident = lambda a: pl.BlockSpec(a.shape, lambda d, c: (0,) * a.ndim, pipeline_mode=pl.Buffered(1))
    expert = lambda shape: pl.BlockSpec(shape, lambda d, c: (layer, _dest_tile(d, c)[0], 0, 0))
    y = pl.pallas_call(
        _moe_ffn_kernel,
        grid_spec=pltpu.PrefetchScalarGridSpec(
            num_scalar_prefetch=1,
            grid=(n_dst,),
            in_specs=[resident(h), resident(info), resident(infot),
                      expert((None, EPG, D_MODEL, D_EXPERT)),
                      expert((None, EPG, D_MODEL, D_EXPERT)),
                      expert((None, EPG, D_EXPERT, D_MODEL))],
            out_specs=pl.BlockSpec((MOE_DST, D_MODEL), lambda d, c: (d, 0)),
            scratch_shapes=[pltpu.VMEM((MOE_DST, D_MODEL), F32), pltpu.VMEM((MOE_DST, LANES), F32),
                            pltpu.SMEM((1,), jnp.int32)],
        ),
        out_shape=jax.ShapeDtypeStruct((n_dst * MOE_DST, D_MODEL), BF16),
        compiler_params=_cparams("arbitrary"),
        name="moe_experts",
    )(cum, h, info, infot, wg, wu, wd)

    return pl.pallas_call(
        _moe_unsort_kernel,
        grid_spec=pltpu.PrefetchScalarGridSpec(
            num_scalar_prefetch=1,
            grid=(n_src,),
            in_specs=[pl.BlockSpec((MOE_SRC, D_MODEL), lambda i, c: (i, 0)),
                      _mod_spec(None if rows_per_cond is None else rows_per_cond // MOE_SRC),
                      pl.BlockSpec((MOE_SRC, LANES), lambda i, c: (i, 0)),
                      resident(y)],
            out_specs=pl.BlockSpec((MOE_SRC, D_MODEL), lambda i, c: (i, 0)),
        ),
        out_shape=jax.ShapeDtypeStruct((n, D_MODEL), F32),
        compiler_params=_cparams("parallel"),
        name="moe_unsort",
    )(cum, x, mod, info, y)


def _final_norm_kernel(x_ref, g_ref, o_ref):
    o_ref[...] = _rms(x_ref[...], g_ref[...])


def _final_norm(x, g, tm):
    n = x.shape[0]
    return pl.pallas_call(
        _final_norm_kernel,
        grid=(n // tm,),
        in_specs=[pl.BlockSpec((tm, D_MODEL), lambda i: (i, 0)),
                  pl.BlockSpec((1, D_MODEL), lambda i: (0, 0))],
        out_specs=pl.BlockSpec((tm, D_MODEL), lambda i: (i, 0)),
        out_shape=jax.ShapeDtypeStruct((n, D_MODEL), F32),
        compiler_params=_cparams("parallel"),
        name="final_norm",
    )(x, g)


def _rope_tables(n_lat):
    t = np.arange(n_lat)
    n_freq = MLA_ROPE // 4
    inv_freq = jnp.asarray(ROPE_THETA, F32) ** (-jnp.arange(n_freq, dtype=F32) / n_freq)
    ar = jnp.asarray(t // GRID_W, F32)[:, None] * inv_freq
    ac = jnp.asarray(t % GRID_W, F32)[:, None] * inv_freq
    cos = jnp.concatenate([jnp.cos(ar), jnp.cos(ar), jnp.cos(ac), jnp.cos(ac)], axis=-1)
    sin = jnp.concatenate([-jnp.sin(ar), jnp.sin(ar), -jnp.sin(ac), jnp.sin(ac)], axis=-1)
    return (jnp.tile(cos, (1, MLA_HEADS)), jnp.tile(sin, (1, MLA_HEADS)),
            jnp.tile(cos, (1, LANES // MLA_ROPE)), jnp.tile(sin, (1, LANES // MLA_ROPE)))


_ROPE_SWAP = np.concatenate([np.arange(8, 16), np.arange(0, 8), np.arange(24, 32), np.arange(16, 24)])


def _mla_weights(w_in, w_uq, w_ukv):
    o = MLA_Q_LORA + MLA_KV_LORA
    kr = w_in[:, o:]
    rep = LANES // MLA_ROPE
    w_in_x = jnp.concatenate([w_in[:, :o], jnp.tile(kr, (1, rep)), jnp.tile(kr[:, _ROPE_SWAP], (1, rep))],
                             axis=-1).astype(BF16)
    uq = w_uq.reshape(MLA_Q_LORA, MLA_HEADS, MLA_NOPE + MLA_ROPE)
    q_nope = uq[:, :, :MLA_NOPE].reshape(MLA_Q_LORA, -1)
    q_rope = uq[:, :, MLA_NOPE:]
    w_uq_x = jnp.concatenate([q_nope, q_rope.reshape(MLA_Q_LORA, -1),
                              q_rope[:, :, _ROPE_SWAP].reshape(MLA_Q_LORA, -1)], axis=-1).astype(BF16)
    ukv = w_ukv.reshape(MLA_KV_LORA, MLA_HEADS, MLA_NOPE + MLA_V)
    w_uk = ukv[:, :, :MLA_NOPE].reshape(MLA_KV_LORA, -1).astype(BF16)
    w_uv = ukv[:, :, MLA_NOPE:].reshape(MLA_KV_LORA, -1).astype(BF16)
    return w_in_x, w_uq_x, w_uk, w_uv


def kernel(x_prompt, x_sample, cache_mla_ckv, cache_mla_krope, cache_nat_k, cache_nat_v, c, c_ctx,
           w_ada, b_ada, norm_mix, norm_ffn, norm_final, mla_w_in, mla_q_norm, mla_w_uq, mla_kv_norm,
           mla_w_ukv, mla_w_o, nat_w_qkv, nat_rpb, nat_w_o, w_router, router_bias,
           moe_w_gate, moe_w_up, moe_w_down):
    B, S, D = x_prompt.shape
    Bd, Sd, _ = x_sample.shape
    assert D == D_MODEL and Bd + 1 <= 8
    tm_c, tm_s = 512, 512

    xc = x_prompt.reshape(B * S, D)
    xs = x_sample.reshape(Bd * Sd, D)
    cond8 = jnp.concatenate([c_ctx[None, :], c, jnp.zeros((8 - 1 - Bd, D), F32)], axis=0)
    mod_all = _ada_modulation(cond8, w_ada, b_ada).reshape(DEPTH, 8, 6, D)

    rope_tabs = _rope_tables(Sd)
    wr = jnp.pad(w_router, ((0, 0), (0, LANES - N_EXPERTS)))
    wr_hi = wr.astype(BF16)
    wr = jnp.stack([wr_hi, (wr - wr_hi.astype(F32)).astype(BF16)], axis=0)
    rb = jnp.pad(router_bias, (0, LANES - N_EXPERTS)).reshape(1, LANES)
    wg = moe_w_gate.astype(BF16)
    wu = moe_w_up.astype(BF16)
    wd = moe_w_down.astype(BF16)

    new_ckv, new_krope, new_k, new_v = [], [], [], []
    for layer in range(DEPTH):
        mod = mod_all[layer]
        g_mix = norm_mix[layer][None, :]
        j = layer // 2
        if layer % 2 == 0:
            w_in_x, w_uq_x, w_uk, w_uv = _mla_weights(mla_w_in[j], mla_w_uq[j], mla_w_ukv[j])
            w_in_c = w_in_x[:, :MLA_Q_LORA + MLA_KV_LORA + LANES]
            w_uq_c = w_uq_x[:, :MLA_HEADS * (MLA_NOPE + MLA_ROPE)]
            qg = mla_q_norm[j][None, :]
            kvg = mla_kv_norm[j][None, :]
            w_o = mla_w_o[j].astype(BF16)
            qn, qr, ckv, kr, kr4 = _premix_mla(xc, mod, g_mix, w_in_c, qg, w_uq_c, kvg, None, None, tm_c)
            new_ckv.append(ckv.reshape(B, S, MLA_KV_LORA))
            new_krope.append(kr.reshape(B, S, MLA_ROPE))
            att_c = _mla_attention(qn, qr, ckv, kr4, None, None, w_uk, w_uv, B, S)
            qn, qr, ckv, kr, kr4 = _premix_mla(xs, mod, g_mix, w_in_x, qg, w_uq_x, kvg, rope_tabs, Sd, tm_s)
            cache_kr4 = jnp.tile(cache_mla_krope[:, j], (1, 1, LANES // MLA_ROPE))
            att_s = _mla_attention(qn, qr, ckv, kr4, cache_mla_ckv[:, j], cache_kr4, w_uk, w_uv, Bd, 256)
        else:
            w_qkv = nat_w_qkv[j].astype(BF16)
            w_o = nat_w_o[j].astype(BF16)
            q, k, v = _premix_nat(xc, mod, g_mix, w_qkv, None, tm_c, F32)
            new_k.append(k.reshape(B, S, NAT_HEADS, NAT_DH))
            new_v.append(v.reshape(B, S, NAT_HEADS, NAT_DH))
            att_c = _dense_attention(q, k, v, B)
            q, k, v = _premix_nat(xs, mod, g_mix, w_qkv, Sd, tm_s, BF16)
            bias = _nat_bias_pairs(nat_rpb[j], Sd // GRID_W)
            att_s = _nat_attention(q, k, v,
                                   cache_nat_k[:, j].reshape(Bd, -1, D), cache_nat_v[:, j].reshape(Bd, -1, D),
                                   bias, Bd)
        xc = _proj_residual(xc, att_c, w_o, mod, None, tm_c)
        xs = _proj_residual(xs, att_s, w_o, mod, Sd, tm_s)
        g_ffn = norm_ffn[layer][None, :]
        xc = _moe(xc, mod, g_ffn, wr, rb, wg, wu, wd, layer, None)
        xs = _moe(xs, mod, g_ffn, wr, rb, wg, wu, wd, layer, Sd)

    gf = norm_final[None, :]
    y_prompt = _final_norm(xc, gf, tm_c).reshape(B, S, D)
    y_sample = _final_norm(xs, gf, tm_s).reshape(Bd, Sd, D)
    return (y_prompt, y_sample, jnp.stack(new_ckv, axis=1), jnp.stack(new_krope, axis=1),
            jnp.stack(new_k, axis=1), jnp.stack(new_v, axis=1))
```

```python
import functools

import numpy as np
import jax
import jax.numpy as jnp
from jax import lax
from jax.experimental import pallas as pl
from jax.experimental.pallas import tpu as pltpu

F32 = jnp.float32
BF16 = jnp.bfloat16

D_MODEL = 1024
DEPTH = 4
GRID_W = 64
LANES = 128
MLA_HEADS = 16
MLA_NOPE = 64
MLA_ROPE = 32
MLA_V = 64
MLA_Q_LORA = 384
MLA_KV_LORA = 256
MLA_SCALE = (MLA_NOPE + MLA_ROPE) ** -0.5
ROPE_THETA = 10000.0
NAT_HEADS = 16
NAT_DH = 64
NAT_SCALE = NAT_DH ** -0.5
WIN_H = 8
WIN_W = 16
NAT_QROWS = 4
NAT_KROWS = NAT_QROWS + WIN_H
N_EXPERTS = 16
N_GROUPS = 4
EPG = N_EXPERTS // N_GROUPS
D_EXPERT = 256
NORM_EPS = 1e-6
NEG_INF = -1e30

HEAD_PAIRS = MLA_HEADS // 2
VMEM_LIMIT = 56 * 1024 * 1024


def _cparams(*sem):
    return pltpu.CompilerParams(dimension_semantics=sem, vmem_limit_bytes=VMEM_LIMIT)


def _sigmoid(x):
    return 1.0 / (1.0 + jnp.exp(-x))


def _rms(x, g):
    ms = jnp.mean(x * x, axis=-1, keepdims=True)
    return x * lax.rsqrt(ms + NORM_EPS) * g


def _dot(a, b):
    return jnp.dot(a, b, preferred_element_type=F32)


def _dot_nt(a, b):
    return lax.dot_general(a, b, (((1,), (1,)), ((), ())), preferred_element_type=F32)


def _lane_mask(width, idx, dtype):
    lane = lax.broadcasted_iota(jnp.int32, (1, LANES), 1)
    return ((lane >= idx * width) & (lane < (idx + 1) * width)).astype(dtype)


def _ada_kernel(cond_ref, w_ref, b_ref, o_ref):
    c = cond_ref[...]
    s = (c * _sigmoid(c)).astype(BF16)
    o_ref[0] = _dot(s, w_ref[0].astype(BF16)) + b_ref[0]


def _ada_modulation(cond8, w_ada, b_ada):
    n_chunk = 6
    return pl.pallas_call(
        _ada_kernel,
        grid=(DEPTH, n_chunk),
        in_specs=[
            pl.BlockSpec((8, D_MODEL), lambda l, j: (0, 0)),
            pl.BlockSpec((1, D_MODEL, D_MODEL), lambda l, j: (l, 0, j)),
            pl.BlockSpec((1, 1, D_MODEL), lambda l, j: (l, 0, j)),
        ],
        out_specs=pl.BlockSpec((1, 8, D_MODEL), lambda l, j: (l, 0, j)),
        out_shape=jax.ShapeDtypeStruct((DEPTH, 8, 6 * D_MODEL), F32),
        compiler_params=_cparams("parallel", "parallel"),
        name="ada_modulation",
    )(cond8, w_ada, b_ada.reshape(DEPTH, 1, 6 * D_MODEL))


def _mod_spec(rows_per_cond):
    if rows_per_cond is None:
        return pl.BlockSpec((1, 6, D_MODEL), lambda i, *_: (0, 0, 0))
    return pl.BlockSpec((1, 6, D_MODEL), lambda i, *_: (1 + i // rows_per_cond, 0, 0))


def _premix_mla_kernel(*refs, rope):
    if rope:
        (x_ref, mod_ref, g_ref, w_in_ref, qg_ref, w_uq_ref, kvg_ref,
         cosq_ref, sinq_ref, cosk_ref, sink_ref,
         qn_ref, qr_ref, ckv_ref, kr_ref, kr4_ref) = refs
    else:
        (x_ref, mod_ref, g_ref, w_in_ref, qg_ref, w_uq_ref, kvg_ref,
         qn_ref, qr_ref, ckv_ref, kr_ref, kr4_ref) = refs
    h = _rms(x_ref[...], g_ref[...]) * (1.0 + mod_ref[0, 1:2, :]) + mod_ref[0, 0:1, :]
    lat = _dot(h.astype(BF16), w_in_ref[...])
    c_q = lat[:, :MLA_Q_LORA]
    c_kv = lat[:, MLA_Q_LORA:MLA_Q_LORA + MLA_KV_LORA]
    o = MLA_Q_LORA + MLA_KV_LORA
    kr4 = lat[:, o:o + LANES]
    q = _dot(_rms(c_q, qg_ref[...]).astype(BF16), w_uq_ref[...])
    n_nope = MLA_HEADS * MLA_NOPE
    n_rope = MLA_HEADS * MLA_ROPE
    qr = q[:, n_nope:n_nope + n_rope]
    if rope:
        qr = qr * cosq_ref[...] + q[:, n_nope + n_rope:] * sinq_ref[...]
        kr4 = kr4 * cosk_ref[...] + lat[:, o + LANES:o + 2 * LANES] * sink_ref[...]
    qn_ref[...] = q[:, :n_nope].astype(BF16)
    qr_ref[...] = qr.astype(BF16)
    ckv_ref[...] = _rms(c_kv, kvg_ref[...])
    kr_ref[...] = kr4[:, :MLA_ROPE]
    kr4_ref[...] = kr4.astype(BF16)


def _premix_mla(x, mod, g, w_in, qg, w_uq, kvg, rope_tabs, rows_per_cond, tm):
    n = x.shape[0]
    rope = rope_tabs is not None
    row = lambda i: (i, 0)
    full = lambda i: (0, 0)
    in_specs = [
        pl.BlockSpec((tm, D_MODEL), row),
        _mod_spec(None if rows_per_cond is None else rows_per_cond // tm),
        pl.BlockSpec((1, D_MODEL), full),
        pl.BlockSpec(w_in.shape, full),
        pl.BlockSpec((1, MLA_Q_LORA), full),
        pl.BlockSpec(w_uq.shape, full),
        pl.BlockSpec((1, MLA_KV_LORA), full),
    ]
    args = [x, mod, g, w_in, qg, w_uq, kvg]
    if rope:
        nblk = rope_tabs[0].shape[0] // tm
        pos = lambda i: (i % nblk, 0)
        for t in rope_tabs:
            in_specs.append(pl.BlockSpec((tm, t.shape[1]), pos))
            args.append(t)
    widths = (MLA_HEADS * MLA_NOPE, MLA_HEADS * MLA_ROPE, MLA_KV_LORA, MLA_ROPE, LANES)
    dtypes = (BF16, BF16, F32, F32, BF16)
    return pl.pallas_call(
        functools.partial(_premix_mla_kernel, rope=rope),
        grid=(n // tm,),
        in_specs=in_specs,
        out_specs=[pl.BlockSpec((tm, w), row) for w in widths],
        out_shape=[jax.ShapeDtypeStruct((n, w), d) for w, d in zip(widths, dtypes)],
        compiler_params=_cparams("parallel"),
        name="premix_mla_rope" if rope else "premix_mla",
    )(*args)


LOG2E = 1.4426950408889634


def _softmax_pv(s_list, v_list, scale=1.0):
    m = functools.reduce(jnp.maximum, [jnp.max(s, axis=-1, keepdims=True) for s in s_list])
    e_list = [jnp.exp2((s - m) * (scale * LOG2E)) for s in s_list]
    l = functools.reduce(lambda a, b: a + b, [jnp.sum(e, axis=-1, keepdims=True) for e in e_list])
    o = functools.reduce(lambda a, b: a + b,
                         [_dot(e.astype(BF16), v) for e, v in zip(e_list, v_list)])
    return o / l


def _stack_heads(q, masks):
    return jnp.concatenate([q * m for m in masks], axis=0)


def _unstack_heads(o2, width):
    tq = o2.shape[0] // 2
    lane = lax.broadcasted_iota(jnp.int32, (1, LANES), 1)
    return jnp.where(lane < width, o2[:tq], o2[tq:])


def _mla_attn_kernel(*refs, cached):
    if cached:
        (qn_ref, qr_ref, ckv_ref, kr4_ref, cckv_ref, ckr4_ref, w_uk_ref, w_uv_ref,
         o_ref, kn_s, v_s, kr_s) = refs
    else:
        (qn_ref, qr_ref, ckv_ref, kr4_ref, w_uk_ref, w_uv_ref, o_ref, kn_s, v_s, kr_s) = refs
    t_own = ckv_ref.shape[0]

    @pl.when(pl.program_id(1) == 0)
    def _():
        c = ckv_ref[...].astype(BF16)
        kn_s[0:t_own, :] = _dot(c, w_uk_ref[...]).astype(BF16)
        v_s[0:t_own, :] = _dot(c, w_uv_ref[...]).astype(BF16)
        kr_s[0:t_own, :] = kr4_ref[...]
        if cached:
            cc = cckv_ref[0].astype(BF16)
            kn_s[t_own:, :] = _dot(cc, w_uk_ref[...]).astype(BF16)
            v_s[t_own:, :] = _dot(cc, w_uv_ref[...]).astype(BF16)
            kr_s[t_own:, :] = ckr4_ref[0].astype(BF16)

    kr4 = kr_s[...]
    for p in range(HEAD_PAIRS):
        sl = slice(p * LANES, (p + 1) * LANES)
        qn = qn_ref[:, sl]
        qr = qr_ref[:, (p // 2) * LANES:(p // 2 + 1) * LANES]
        k_cat = jnp.concatenate([kn_s[:, sl], kr4], axis=-1)
        q_cat = jnp.concatenate(
            [_stack_heads(qn, [_lane_mask(MLA_NOPE, i, BF16) for i in range(2)]),
             _stack_heads(qr, [_lane_mask(MLA_ROPE, (2 * p + i) % 4, BF16) for i in range(2)])], axis=-1)
        o2 = _softmax_pv([_dot_nt(q_cat, k_cat)], [v_s[:, sl]], MLA_SCALE)
        o_ref[:, sl] = _unstack_heads(o2, MLA_V).astype(BF16)


def _mla_attention(qn, qr, ckv, kr4, cache_ckv, cache_kr4, w_uk, w_uv, n_batch, tq):
    n = qn.shape[0]
    s_own = n // n_batch
    nq = s_own // tq
    cached = cache_ckv is not None
    t_all = s_own + (cache_ckv.shape[1] if cached else 0)
    qrow = lambda b, j: (b * nq + j, 0)
    own = lambda b, j: (b, 0)
    full = lambda b, j: (0, 0)
    in_specs = [
        pl.BlockSpec((tq, qn.shape[1]), qrow),
        pl.BlockSpec((tq, qr.shape[1]), qrow),
        pl.BlockSpec((s_own, MLA_KV_LORA), own),
        pl.BlockSpec((s_own, LANES), own),
    ]
    args = [qn, qr, ckv, kr4]
    if cached:
        in_specs += [pl.BlockSpec((1,) + cache_ckv.shape[1:], lambda b, j: (b, 0, 0)),
                     pl.BlockSpec((1,) + cache_kr4.shape[1:], lambda b, j: (b, 0, 0))]
        args += [cache_ckv, cache_kr4]
    in_specs += [pl.BlockSpec(w_uk.shape, full), pl.BlockSpec(w_uv.shape, full)]
    args += [w_uk, w_uv]
    return pl.pallas_call(
        functools.partial(_mla_attn_kernel, cached=cached),
        grid=(n_batch, nq),
        in_specs=in_specs,
        out_specs=pl.BlockSpec((tq, D_MODEL), qrow),
        out_shape=jax.ShapeDtypeStruct((n, D_MODEL), BF16),
        scratch_shapes=[pltpu.VMEM((t_all, D_MODEL), BF16), pltpu.VMEM((t_all, D_MODEL), BF16),
                        pltpu.VMEM((t_all, LANES), BF16)],
        compiler_params=_cparams("parallel", "arbitrary"),
        name="mla_attention_cached" if cached else "mla_attention",
    )(*args)


def _premix_nat_kernel(x_ref, mod_ref, g_ref, w_ref, q_ref, k_ref, v_ref):
    h = _rms(x_ref[...], g_ref[...]) * (1.0 + mod_ref[0, 1:2, :]) + mod_ref[0, 0:1, :]
    qkv = _dot(h.astype(BF16), w_ref[...])
    q_ref[...] = qkv[:, :D_MODEL].astype(q_ref.dtype)
    k_ref[...] = qkv[:, D_MODEL:2 * D_MODEL].astype(k_ref.dtype)
    v_ref[...] = qkv[:, 2 * D_MODEL:].astype(v_ref.dtype)


def _premix_nat(x, mod, g, w_qkv, rows_per_cond, tm, kv_dtype):
    n = x.shape[0]
    row = lambda i: (i, 0)
    full = lambda i: (0, 0)
    return pl.pallas_call(
        _premix_nat_kernel,
        grid=(n // tm,),
        in_specs=[
            pl.BlockSpec((tm, D_MODEL), row),
            _mod_spec(None if rows_per_cond is None else rows_per_cond // tm),
            pl.BlockSpec((1, D_MODEL), full),
            pl.BlockSpec(w_qkv.shape, full),
        ],
        out_specs=[pl.BlockSpec((tm, D_MODEL), row)] * 3,
        out_shape=[jax.ShapeDtypeStruct((n, D_MODEL), BF16),
                   jax.ShapeDtypeStruct((n, D_MODEL), kv_dtype),
                   jax.ShapeDtypeStruct((n, D_MODEL), kv_dtype)],
        compiler_params=_cparams("parallel"),
        name="premix_nat",
    )(x, mod, g, w_qkv)


def _premix_nat_cache_kernel(*refs, first):
    if first:
        x_ref, mod_ref, g_ref, w_ref, q_ref, k_ref, v_ref = refs
    else:
        x_ref, mod_ref, g_ref, w_ref, _, _, q_ref, k_ref, v_ref = refs
    h = _rms(x_ref[...], g_ref[...]) * (1.0 + mod_ref[0, 1:2, :]) + mod_ref[0, 0:1, :]
    qkv = _dot(h.astype(BF16), w_ref[...])
    q_ref[...] = qkv[:, :D_MODEL].astype(BF16)
    nb, n_slots, s, _ = k_ref.shape
    k_ref[:, 0] = qkv[:, D_MODEL:2 * D_MODEL].reshape(nb, s, D_MODEL)
    v_ref[:, 0] = qkv[:, 2 * D_MODEL:].reshape(nb, s, D_MODEL)
    for slot in range(1, n_slots):
        k_ref[:, slot] = jnp.zeros((nb, s, D_MODEL), F32)
        v_ref[:, slot] = jnp.zeros((nb, s, D_MODEL), F32)


def _premix_nat_cache(x, mod, g, w_qkv, k_buf, v_buf, slot, n_slots, n_batch, tm):
    n = x.shape[0]
    s = n // n_batch
    nb = tm // s
    first = k_buf is None
    row = lambda i: (i, 0)
    full = lambda i: (0, 0)
    in_specs = [pl.BlockSpec((tm, D_MODEL), row), _mod_spec(None), pl.BlockSpec((1, D_MODEL), full),
                pl.BlockSpec(w_qkv.shape, full)]
    args = [x, mod, g, w_qkv]
    if first:
        assert slot == 0
        kv_spec = pl.BlockSpec((nb, n_slots, s, D_MODEL), lambda i: (i, 0, 0, 0))
        aliases = {}
    else:
        in_specs += [pl.BlockSpec(memory_space=pl.ANY)] * 2
        args += [k_buf, v_buf]
        kv_spec = pl.BlockSpec((nb, 1, s, D_MODEL), lambda i: (i, slot, 0, 0))
        aliases = {4: 1, 5: 2}
    kv_shape = jax.ShapeDtypeStruct((n_batch, n_slots, s, D_MODEL), F32)
    return pl.pallas_call(
        functools.partial(_premix_nat_cache_kernel, first=first),
        grid=(n // tm,),
        in_specs=in_specs,
        out_specs=[pl.BlockSpec((tm, D_MODEL), row), kv_spec, kv_spec],
        out_shape=[jax.ShapeDtypeStruct((n, D_MODEL), BF16), kv_shape, kv_shape],
        input_output_aliases=aliases,
        compiler_params=_cparams("parallel"),
        name="premix_nat_cache",
    )(*args)


def _dense_attn_kernel(q_ref, k_ref, v_ref, o_ref):
    for p in range(HEAD_PAIRS):
        sl = slice(p * LANES, (p + 1) * LANES)
        q2 = _stack_heads(q_ref[:, sl], [_lane_mask(NAT_DH, i, BF16) for i in range(2)])
        k = k_ref[:, sl].astype(BF16)
        v = v_ref[:, sl].astype(BF16)
        o2 = _softmax_pv([_dot_nt(q2, k)], [v], NAT_SCALE)
        o_ref[:, sl] = _unstack_heads(o2, NAT_DH).astype(BF16)


def _dense_attention(q, k_buf, v_buf, slot):
    n = q.shape[0]
    n_batch, _, s, _ = k_buf.shape
    blk = pl.BlockSpec((s, D_MODEL), lambda b: (b, 0))
    kv = pl.BlockSpec((None, None, s, D_MODEL), lambda b: (b, slot, 0, 0))
    return pl.pallas_call(
        _dense_attn_kernel,
        grid=(n_batch,),
        in_specs=[blk, kv, kv],
        out_specs=blk,
        out_shape=jax.ShapeDtypeStruct((n, D_MODEL), BF16),
        compiler_params=_cparams("parallel"),
        name="dense_attention",
    )(q, k_buf, v_buf)


_NAT_QBLK = NAT_QROWS * GRID_W
_NAT_KBLK = NAT_KROWS * GRID_W


def _nat_block_plan(rows):
    assert rows % NAT_QROWS == 0 and rows >= NAT_KROWS and NAT_KROWS % 2 == 0
    plan, variants = [], []
    for r0 in range(0, rows, NAT_QROWS):
        ks = min(max(r0 - WIN_H // 2, 0), rows - NAT_KROWS)
        r = r0 + np.arange(NAT_QROWS)
        kr = ks + np.arange(NAT_KROWS)
        rs = np.clip(r - WIN_H // 2, 0, rows - WIN_H)
        valid_row = (kr[None, :] >= rs[:, None]) & (kr[None, :] < rs[:, None] + WIN_H)
        d0 = ks - r + (WIN_H - 1)
        for vi, (d0_v, valid_v) in enumerate(variants):
            if np.array_equal(d0, d0_v) and np.array_equal(valid_row, valid_v):
                break
        else:
            vi = len(variants)
            variants.append((d0, valid_row))
        plan.append((ks, vi))
    return plan, variants


def _nat_bias_row_range(variants):
    lo = min(int(d0.min()) for d0, _ in variants)
    hi = max(int(d0.max()) for d0, _ in variants) + NAT_KROWS
    return lo, hi


def _nat_bias_pairs(rpb, rows):
    n_heads, n_dr, n_dc = rpb.shape
    _, variants = _nat_block_plan(rows)
    lo, hi = _nat_bias_row_range(variants)
    c = np.arange(GRID_W)
    cs = np.clip(c - WIN_W // 2, 0, GRID_W - WIN_W)
    valid_col = (c[None, :] >= cs[:, None]) & (c[None, :] < cs[:, None] + WIN_W)
    d_col = c[None, :] - c[:, None] + (WIN_W - 1)
    sel = (d_col[None] == np.arange(n_dc)[:, None, None]) & valid_col[None]
    toep = jnp.einsum('hdj,jck->hdck', rpb, jnp.asarray(sel, F32), precision=lax.Precision.HIGHEST)
    toep = jnp.where(jnp.asarray(valid_col)[None, None], toep, NEG_INF)
    toep = jnp.pad(toep, ((0, 0), (max(-lo, 0), max(hi + 1 - n_dr, 0)), (0, 0), (0, 0)),
                   constant_values=NEG_INF)
    toep = toep[:, max(lo, 0):]
    return jnp.concatenate([toep[:, :-1], toep[:, 1:]], axis=-1)


def _nat_block_bias(tp_ref, head, d0, valid_row, row_lo):
    neg = jnp.full((GRID_W, 2 * GRID_W), NEG_INF, F32)
    left = lax.broadcasted_iota(jnp.int32, (1, 2 * GRID_W), 1) < GRID_W
    rows_out = []
    for dr in range(NAT_QROWS):
        pieces = []
        for a in range(0, NAT_KROWS, 2):
            ok0, ok1 = bool(valid_row[dr, a]), bool(valid_row[dr, a + 1])
            if not (ok0 or ok1):
                pieces.append(neg)
                continue
            piece = tp_ref[head, int(d0[dr]) + a - row_lo]
            if ok0 and not ok1:
                piece = jnp.where(left, piece, NEG_INF)
            elif ok1 and not ok0:
                piece = jnp.where(left, NEG_INF, piece)
            pieces.append(piece)
        rows_out.append(jnp.concatenate(pieces, axis=-1))
    return jnp.concatenate(rows_out, axis=0)


def _nat_attn_kernel(q_ref, k_ref, v_ref, kc_ref, vc_ref, tp_ref, o_ref, *, plan, variants, row_lo):
    kc = kc_ref[0].astype(BF16)
    vc = vc_ref[0].astype(BF16)
    assert NAT_SCALE == 2.0 ** round(np.log2(NAT_SCALE))
    masks = [_lane_mask(NAT_DH, i, BF16) * NAT_SCALE for i in range(2)]
    for bi, (ks, var) in enumerate(plan):
        q2 = _stack_heads(q_ref[bi * _NAT_QBLK:(bi + 1) * _NAT_QBLK, :], masks)
        k = k_ref[ks * GRID_W:ks * GRID_W + _NAT_KBLK, :]
        v = v_ref[ks * GRID_W:ks * GRID_W + _NAT_KBLK, :]
        bias2 = jnp.concatenate([_nat_block_bias(tp_ref, i, *variants[var], row_lo) for i in range(2)], axis=0)
        o2 = _softmax_pv([_dot_nt(q2, k) + bias2, _dot_nt(q2, kc)], [v, vc])
        o_ref[bi * _NAT_QBLK:(bi + 1) * _NAT_QBLK, :] = _unstack_heads(o2, NAT_DH).astype(BF16)


def _nat_attention(q, k, v, cache_k, cache_v, bias, n_batch):
    n = q.shape[0]
    s = n // n_batch
    plan, variants = _nat_block_plan(s // GRID_W)
    row_lo, _ = _nat_bias_row_range(variants)
    own = pl.BlockSpec((s, LANES), lambda p, b: (b, p))
    cache = pl.BlockSpec((1, cache_k.shape[1], LANES), lambda p, b: (b, 0, p))
    return pl.pallas_call(
        functools.partial(_nat_attn_kernel, plan=plan, variants=variants, row_lo=row_lo),
        grid=(HEAD_PAIRS, n_batch),
        in_specs=[own, own, own, cache, cache,
                  pl.BlockSpec((2,) + bias.shape[1:], lambda p, b: (p, 0, 0, 0))],
        out_specs=own,
        out_shape=jax.ShapeDtypeStruct((n, D_MODEL), BF16),
        compiler_params=_cparams("parallel", "parallel"),
        name="nat_attention",
    )(q, k, v, cache_k, cache_v, bias)


def _proj_res_kernel(x_ref, a_ref, w_ref, mod_ref, o_ref):
    o_ref[...] = x_ref[...] + mod_ref[0, 2:3, :] * _dot(a_ref[...], w_ref[...])


def _proj_residual(x, a, w, mod, rows_per_cond, tm):
    n = x.shape[0]
    row = lambda i: (i, 0)
    return pl.pallas_call(
        _proj_res_kernel,
        grid=(n // tm,),
        in_specs=[
            pl.BlockSpec((tm, D_MODEL), row),
            pl.BlockSpec((tm, D_MODEL), row),
            pl.BlockSpec(w.shape, lambda i: (0, 0)),
            _mod_spec(None if rows_per_cond is None else rows_per_cond // tm),
        ],
        out_specs=pl.BlockSpec((tm, D_MODEL), row),
        out_shape=jax.ShapeDtypeStruct((n, D_MODEL), F32),
        compiler_params=_cparams("parallel"),
        name="proj_residual",
    )(x, a, w, mod)


def _top2_sum(a, b, c, d):
    hi1, lo1 = jnp.maximum(a, b), jnp.minimum(a, b)
    hi2, lo2 = jnp.maximum(c, d), jnp.minimum(c, d)
    return jnp.maximum(hi1, hi2) + jnp.maximum(jnp.minimum(hi1, hi2), jnp.maximum(lo1, lo2))


def _route(scores, biased):
    sc = [scores[e:e + 1, :] for e in range(N_EXPERTS)]
    bs = [biased[e:e + 1, :] for e in range(N_EXPERTS)]
    gscore = [_top2_sum(*bs[EPG * g:EPG * (g + 1)]) for g in range(N_GROUPS)]
    best, gidx = gscore[0], jnp.zeros_like(gscore[0], dtype=jnp.int32)
    for g in range(1, N_GROUPS):
        better = gscore[g] > best
        gidx = jnp.where(better, g, gidx)
        best = jnp.where(better, gscore[g], best)
    cb = [functools.reduce(lambda a, b: a + b,
                           [jnp.where(gidx == g, bs[EPG * g + i], 0.0) for g in range(N_GROUPS)])
          for i in range(EPG)]
    cs = [functools.reduce(lambda a, b: a + b,
                           [jnp.where(gidx == g, sc[EPG * g + i], 0.0) for g in range(N_GROUPS)])
          for i in range(EPG)]
    b1, i1 = cb[0], jnp.zeros_like(gidx)
    for i in range(1, EPG):
        better = cb[i] > b1
        i1 = jnp.where(better, i, i1)
        b1 = jnp.where(better, cb[i], b1)
    b2, i2 = jnp.full_like(b1, -jnp.inf), jnp.full_like(i1, -1)
    for i in range(EPG):
        better = (i1 != i) & (cb[i] > b2)
        i2 = jnp.where(better, i, i2)
        b2 = jnp.where(better, cb[i], b2)
    sel = [(i1 == i) | (i2 == i) for i in range(EPG)]
    w = [jnp.where(sel[i], cs[i], 0.0) for i in range(EPG)]
    tot = w[0] + w[1] + w[2] + w[3]
    return gidx, [w[i] / tot for i in range(EPG)]


MOE_SRC = 256
MOE_DST = 256
MOE_CHUNK = 4
_LOG2_DST = MOE_DST.bit_length() - 1
assert MOE_DST == 1 << _LOG2_DST and MOE_SRC <= MOE_DST
_INFO_GID = EPG
_INFO_RANK = EPG + 1


def _moe_route_kernel(x_ref, mod_ref, g_ref, wr_ref, rb_ref, h_ref, info_ref, infot_ref, cum_ref, carry_s):
    @pl.when(pl.program_id(0) == 0)
    def _():
        carry_s[...] = jnp.zeros_like(carry_s)

    h = _rms(x_ref[...], g_ref[...]) * (1.0 + mod_ref[0, 4:5, :]) + mod_ref[0, 3:4, :]
    h_hi = h.astype(BF16)
    h_ref[:, :D_MODEL] = h_hi
    h_lo = (h - h_hi.astype(F32)).astype(BF16)
    logits = _dot(h_hi, wr_ref[0]) + (_dot(h_lo, wr_ref[0]) + _dot(h_hi, wr_ref[1]))
    scores = _sigmoid(logits)
    gidx, gates = _route(scores.T[:N_EXPERTS], (scores + rb_ref[...]).T[:N_EXPERTS])
    tm = h.shape[0]
    sub = lax.broadcasted_iota(jnp.int32, (8, 1), 0)
    onehot = (sub == gidx).astype(F32)
    tri = (lax.broadcasted_iota(jnp.int32, (tm, tm), 0)
           <= lax.broadcasted_iota(jnp.int32, (tm, tm), 1)).astype(BF16)
    prefix = _dot(onehot.astype(BF16), tri)
    carry = carry_s[:, 0:1]
    rank = jnp.sum((prefix - 1.0 + carry) * onehot, axis=0, keepdims=True)
    infot = jnp.concatenate(gates + [gidx.astype(F32), rank, jnp.zeros((2, tm), F32)], axis=0)
    infot_ref[0] = infot
    info_ref[...] = jnp.concatenate([infot, jnp.zeros((LANES - 8, tm), F32)], axis=0).T
    g_hi = [gt.astype(BF16).astype(F32) for gt in gates]
    g_lo = [gt - gh for gt, gh in zip(gates, g_hi)]
    gext = jnp.concatenate(g_hi + g_lo + [jnp.zeros((LANES - 2 * EPG, tm), F32)], axis=0)
    h_ref[:, D_MODEL:] = gext.T.astype(BF16)
    carry = jnp.broadcast_to(carry + prefix[:, tm - 1:tm], (8, LANES))
    carry_s[...] = carry
    cum_ref[0] = carry.astype(jnp.int32)


def _group_tiles(cum_ref):
    n_src = cum_ref.shape[0]
    tot = [cum_ref[n_src - 1, g] for g in range(N_GROUPS)]
    ends, acc = [], 0
    for t in tot:
        acc = acc + lax.shift_right_logical(t + (MOE_DST - 1), _LOG2_DST)
        ends.append(acc)
    return tot, ends


def _dest_tile(d, cum_ref):
    tot, ends = _group_tiles(cum_ref)
    g = ((d >= ends[0]).astype(jnp.int32) + (d >= ends[1]).astype(jnp.int32)
         + (d >= ends[2]).astype(jnp.int32))
    first = jnp.where(g == 0, 0, jnp.where(g == 1, ends[0], jnp.where(g == 2, ends[1], ends[2])))
    tot_g = jnp.where(g == 0, tot[0], jnp.where(g == 1, tot[1], jnp.where(g == 2, tot[2], tot[3])))
    k0 = (d - first) * MOE_DST
    n_valid = jnp.clip(tot_g - k0, 0, MOE_DST)
    return g, k0, n_valid


def _counts_before(cum_ref, s, g):
    return jnp.where(s > 0, cum_ref[jnp.maximum(s - 1, 0), g], 0)


def _moe_ffn_kernel(cum_ref, h_ref, infot_ref, wg_ref, wu_ref, wd_ref, y_ref, xs_s, ptr_s):
    d = pl.program_id(0)
    g, k0, n_valid = _dest_tile(d, cum_ref)
    n_src = cum_ref.shape[0]

    @pl.when(d == 0)
    def _():
        ptr_s[0] = 0

    @pl.when(n_valid == 0)
    def _():
        y_ref[...] = jnp.zeros_like(y_ref)

    @pl.when(n_valid > 0)
    def _():
        need = k0 + n_valid
        start = jnp.where(k0 == 0, 0, ptr_s[0])
        s_lo = lax.while_loop(lambda s: cum_ref[s, g] <= k0, lambda s: s + 1, start)
        ptr_s[0] = s_lo
        xs_s[...] = jnp.zeros_like(xs_s)
        row = lax.broadcasted_iota(jnp.int32, (MOE_DST, 1), 0).astype(F32)
        gf = g.astype(F32)
        k0f = k0.astype(F32)

        def more(s0):
            return (_counts_before(cum_ref, s0, g) < need) & (s0 < n_src)

        def gather(s0):
            base = jnp.minimum(s0, n_src - MOE_CHUNK)
            perms = []
            for j in range(MOE_CHUNK):
                it = infot_ref[base + j]
                shift = k0f + jnp.where(base + j >= s0, 0.0, 1e9)
                rel = jnp.where(it[_INFO_GID:_INFO_GID + 1] == gf,
                                it[_INFO_RANK:_INFO_RANK + 1] - shift, -1.0)
                perms.append((row == rel).astype(BF16))
            perm = jnp.concatenate(perms, axis=-1)
            rows = pl.ds(pl.multiple_of(base * MOE_SRC, MOE_SRC), MOE_CHUNK * MOE_SRC)
            xs_s[...] += _dot(perm, h_ref[rows, :])
            return base + MOE_CHUNK

        lax.while_loop(more, gather, s_lo)
        xs = xs_s[:, :D_MODEL].astype(BF16)
        gext = xs_s[:, D_MODEL:]
        acts = []
        for i in range(EPG):
            a = _dot(xs, wg_ref[i])
            u = _dot(xs, wu_ref[i])
            gate = gext[:, i:i + 1] + gext[:, EPG + i:EPG + i + 1]
            acts.append(((a * _sigmoid(a)) * u * gate).astype(BF16))
        act = jnp.concatenate(acts, axis=-1)
        y_ref[...] = _dot(act, wd_ref[...].reshape(EPG * D_EXPERT, D_MODEL)).astype(BF16)


def _moe_unsort_kernel(cum_ref, x_ref, mod_ref, info_ref, y_ref, o_ref):
    s = pl.program_id(0)
    n_dst = y_ref.shape[0] // MOE_DST
    _, ends = _group_tiles(cum_ref)
    firsts = [0] + ends[:-1]
    inf = info_ref[...]
    gid = inf[:, _INFO_GID:_INFO_GID + 1]
    rank = inf[:, _INFO_RANK:_INFO_RANK + 1]
    col = lax.broadcasted_iota(jnp.int32, (1, MOE_DST), 1).astype(F32)
    acc = None
    for g in range(N_GROUPS):
        k_lo = lax.shift_right_logical(_counts_before(cum_ref, s, g), _LOG2_DST)
        for j in range(2):
            k = k_lo + j
            dt = jnp.minimum(firsts[g] + k, n_dst - 1)
            rel = jnp.where(gid == float(g), rank - (k * MOE_DST).astype(F32), -1.0)
            perm = (rel == col).astype(BF16)
            rows = pl.ds(pl.multiple_of(dt * MOE_DST, MOE_DST), MOE_DST)
            part = _dot(perm, y_ref[rows, :])
            acc = part if acc is None else acc + part
    o_ref[...] = x_ref[...] + mod_ref[0, 5:6, :] * acc


def _moe(x, mod, g, w_router, router_bias, wg, wu, wd, layer, rows_per_cond):
    n = x.shape[0]
    n_src = n // MOE_SRC
    n_dst = n // MOE_DST + N_GROUPS
    row = lambda i: (i, 0)
    full = lambda i: (0, 0)
    h, info, infot, cum = pl.pallas_call(
        _moe_route_kernel,
        grid=(n_src,),
        in_specs=[
            pl.BlockSpec((MOE_SRC, D_MODEL), row),
            _mod_spec(None if rows_per_cond is None else rows_per_cond // MOE_SRC),
            pl.BlockSpec((1, D_MODEL), full),
            pl.BlockSpec(w_router.shape, lambda i: (0, 0, 0)),
            pl.BlockSpec(router_bias.shape, full),
        ],
        out_specs=[pl.BlockSpec((MOE_SRC, D_MODEL + LANES), row),
                   pl.BlockSpec((MOE_SRC, LANES), row),
                   pl.BlockSpec((1, 8, MOE_SRC), lambda i: (i, 0, 0)),
                   pl.BlockSpec((1, 8, LANES), lambda i: (i, 0, 0))],
        out_shape=[jax.ShapeDtypeStruct((n, D_MODEL + LANES), BF16),
                   jax.ShapeDtypeStruct((n, LANES), F32),
                   jax.ShapeDtypeStruct((n_src, 8, MOE_SRC), F32),
                   jax.ShapeDtypeStruct((n_src, 8, LANES), jnp.int32)],
        scratch_shapes=[pltpu.VMEM((8, LANES), F32)],
        compiler_params=_cparams("arbitrary"),
        name="moe_route",
    )(x, mod, g, w_router, router_bias)
    cum = cum[:, :N_GROUPS, 0]

    resident = lambda a: pl.BlockSpec(a.shape, lambda d, c: (0,) * a.ndim, pipeline_mode=pl.Buffered(1))
    expert = lambda shape: pl.BlockSpec(shape, lambda d, c: (layer, _dest_tile(d, c)[0], 0, 0))
    y = pl.pallas_call(
        _moe_ffn_kernel,
        grid_spec=pltpu.PrefetchScalarGridSpec(
            num_scalar_prefetch=1,
            grid=(n_dst,),
            in_specs=[resident(h), resident(infot),
                      expert((None, EPG, D_MODEL, D_EXPERT)),
                      expert((None, EPG, D_MODEL, D_EXPERT)),
                      expert((None, EPG, D_EXPERT, D_MODEL))],
            out_specs=pl.BlockSpec((MOE_DST, D_MODEL), lambda d, c: (d, 0)),
            scratch_shapes=[pltpu.VMEM((MOE_DST, D_MODEL + LANES), F32), pltpu.SMEM((1,), jnp.int32)],
        ),
        out_shape=jax.ShapeDtypeStruct((n_dst * MOE_DST, D_MODEL), BF16),
        compiler_params=_cparams("arbitrary"),
        name="moe_experts",
    )(cum, h, infot, wg, wu, wd)

    return pl.pallas_call(
        _moe_unsort_kernel,
        grid_spec=pltpu.PrefetchScalarGridSpec(
            num_scalar_prefetch=1,
            grid=(n_src,),
            in_specs=[pl.BlockSpec((MOE_SRC, D_MODEL), lambda i, c: (i, 0)),
                      _mod_spec(None if rows_per_cond is None else rows_per_cond // MOE_SRC),
                      pl.BlockSpec((MOE_SRC, LANES), lambda i, c: (i, 0)),
                      resident(y)],
            out_specs=pl.BlockSpec((MOE_SRC, D_MODEL), lambda i, c: (i, 0)),
        ),
        out_shape=jax.ShapeDtypeStruct((n, D_MODEL), F32),
        compiler_params=_cparams("parallel"),
        name="moe_unsort",
    )(cum, x, mod, info, y)


def _final_norm_kernel(x_ref, g_ref, o_ref):
    o_ref[...] = _rms(x_ref[...], g_ref[...])


def _final_norm(x, g, tm):
    n = x.shape[0]
    return pl.pallas_call(
        _final_norm_kernel,
        grid=(n // tm,),
        in_specs=[pl.BlockSpec((tm, D_MODEL), lambda i: (i, 0)),
                  pl.BlockSpec((1, D_MODEL), lambda i: (0, 0))],
        out_specs=pl.BlockSpec((tm, D_MODEL), lambda i: (i, 0)),
        out_shape=jax.ShapeDtypeStruct((n, D_MODEL), F32),
        compiler_params=_cparams("parallel"),
        name="final_norm",
    )(x, g)


def _rope_tables(n_lat):
    t = np.arange(n_lat)
    n_freq = MLA_ROPE // 4
    inv_freq = jnp.asarray(ROPE_THETA, F32) ** (-jnp.arange(n_freq, dtype=F32) / n_freq)
    ar = jnp.asarray(t // GRID_W, F32)[:, None] * inv_freq
    ac = jnp.asarray(t % GRID_W, F32)[:, None] * inv_freq
    cos = jnp.concatenate([jnp.cos(ar), jnp.cos(ar), jnp.cos(ac), jnp.cos(ac)], axis=-1)
    sin = jnp.concatenate([-jnp.sin(ar), jnp.sin(ar), -jnp.sin(ac), jnp.sin(ac)], axis=-1)
    return (jnp.tile(cos, (1, MLA_HEADS)), jnp.tile(sin, (1, MLA_HEADS)),
            jnp.tile(cos, (1, LANES // MLA_ROPE)), jnp.tile(sin, (1, LANES // MLA_ROPE)))


_ROPE_SWAP = np.concatenate([np.arange(8, 16), np.arange(0, 8), np.arange(24, 32), np.arange(16, 24)])


def _mla_weights(w_in, w_uq, w_ukv):
    o = MLA_Q_LORA + MLA_KV_LORA
    kr = w_in[:, o:]
    rep = LANES // MLA_ROPE
    w_in_x = jnp.concatenate([w_in[:, :o], jnp.tile(kr, (1, rep)), jnp.tile(kr[:, _ROPE_SWAP], (1, rep))],
                             axis=-1).astype(BF16)
    uq = w_uq.reshape(MLA_Q_LORA, MLA_HEADS, MLA_NOPE + MLA_ROPE)
    q_nope = uq[:, :, :MLA_NOPE].reshape(MLA_Q_LORA, -1)
    q_rope = uq[:, :, MLA_NOPE:]
    w_uq_x = jnp.concatenate([q_nope, q_rope.reshape(MLA_Q_LORA, -1),
                              q_rope[:, :, _ROPE_SWAP].reshape(MLA_Q_LORA, -1)], axis=-1).astype(BF16)
    ukv = w_ukv.reshape(MLA_KV_LORA, MLA_HEADS, MLA_NOPE + MLA_V)
    w_uk = ukv[:, :, :MLA_NOPE].reshape(MLA_KV_LORA, -1).astype(BF16)
    w_uv = ukv[:, :, MLA_NOPE:].reshape(MLA_KV_LORA, -1).astype(BF16)
    return w_in_x, w_uq_x, w_uk, w_uv


def kernel(x_prompt, x_sample, cache_mla_ckv, cache_mla_krope, cache_nat_k, cache_nat_v, c, c_ctx,
           w_ada, b_ada, norm_mix, norm_ffn, norm_final, mla_w_in, mla_q_norm, mla_w_uq, mla_kv_norm,
           mla_w_ukv, mla_w_o, nat_w_qkv, nat_rpb, nat_w_o, w_router, router_bias,
           moe_w_gate, moe_w_up, moe_w_down):
    B, S, D = x_prompt.shape
    Bd, Sd, _ = x_sample.shape
    assert D == D_MODEL and Bd + 1 <= 8
    tm_c, tm_s = 512, 512

    xc = x_prompt.reshape(B * S, D)
    xs = x_sample.reshape(Bd * Sd, D)
    cond8 = jnp.concatenate([c_ctx[None, :], c, jnp.zeros((8 - 1 - Bd, D), F32)], axis=0)
    mod_all = _ada_modulation(cond8, w_ada, b_ada).reshape(DEPTH, 8, 6, D)

    rope_tabs = _rope_tables(Sd)
    wr = jnp.pad(w_router, ((0, 0), (0, LANES - N_EXPERTS)))
    wr_hi = wr.astype(BF16)
    wr = jnp.stack([wr_hi, (wr - wr_hi.astype(F32)).astype(BF16)], axis=0)
    rb = jnp.pad(router_bias, (0, LANES - N_EXPERTS)).reshape(1, LANES)
    wg = moe_w_gate.astype(BF16)
    wu = moe_w_up.astype(BF16)
    wd = moe_w_down.astype(BF16)

    new_ckv, new_krope = [], []
    k_buf = v_buf = None
    for layer in range(DEPTH):
        mod = mod_all[layer]
        g_mix = norm_mix[layer][None, :]
        j = layer // 2
        if layer % 2 == 0:
            w_in_x, w_uq_x, w_uk, w_uv = _mla_weights(mla_w_in[j], mla_w_uq[j], mla_w_ukv[j])
            w_in_c = w_in_x[:, :MLA_Q_LORA + MLA_KV_LORA + LANES]
            w_uq_c = w_uq_x[:, :MLA_HEADS * (MLA_NOPE + MLA_ROPE)]
            qg = mla_q_norm[j][None, :]
            kvg = mla_kv_norm[j][None, :]
            w_o = mla_w_o[j].astype(BF16)
            qn, qr, ckv, kr, kr4 = _premix_mla(xc, mod, g_mix, w_in_c, qg, w_uq_c, kvg, None, None, tm_c)
            new_ckv.append(ckv.reshape(B, S, MLA_KV_LORA))
            new_krope.append(kr.reshape(B, S, MLA_ROPE))
            att_c = _mla_attention(qn, qr, ckv, kr4, None, None, w_uk, w_uv, B, S)
            qn, qr, ckv, kr, kr4 = _premix_mla(xs, mod, g_mix, w_in_x, qg, w_uq_x, kvg, rope_tabs, Sd, tm_s)
            cache_kr4 = jnp.tile(cache_mla_krope[:, j], (1, 1, LANES // MLA_ROPE))
            att_s = _mla_attention(qn, qr, ckv, kr4, cache_mla_ckv[:, j], cache_kr4, w_uk, w_uv, Bd, 256)
        else:
            w_qkv = nat_w_qkv[j].astype(BF16)
            w_o = nat_w_o[j].astype(BF16)
            q, k_buf, v_buf = _premix_nat_cache(xc, mod, g_mix, w_qkv, k_buf, v_buf, j, DEPTH // 2, B, tm_c)
            att_c = _dense_attention(q, k_buf, v_buf, j)
            q, k, v = _premix_nat(xs, mod, g_mix, w_qkv, Sd, tm_s, BF16)
            bias = _nat_bias_pairs(nat_rpb[j], Sd // GRID_W)
            att_s = _nat_attention(q, k, v,
                                   cache_nat_k[:, j].reshape(Bd, -1, D), cache_nat_v[:, j].reshape(Bd, -1, D),
                                   bias, Bd)
        xc = _proj_residual(xc, att_c, w_o, mod, None, tm_c)
        xs = _proj_residual(xs, att_s, w_o, mod, Sd, tm_s)
        g_ffn = norm_ffn[layer][None, :]
        xc = _moe(xc, mod, g_ffn, wr, rb, wg, wu, wd, layer, None)
        xs = _moe(xs, mod, g_ffn, wr, rb, wg, wu, wd, layer, Sd)

    gf = norm_final[None, :]
    y_prompt = _final_norm(xc, gf, tm_c).reshape(B, S, D)
    y_sample = _final_norm(xs, gf, tm_s).reshape(Bd, Sd, D)
    kv_shape = (B, DEPTH // 2, S, NAT_HEADS, NAT_DH)
    return (y_prompt, y_sample, jnp.stack(new_ckv, axis=1), jnp.stack(new_krope, axis=1),
            k_buf.reshape(kv_shape), v_buf.reshape(kv_shape))
```

```python
import functools

import numpy as np
import jax
import jax.numpy as jnp
from jax import lax
from jax.experimental import pallas as pl
from jax.experimental.pallas import tpu as pltpu

F32 = jnp.float32
BF16 = jnp.bfloat16

D_MODEL = 1024
DEPTH = 4
GRID_W = 64
LANES = 128
MLA_HEADS = 16
MLA_NOPE = 64
MLA_ROPE = 32
MLA_V = 64
MLA_Q_LORA = 384
MLA_KV_LORA = 256
MLA_SCALE = (MLA_NOPE + MLA_ROPE) ** -0.5
ROPE_THETA = 10000.0
NAT_HEADS = 16
NAT_DH = 64
NAT_SCALE = NAT_DH ** -0.5
WIN_H = 8
WIN_W = 16
NAT_QROWS = 4
NAT_KROWS = NAT_QROWS + WIN_H
N_EXPERTS = 16
N_GROUPS = 4
EPG = N_EXPERTS // N_GROUPS
D_EXPERT = 256
NORM_EPS = 1e-6
NEG_INF = -1e30

HEAD_PAIRS = MLA_HEADS // 2
VMEM_LIMIT = 56 * 1024 * 1024


def _cparams(*sem):
    return pltpu.CompilerParams(dimension_semantics=sem, vmem_limit_bytes=VMEM_LIMIT)


def _sigmoid(x):
    return 1.0 / (1.0 + jnp.exp(-x))


def _rms(x, g):
    ms = jnp.mean(x * x, axis=-1, keepdims=True)
    return x * lax.rsqrt(ms + NORM_EPS) * g


def _dot(a, b):
    return jnp.dot(a, b, preferred_element_type=F32)


def _dot_nt(a, b):
    return lax.dot_general(a, b, (((1,), (1,)), ((), ())), preferred_element_type=F32)


def _lane_mask(width, idx, dtype):
    lane = lax.broadcasted_iota(jnp.int32, (1, LANES), 1)
    return ((lane >= idx * width) & (lane < (idx + 1) * width)).astype(dtype)


def _ada_kernel(cond_ref, w_ref, b_ref, o_ref):
    c = cond_ref[...]
    s = (c * _sigmoid(c)).astype(BF16)
    o_ref[0] = _dot(s, w_ref[0].astype(BF16)) + b_ref[0]


def _ada_modulation(cond8, w_ada, b_ada):
    n_chunk = 6
    return pl.pallas_call(
        _ada_kernel,
        grid=(DEPTH, n_chunk),
        in_specs=[
            pl.BlockSpec((8, D_MODEL), lambda l, j: (0, 0)),
            pl.BlockSpec((1, D_MODEL, D_MODEL), lambda l, j: (l, 0, j)),
            pl.BlockSpec((1, 1, D_MODEL), lambda l, j: (l, 0, j)),
        ],
        out_specs=pl.BlockSpec((1, 8, D_MODEL), lambda l, j: (l, 0, j)),
        out_shape=jax.ShapeDtypeStruct((DEPTH, 8, 6 * D_MODEL), F32),
        compiler_params=_cparams("parallel", "parallel"),
        name="ada_modulation",
    )(cond8, w_ada, b_ada.reshape(DEPTH, 1, 6 * D_MODEL))


def _mod_spec(rows_per_cond):
    if rows_per_cond is None:
        return pl.BlockSpec((1, 6, D_MODEL), lambda i, *_: (0, 0, 0))
    return pl.BlockSpec((1, 6, D_MODEL), lambda i, *_: (1 + i // rows_per_cond, 0, 0))


def _premix_mla_kernel(*refs, rope):
    if rope:
        (x_ref, mod_ref, g_ref, w_in_ref, qg_ref, w_uq_ref, kvg_ref,
         cosq_ref, sinq_ref, cosk_ref, sink_ref,
         qn_ref, qr_ref, ckv_ref, kr_ref, kr4_ref) = refs
    else:
        (x_ref, mod_ref, g_ref, w_in_ref, qg_ref, w_uq_ref, kvg_ref,
         qn_ref, qr_ref, ckv_ref, kr_ref, kr4_ref) = refs
    h = _rms(x_ref[...], g_ref[...]) * (1.0 + mod_ref[0, 1:2, :]) + mod_ref[0, 0:1, :]
    lat = _dot(h.astype(BF16), w_in_ref[...])
    c_q = lat[:, :MLA_Q_LORA]
    c_kv = lat[:, MLA_Q_LORA:MLA_Q_LORA + MLA_KV_LORA]
    o = MLA_Q_LORA + MLA_KV_LORA
    kr4 = lat[:, o:o + LANES]
    q = _dot(_rms(c_q, qg_ref[...]).astype(BF16), w_uq_ref[...])
    n_nope = MLA_HEADS * MLA_NOPE
    n_rope = MLA_HEADS * MLA_ROPE
    qr = q[:, n_nope:n_nope + n_rope]
    if rope:
        qr = qr * cosq_ref[...] + q[:, n_nope + n_rope:] * sinq_ref[...]
        kr4 = kr4 * cosk_ref[...] + lat[:, o + LANES:o + 2 * LANES] * sink_ref[...]
    qn_ref[...] = q[:, :n_nope].astype(BF16)
    qr_ref[...] = qr.astype(BF16)
    ckv_ref[...] = _rms(c_kv, kvg_ref[...])
    kr_ref[...] = kr4[:, :MLA_ROPE]
    kr4_ref[...] = kr4.astype(BF16)


def _premix_mla(x, mod, g, w_in, qg, w_uq, kvg, rope_tabs, rows_per_cond, tm):
    n = x.shape[0]
    rope = rope_tabs is not None
    row = lambda i: (i, 0)
    full = lambda i: (0, 0)
    in_specs = [
        pl.BlockSpec((tm, D_MODEL), row),
        _mod_spec(None if rows_per_cond is None else rows_per_cond // tm),
        pl.BlockSpec((1, D_MODEL), full),
        pl.BlockSpec(w_in.shape, full),
        pl.BlockSpec((1, MLA_Q_LORA), full),
        pl.BlockSpec(w_uq.shape, full),
        pl.BlockSpec((1, MLA_KV_LORA), full),
    ]
    args = [x, mod, g, w_in, qg, w_uq, kvg]
    if rope:
        nblk = rope_tabs[0].shape[0] // tm
        pos = lambda i: (i % nblk, 0)
        for t in rope_tabs:
            in_specs.append(pl.BlockSpec((tm, t.shape[1]), pos))
            args.append(t)
    widths = (MLA_HEADS * MLA_NOPE, MLA_HEADS * MLA_ROPE, MLA_KV_LORA, MLA_ROPE, LANES)
    dtypes = (BF16, BF16, F32, F32, BF16)
    return pl.pallas_call(
        functools.partial(_premix_mla_kernel, rope=rope),
        grid=(n // tm,),
        in_specs=in_specs,
        out_specs=[pl.BlockSpec((tm, w), row) for w in widths],
        out_shape=[jax.ShapeDtypeStruct((n, w), d) for w, d in zip(widths, dtypes)],
        compiler_params=_cparams("parallel"),
        name="premix_mla_rope" if rope else "premix_mla",
    )(*args)


LOG2E = 1.4426950408889634


def _softmax_pv(s_list, v_list, scale=1.0):
    m = functools.reduce(jnp.maximum, [jnp.max(s, axis=-1, keepdims=True) for s in s_list])
    e_list = [jnp.exp2((s - m) * (scale * LOG2E)) for s in s_list]
    l = functools.reduce(lambda a, b: a + b, [jnp.sum(e, axis=-1, keepdims=True) for e in e_list])
    o = functools.reduce(lambda a, b: a + b,
                         [_dot(e.astype(BF16), v) for e, v in zip(e_list, v_list)])
    return o / l


def _stack_heads(q, masks):
    return jnp.concatenate([q * m for m in masks], axis=0)


def _unstack_heads(o2, width):
    tq = o2.shape[0] // 2
    lane = lax.broadcasted_iota(jnp.int32, (1, LANES), 1)
    return jnp.where(lane < width, o2[:tq], o2[tq:])


def _mla_attn_kernel(*refs, cached):
    if cached:
        (qn_ref, qr_ref, ckv_ref, kr4_ref, cckv_ref, ckr4_ref, w_uk_ref, w_uv_ref,
         o_ref, kn_s, v_s, kr_s) = refs
    else:
        (qn_ref, qr_ref, ckv_ref, kr4_ref, w_uk_ref, w_uv_ref, o_ref, kn_s, v_s, kr_s) = refs
    t_own = ckv_ref.shape[0]

    @pl.when(pl.program_id(1) == 0)
    def _():
        c = ckv_ref[...].astype(BF16)
        kn_s[0:t_own, :] = _dot(c, w_uk_ref[...]).astype(BF16)
        v_s[0:t_own, :] = _dot(c, w_uv_ref[...]).astype(BF16)
        kr_s[0:t_own, :] = kr4_ref[...]
        if cached:
            cc = cckv_ref[0].astype(BF16)
            kn_s[t_own:, :] = _dot(cc, w_uk_ref[...]).astype(BF16)
            v_s[t_own:, :] = _dot(cc, w_uv_ref[...]).astype(BF16)
            kr_s[t_own:, :] = ckr4_ref[0].astype(BF16)

    kr4 = kr_s[...]
    for p in range(HEAD_PAIRS):
        sl = slice(p * LANES, (p + 1) * LANES)
        qn = qn_ref[:, sl]
        qr = qr_ref[:, (p // 2) * LANES:(p // 2 + 1) * LANES]
        k_cat = jnp.concatenate([kn_s[:, sl], kr4], axis=-1)
        q_cat = jnp.concatenate(
            [_stack_heads(qn, [_lane_mask(MLA_NOPE, i, BF16) for i in range(2)]),
             _stack_heads(qr, [_lane_mask(MLA_ROPE, (2 * p + i) % 4, BF16) for i in range(2)])], axis=-1)
        o2 = _softmax_pv([_dot_nt(q_cat, k_cat)], [v_s[:, sl]], MLA_SCALE)
        o_ref[:, sl] = _unstack_heads(o2, MLA_V).astype(BF16)


def _mla_attention(qn, qr, ckv, kr4, cache_ckv, cache_kr4, w_uk, w_uv, n_batch, tq):
    n = qn.shape[0]
    s_own = n // n_batch
    nq = s_own // tq
    cached = cache_ckv is not None
    t_all = s_own + (cache_ckv.shape[1] if cached else 0)
    qrow = lambda b, j: (b * nq + j, 0)
    own = lambda b, j: (b, 0)
    full = lambda b, j: (0, 0)
    in_specs = [
        pl.BlockSpec((tq, qn.shape[1]), qrow),
        pl.BlockSpec((tq, qr.shape[1]), qrow),
        pl.BlockSpec((s_own, MLA_KV_LORA), own),
        pl.BlockSpec((s_own, LANES), own),
    ]
    args = [qn, qr, ckv, kr4]
    if cached:
        in_specs += [pl.BlockSpec((1,) + cache_ckv.shape[1:], lambda b, j: (b, 0, 0)),
                     pl.BlockSpec((1,) + cache_kr4.shape[1:], lambda b, j: (b, 0, 0))]
        args += [cache_ckv, cache_kr4]
    in_specs += [pl.BlockSpec(w_uk.shape, full), pl.BlockSpec(w_uv.shape, full)]
    args += [w_uk, w_uv]
    return pl.pallas_call(
        functools.partial(_mla_attn_kernel, cached=cached),
        grid=(n_batch, nq),
        in_specs=in_specs,
        out_specs=pl.BlockSpec((tq, D_MODEL), qrow),
        out_shape=jax.ShapeDtypeStruct((n, D_MODEL), BF16),
        scratch_shapes=[pltpu.VMEM((t_all, D_MODEL), BF16), pltpu.VMEM((t_all, D_MODEL), BF16),
                        pltpu.VMEM((t_all, LANES), BF16)],
        compiler_params=_cparams("parallel", "arbitrary"),
        name="mla_attention_cached" if cached else "mla_attention",
    )(*args)


def _premix_nat_kernel(x_ref, mod_ref, g_ref, w_ref, q_ref, k_ref, v_ref):
    h = _rms(x_ref[...], g_ref[...]) * (1.0 + mod_ref[0, 1:2, :]) + mod_ref[0, 0:1, :]
    qkv = _dot(h.astype(BF16), w_ref[...])
    q_ref[...] = qkv[:, :D_MODEL].astype(q_ref.dtype)
    k_ref[...] = qkv[:, D_MODEL:2 * D_MODEL].astype(k_ref.dtype)
    v_ref[...] = qkv[:, 2 * D_MODEL:].astype(v_ref.dtype)


def _premix_nat(x, mod, g, w_qkv, rows_per_cond, tm, kv_dtype):
    n = x.shape[0]
    row = lambda i: (i, 0)
    full = lambda i: (0, 0)
    return pl.pallas_call(
        _premix_nat_kernel,
        grid=(n // tm,),
        in_specs=[
            pl.BlockSpec((tm, D_MODEL), row),
            _mod_spec(None if rows_per_cond is None else rows_per_cond // tm),
            pl.BlockSpec((1, D_MODEL), full),
            pl.BlockSpec(w_qkv.shape, full),
        ],
        out_specs=[pl.BlockSpec((tm, D_MODEL), row)] * 3,
        out_shape=[jax.ShapeDtypeStruct((n, D_MODEL), BF16),
                   jax.ShapeDtypeStruct((n, D_MODEL), kv_dtype),
                   jax.ShapeDtypeStruct((n, D_MODEL), kv_dtype)],
        compiler_params=_cparams("parallel"),
        name="premix_nat",
    )(x, mod, g, w_qkv)


def _premix_nat_cache_kernel(*refs, first):
    if first:
        x_ref, mod_ref, g_ref, w_ref, q_ref, k_ref, v_ref = refs
    else:
        x_ref, mod_ref, g_ref, w_ref, _, _, q_ref, k_ref, v_ref = refs
    h = _rms(x_ref[...], g_ref[...]) * (1.0 + mod_ref[0, 1:2, :]) + mod_ref[0, 0:1, :]
    qkv = _dot(h.astype(BF16), w_ref[...])
    q_ref[...] = qkv[:, :D_MODEL].astype(BF16)
    nb, n_slots, s, _ = k_ref.shape
    k_ref[:, 0] = qkv[:, D_MODEL:2 * D_MODEL].reshape(nb, s, D_MODEL)
    v_ref[:, 0] = qkv[:, 2 * D_MODEL:].reshape(nb, s, D_MODEL)
    for slot in range(1, n_slots):
        k_ref[:, slot] = jnp.zeros((nb, s, D_MODEL), F32)
        v_ref[:, slot] = jnp.zeros((nb, s, D_MODEL), F32)


def _premix_nat_cache(x, mod, g, w_qkv, k_buf, v_buf, slot, n_slots, n_batch, tm):
    n = x.shape[0]
    s = n // n_batch
    nb = tm // s
    first = k_buf is None
    row = lambda i: (i, 0)
    full = lambda i: (0, 0)
    in_specs = [pl.BlockSpec((tm, D_MODEL), row), _mod_spec(None), pl.BlockSpec((1, D_MODEL), full),
                pl.BlockSpec(w_qkv.shape, full)]
    args = [x, mod, g, w_qkv]
    if first:
        assert slot == 0
        kv_spec = pl.BlockSpec((nb, n_slots, s, D_MODEL), lambda i: (i, 0, 0, 0))
        aliases = {}
    else:
        in_specs += [pl.BlockSpec(memory_space=pl.ANY)] * 2
        args += [k_buf, v_buf]
        kv_spec = pl.BlockSpec((nb, 1, s, D_MODEL), lambda i: (i, slot, 0, 0))
        aliases = {4: 1, 5: 2}
    kv_shape = jax.ShapeDtypeStruct((n_batch, n_slots, s, D_MODEL), F32)
    return pl.pallas_call(
        functools.partial(_premix_nat_cache_kernel, first=first),
        grid=(n // tm,),
        in_specs=in_specs,
        out_specs=[pl.BlockSpec((tm, D_MODEL), row), kv_spec, kv_spec],
        out_shape=[jax.ShapeDtypeStruct((n, D_MODEL), BF16), kv_shape, kv_shape],
        input_output_aliases=aliases,
        compiler_params=_cparams("parallel"),
        name="premix_nat_cache",
    )(*args)


def _dense_attn_kernel(q_ref, k_ref, v_ref, o_ref):
    for p in range(HEAD_PAIRS):
        sl = slice(p * LANES, (p + 1) * LANES)
        q2 = _stack_heads(q_ref[:, sl], [_lane_mask(NAT_DH, i, BF16) for i in range(2)])
        k = k_ref[:, sl].astype(BF16)
        v = v_ref[:, sl].astype(BF16)
        o2 = _softmax_pv([_dot_nt(q2, k)], [v], NAT_SCALE)
        o_ref[:, sl] = _unstack_heads(o2, NAT_DH).astype(BF16)


def _dense_attention(q, k_buf, v_buf, slot):
    n = q.shape[0]
    n_batch, _, s, _ = k_buf.shape
    blk = pl.BlockSpec((s, D_MODEL), lambda b: (b, 0))
    kv = pl.BlockSpec((None, None, s, D_MODEL), lambda b: (b, slot, 0, 0))
    return pl.pallas_call(
        _dense_attn_kernel,
        grid=(n_batch,),
        in_specs=[blk, kv, kv],
        out_specs=blk,
        out_shape=jax.ShapeDtypeStruct((n, D_MODEL), BF16),
        compiler_params=_cparams("parallel"),
        name="dense_attention",
    )(q, k_buf, v_buf)


_NAT_QBLK = NAT_QROWS * GRID_W
_NAT_KBLK = NAT_KROWS * GRID_W


def _nat_block_plan(rows):
    assert rows % NAT_QROWS == 0 and rows >= NAT_KROWS and NAT_KROWS % 2 == 0
    plan, variants = [], []
    for r0 in range(0, rows, NAT_QROWS):
        ks = min(max(r0 - WIN_H // 2, 0), rows - NAT_KROWS)
        r = r0 + np.arange(NAT_QROWS)
        kr = ks + np.arange(NAT_KROWS)
        rs = np.clip(r - WIN_H // 2, 0, rows - WIN_H)
        valid_row = (kr[None, :] >= rs[:, None]) & (kr[None, :] < rs[:, None] + WIN_H)
        d0 = ks - r + (WIN_H - 1)
        for vi, (d0_v, valid_v) in enumerate(variants):
            if np.array_equal(d0, d0_v) and np.array_equal(valid_row, valid_v):
                break
        else:
            vi = len(variants)
            variants.append((d0, valid_row))
        plan.append((ks, vi))
    return plan, variants


def _nat_bias_row_range(variants):
    lo = min(int(d0.min()) for d0, _ in variants)
    hi = max(int(d0.max()) for d0, _ in variants) + NAT_KROWS
    return lo, hi


def _nat_bias_pairs(rpb, rows):
    n_heads, n_dr, n_dc = rpb.shape
    _, variants = _nat_block_plan(rows)
    lo, hi = _nat_bias_row_range(variants)
    c = np.arange(GRID_W)
    cs = np.clip(c - WIN_W // 2, 0, GRID_W - WIN_W)
    valid_col = (c[None, :] >= cs[:, None]) & (c[None, :] < cs[:, None] + WIN_W)
    d_col = c[None, :] - c[:, None] + (WIN_W - 1)
    sel = (d_col[None] == np.arange(n_dc)[:, None, None]) & valid_col[None]
    toep = jnp.einsum('hdj,jck->hdck', rpb, jnp.asarray(sel, F32), precision=lax.Precision.HIGHEST)
    toep = jnp.where(jnp.asarray(valid_col)[None, None], toep, NEG_INF)
    toep = jnp.pad(toep, ((0, 0), (max(-lo, 0), max(hi + 1 - n_dr, 0)), (0, 0), (0, 0)),
                   constant_values=NEG_INF)
    toep = toep[:, max(lo, 0):]
    return jnp.concatenate([toep[:, :-1], toep[:, 1:]], axis=-1)


def _nat_block_bias(tp_ref, head, d0, valid_row, row_lo):
    neg = jnp.full((GRID_W, 2 * GRID_W), NEG_INF, F32)
    left = lax.broadcasted_iota(jnp.int32, (1, 2 * GRID_W), 1) < GRID_W
    rows_out = []
    for dr in range(NAT_QROWS):
        pieces = []
        for a in range(0, NAT_KROWS, 2):
            ok0, ok1 = bool(valid_row[dr, a]), bool(valid_row[dr, a + 1])
            if not (ok0 or ok1):
                pieces.append(neg)
                continue
            piece = tp_ref[head, int(d0[dr]) + a - row_lo]
            if ok0 and not ok1:
                piece = jnp.where(left, piece, NEG_INF)
            elif ok1 and not ok0:
                piece = jnp.where(left, NEG_INF, piece)
            pieces.append(piece)
        rows_out.append(jnp.concatenate(pieces, axis=-1))
    return jnp.concatenate(rows_out, axis=0)


def _nat_attn_kernel(q_ref, k_ref, v_ref, kc_ref, vc_ref, tp_ref, o_ref, *, plan, variants, row_lo):
    kc = kc_ref[0].astype(BF16)
    vc = vc_ref[0].astype(BF16)
    assert NAT_SCALE == 2.0 ** round(np.log2(NAT_SCALE))
    masks = [_lane_mask(NAT_DH, i, BF16) * NAT_SCALE for i in range(2)]
    for bi, (ks, var) in enumerate(plan):
        q2 = _stack_heads(q_ref[bi * _NAT_QBLK:(bi + 1) * _NAT_QBLK, :], masks)
        k = k_ref[ks * GRID_W:ks * GRID_W + _NAT_KBLK, :]
        v = v_ref[ks * GRID_W:ks * GRID_W + _NAT_KBLK, :]
        bias2 = jnp.concatenate([_nat_block_bias(tp_ref, i, *variants[var], row_lo) for i in range(2)], axis=0)
        o2 = _softmax_pv([_dot_nt(q2, k) + bias2, _dot_nt(q2, kc)], [v, vc])
        o_ref[bi * _NAT_QBLK:(bi + 1) * _NAT_QBLK, :] = _unstack_heads(o2, NAT_DH).astype(BF16)


def _nat_attention(q, k, v, cache_k, cache_v, bias, n_batch):
    n = q.shape[0]
    s = n // n_batch
    plan, variants = _nat_block_plan(s // GRID_W)
    row_lo, _ = _nat_bias_row_range(variants)
    own = pl.BlockSpec((s, LANES), lambda p, b: (b, p))
    cache = pl.BlockSpec((1, cache_k.shape[1], LANES), lambda p, b: (b, 0, p))
    return pl.pallas_call(
        functools.partial(_nat_attn_kernel, plan=plan, variants=variants, row_lo=row_lo),
        grid=(HEAD_PAIRS, n_batch),
        in_specs=[own, own, own, cache, cache,
                  pl.BlockSpec((2,) + bias.shape[1:], lambda p, b: (p, 0, 0, 0))],
        out_specs=own,
        out_shape=jax.ShapeDtypeStruct((n, D_MODEL), BF16),
        compiler_params=_cparams("parallel", "parallel"),
        name="nat_attention",
    )(q, k, v, cache_k, cache_v, bias)


def _proj_res_kernel(x_ref, a_ref, w_ref, mod_ref, o_ref):
    o_ref[...] = x_ref[...] + mod_ref[0, 2:3, :] * _dot(a_ref[...], w_ref[...])


def _proj_residual(x, a, w, mod, rows_per_cond, tm):
    n = x.shape[0]
    row = lambda i: (i, 0)
    return pl.pallas_call(
        _proj_res_kernel,
        grid=(n // tm,),
        in_specs=[
            pl.BlockSpec((tm, D_MODEL), row),
            pl.BlockSpec((tm, D_MODEL), row),
            pl.BlockSpec(w.shape, lambda i: (0, 0)),
            _mod_spec(None if rows_per_cond is None else rows_per_cond // tm),
        ],
        out_specs=pl.BlockSpec((tm, D_MODEL), row),
        out_shape=jax.ShapeDtypeStruct((n, D_MODEL), F32),
        compiler_params=_cparams("parallel"),
        name="proj_residual",
    )(x, a, w, mod)


def _top2_sum(a, b, c, d):
    hi1, lo1 = jnp.maximum(a, b), jnp.minimum(a, b)
    hi2, lo2 = jnp.maximum(c, d), jnp.minimum(c, d)
    return jnp.maximum(hi1, hi2) + jnp.maximum(jnp.minimum(hi1, hi2), jnp.maximum(lo1, lo2))


def _route(scores, biased):
    sc = [scores[e:e + 1, :] for e in range(N_EXPERTS)]
    bs = [biased[e:e + 1, :] for e in range(N_EXPERTS)]
    gscore = [_top2_sum(*bs[EPG * g:EPG * (g + 1)]) for g in range(N_GROUPS)]
    best, gidx = gscore[0], jnp.zeros_like(gscore[0], dtype=jnp.int32)
    for g in range(1, N_GROUPS):
        better = gscore[g] > best
        gidx = jnp.where(better, g, gidx)
        best = jnp.where(better, gscore[g], best)
    cb = [functools.reduce(lambda a, b: a + b,
                           [jnp.where(gidx == g, bs[EPG * g + i], 0.0) for g in range(N_GROUPS)])
          for i in range(EPG)]
    cs = [functools.reduce(lambda a, b: a + b,
                           [jnp.where(gidx == g, sc[EPG * g + i], 0.0) for g in range(N_GROUPS)])
          for i in range(EPG)]
    b1, i1 = cb[0], jnp.zeros_like(gidx)
    for i in range(1, EPG):
        better = cb[i] > b1
        i1 = jnp.where(better, i, i1)
        b1 = jnp.where(better, cb[i], b1)
    b2, i2 = jnp.full_like(b1, -jnp.inf), jnp.full_like(i1, -1)
    for i in range(EPG):
        better = (i1 != i) & (cb[i] > b2)
        i2 = jnp.where(better, i, i2)
        b2 = jnp.where(better, cb[i], b2)
    sel = [(i1 == i) | (i2 == i) for i in range(EPG)]
    w = [jnp.where(sel[i], cs[i], 0.0) for i in range(EPG)]
    tot = w[0] + w[1] + w[2] + w[3]
    return gidx, [w[i] / tot for i in range(EPG)]


MOE_SRC = 256
MOE_DST = 256
MOE_CHUNK = 4
_LOG2_DST = MOE_DST.bit_length() - 1
assert MOE_DST == 1 << _LOG2_DST and MOE_SRC <= MOE_DST
_INFO_GID = EPG
_INFO_RANK = EPG + 1


def _moe_route_kernel(x_ref, mod_ref, g_ref, wr_ref, rb_ref, h_ref, info_ref, infot_ref, cum_ref, carry_s):
    @pl.when(pl.program_id(0) == 0)
    def _():
        carry_s[...] = jnp.zeros_like(carry_s)

    h = _rms(x_ref[...], g_ref[...]) * (1.0 + mod_ref[0, 4:5, :]) + mod_ref[0, 3:4, :]
    h_hi = h.astype(BF16)
    h_ref[:, :D_MODEL] = h_hi
    h_lo = (h - h_hi.astype(F32)).astype(BF16)
    hi_w = _dot(h_hi, wr_ref[...])
    logits = hi_w[:, :LANES] + (_dot(h_lo, wr_ref[:, :LANES]) + hi_w[:, LANES:])
    scores = _sigmoid(logits)
    gidx, gates = _route(scores.T[:N_EXPERTS], (scores + rb_ref[...]).T[:N_EXPERTS])
    tm = h.shape[0]
    sub = lax.broadcasted_iota(jnp.int32, (8, 1), 0)
    onehot = (sub == gidx).astype(F32)
    tri = (lax.broadcasted_iota(jnp.int32, (tm, tm), 0)
           <= lax.broadcasted_iota(jnp.int32, (tm, tm), 1)).astype(BF16)
    prefix = _dot(onehot.astype(BF16), tri)
    carry = carry_s[:, 0:1]
    rank = jnp.sum((prefix - 1.0 + carry) * onehot, axis=0, keepdims=True)
    infot = jnp.concatenate(gates + [gidx.astype(F32), rank, jnp.zeros((2, tm), F32)], axis=0)
    infot_ref[0] = infot
    info_ref[...] = jnp.concatenate([infot, jnp.zeros((LANES - 8, tm), F32)], axis=0).T
    g_hi = [gt.astype(BF16).astype(F32) for gt in gates]
    g_lo = [gt - gh for gt, gh in zip(gates, g_hi)]
    gext = jnp.concatenate(g_hi + g_lo + [jnp.zeros((LANES - 2 * EPG, tm), F32)], axis=0)
    h_ref[:, D_MODEL:] = gext.T.astype(BF16)
    carry = jnp.broadcast_to(carry + prefix[:, tm - 1:tm], (8, LANES))
    carry_s[...] = carry
    cum_ref[0] = carry.astype(jnp.int32)


def _group_tiles(cum_ref):
    n_src = cum_ref.shape[0]
    tot = [cum_ref[n_src - 1, g] for g in range(N_GROUPS)]
    ends, acc = [], 0
    for t in tot:
        acc = acc + lax.shift_right_logical(t + (MOE_DST - 1), _LOG2_DST)
        ends.append(acc)
    return tot, ends


def _dest_tile(d, cum_ref):
    tot, ends = _group_tiles(cum_ref)
    g = ((d >= ends[0]).astype(jnp.int32) + (d >= ends[1]).astype(jnp.int32)
         + (d >= ends[2]).astype(jnp.int32))
    first = jnp.where(g == 0, 0, jnp.where(g == 1, ends[0], jnp.where(g == 2, ends[1], ends[2])))
    tot_g = jnp.where(g == 0, tot[0], jnp.where(g == 1, tot[1], jnp.where(g == 2, tot[2], tot[3])))
    k0 = (d - first) * MOE_DST
    n_valid = jnp.clip(tot_g - k0, 0, MOE_DST)
    return g, k0, n_valid


def _counts_before(cum_ref, s, g):
    return jnp.where(s > 0, cum_ref[jnp.maximum(s - 1, 0), g], 0)


def _moe_ffn_kernel(cum_ref, h_ref, infot_ref, wg_ref, wu_ref, wd_ref, y_ref, xs_s, ptr_s):
    d = pl.program_id(0)
    g, k0, n_valid = _dest_tile(d, cum_ref)
    n_src = cum_ref.shape[0]

    @pl.when(d == 0)
    def _():
        ptr_s[0] = 0

    @pl.when(n_valid == 0)
    def _():
        y_ref[...] = jnp.zeros_like(y_ref)

    @pl.when(n_valid > 0)
    def _():
        need = k0 + n_valid
        start = jnp.where(k0 == 0, 0, ptr_s[0])
        s_lo = lax.while_loop(lambda s: cum_ref[s, g] <= k0, lambda s: s + 1, start)
        ptr_s[0] = s_lo
        xs_s[...] = jnp.zeros_like(xs_s)
        row = lax.broadcasted_iota(jnp.int32, (MOE_DST, 1), 0).astype(F32)
        gf = g.astype(F32)
        k0f = k0.astype(F32)

        def more(s0):
            return (_counts_before(cum_ref, s0, g) < need) & (s0 < n_src)

        def gather(s0):
            base = jnp.minimum(s0, n_src - MOE_CHUNK)
            perms = []
            for j in range(MOE_CHUNK):
                it = infot_ref[base + j]
                shift = k0f + jnp.where(base + j >= s0, 0.0, 1e9)
                rel = jnp.where(it[_INFO_GID:_INFO_GID + 1] == gf,
                                it[_INFO_RANK:_INFO_RANK + 1] - shift, -1.0)
                perms.append((row == rel).astype(BF16))
            perm = jnp.concatenate(perms, axis=-1)
            rows = pl.ds(pl.multiple_of(base * MOE_SRC, MOE_SRC), MOE_CHUNK * MOE_SRC)
            xs_s[...] += _dot(perm, h_ref[rows, :])
            return base + MOE_CHUNK

        lax.while_loop(more, gather, s_lo)
        xs = xs_s[:, :D_MODEL].astype(BF16)
        gext = xs_s[:, D_MODEL:]
        acts = []
        for i in range(EPG):
            a = _dot(xs, wg_ref[i])
            u = _dot(xs, wu_ref[i])
            gate = gext[:, i:i + 1] + gext[:, EPG + i:EPG + i + 1]
            acts.append(((a * _sigmoid(a)) * u * gate).astype(BF16))
        act = jnp.concatenate(acts, axis=-1)
        y_ref[...] = _dot(act, wd_ref[...].reshape(EPG * D_EXPERT, D_MODEL)).astype(BF16)


def _moe_unsort_kernel(cum_ref, x_ref, mod_ref, info_ref, y_ref, o_ref, acc_s):
    s = pl.program_id(0)
    _, ends = _group_tiles(cum_ref)
    firsts = [0] + ends[:-1]
    inf = info_ref[...]
    gid = inf[:, _INFO_GID:_INFO_GID + 1]
    rank = inf[:, _INFO_RANK:_INFO_RANK + 1]
    col = lax.broadcasted_iota(jnp.int32, (1, MOE_DST), 1).astype(F32)

    def from_tile(g, k):
        rel = jnp.where(gid == float(g), rank - (k * MOE_DST).astype(F32), -1.0)
        perm = (rel == col).astype(BF16)
        rows = pl.ds(pl.multiple_of((firsts[g] + k) * MOE_DST, MOE_DST), MOE_DST)
        return _dot(perm, y_ref[rows, :])

    lo = [_counts_before(cum_ref, s, g) for g in range(N_GROUPS)]
    k_lo = [lax.shift_right_logical(l, _LOG2_DST) for l in lo]
    first_tile = [jnp.minimum(k, ends[g] - firsts[g] - 1) for g, k in enumerate(k_lo)]
    acc_s[...] = functools.reduce(lambda a, b: a + b,
                                  [from_tile(g, jnp.maximum(first_tile[g], 0)) for g in range(N_GROUPS)])
    for g in range(N_GROUPS):
        @pl.when(cum_ref[s, g] > (k_lo[g] + 1) * MOE_DST)
        def _():
            acc_s[...] += from_tile(g, k_lo[g] + 1)
    o_ref[...] = x_ref[...] + mod_ref[0, 5:6, :] * acc_s[...]


def _moe(x, mod, g, w_router, router_bias, wg, wu, wd, layer, rows_per_cond):
    n = x.shape[0]
    n_src = n // MOE_SRC
    n_dst = n // MOE_DST + N_GROUPS
    row = lambda i: (i, 0)
    full = lambda i: (0, 0)
    h, info, infot, cum = pl.pallas_call(
        _moe_route_kernel,
        grid=(n_src,),
        in_specs=[
            pl.BlockSpec((MOE_SRC, D_MODEL), row),
            _mod_spec(None if rows_per_cond is None else rows_per_cond // MOE_SRC),
            pl.BlockSpec((1, D_MODEL), full),
            pl.BlockSpec(w_router.shape, full),
            pl.BlockSpec(router_bias.shape, full),
        ],
        out_specs=[pl.BlockSpec((MOE_SRC, D_MODEL + LANES), row),
                   pl.BlockSpec((MOE_SRC, LANES), row),
                   pl.BlockSpec((1, 8, MOE_SRC), lambda i: (i, 0, 0)),
                   pl.BlockSpec((1, 8, LANES), lambda i: (i, 0, 0))],
        out_shape=[jax.ShapeDtypeStruct((n, D_MODEL + LANES), BF16),
                   jax.ShapeDtypeStruct((n, LANES), F32),
                   jax.ShapeDtypeStruct((n_src, 8, MOE_SRC), F32),
                   jax.ShapeDtypeStruct((n_src, 8, LANES), jnp.int32)],
        scratch_shapes=[pltpu.VMEM((8, LANES), F32)],
        compiler_params=_cparams("arbitrary"),
        name="moe_route",
    )(x, mod, g, w_router, router_bias)
    cum = cum[:, :N_GROUPS, 0]

    resident = lambda a: pl.BlockSpec(a.shape, lambda d, c: (0,) * a.ndim, pipeline_mode=pl.Buffered(1))
    expert = lambda shape: pl.BlockSpec(shape, lambda d, c: (layer, _dest_tile(d, c)[0], 0, 0))
    y = pl.pallas_call(
        _moe_ffn_kernel,
        grid_spec=pltpu.PrefetchScalarGridSpec(
            num_scalar_prefetch=1,
            grid=(n_dst,),
            in_specs=[resident(h), resident(infot),
                      expert((None, EPG, D_MODEL, D_EXPERT)),
                      expert((None, EPG, D_MODEL, D_EXPERT)),
                      expert((None, EPG, D_EXPERT, D_MODEL))],
            out_specs=pl.BlockSpec((MOE_DST, D_MODEL), lambda d, c: (d, 0)),
            scratch_shapes=[pltpu.VMEM((MOE_DST, D_MODEL + LANES), F32), pltpu.SMEM((1,), jnp.int32)],
        ),
        out_shape=jax.ShapeDtypeStruct((n_dst * MOE_DST, D_MODEL), BF16),
        compiler_params=_cparams("arbitrary"),
        name="moe_experts",
    )(cum, h, infot, wg, wu, wd)

    return pl.pallas_call(
        _moe_unsort_kernel,
        grid_spec=pltpu.PrefetchScalarGridSpec(
            num_scalar_prefetch=1,
            grid=(n_src,),
            in_specs=[pl.BlockSpec((MOE_SRC, D_MODEL), lambda i, c: (i, 0)),
                      _mod_spec(None if rows_per_cond is None else rows_per_cond // MOE_SRC),
                      pl.BlockSpec((MOE_SRC, LANES), lambda i, c: (i, 0)),
                      resident(y)],
            out_specs=pl.BlockSpec((MOE_SRC, D_MODEL), lambda i, c: (i, 0)),
            scratch_shapes=[pltpu.VMEM((MOE_SRC, D_MODEL), F32)],
        ),
        out_shape=jax.ShapeDtypeStruct((n, D_MODEL), F32),
        compiler_params=_cparams("parallel"),
        name="moe_unsort",
    )(cum, x, mod, info, y)


def _final_norm_kernel(x_ref, g_ref, o_ref):
    o_ref[...] = _rms(x_ref[...], g_ref[...])


def _final_norm(x, g, tm):
    n = x.shape[0]
    return pl.pallas_call(
        _final_norm_kernel,
        grid=(n // tm,),
        in_specs=[pl.BlockSpec((tm, D_MODEL), lambda i: (i, 0)),
                  pl.BlockSpec((1, D_MODEL), lambda i: (0, 0))],
        out_specs=pl.BlockSpec((tm, D_MODEL), lambda i: (i, 0)),
        out_shape=jax.ShapeDtypeStruct((n, D_MODEL), F32),
        compiler_params=_cparams("parallel"),
        name="final_norm",
    )(x, g)


def _rope_tables(n_lat):
    t = np.arange(n_lat)
    n_freq = MLA_ROPE // 4
    inv_freq = jnp.asarray(ROPE_THETA, F32) ** (-jnp.arange(n_freq, dtype=F32) / n_freq)
    ar = jnp.asarray(t // GRID_W, F32)[:, None] * inv_freq
    ac = jnp.asarray(t % GRID_W, F32)[:, None] * inv_freq
    cos = jnp.concatenate([jnp.cos(ar), jnp.cos(ar), jnp.cos(ac), jnp.cos(ac)], axis=-1)
    sin = jnp.concatenate([-jnp.sin(ar), jnp.sin(ar), -jnp.sin(ac), jnp.sin(ac)], axis=-1)
    return (jnp.tile(cos, (1, MLA_HEADS)), jnp.tile(sin, (1, MLA_HEADS)),
            jnp.tile(cos, (1, LANES // MLA_ROPE)), jnp.tile(sin, (1, LANES // MLA_ROPE)))


_ROPE_SWAP = np.concatenate([np.arange(8, 16), np.arange(0, 8), np.arange(24, 32), np.arange(16, 24)])


def _mla_weights(w_in, w_uq, w_ukv):
    o = MLA_Q_LORA + MLA_KV_LORA
    kr = w_in[:, o:]
    rep = LANES // MLA_ROPE
    w_in_x = jnp.concatenate([w_in[:, :o], jnp.tile(kr, (1, rep)), jnp.tile(kr[:, _ROPE_SWAP], (1, rep))],
                             axis=-1).astype(BF16)
    uq = w_uq.reshape(MLA_Q_LORA, MLA_HEADS, MLA_NOPE + MLA_ROPE)
    q_nope = uq[:, :, :MLA_NOPE].reshape(MLA_Q_LORA, -1)
    q_rope = uq[:, :, MLA_NOPE:]
    w_uq_x = jnp.concatenate([q_nope, q_rope.reshape(MLA_Q_LORA, -1),
                              q_rope[:, :, _ROPE_SWAP].reshape(MLA_Q_LORA, -1)], axis=-1).astype(BF16)
    ukv = w_ukv.reshape(MLA_KV_LORA, MLA_HEADS, MLA_NOPE + MLA_V)
    w_uk = ukv[:, :, :MLA_NOPE].reshape(MLA_KV_LORA, -1).astype(BF16)
    w_uv = ukv[:, :, MLA_NOPE:].reshape(MLA_KV_LORA, -1).astype(BF16)
    return w_in_x, w_uq_x, w_uk, w_uv


def kernel(x_prompt, x_sample, cache_mla_ckv, cache_mla_krope, cache_nat_k, cache_nat_v, c, c_ctx,
           w_ada, b_ada, norm_mix, norm_ffn, norm_final, mla_w_in, mla_q_norm, mla_w_uq, mla_kv_norm,
           mla_w_ukv, mla_w_o, nat_w_qkv, nat_rpb, nat_w_o, w_router, router_bias,
           moe_w_gate, moe_w_up, moe_w_down):
    B, S, D = x_prompt.shape
    Bd, Sd, _ = x_sample.shape
    assert D == D_MODEL and Bd + 1 <= 8
    tm_c, tm_s = 512, 512

    xc = x_prompt.reshape(B * S, D)
    xs = x_sample.reshape(Bd * Sd, D)
    cond8 = jnp.concatenate([c_ctx[None, :], c, jnp.zeros((8 - 1 - Bd, D), F32)], axis=0)
    mod_all = _ada_modulation(cond8, w_ada, b_ada).reshape(DEPTH, 8, 6, D)

    rope_tabs = _rope_tables(Sd)
    wr = jnp.pad(w_router, ((0, 0), (0, LANES - N_EXPERTS)))
    wr_hi = wr.astype(BF16)
    wr = jnp.concatenate([wr_hi, (wr - wr_hi.astype(F32)).astype(BF16)], axis=1)
    rb = jnp.pad(router_bias, (0, LANES - N_EXPERTS)).reshape(1, LANES)
    wg = moe_w_gate.astype(BF16)
    wu = moe_w_up.astype(BF16)
    wd = moe_w_down.astype(BF16)

    new_ckv, new_krope = [], []
    k_buf = v_buf = None
    for layer in range(DEPTH):
        mod = mod_all[layer]
        g_mix = norm_mix[layer][None, :]
        j = layer // 2
        if layer % 2 == 0:
            w_in_x, w_uq_x, w_uk, w_uv = _mla_weights(mla_w_in[j], mla_w_uq[j], mla_w_ukv[j])
            w_in_c = w_in_x[:, :MLA_Q_LORA + MLA_KV_LORA + LANES]
            w_uq_c = w_uq_x[:, :MLA_HEADS * (MLA_NOPE + MLA_ROPE)]
            qg = mla_q_norm[j][None, :]
            kvg = mla_kv_norm[j][None, :]
            w_o = mla_w_o[j].astype(BF16)
            qn, qr, ckv, kr, kr4 = _premix_mla(xc, mod, g_mix, w_in_c, qg, w_uq_c, kvg, None, None, tm_c)
            new_ckv.append(ckv.reshape(B, S, MLA_KV_LORA))
            new_krope.append(kr.reshape(B, S, MLA_ROPE))
            att_c = _mla_attention(qn, qr, ckv, kr4, None, None, w_uk, w_uv, B, S)
            qn, qr, ckv, kr, kr4 = _premix_mla(xs, mod, g_mix, w_in_x, qg, w_uq_x, kvg, rope_tabs, Sd, tm_s)
            cache_kr4 = jnp.tile(cache_mla_krope[:, j], (1, 1, LANES // MLA_ROPE))
            att_s = _mla_attention(qn, qr, ckv, kr4, cache_mla_ckv[:, j], cache_kr4, w_uk, w_uv, Bd, 256)
        else:
            w_qkv = nat_w_qkv[j].astype(BF16)
            w_o = nat_w_o[j].astype(BF16)
            q, k_buf, v_buf = _premix_nat_cache(xc, mod, g_mix, w_qkv, k_buf, v_buf, j, DEPTH // 2, B, tm_c)
            att_c = _dense_attention(q, k_buf, v_buf, j)
            q, k, v = _premix_nat(xs, mod, g_mix, w_qkv, Sd, tm_s, BF16)
            bias = _nat_bias_pairs(nat_rpb[j], Sd // GRID_W)
            att_s = _nat_attention(q, k, v,
                                   cache_nat_k[:, j].reshape(Bd, -1, D), cache_nat_v[:, j].reshape(Bd, -1, D),
                                   bias, Bd)
        xc = _proj_residual(xc, att_c, w_o, mod, None, tm_c)
        xs = _proj_residual(xs, att_s, w_o, mod, Sd, tm_s)
        g_ffn = norm_ffn[layer][None, :]
        xc = _moe(xc, mod, g_ffn, wr, rb, wg, wu, wd, layer, None)
        xs = _moe(xs, mod, g_ffn, wr, rb, wg, wu, wd, layer, Sd)

    gf = norm_final[None, :]
    y_prompt = _final_norm(xc, gf, tm_c).reshape(B, S, D)
    y_sample = _final_norm(xs, gf, tm_s).reshape(Bd, Sd, D)
    kv_shape = (B, DEPTH // 2, S, NAT_HEADS, NAT_DH)
    return (y_prompt, y_sample, jnp.stack(new_ckv, axis=1), jnp.stack(new_krope, axis=1),
            k_buf.reshape(kv_shape), v_buf.reshape(kv_shape))
```

```python
import functools

import numpy as np
import jax
import jax.numpy as jnp
from jax import lax
from jax.experimental import pallas as pl
from jax.experimental.pallas import tpu as pltpu

F32 = jnp.float32
BF16 = jnp.bfloat16

D_MODEL = 1024
DEPTH = 4
GRID_W = 64
LANES = 128
MLA_HEADS = 16
MLA_NOPE = 64
MLA_ROPE = 32
MLA_V = 64
MLA_Q_LORA = 384
MLA_KV_LORA = 256
MLA_SCALE = (MLA_NOPE + MLA_ROPE) ** -0.5
ROPE_THETA = 10000.0
NAT_HEADS = 16
NAT_DH = 64
NAT_SCALE = NAT_DH ** -0.5
WIN_H = 8
WIN_W = 16
NAT_QROWS = 4
NAT_KROWS = NAT_QROWS + WIN_H
N_EXPERTS = 16
N_GROUPS = 4
EPG = N_EXPERTS // N_GROUPS
D_EXPERT = 256
NORM_EPS = 1e-6
NEG_INF = -1e30

HEAD_PAIRS = MLA_HEADS // 2
VMEM_LIMIT = 56 * 1024 * 1024


def _cparams(*sem):
    return pltpu.CompilerParams(dimension_semantics=sem, vmem_limit_bytes=VMEM_LIMIT)


def _sigmoid(x):
    return 1.0 / (1.0 + jnp.exp(-x))


def _rms(x, g):
    ms = jnp.mean(x * x, axis=-1, keepdims=True)
    return x * lax.rsqrt(ms + NORM_EPS) * g


def _dot(a, b):
    return jnp.dot(a, b, preferred_element_type=F32)


def _dot_nt(a, b):
    return lax.dot_general(a, b, (((1,), (1,)), ((), ())), preferred_element_type=F32)


def _lane_mask(width, idx, dtype):
    lane = lax.broadcasted_iota(jnp.int32, (1, LANES), 1)
    return ((lane >= idx * width) & (lane < (idx + 1) * width)).astype(dtype)


def _ada_kernel(cond_ref, w_ref, b_ref, o_ref):
    c = cond_ref[...]
    s = (c * _sigmoid(c)).astype(BF16)
    o_ref[0] = _dot(s, w_ref[0].astype(BF16)) + b_ref[0]


def _ada_modulation(cond8, w_ada, b_ada):
    n_chunk = 6
    return pl.pallas_call(
        _ada_kernel,
        grid=(DEPTH, n_chunk),
        in_specs=[
            pl.BlockSpec((8, D_MODEL), lambda l, j: (0, 0)),
            pl.BlockSpec((1, D_MODEL, D_MODEL), lambda l, j: (l, 0, j)),
            pl.BlockSpec((1, 1, D_MODEL), lambda l, j: (l, 0, j)),
        ],
        out_specs=pl.BlockSpec((1, 8, D_MODEL), lambda l, j: (l, 0, j)),
        out_shape=jax.ShapeDtypeStruct((DEPTH, 8, 6 * D_MODEL), F32),
        compiler_params=_cparams("parallel", "parallel"),
        name="ada_modulation",
    )(cond8, w_ada, b_ada.reshape(DEPTH, 1, 6 * D_MODEL))


def _mod_spec(rows_per_cond):
    if rows_per_cond is None:
        return pl.BlockSpec((1, 6, D_MODEL), lambda i, *_: (0, 0, 0))
    return pl.BlockSpec((1, 6, D_MODEL), lambda i, *_: (1 + i // rows_per_cond, 0, 0))


def _premix_mla_kernel(*refs, rope):
    if rope:
        (x_ref, mod_ref, g_ref, w_in_ref, qg_ref, w_uq_ref, kvg_ref,
         cosq_ref, sinq_ref, cosk_ref, sink_ref,
         qn_ref, qr_ref, ckv_ref, kr_ref, kr4_ref) = refs
    else:
        (x_ref, mod_ref, g_ref, w_in_ref, qg_ref, w_uq_ref, kvg_ref,
         qn_ref, qr_ref, ckv_ref, kr_ref, kr4_ref) = refs
    h = _rms(x_ref[...], g_ref[...]) * (1.0 + mod_ref[0, 1:2, :]) + mod_ref[0, 0:1, :]
    lat = _dot(h.astype(BF16), w_in_ref[...])
    c_q = lat[:, :MLA_Q_LORA]
    c_kv = lat[:, MLA_Q_LORA:MLA_Q_LORA + MLA_KV_LORA]
    o = MLA_Q_LORA + MLA_KV_LORA
    kr4 = lat[:, o:o + LANES]
    q = _dot(_rms(c_q, qg_ref[...]).astype(BF16), w_uq_ref[...])
    n_nope = MLA_HEADS * MLA_NOPE
    n_rope = MLA_HEADS * MLA_ROPE
    qr = q[:, n_nope:n_nope + n_rope]
    if rope:
        qr = qr * cosq_ref[...] + q[:, n_nope + n_rope:] * sinq_ref[...]
        kr4 = kr4 * cosk_ref[...] + lat[:, o + LANES:o + 2 * LANES] * sink_ref[...]
    qn_ref[...] = q[:, :n_nope].astype(BF16)
    qr_ref[...] = qr.astype(BF16)
    ckv_ref[...] = _rms(c_kv, kvg_ref[...])
    kr_ref[...] = kr4[:, :MLA_ROPE]
    kr4_ref[...] = kr4.astype(BF16)


def _premix_mla(x, mod, g, w_in, qg, w_uq, kvg, rope_tabs, rows_per_cond, tm):
    n = x.shape[0]
    rope = rope_tabs is not None
    row = lambda i: (i, 0)
    full = lambda i: (0, 0)
    in_specs = [
        pl.BlockSpec((tm, D_MODEL), row),
        _mod_spec(None if rows_per_cond is None else rows_per_cond // tm),
        pl.BlockSpec((1, D_MODEL), full),
        pl.BlockSpec(w_in.shape, full),
        pl.BlockSpec((1, MLA_Q_LORA), full),
        pl.BlockSpec(w_uq.shape, full),
        pl.BlockSpec((1, MLA_KV_LORA), full),
    ]
    args = [x, mod, g, w_in, qg, w_uq, kvg]
    if rope:
        nblk = rope_tabs[0].shape[0] // tm
        pos = lambda i: (i % nblk, 0)
        for t in rope_tabs:
            in_specs.append(pl.BlockSpec((tm, t.shape[1]), pos))
            args.append(t)
    widths = (MLA_HEADS * MLA_NOPE, MLA_HEADS * MLA_ROPE, MLA_KV_LORA, MLA_ROPE, LANES)
    dtypes = (BF16, BF16, F32, F32, BF16)
    return pl.pallas_call(
        functools.partial(_premix_mla_kernel, rope=rope),
        grid=(n // tm,),
        in_specs=in_specs,
        out_specs=[pl.BlockSpec((tm, w), row) for w in widths],
        out_shape=[jax.ShapeDtypeStruct((n, w), d) for w, d in zip(widths, dtypes)],
        compiler_params=_cparams("parallel"),
        name="premix_mla_rope" if rope else "premix_mla",
    )(*args)


LOG2E = 1.4426950408889634


def _softmax_pv(s_list, v_list, scale=1.0):
    m = functools.reduce(jnp.maximum, [jnp.max(s, axis=-1, keepdims=True) for s in s_list])
    e_list = [jnp.exp2((s - m) * (scale * LOG2E)) for s in s_list]
    l = functools.reduce(lambda a, b: a + b, [jnp.sum(e, axis=-1, keepdims=True) for e in e_list])
    o = functools.reduce(lambda a, b: a + b,
                         [_dot(e.astype(BF16), v) for e, v in zip(e_list, v_list)])
    return o / l


def _stack_heads(q, masks):
    return jnp.concatenate([q * m for m in masks], axis=0)


def _unstack_heads(o2, width):
    tq = o2.shape[0] // 2
    lane = lax.broadcasted_iota(jnp.int32, (1, LANES), 1)
    return jnp.where(lane < width, o2[:tq], o2[tq:])


def _mla_attn_kernel(*refs, cached):
    if cached:
        (qn_ref, qr_ref, ckv_ref, kr4_ref, cckv_ref, ckr4_ref, w_uk_ref, w_uv_ref,
         o_ref, kn_s, v_s, kr_s) = refs
    else:
        (qn_ref, qr_ref, ckv_ref, kr4_ref, w_uk_ref, w_uv_ref, o_ref, kn_s, v_s, kr_s) = refs
    t_own = ckv_ref.shape[0]

    @pl.when(pl.program_id(1) == 0)
    def _():
        c = ckv_ref[...].astype(BF16)
        kn_s[0:t_own, :] = _dot(c, w_uk_ref[...]).astype(BF16)
        v_s[0:t_own, :] = _dot(c, w_uv_ref[...]).astype(BF16)
        kr_s[0:t_own, :] = kr4_ref[...]
        if cached:
            cc = cckv_ref[0].astype(BF16)
            kn_s[t_own:, :] = _dot(cc, w_uk_ref[...]).astype(BF16)
            v_s[t_own:, :] = _dot(cc, w_uv_ref[...]).astype(BF16)
            kr_s[t_own:, :] = ckr4_ref[0].astype(BF16)

    kr4 = kr_s[...]
    for p in range(HEAD_PAIRS):
        sl = slice(p * LANES, (p + 1) * LANES)
        qn = qn_ref[:, sl]
        qr = qr_ref[:, (p // 2) * LANES:(p // 2 + 1) * LANES]
        k_cat = jnp.concatenate([kn_s[:, sl], kr4], axis=-1)
        q_cat = jnp.concatenate(
            [_stack_heads(qn, [_lane_mask(MLA_NOPE, i, BF16) for i in range(2)]),
             _stack_heads(qr, [_lane_mask(MLA_ROPE, (2 * p + i) % 4, BF16) for i in range(2)])], axis=-1)
        o2 = _softmax_pv([_dot_nt(q_cat, k_cat)], [v_s[:, sl]], MLA_SCALE)
        o_ref[:, sl] = _unstack_heads(o2, MLA_V).astype(BF16)


def _mla_attention(qn, qr, ckv, kr4, cache_ckv, cache_kr4, w_uk, w_uv, n_batch, tq):
    n = qn.shape[0]
    s_own = n // n_batch
    nq = s_own // tq
    cached = cache_ckv is not None
    t_all = s_own + (cache_ckv.shape[1] if cached else 0)
    qrow = lambda b, j: (b * nq + j, 0)
    own = lambda b, j: (b, 0)
    full = lambda b, j: (0, 0)
    in_specs = [
        pl.BlockSpec((tq, qn.shape[1]), qrow),
        pl.BlockSpec((tq, qr.shape[1]), qrow),
        pl.BlockSpec((s_own, MLA_KV_LORA), own),
        pl.BlockSpec((s_own, LANES), own),
    ]
    args = [qn, qr, ckv, kr4]
    if cached:
        in_specs += [pl.BlockSpec((1,) + cache_ckv.shape[1:], lambda b, j: (b, 0, 0)),
                     pl.BlockSpec((1,) + cache_kr4.shape[1:], lambda b, j: (b, 0, 0))]
        args += [cache_ckv, cache_kr4]
    in_specs += [pl.BlockSpec(w_uk.shape, full), pl.BlockSpec(w_uv.shape, full)]
    args += [w_uk, w_uv]
    return pl.pallas_call(
        functools.partial(_mla_attn_kernel, cached=cached),
        grid=(n_batch, nq),
        in_specs=in_specs,
        out_specs=pl.BlockSpec((tq, D_MODEL), qrow),
        out_shape=jax.ShapeDtypeStruct((n, D_MODEL), BF16),
        scratch_shapes=[pltpu.VMEM((t_all, D_MODEL), BF16), pltpu.VMEM((t_all, D_MODEL), BF16),
                        pltpu.VMEM((t_all, LANES), BF16)],
        compiler_params=_cparams("parallel", "arbitrary"),
        name="mla_attention_cached" if cached else "mla_attention",
    )(*args)


def _premix_nat_kernel(x_ref, mod_ref, g_ref, w_ref, q_ref, k_ref, v_ref):
    h = _rms(x_ref[...], g_ref[...]) * (1.0 + mod_ref[0, 1:2, :]) + mod_ref[0, 0:1, :]
    qkv = _dot(h.astype(BF16), w_ref[...])
    q_ref[...] = qkv[:, :D_MODEL].astype(q_ref.dtype)
    k_ref[...] = qkv[:, D_MODEL:2 * D_MODEL].astype(k_ref.dtype)
    v_ref[...] = qkv[:, 2 * D_MODEL:].astype(v_ref.dtype)


def _premix_nat(x, mod, g, w_qkv, rows_per_cond, tm, kv_dtype):
    n = x.shape[0]
    row = lambda i: (i, 0)
    full = lambda i: (0, 0)
    return pl.pallas_call(
        _premix_nat_kernel,
        grid=(n // tm,),
        in_specs=[
            pl.BlockSpec((tm, D_MODEL), row),
            _mod_spec(None if rows_per_cond is None else rows_per_cond // tm),
            pl.BlockSpec((1, D_MODEL), full),
            pl.BlockSpec(w_qkv.shape, full),
        ],
        out_specs=[pl.BlockSpec((tm, D_MODEL), row)] * 3,
        out_shape=[jax.ShapeDtypeStruct((n, D_MODEL), BF16),
                   jax.ShapeDtypeStruct((n, D_MODEL), kv_dtype),
                   jax.ShapeDtypeStruct((n, D_MODEL), kv_dtype)],
        compiler_params=_cparams("parallel"),
        name="premix_nat",
    )(x, mod, g, w_qkv)


def _premix_nat_cache_kernel(*refs, first):
    if first:
        x_ref, mod_ref, g_ref, w_ref, q_ref, k_ref, v_ref = refs
    else:
        x_ref, mod_ref, g_ref, w_ref, _, _, q_ref, k_ref, v_ref = refs
    h = _rms(x_ref[...], g_ref[...]) * (1.0 + mod_ref[0, 1:2, :]) + mod_ref[0, 0:1, :]
    qkv = _dot(h.astype(BF16), w_ref[...])
    q_ref[...] = qkv[:, :D_MODEL].astype(BF16)
    nb, n_slots, s, _ = k_ref.shape
    k_ref[:, 0] = qkv[:, D_MODEL:2 * D_MODEL].reshape(nb, s, D_MODEL)
    v_ref[:, 0] = qkv[:, 2 * D_MODEL:].reshape(nb, s, D_MODEL)
    for slot in range(1, n_slots):
        k_ref[:, slot] = jnp.zeros((nb, s, D_MODEL), F32)
        v_ref[:, slot] = jnp.zeros((nb, s, D_MODEL), F32)


def _premix_nat_cache(x, mod, g, w_qkv, k_buf, v_buf, slot, n_slots, n_batch, tm):
    n = x.shape[0]
    s = n // n_batch
    nb = tm // s
    first = k_buf is None
    row = lambda i: (i, 0)
    full = lambda i: (0, 0)
    in_specs = [pl.BlockSpec((tm, D_MODEL), row), _mod_spec(None), pl.BlockSpec((1, D_MODEL), full),
                pl.BlockSpec(w_qkv.shape, full)]
    args = [x, mod, g, w_qkv]
    if first:
        assert slot == 0
        kv_spec = pl.BlockSpec((nb, n_slots, s, D_MODEL), lambda i: (i, 0, 0, 0))
        aliases = {}
    else:
        in_specs += [pl.BlockSpec(memory_space=pl.ANY)] * 2
        args += [k_buf, v_buf]
        kv_spec = pl.BlockSpec((nb, 1, s, D_MODEL), lambda i: (i, slot, 0, 0))
        aliases = {4: 1, 5: 2}
    kv_shape = jax.ShapeDtypeStruct((n_batch, n_slots, s, D_MODEL), F32)
    return pl.pallas_call(
        functools.partial(_premix_nat_cache_kernel, first=first),
        grid=(n // tm,),
        in_specs=in_specs,
        out_specs=[pl.BlockSpec((tm, D_MODEL), row), kv_spec, kv_spec],
        out_shape=[jax.ShapeDtypeStruct((n, D_MODEL), BF16), kv_shape, kv_shape],
        input_output_aliases=aliases,
        compiler_params=_cparams("parallel"),
        name="premix_nat_cache",
    )(*args)


def _dense_attn_kernel(q_ref, k_ref, v_ref, o_ref):
    for p in range(HEAD_PAIRS):
        sl = slice(p * LANES, (p + 1) * LANES)
        q2 = _stack_heads(q_ref[:, sl], [_lane_mask(NAT_DH, i, BF16) for i in range(2)])
        k = k_ref[:, sl].astype(BF16)
        v = v_ref[:, sl].astype(BF16)
        o2 = _softmax_pv([_dot_nt(q2, k)], [v], NAT_SCALE)
        o_ref[:, sl] = _unstack_heads(o2, NAT_DH).astype(BF16)


def _dense_attention(q, k_buf, v_buf, slot):
    n = q.shape[0]
    n_batch, _, s, _ = k_buf.shape
    blk = pl.BlockSpec((s, D_MODEL), lambda b: (b, 0))
    kv = pl.BlockSpec((None, None, s, D_MODEL), lambda b: (b, slot, 0, 0))
    return pl.pallas_call(
        _dense_attn_kernel,
        grid=(n_batch,),
        in_specs=[blk, kv, kv],
        out_specs=blk,
        out_shape=jax.ShapeDtypeStruct((n, D_MODEL), BF16),
        compiler_params=_cparams("parallel"),
        name="dense_attention",
    )(q, k_buf, v_buf)


_NAT_QBLK = NAT_QROWS * GRID_W
_NAT_KBLK = NAT_KROWS * GRID_W


def _nat_block_plan(rows):
    assert rows % NAT_QROWS == 0 and rows >= NAT_KROWS and NAT_KROWS % 2 == 0
    plan, variants = [], []
    for r0 in range(0, rows, NAT_QROWS):
        ks = min(max(r0 - WIN_H // 2, 0), rows - NAT_KROWS)
        r = r0 + np.arange(NAT_QROWS)
        kr = ks + np.arange(NAT_KROWS)
        rs = np.clip(r - WIN_H // 2, 0, rows - WIN_H)
        valid_row = (kr[None, :] >= rs[:, None]) & (kr[None, :] < rs[:, None] + WIN_H)
        d0 = ks - r + (WIN_H - 1)
        for vi, (d0_v, valid_v) in enumerate(variants):
            if np.array_equal(d0, d0_v) and np.array_equal(valid_row, valid_v):
                break
        else:
            vi = len(variants)
            variants.append((d0, valid_row))
        plan.append((ks, vi))
    return plan, variants


def _nat_bias_row_range(variants):
    lo = min(int(d0.min()) for d0, _ in variants)
    hi = max(int(d0.max()) for d0, _ in variants) + NAT_KROWS
    return lo, hi


def _nat_bias_pairs(rpb, rows):
    n_heads, n_dr, n_dc = rpb.shape
    _, variants = _nat_block_plan(rows)
    lo, hi = _nat_bias_row_range(variants)
    c = np.arange(GRID_W)
    cs = np.clip(c - WIN_W // 2, 0, GRID_W - WIN_W)
    valid_col = (c[None, :] >= cs[:, None]) & (c[None, :] < cs[:, None] + WIN_W)
    d_col = c[None, :] - c[:, None] + (WIN_W - 1)
    sel = (d_col[None] == np.arange(n_dc)[:, None, None]) & valid_col[None]
    toep = jnp.einsum('hdj,jck->hdck', rpb, jnp.asarray(sel, F32), precision=lax.Precision.HIGHEST)
    toep = jnp.where(jnp.asarray(valid_col)[None, None], toep, NEG_INF)
    toep = jnp.pad(toep, ((0, 0), (max(-lo, 0), max(hi + 1 - n_dr, 0)), (0, 0), (0, 0)),
                   constant_values=NEG_INF)
    toep = toep[:, max(lo, 0):]
    return jnp.concatenate([toep[:, :-1], toep[:, 1:]], axis=-1)


def _nat_block_bias(tp_ref, head, d0, valid_row, row_lo):
    neg = jnp.full((GRID_W, 2 * GRID_W), NEG_INF, F32)
    left = lax.broadcasted_iota(jnp.int32, (1, 2 * GRID_W), 1) < GRID_W
    rows_out = []
    for dr in range(NAT_QROWS):
        pieces = []
        for a in range(0, NAT_KROWS, 2):
            ok0, ok1 = bool(valid_row[dr, a]), bool(valid_row[dr, a + 1])
            if not (ok0 or ok1):
                pieces.append(neg)
                continue
            piece = tp_ref[head, int(d0[dr]) + a - row_lo]
            if ok0 and not ok1:
                piece = jnp.where(left, piece, NEG_INF)
            elif ok1 and not ok0:
                piece = jnp.where(left, NEG_INF, piece)
            pieces.append(piece)
        rows_out.append(jnp.concatenate(pieces, axis=-1))
    return jnp.concatenate(rows_out, axis=0)


def _nat_attn_kernel(q_ref, k_ref, v_ref, kc_ref, vc_ref, tp_ref, o_ref, *, plan, variants, row_lo):
    kc = kc_ref[0].astype(BF16)
    vc = vc_ref[0].astype(BF16)
    assert NAT_SCALE == 2.0 ** round(np.log2(NAT_SCALE))
    masks = [_lane_mask(NAT_DH, i, BF16) * NAT_SCALE for i in range(2)]
    for bi, (ks, var) in enumerate(plan):
        q2 = _stack_heads(q_ref[bi * _NAT_QBLK:(bi + 1) * _NAT_QBLK, :], masks)
        k = k_ref[ks * GRID_W:ks * GRID_W + _NAT_KBLK, :]
        v = v_ref[ks * GRID_W:ks * GRID_W + _NAT_KBLK, :]
        bias2 = jnp.concatenate([_nat_block_bias(tp_ref, i, *variants[var], row_lo) for i in range(2)], axis=0)
        o2 = _softmax_pv([_dot_nt(q2, k) + bias2, _dot_nt(q2, kc)], [v, vc])
        o_ref[bi * _NAT_QBLK:(bi + 1) * _NAT_QBLK, :] = _unstack_heads(o2, NAT_DH).astype(BF16)


def _nat_attention(q, k, v, cache_k, cache_v, bias, n_batch):
    n = q.shape[0]
    s = n // n_batch
    plan, variants = _nat_block_plan(s // GRID_W)
    row_lo, _ = _nat_bias_row_range(variants)
    own = pl.BlockSpec((s, LANES), lambda p, b: (b, p))
    cache = pl.BlockSpec((1, cache_k.shape[1], LANES), lambda p, b: (b, 0, p))
    return pl.pallas_call(
        functools.partial(_nat_attn_kernel, plan=plan, variants=variants, row_lo=row_lo),
        grid=(HEAD_PAIRS, n_batch),
        in_specs=[own, own, own, cache, cache,
                  pl.BlockSpec((2,) + bias.shape[1:], lambda p, b: (p, 0, 0, 0))],
        out_specs=own,
        out_shape=jax.ShapeDtypeStruct((n, D_MODEL), BF16),
        compiler_params=_cparams("parallel", "parallel"),
        name="nat_attention",
    )(q, k, v, cache_k, cache_v, bias)


def _proj_res_kernel(x_ref, a_ref, w_ref, mod_ref, o_ref):
    o_ref[...] = x_ref[...] + mod_ref[0, 2:3, :] * _dot(a_ref[...], w_ref[...])


def _proj_residual(x, a, w, mod, rows_per_cond, tm):
    n = x.shape[0]
    row = lambda i: (i, 0)
    return pl.pallas_call(
        _proj_res_kernel,
        grid=(n // tm,),
        in_specs=[
            pl.BlockSpec((tm, D_MODEL), row),
            pl.BlockSpec((tm, D_MODEL), row),
            pl.BlockSpec(w.shape, lambda i: (0, 0)),
            _mod_spec(None if rows_per_cond is None else rows_per_cond // tm),
        ],
        out_specs=pl.BlockSpec((tm, D_MODEL), row),
        out_shape=jax.ShapeDtypeStruct((n, D_MODEL), F32),
        compiler_params=_cparams("parallel"),
        name="proj_residual",
    )(x, a, w, mod)


def _top2_sum(a, b, c, d):
    hi1, lo1 = jnp.maximum(a, b), jnp.minimum(a, b)
    hi2, lo2 = jnp.maximum(c, d), jnp.minimum(c, d)
    return jnp.maximum(hi1, hi2) + jnp.maximum(jnp.minimum(hi1, hi2), jnp.maximum(lo1, lo2))


def _route(scores, biased):
    sc = [scores[e:e + 1, :] for e in range(N_EXPERTS)]
    bs = [biased[e:e + 1, :] for e in range(N_EXPERTS)]
    gscore = [_top2_sum(*bs[EPG * g:EPG * (g + 1)]) for g in range(N_GROUPS)]
    best, gidx = gscore[0], jnp.zeros_like(gscore[0], dtype=jnp.int32)
    for g in range(1, N_GROUPS):
        better = gscore[g] > best
        gidx = jnp.where(better, g, gidx)
        best = jnp.where(better, gscore[g], best)
    cb = [functools.reduce(lambda a, b: a + b,
                           [jnp.where(gidx == g, bs[EPG * g + i], 0.0) for g in range(N_GROUPS)])
          for i in range(EPG)]
    cs = [functools.reduce(lambda a, b: a + b,
                           [jnp.where(gidx == g, sc[EPG * g + i], 0.0) for g in range(N_GROUPS)])
          for i in range(EPG)]
    b1, i1 = cb[0], jnp.zeros_like(gidx)
    for i in range(1, EPG):
        better = cb[i] > b1
        i1 = jnp.where(better, i, i1)
        b1 = jnp.where(better, cb[i], b1)
    b2, i2 = jnp.full_like(b1, -jnp.inf), jnp.full_like(i1, -1)
    for i in range(EPG):
        better = (i1 != i) & (cb[i] > b2)
        i2 = jnp.where(better, i, i2)
        b2 = jnp.where(better, cb[i], b2)
    sel = [(i1 == i) | (i2 == i) for i in range(EPG)]
    w = [jnp.where(sel[i], cs[i], 0.0) for i in range(EPG)]
    tot = w[0] + w[1] + w[2] + w[3]
    return gidx, [w[i] / tot for i in range(EPG)]


MOE_SRC = 256
MOE_DST = 256
MOE_CHUNK = 4
_LOG2_DST = MOE_DST.bit_length() - 1
assert MOE_DST == 1 << _LOG2_DST and MOE_SRC <= MOE_DST
_INFO_GID = EPG
_INFO_RANK = EPG + 1


def _moe_route_kernel(x_ref, mod_ref, g_ref, wr_ref, rb_ref, h_ref, info_ref, infot_ref, cum_ref, carry_s):
    @pl.when(pl.program_id(0) == 0)
    def _():
        carry_s[...] = jnp.zeros_like(carry_s)

    h = _rms(x_ref[...], g_ref[...]) * (1.0 + mod_ref[0, 4:5, :]) + mod_ref[0, 3:4, :]
    h_hi = h.astype(BF16)
    h_ref[:, :D_MODEL] = h_hi
    h_lo = (h - h_hi.astype(F32)).astype(BF16)
    hi_w = _dot(h_hi, wr_ref[...])
    logits = hi_w[:, :LANES] + (_dot(h_lo, wr_ref[:, :LANES]) + hi_w[:, LANES:])
    scores = _sigmoid(logits)
    gidx, gates = _route(scores.T[:N_EXPERTS], (scores + rb_ref[...]).T[:N_EXPERTS])
    tm = h.shape[0]
    sub = lax.broadcasted_iota(jnp.int32, (8, 1), 0)
    onehot = (sub == gidx).astype(F32)
    tri = (lax.broadcasted_iota(jnp.int32, (tm, tm), 0)
           <= lax.broadcasted_iota(jnp.int32, (tm, tm), 1)).astype(BF16)
    prefix = _dot(onehot.astype(BF16), tri)
    carry = carry_s[:, 0:1]
    rank = jnp.sum((prefix - 1.0 + carry) * onehot, axis=0, keepdims=True)
    infot = jnp.concatenate(gates + [gidx.astype(F32), rank, jnp.zeros((2, tm), F32)], axis=0)
    infot_ref[0] = infot
    info_ref[...] = jnp.concatenate([infot, jnp.zeros((LANES - 8, tm), F32)], axis=0).T
    g_hi = [gt.astype(BF16).astype(F32) for gt in gates]
    g_lo = [gt - gh for gt, gh in zip(gates, g_hi)]
    gext = jnp.concatenate(g_hi + g_lo + [jnp.zeros((LANES - 2 * EPG, tm), F32)], axis=0)
    h_ref[:, D_MODEL:] = gext.T.astype(BF16)
    carry = jnp.broadcast_to(carry + prefix[:, tm - 1:tm], (8, LANES))
    carry_s[...] = carry
    cum_ref[0] = carry.astype(jnp.int32)


def _group_tiles(cum_ref):
    n_src = cum_ref.shape[0]
    tot = [cum_ref[n_src - 1, g] for g in range(N_GROUPS)]
    ends, acc = [], 0
    for t in tot:
        acc = acc + lax.shift_right_logical(t + (MOE_DST - 1), _LOG2_DST)
        ends.append(acc)
    return tot, ends


def _dest_tile(d, cum_ref):
    tot, ends = _group_tiles(cum_ref)
    g = ((d >= ends[0]).astype(jnp.int32) + (d >= ends[1]).astype(jnp.int32)
         + (d >= ends[2]).astype(jnp.int32))
    first = jnp.where(g == 0, 0, jnp.where(g == 1, ends[0], jnp.where(g == 2, ends[1], ends[2])))
    tot_g = jnp.where(g == 0, tot[0], jnp.where(g == 1, tot[1], jnp.where(g == 2, tot[2], tot[3])))
    k0 = (d - first) * MOE_DST
    n_valid = jnp.clip(tot_g - k0, 0, MOE_DST)
    return g, k0, n_valid


def _counts_before(cum_ref, s, g):
    return jnp.where(s > 0, cum_ref[jnp.maximum(s - 1, 0), g], 0)


def _moe_ffn_kernel(cum_ref, h_ref, infot_ref, wg_ref, wu_ref, wd_ref, y_ref, xs_s, ptr_s):
    d = pl.program_id(0)
    g, k0, n_valid = _dest_tile(d, cum_ref)
    n_src = cum_ref.shape[0]

    @pl.when(d == 0)
    def _():
        ptr_s[0] = 0

    @pl.when(n_valid == 0)
    def _():
        y_ref[...] = jnp.zeros_like(y_ref)

    @pl.when(n_valid > 0)
    def _():
        need = k0 + n_valid
        start = jnp.where(k0 == 0, 0, ptr_s[0])
        s_lo = lax.while_loop(lambda s: cum_ref[s, g] <= k0, lambda s: s + 1, start)
        ptr_s[0] = s_lo
        xs_s[...] = jnp.zeros_like(xs_s)
        row = lax.broadcasted_iota(jnp.int32, (MOE_DST, 1), 0).astype(F32)
        gf = g.astype(F32)
        k0f = k0.astype(F32)

        def more(s0):
            return (_counts_before(cum_ref, s0, g) < need) & (s0 < n_src)

        def gather(s0):
            base = jnp.minimum(s0, n_src - MOE_CHUNK)
            perms = []
            for j in range(MOE_CHUNK):
                it = infot_ref[base + j]
                shift = k0f + jnp.where(base + j >= s0, 0.0, 1e9)
                rel = jnp.where(it[_INFO_GID:_INFO_GID + 1] == gf,
                                it[_INFO_RANK:_INFO_RANK + 1] - shift, -1.0)
                perms.append((row == rel).astype(BF16))
            perm = jnp.concatenate(perms, axis=-1)
            rows = pl.ds(pl.multiple_of(base * MOE_SRC, MOE_SRC), MOE_CHUNK * MOE_SRC)
            xs_s[...] += _dot(perm, h_ref[rows, :])
            return base + MOE_CHUNK

        lax.while_loop(more, gather, s_lo)
        xs = xs_s[:, :D_MODEL].astype(BF16)
        gext = xs_s[:, D_MODEL:]
        acts = []
        for i in range(EPG):
            a = _dot(xs, wg_ref[i])
            u = _dot(xs, wu_ref[i])
            gate = gext[:, i:i + 1] + gext[:, EPG + i:EPG + i + 1]
            acts.append(((a * _sigmoid(a)) * u * gate).astype(BF16))
        act = jnp.concatenate(acts, axis=-1)
        y_ref[...] = _dot(act, wd_ref[...].reshape(EPG * D_EXPERT, D_MODEL)).astype(BF16)


def _moe_unsort_kernel(cum_ref, x_ref, mod_ref, info_ref, y_ref, o_ref, acc_s):
    s = pl.program_id(0)
    _, ends = _group_tiles(cum_ref)
    firsts = [0] + ends[:-1]
    inf = info_ref[...]
    gid = inf[:, _INFO_GID:_INFO_GID + 1]
    rank = inf[:, _INFO_RANK:_INFO_RANK + 1]
    col = lax.broadcasted_iota(jnp.int32, (1, MOE_DST), 1).astype(F32)

    def from_tile(g, k):
        rel = jnp.where(gid == float(g), rank - (k * MOE_DST).astype(F32), -1.0)
        perm = (rel == col).astype(BF16)
        rows = pl.ds(pl.multiple_of((firsts[g] + k) * MOE_DST, MOE_DST), MOE_DST)
        return _dot(perm, y_ref[rows, :])

    lo = [_counts_before(cum_ref, s, g) for g in range(N_GROUPS)]
    k_lo = [lax.shift_right_logical(l, _LOG2_DST) for l in lo]
    first_tile = [jnp.minimum(k, ends[g] - firsts[g] - 1) for g, k in enumerate(k_lo)]
    acc_s[...] = functools.reduce(lambda a, b: a + b,
                                  [from_tile(g, jnp.maximum(first_tile[g], 0)) for g in range(N_GROUPS)])
    for g in range(N_GROUPS):
        @pl.when(cum_ref[s, g] > (k_lo[g] + 1) * MOE_DST)
        def _():
            acc_s[...] += from_tile(g, k_lo[g] + 1)
    o_ref[...] = x_ref[...] + mod_ref[0, 5:6, :] * acc_s[...]


def _moe(x, mod, g, w_router, router_bias, wg, wu, wd, layer, rows_per_cond):
    n = x.shape[0]
    n_src = n // MOE_SRC
    n_dst = n // MOE_DST + N_GROUPS
    row = lambda i: (i, 0)
    full = lambda i: (0, 0)
    h, info, infot, cum = pl.pallas_call(
        _moe_route_kernel,
        grid=(n_src,),
        in_specs=[
            pl.BlockSpec((MOE_SRC, D_MODEL), row),
            _mod_spec(None if rows_per_cond is None else rows_per_cond // MOE_SRC),
            pl.BlockSpec((1, D_MODEL), full),
            pl.BlockSpec(w_router.shape, full),
            pl.BlockSpec(router_bias.shape, full),
        ],
        out_specs=[pl.BlockSpec((MOE_SRC, D_MODEL + LANES), row),
                   pl.BlockSpec((MOE_SRC, LANES), row),
                   pl.BlockSpec((1, 8, MOE_SRC), lambda i: (i, 0, 0)),
                   pl.BlockSpec((1, 8, LANES), lambda i: (i, 0, 0))],
        out_shape=[jax.ShapeDtypeStruct((n, D_MODEL + LANES), BF16),
                   jax.ShapeDtypeStruct((n, LANES), F32),
                   jax.ShapeDtypeStruct((n_src, 8, MOE_SRC), F32),
                   jax.ShapeDtypeStruct((n_src, 8, LANES), jnp.int32)],
        scratch_shapes=[pltpu.VMEM((8, LANES), F32)],
        compiler_params=_cparams("arbitrary"),
        name="moe_route",
    )(x, mod, g, w_router, router_bias)
    cum = cum[:, :N_GROUPS, 0]

    resident = lambda a: pl.BlockSpec(a.shape, lambda d, c: (0,) * a.ndim, pipeline_mode=pl.Buffered(1))
    expert = lambda shape: pl.BlockSpec(shape, lambda d, c: (layer, _dest_tile(d, c)[0], 0, 0))
    y = pl.pallas_call(
        _moe_ffn_kernel,
        grid_spec=pltpu.PrefetchScalarGridSpec(
            num_scalar_prefetch=1,
            grid=(n_dst,),
            in_specs=[resident(h), resident(infot),
                      expert((None, EPG, D_MODEL, D_EXPERT)),
                      expert((None, EPG, D_MODEL, D_EXPERT)),
                      expert((None, EPG, D_EXPERT, D_MODEL))],
            out_specs=pl.BlockSpec((MOE_DST, D_MODEL), lambda d, c: (d, 0)),
            scratch_shapes=[pltpu.VMEM((MOE_DST, D_MODEL + LANES), F32), pltpu.SMEM((1,), jnp.int32)],
        ),
        out_shape=jax.ShapeDtypeStruct((n_dst * MOE_DST, D_MODEL), BF16),
        compiler_params=_cparams("arbitrary"),
        name="moe_experts",
    )(cum, h, infot, wg, wu, wd)

    return pl.pallas_call(
        _moe_unsort_kernel,
        grid_spec=pltpu.PrefetchScalarGridSpec(
            num_scalar_prefetch=1,
            grid=(n_src,),
            in_specs=[pl.BlockSpec((MOE_SRC, D_MODEL), lambda i, c: (i, 0)),
                      _mod_spec(None if rows_per_cond is None else rows_per_cond // MOE_SRC),
                      pl.BlockSpec((MOE_SRC, LANES), lambda i, c: (i, 0)),
                      resident(y)],
            out_specs=pl.BlockSpec((MOE_SRC, D_MODEL), lambda i, c: (i, 0)),
            scratch_shapes=[pltpu.VMEM((MOE_SRC, D_MODEL), F32)],
        ),
        out_shape=jax.ShapeDtypeStruct((n, D_MODEL), F32),
        compiler_params=_cparams("parallel"),
        name="moe_unsort",
    )(cum, x, mod, info, y)


MOE_ALIGN = 16
_LOG2_ALIGN = MOE_ALIGN.bit_length() - 1
_LOC_ROWS = MOE_SRC + N_GROUPS * MOE_ALIGN
_INFO_LRANK = EPG + 1
H_EXT = D_MODEL + LANES
_PIECES = [MOE_SRC >> i for i in range((MOE_SRC // MOE_ALIGN).bit_length())]


def _moe_sort_tile(s, x_ref, mod_ref, g_ref, wr_ref, rb_ref, info_ref, cum_ref, lsort, seg_s, base_s):
    h = _rms(x_ref[...], g_ref[...]) * (1.0 + mod_ref[0, 4:5, :]) + mod_ref[0, 3:4, :]
    h_hi = h.astype(BF16)
    h_lo = (h - h_hi.astype(F32)).astype(BF16)
    hi_w = _dot(h_hi, wr_ref[...])
    logits = hi_w[:, :LANES] + (_dot(h_lo, wr_ref[:, :LANES]) + hi_w[:, LANES:])
    scores = _sigmoid(logits)
    gidx, gates = _route(scores.T[:N_EXPERTS], (scores + rb_ref[...]).T[:N_EXPERTS])
    tm = h.shape[0]
    sub = lax.broadcasted_iota(jnp.int32, (8, 1), 0)
    onehot = (sub == gidx).astype(F32)
    tri = (lax.broadcasted_iota(jnp.int32, (tm, tm), 0)
           <= lax.broadcasted_iota(jnp.int32, (tm, tm), 1)).astype(BF16)
    prefix = _dot(onehot.astype(BF16), tri)
    lrank = jnp.sum((prefix - 1.0) * onehot, axis=0, keepdims=True)

    count = [jnp.sum(onehot[g:g + 1, :]).astype(jnp.int32) for g in range(N_GROUPS)]
    padded = [lax.shift_left(lax.shift_right_logical(c + (MOE_ALIGN - 1), _LOG2_ALIGN), _LOG2_ALIGN)
              for c in count]
    loff, acc = [], 0
    for p in padded:
        loff.append(acc)
        acc = acc + p
    lpos = lrank + functools.reduce(
        lambda a, b: a + b, [jnp.where(gidx == g, jnp.asarray(v, jnp.int32).astype(F32), 0.0)
                             for g, v in enumerate(loff)])
    infot = jnp.concatenate(gates + [gidx.astype(F32), lrank, jnp.zeros((LANES - EPG - 2, tm), F32)], axis=0)
    info_ref[...] = infot.T
    g_hi = [gt.astype(BF16).astype(F32) for gt in gates]
    g_lo = [gt - gh for gt, gh in zip(gates, g_hi)]
    gext = jnp.concatenate(g_hi + g_lo + [jnp.zeros((LANES - 2 * EPG, tm), F32)], axis=0)
    h_ext = jnp.concatenate([h_hi, gext.T.astype(BF16)], axis=-1)
    perm = (lax.broadcasted_iota(jnp.int32, (_LOC_ROWS, 1), 0).astype(F32) == lpos).astype(BF16)
    lsort[s] = _dot(perm, h_ext).astype(BF16)

    new_base = []
    for g in range(N_GROUPS):
        seg_s[s, g] = padded[g]
        new_base.append(base_s[g] + padded[g])
        base_s[g] = new_base[g]
    cum = functools.reduce(lambda a, b: a + b,
                           [jnp.where(sub == g, new_base[g], 0) for g in range(N_GROUPS)])
    cum_ref[0] = jnp.broadcast_to(cum, (8, LANES))


def _copy_pieces(length, make_copy, wait):
    done = jnp.int32(0)
    for rows in _PIECES:
        take = (length & rows) != 0 if rows < MOE_SRC else length >= rows

        @pl.when(take)
        def _():
            cp = make_copy(pl.multiple_of(done, MOE_ALIGN), rows)
            cp.wait() if wait else cp.start()
        done = done + jnp.where(take, rows, 0)


def _moe_sort_kernel(x_ref, mod_ref, g_ref, wr_ref, rb_ref, info_ref, cum_ref, srt_ref,
                     lsort, zeros_v, seg_s, base_s, sem):
    s = pl.program_id(0)
    n_src = lsort.shape[0]
    n_rows = srt_ref.shape[0]

    @pl.when(s == 0)
    def _():
        for g in range(N_GROUPS):
            base_s[g] = 0
        zeros_v[...] = jnp.zeros_like(zeros_v)

    @pl.when(s < n_src)
    def _():
        _moe_sort_tile(s, x_ref, mod_ref, g_ref, wr_ref, rb_ref, info_ref, cum_ref, lsort, seg_s, base_s)

    @pl.when(s == n_src)
    def _():
        starts, acc = [], jnp.int32(0)
        for g in range(N_GROUPS):
            starts.append(acc)
            acc = acc + lax.shift_left(
                lax.shift_right_logical(base_s[g] + (MOE_DST - 1), _LOG2_DST), _LOG2_DST)
        end = acc

        def all_copies(wait):
            def tile_copies(t, dst):
                src = jnp.int32(0)
                new_dst = []
                for g in range(N_GROUPS):
                    n = seg_s[t, g]
                    _copy_pieces(n, lambda off, rows, src=src, g=g: pltpu.make_async_copy(
                        lsort.at[t, pl.ds(pl.multiple_of(src + off, MOE_ALIGN), rows)],
                        srt_ref.at[pl.ds(pl.multiple_of(dst[g] + off, MOE_ALIGN), rows)], sem.at[0]), wait)
                    src = src + n
                    new_dst.append(dst[g] + n)
                return tuple(new_dst)

            dst = lax.fori_loop(0, n_src, tile_copies, tuple(starts))
            for g in range(N_GROUPS):
                gap_end = starts[g + 1] if g + 1 < N_GROUPS else end
                _copy_pieces(gap_end - dst[g], lambda off, rows, g=g: pltpu.make_async_copy(
                    zeros_v.at[pl.ds(0, rows)],
                    srt_ref.at[pl.ds(pl.multiple_of(dst[g] + off, MOE_ALIGN), rows)], sem.at[0]), wait)

            def zero_tile(r, carry):
                cp = pltpu.make_async_copy(
                    zeros_v, srt_ref.at[pl.ds(pl.multiple_of(end + r * MOE_DST, MOE_DST), MOE_DST)], sem.at[0])
                cp.wait() if wait else cp.start()
                return carry

            lax.fori_loop(0, lax.shift_right_logical(n_rows - end, _LOG2_DST), zero_tile, 0)

        all_copies(False)
        all_copies(True)


def _moe_ffn2_kernel(cum_ref, xs_ref, wg_ref, wu_ref, wd_ref, y_ref):
    _, ends = _group_tiles(cum_ref)

    @pl.when(pl.program_id(0) >= ends[-1])
    def _():
        y_ref[...] = jnp.zeros_like(y_ref)

    @pl.when(pl.program_id(0) < ends[-1])
    def _():
        xs = xs_ref[:, :D_MODEL]
        gext = xs_ref[:, D_MODEL:].astype(F32)
        acts = []
        for i in range(EPG):
            a = _dot(xs, wg_ref[i])
            u = _dot(xs, wu_ref[i])
            gate = gext[:, i:i + 1] + gext[:, EPG + i:EPG + i + 1]
            acts.append(((a * _sigmoid(a)) * u * gate).astype(BF16))
        act = jnp.concatenate(acts, axis=-1)
        y_ref[...] = _dot(act, wd_ref[...].reshape(EPG * D_EXPERT, D_MODEL)).astype(BF16)


def _segment_row(cum_ref, s, g):
    _, ends = _group_tiles(cum_ref)
    first_tile = ends[g - 1] if g > 0 else 0
    return first_tile * MOE_DST + _counts_before(cum_ref, s, g)


def _unsort_window(cum_ref, s, g, n_rows):
    return pl.multiple_of(jnp.minimum(_segment_row(cum_ref, s, g), n_rows - MOE_DST), MOE_ALIGN)


def _moe_unsort2_kernel(cum_ref, x_ref, mod_ref, info_ref, y0_ref, y1_ref, y2_ref, y3_ref, o_ref, *, n_rows):
    s = pl.program_id(0)
    inf = info_ref[...]
    gid = inf[:, _INFO_GID:_INFO_GID + 1]
    lrank = inf[:, _INFO_LRANK:_INFO_LRANK + 1]
    col = lax.broadcasted_iota(jnp.int32, (1, MOE_DST), 1).astype(F32)
    perms = []
    for g in range(N_GROUPS):
        shift = _segment_row(cum_ref, s, g) - _unsort_window(cum_ref, s, g, n_rows)
        rel = jnp.where(gid == float(g), lrank + shift.astype(F32), -1.0)
        perms.append((rel == col).astype(BF16))
    perm = jnp.concatenate(perms, axis=-1)
    ycat = jnp.concatenate([y0_ref[...], y1_ref[...], y2_ref[...], y3_ref[...]], axis=0)
    o_ref[...] = x_ref[...] + mod_ref[0, 5:6, :] * _dot(perm, ycat)


def _moe_sorted(x, mod, g, w_router, router_bias, wg, wu, wd, layer, rows_per_cond):
    n = x.shape[0]
    n_src = n // MOE_SRC
    n_dst = (n + N_GROUPS * n_src * (MOE_ALIGN - 1)) // MOE_DST + N_GROUPS
    n_rows = n_dst * MOE_DST
    last = n_src - 1
    row = lambda i: (jnp.minimum(i, last), 0)
    full = lambda i: (0, 0)
    info, cum, srt = pl.pallas_call(
        _moe_sort_kernel,
        grid=(n_src + 1,),
        in_specs=[
            pl.BlockSpec((MOE_SRC, D_MODEL), row),
            pl.BlockSpec((1, 6, D_MODEL), (lambda i: (0, 0, 0)) if rows_per_cond is None else
                         (lambda i: (1 + jnp.minimum(i, last) // (rows_per_cond // MOE_SRC), 0, 0))),
            pl.BlockSpec((1, D_MODEL), full),
            pl.BlockSpec(w_router.shape, full),
            pl.BlockSpec(router_bias.shape, full),
        ],
        out_specs=[pl.BlockSpec((MOE_SRC, LANES), row),
                   pl.BlockSpec((1, 8, LANES), lambda i: (jnp.minimum(i, last), 0, 0)),
                   pl.BlockSpec(memory_space=pl.ANY)],
        out_shape=[jax.ShapeDtypeStruct((n, LANES), F32),
                   jax.ShapeDtypeStruct((n_src, 8, LANES), jnp.int32),
                   jax.ShapeDtypeStruct((n_rows, H_EXT), BF16)],
        scratch_shapes=[pltpu.VMEM((n_src, _LOC_ROWS, H_EXT), BF16),
                        pltpu.VMEM((MOE_DST, H_EXT), BF16),
                        pltpu.SMEM((n_src, N_GROUPS), jnp.int32),
                        pltpu.SMEM((N_GROUPS,), jnp.int32),
                        pltpu.SemaphoreType.DMA((1,))],
        compiler_params=_cparams("arbitrary"),
        name="moe_sort",
    )(x, mod, g, w_router, router_bias)
    cum = cum[:, :N_GROUPS, 0]

    blk = lambda width: pl.BlockSpec((MOE_DST, width), lambda d, c: (d, 0))
    expert = lambda shape: pl.BlockSpec(shape, lambda d, c: (layer, _dest_tile(d, c)[0], 0, 0))
    y = pl.pallas_call(
        _moe_ffn2_kernel,
        grid_spec=pltpu.PrefetchScalarGridSpec(
            num_scalar_prefetch=1,
            grid=(n_dst,),
            in_specs=[blk(H_EXT),
                      expert((None, EPG, D_MODEL, D_EXPERT)),
                      expert((None, EPG, D_MODEL, D_EXPERT)),
                      expert((None, EPG, D_EXPERT, D_MODEL))],
            out_specs=blk(D_MODEL),
        ),
        out_shape=jax.ShapeDtypeStruct((n_rows, D_MODEL), BF16),
        compiler_params=_cparams("arbitrary"),
        name="moe_experts",
    )(cum, srt, wg, wu, wd)

    win = lambda g: pl.BlockSpec((pl.Element(MOE_DST), pl.Element(D_MODEL)),
                                 lambda i, c: (_unsort_window(c, i, g, n_rows), 0))
    return pl.pallas_call(
        functools.partial(_moe_unsort2_kernel, n_rows=n_rows),
        grid_spec=pltpu.PrefetchScalarGridSpec(
            num_scalar_prefetch=1,
            grid=(n_src,),
            in_specs=[pl.BlockSpec((MOE_SRC, D_MODEL), lambda i, c: (i, 0)),
                      _mod_spec(None if rows_per_cond is None else rows_per_cond // MOE_SRC),
                      pl.BlockSpec((MOE_SRC, LANES), lambda i, c: (i, 0))]
                     + [win(g) for g in range(N_GROUPS)],
            out_specs=pl.BlockSpec((MOE_SRC, D_MODEL), lambda i, c: (i, 0)),
        ),
        out_shape=jax.ShapeDtypeStruct((n, D_MODEL), F32),
        compiler_params=_cparams("parallel"),
        name="moe_unsort",
    )(cum, x, mod, info, y, y, y, y)


def _final_norm_kernel(x_ref, g_ref, o_ref):
    o_ref[...] = _rms(x_ref[...], g_ref[...])


def _final_norm(x, g, tm):
    n = x.shape[0]
    return pl.pallas_call(
        _final_norm_kernel,
        grid=(n // tm,),
        in_specs=[pl.BlockSpec((tm, D_MODEL), lambda i: (i, 0)),
                  pl.BlockSpec((1, D_MODEL), lambda i: (0, 0))],
        out_specs=pl.BlockSpec((tm, D_MODEL), lambda i: (i, 0)),
        out_shape=jax.ShapeDtypeStruct((n, D_MODEL), F32),
        compiler_params=_cparams("parallel"),
        name="final_norm",
    )(x, g)


def _rope_tables(n_lat):
    t = np.arange(n_lat)
    n_freq = MLA_ROPE // 4
    inv_freq = jnp.asarray(ROPE_THETA, F32) ** (-jnp.arange(n_freq, dtype=F32) / n_freq)
    ar = jnp.asarray(t // GRID_W, F32)[:, None] * inv_freq
    ac = jnp.asarray(t % GRID_W, F32)[:, None] * inv_freq
    cos = jnp.concatenate([jnp.cos(ar), jnp.cos(ar), jnp.cos(ac), jnp.cos(ac)], axis=-1)
    sin = jnp.concatenate([-jnp.sin(ar), jnp.sin(ar), -jnp.sin(ac), jnp.sin(ac)], axis=-1)
    return (jnp.tile(cos, (1, MLA_HEADS)), jnp.tile(sin, (1, MLA_HEADS)),
            jnp.tile(cos, (1, LANES // MLA_ROPE)), jnp.tile(sin, (1, LANES // MLA_ROPE)))


_ROPE_SWAP = np.concatenate([np.arange(8, 16), np.arange(0, 8), np.arange(24, 32), np.arange(16, 24)])


def _mla_weights(w_in, w_uq, w_ukv):
    o = MLA_Q_LORA + MLA_KV_LORA
    kr = w_in[:, o:]
    rep = LANES // MLA_ROPE
    w_in_x = jnp.concatenate([w_in[:, :o], jnp.tile(kr, (1, rep)), jnp.tile(kr[:, _ROPE_SWAP], (1, rep))],
                             axis=-1).astype(BF16)
    uq = w_uq.reshape(MLA_Q_LORA, MLA_HEADS, MLA_NOPE + MLA_ROPE)
    q_nope = uq[:, :, :MLA_NOPE].reshape(MLA_Q_LORA, -1)
    q_rope = uq[:, :, MLA_NOPE:]
    w_uq_x = jnp.concatenate([q_nope, q_rope.reshape(MLA_Q_LORA, -1),
                              q_rope[:, :, _ROPE_SWAP].reshape(MLA_Q_LORA, -1)], axis=-1).astype(BF16)
    ukv = w_ukv.reshape(MLA_KV_LORA, MLA_HEADS, MLA_NOPE + MLA_V)
    w_uk = ukv[:, :, :MLA_NOPE].reshape(MLA_KV_LORA, -1).astype(BF16)
    w_uv = ukv[:, :, MLA_NOPE:].reshape(MLA_KV_LORA, -1).astype(BF16)
    return w_in_x, w_uq_x, w_uk, w_uv


def kernel(x_prompt, x_sample, cache_mla_ckv, cache_mla_krope, cache_nat_k, cache_nat_v, c, c_ctx,
           w_ada, b_ada, norm_mix, norm_ffn, norm_final, mla_w_in, mla_q_norm, mla_w_uq, mla_kv_norm,
           mla_w_ukv, mla_w_o, nat_w_qkv, nat_rpb, nat_w_o, w_router, router_bias,
           moe_w_gate, moe_w_up, moe_w_down):
    B, S, D = x_prompt.shape
    Bd, Sd, _ = x_sample.shape
    assert D == D_MODEL and Bd + 1 <= 8
    tm_c, tm_s = 512, 512

    xc = x_prompt.reshape(B * S, D)
    xs = x_sample.reshape(Bd * Sd, D)
    cond8 = jnp.concatenate([c_ctx[None, :], c, jnp.zeros((8 - 1 - Bd, D), F32)], axis=0)
    mod_all = _ada_modulation(cond8, w_ada, b_ada).reshape(DEPTH, 8, 6, D)

    rope_tabs = _rope_tables(Sd)
    wr = jnp.pad(w_router, ((0, 0), (0, LANES - N_EXPERTS)))
    wr_hi = wr.astype(BF16)
    wr = jnp.concatenate([wr_hi, (wr - wr_hi.astype(F32)).astype(BF16)], axis=1)
    rb = jnp.pad(router_bias, (0, LANES - N_EXPERTS)).reshape(1, LANES)
    wg = moe_w_gate.astype(BF16)
    wu = moe_w_up.astype(BF16)
    wd = moe_w_down.astype(BF16)

    new_ckv, new_krope = [], []
    k_buf = v_buf = None
    for layer in range(DEPTH):
        mod = mod_all[layer]
        g_mix = norm_mix[layer][None, :]
        j = layer // 2
        if layer % 2 == 0:
            w_in_x, w_uq_x, w_uk, w_uv = _mla_weights(mla_w_in[j], mla_w_uq[j], mla_w_ukv[j])
            w_in_c = w_in_x[:, :MLA_Q_LORA + MLA_KV_LORA + LANES]
            w_uq_c = w_uq_x[:, :MLA_HEADS * (MLA_NOPE + MLA_ROPE)]
            qg = mla_q_norm[j][None, :]
            kvg = mla_kv_norm[j][None, :]
            w_o = mla_w_o[j].astype(BF16)
            qn, qr, ckv, kr, kr4 = _premix_mla(xc, mod, g_mix, w_in_c, qg, w_uq_c, kvg, None, None, tm_c)
            new_ckv.append(ckv.reshape(B, S, MLA_KV_LORA))
            new_krope.append(kr.reshape(B, S, MLA_ROPE))
            att_c = _mla_attention(qn, qr, ckv, kr4, None, None, w_uk, w_uv, B, S)
            qn, qr, ckv, kr, kr4 = _premix_mla(xs, mod, g_mix, w_in_x, qg, w_uq_x, kvg, rope_tabs, Sd, tm_s)
            cache_kr4 = jnp.tile(cache_mla_krope[:, j], (1, 1, LANES // MLA_ROPE))
            att_s = _mla_attention(qn, qr, ckv, kr4, cache_mla_ckv[:, j], cache_kr4, w_uk, w_uv, Bd, 256)
        else:
            w_qkv = nat_w_qkv[j].astype(BF16)
            w_o = nat_w_o[j].astype(BF16)
            q, k_buf, v_buf = _premix_nat_cache(xc, mod, g_mix, w_qkv, k_buf, v_buf, j, DEPTH // 2, B, tm_c)
            att_c = _dense_attention(q, k_buf, v_buf, j)
            q, k, v = _premix_nat(xs, mod, g_mix, w_qkv, Sd, tm_s, BF16)
            bias = _nat_bias_pairs(nat_rpb[j], Sd // GRID_W)
            att_s = _nat_attention(q, k, v,
                                   cache_nat_k[:, j].reshape(Bd, -1, D), cache_nat_v[:, j].reshape(Bd, -1, D),
                                   bias, Bd)
        xc = _proj_residual(xc, att_c, w_o, mod, None, tm_c)
        xs = _proj_residual(xs, att_s, w_o, mod, Sd, tm_s)
        g_ffn = norm_ffn[layer][None, :]
        xc = _moe_sorted(xc, mod, g_ffn, wr, rb, wg, wu, wd, layer, None)
        xs = _moe_sorted(xs, mod, g_ffn, wr, rb, wg, wu, wd, layer, Sd)

    gf = norm_final[None, :]
    y_prompt = _final_norm(xc, gf, tm_c).reshape(B, S, D)
    y_sample = _final_norm(xs, gf, tm_s).reshape(Bd, Sd, D)
    kv_shape = (B, DEPTH // 2, S, NAT_HEADS, NAT_DH)
    return (y_prompt, y_sample, jnp.stack(new_ckv, axis=1), jnp.stack(new_krope, axis=1),
            k_buf.reshape(kv_shape), v_buf.reshape(kv_shape))
```

```python
import functools

import numpy as np
import jax
import jax.numpy as jnp
from jax import lax
from jax.experimental import pallas as pl
from jax.experimental.pallas import tpu as pltpu

F32 = jnp.float32
BF16 = jnp.bfloat16

D_MODEL = 1024
DEPTH = 4
GRID_W = 64
LANES = 128
MLA_HEADS = 16
MLA_NOPE = 64
MLA_ROPE = 32
MLA_V = 64
MLA_Q_LORA = 384
MLA_KV_LORA = 256
MLA_SCALE = (MLA_NOPE + MLA_ROPE) ** -0.5
ROPE_THETA = 10000.0
NAT_HEADS = 16
NAT_DH = 64
NAT_SCALE = NAT_DH ** -0.5
WIN_H = 8
WIN_W = 16
NAT_QROWS = 4
NAT_KROWS = NAT_QROWS + WIN_H
N_EXPERTS = 16
N_GROUPS = 4
EPG = N_EXPERTS // N_GROUPS
D_EXPERT = 256
NORM_EPS = 1e-6
NEG_INF = -1e30

HEAD_PAIRS = MLA_HEADS // 2
VMEM_LIMIT = 56 * 1024 * 1024


def _cparams(*sem):
    return pltpu.CompilerParams(dimension_semantics=sem, vmem_limit_bytes=VMEM_LIMIT)


def _sigmoid(x):
    return 1.0 / (1.0 + jnp.exp(-x))


def _rms(x, g):
    ms = jnp.mean(x * x, axis=-1, keepdims=True)
    return x * lax.rsqrt(ms + NORM_EPS) * g


def _dot(a, b):
    return jnp.dot(a, b, preferred_element_type=F32)


def _dot_nt(a, b):
    return lax.dot_general(a, b, (((1,), (1,)), ((), ())), preferred_element_type=F32)


def _lane_mask(width, idx, dtype):
    lane = lax.broadcasted_iota(jnp.int32, (1, LANES), 1)
    return ((lane >= idx * width) & (lane < (idx + 1) * width)).astype(dtype)


def _ada_kernel(cond_ref, w_ref, b_ref, o_ref):
    c = cond_ref[...]
    s = (c * _sigmoid(c)).astype(BF16)
    o_ref[0] = _dot(s, w_ref[0].astype(BF16)) + b_ref[0]


def _ada_modulation(cond8, w_ada, b_ada):
    n_chunk = 6
    return pl.pallas_call(
        _ada_kernel,
        grid=(DEPTH, n_chunk),
        in_specs=[
            pl.BlockSpec((8, D_MODEL), lambda l, j: (0, 0)),
            pl.BlockSpec((1, D_MODEL, D_MODEL), lambda l, j: (l, 0, j)),
            pl.BlockSpec((1, 1, D_MODEL), lambda l, j: (l, 0, j)),
        ],
        out_specs=pl.BlockSpec((1, 8, D_MODEL), lambda l, j: (l, 0, j)),
        out_shape=jax.ShapeDtypeStruct((DEPTH, 8, 6 * D_MODEL), F32),
        compiler_params=_cparams("parallel", "parallel"),
        name="ada_modulation",
    )(cond8, w_ada, b_ada.reshape(DEPTH, 1, 6 * D_MODEL))


def _mod_spec(rows_per_cond):
    if rows_per_cond is None:
        return pl.BlockSpec((1, 6, D_MODEL), lambda i, *_: (0, 0, 0))
    return pl.BlockSpec((1, 6, D_MODEL), lambda i, *_: (1 + i // rows_per_cond, 0, 0))


def _premix_mla_kernel(*refs, rope):
    if rope:
        (x_ref, mod_ref, g_ref, w_in_ref, qg_ref, w_uq_ref, kvg_ref,
         cosq_ref, sinq_ref, cosk_ref, sink_ref,
         qn_ref, qr_ref, ckv_ref, kr_ref, kr4_ref) = refs
    else:
        (x_ref, mod_ref, g_ref, w_in_ref, qg_ref, w_uq_ref, kvg_ref,
         qn_ref, qr_ref, ckv_ref, kr_ref, kr4_ref) = refs
    h = _rms(x_ref[...], g_ref[...]) * (1.0 + mod_ref[0, 1:2, :]) + mod_ref[0, 0:1, :]
    lat = _dot(h.astype(BF16), w_in_ref[...])
    c_q = lat[:, :MLA_Q_LORA]
    c_kv = lat[:, MLA_Q_LORA:MLA_Q_LORA + MLA_KV_LORA]
    o = MLA_Q_LORA + MLA_KV_LORA
    kr4 = lat[:, o:o + LANES]
    q = _dot(_rms(c_q, qg_ref[...]).astype(BF16), w_uq_ref[...])
    n_nope = MLA_HEADS * MLA_NOPE
    n_rope = MLA_HEADS * MLA_ROPE
    qr = q[:, n_nope:n_nope + n_rope]
    if rope:
        qr = qr * cosq_ref[...] + q[:, n_nope + n_rope:] * sinq_ref[...]
        kr4 = kr4 * cosk_ref[...] + lat[:, o + LANES:o + 2 * LANES] * sink_ref[...]
    qn_ref[...] = q[:, :n_nope].astype(BF16)
    qr_ref[...] = qr.astype(BF16)
    ckv_ref[...] = _rms(c_kv, kvg_ref[...])
    kr_ref[...] = kr4[:, :MLA_ROPE]
    kr4_ref[...] = kr4.astype(BF16)


def _premix_mla(x, mod, g, w_in, qg, w_uq, kvg, rope_tabs, rows_per_cond, tm):
    n = x.shape[0]
    rope = rope_tabs is not None
    row = lambda i: (i, 0)
    full = lambda i: (0, 0)
    in_specs = [
        pl.BlockSpec((tm, D_MODEL), row),
        _mod_spec(None if rows_per_cond is None else rows_per_cond // tm),
        pl.BlockSpec((1, D_MODEL), full),
        pl.BlockSpec(w_in.shape, full),
        pl.BlockSpec((1, MLA_Q_LORA), full),
        pl.BlockSpec(w_uq.shape, full),
        pl.BlockSpec((1, MLA_KV_LORA), full),
    ]
    args = [x, mod, g, w_in, qg, w_uq, kvg]
    if rope:
        nblk = rope_tabs[0].shape[0] // tm
        pos = lambda i: (i % nblk, 0)
        for t in rope_tabs:
            in_specs.append(pl.BlockSpec((tm, t.shape[1]), pos))
            args.append(t)
    widths = (MLA_HEADS * MLA_NOPE, MLA_HEADS * MLA_ROPE, MLA_KV_LORA, MLA_ROPE, LANES)
    dtypes = (BF16, BF16, F32, F32, BF16)
    return pl.pallas_call(
        functools.partial(_premix_mla_kernel, rope=rope),
        grid=(n // tm,),
        in_specs=in_specs,
        out_specs=[pl.BlockSpec((tm, w), row) for w in widths],
        out_shape=[jax.ShapeDtypeStruct((n, w), d) for w, d in zip(widths, dtypes)],
        compiler_params=_cparams("parallel"),
        name="premix_mla_rope" if rope else "premix_mla",
    )(*args)


LOG2E = 1.4426950408889634


def _softmax_pv(s_list, v_list, scale=1.0):
    m = functools.reduce(jnp.maximum, [jnp.max(s, axis=-1, keepdims=True) for s in s_list])
    e_list = [jnp.exp2((s - m) * (scale * LOG2E)) for s in s_list]
    l = functools.reduce(lambda a, b: a + b, [jnp.sum(e, axis=-1, keepdims=True) for e in e_list])
    o = functools.reduce(lambda a, b: a + b,
                         [_dot(e.astype(BF16), v) for e, v in zip(e_list, v_list)])
    return o / l


def _stack_heads(q, masks):
    return jnp.concatenate([q * m for m in masks], axis=0)


def _unstack_heads(o2, width):
    tq = o2.shape[0] // 2
    lane = lax.broadcasted_iota(jnp.int32, (1, LANES), 1)
    return jnp.where(lane < width, o2[:tq], o2[tq:])


def _mla_attn_kernel(*refs, cached):
    if cached:
        (qn_ref, qr_ref, ckv_ref, kr4_ref, cckv_ref, ckr4_ref, w_uk_ref, w_uv_ref,
         o_ref, kn_s, v_s, kr_s) = refs
    else:
        (qn_ref, qr_ref, ckv_ref, kr4_ref, w_uk_ref, w_uv_ref, o_ref, kn_s, v_s, kr_s) = refs
    t_own = ckv_ref.shape[0]

    @pl.when(pl.program_id(1) == 0)
    def _():
        c = ckv_ref[...].astype(BF16)
        kn_s[0:t_own, :] = _dot(c, w_uk_ref[...]).astype(BF16)
        v_s[0:t_own, :] = _dot(c, w_uv_ref[...]).astype(BF16)
        kr_s[0:t_own, :] = kr4_ref[...]
        if cached:
            cc = cckv_ref[0].astype(BF16)
            kn_s[t_own:, :] = _dot(cc, w_uk_ref[...]).astype(BF16)
            v_s[t_own:, :] = _dot(cc, w_uv_ref[...]).astype(BF16)
            kr_s[t_own:, :] = ckr4_ref[0].astype(BF16)

    kr4 = kr_s[...]
    for p in range(HEAD_PAIRS):
        sl = slice(p * LANES, (p + 1) * LANES)
        qn = qn_ref[:, sl]
        qr = qr_ref[:, (p // 2) * LANES:(p // 2 + 1) * LANES]
        k_cat = jnp.concatenate([kn_s[:, sl], kr4], axis=-1)
        q_cat = jnp.concatenate(
            [_stack_heads(qn, [_lane_mask(MLA_NOPE, i, BF16) for i in range(2)]),
             _stack_heads(qr, [_lane_mask(MLA_ROPE, (2 * p + i) % 4, BF16) for i in range(2)])], axis=-1)
        o2 = _softmax_pv([_dot_nt(q_cat, k_cat)], [v_s[:, sl]], MLA_SCALE)
        o_ref[:, sl] = _unstack_heads(o2, MLA_V).astype(BF16)


def _mla_attention(qn, qr, ckv, kr4, cache_ckv, cache_kr4, w_uk, w_uv, n_batch, tq):
    n = qn.shape[0]
    s_own = n // n_batch
    nq = s_own // tq
    cached = cache_ckv is not None
    t_all = s_own + (cache_ckv.shape[1] if cached else 0)
    qrow = lambda b, j: (b * nq + j, 0)
    own = lambda b, j: (b, 0)
    full = lambda b, j: (0, 0)
    in_specs = [
        pl.BlockSpec((tq, qn.shape[1]), qrow),
        pl.BlockSpec((tq, qr.shape[1]), qrow),
        pl.BlockSpec((s_own, MLA_KV_LORA), own),
        pl.BlockSpec((s_own, LANES), own),
    ]
    args = [qn, qr, ckv, kr4]
    if cached:
        in_specs += [pl.BlockSpec((1,) + cache_ckv.shape[1:], lambda b, j: (b, 0, 0)),
                     pl.BlockSpec((1,) + cache_kr4.shape[1:], lambda b, j: (b, 0, 0))]
        args += [cache_ckv, cache_kr4]
    in_specs += [pl.BlockSpec(w_uk.shape, full), pl.BlockSpec(w_uv.shape, full)]
    args += [w_uk, w_uv]
    return pl.pallas_call(
        functools.partial(_mla_attn_kernel, cached=cached),
        grid=(n_batch, nq),
        in_specs=in_specs,
        out_specs=pl.BlockSpec((tq, D_MODEL), qrow),
        out_shape=jax.ShapeDtypeStruct((n, D_MODEL), BF16),
        scratch_shapes=[pltpu.VMEM((t_all, D_MODEL), BF16), pltpu.VMEM((t_all, D_MODEL), BF16),
                        pltpu.VMEM((t_all, LANES), BF16)],
        compiler_params=_cparams("parallel", "arbitrary"),
        name="mla_attention_cached" if cached else "mla_attention",
    )(*args)


def _premix_nat_kernel(x_ref, mod_ref, g_ref, w_ref, q_ref, k_ref, v_ref):
    h = _rms(x_ref[...], g_ref[...]) * (1.0 + mod_ref[0, 1:2, :]) + mod_ref[0, 0:1, :]
    qkv = _dot(h.astype(BF16), w_ref[...])
    q_ref[...] = qkv[:, :D_MODEL].astype(q_ref.dtype)
    k_ref[...] = qkv[:, D_MODEL:2 * D_MODEL].astype(k_ref.dtype)
    v_ref[...] = qkv[:, 2 * D_MODEL:].astype(v_ref.dtype)


def _premix_nat(x, mod, g, w_qkv, rows_per_cond, tm, kv_dtype):
    n = x.shape[0]
    row = lambda i: (i, 0)
    full = lambda i: (0, 0)
    return pl.pallas_call(
        _premix_nat_kernel,
        grid=(n // tm,),
        in_specs=[
            pl.BlockSpec((tm, D_MODEL), row),
            _mod_spec(None if rows_per_cond is None else rows_per_cond // tm),
            pl.BlockSpec((1, D_MODEL), full),
            pl.BlockSpec(w_qkv.shape, full),
        ],
        out_specs=[pl.BlockSpec((tm, D_MODEL), row)] * 3,
        out_shape=[jax.ShapeDtypeStruct((n, D_MODEL), BF16),
                   jax.ShapeDtypeStruct((n, D_MODEL), kv_dtype),
                   jax.ShapeDtypeStruct((n, D_MODEL), kv_dtype)],
        compiler_params=_cparams("parallel"),
        name="premix_nat",
    )(x, mod, g, w_qkv)


def _premix_nat_cache_kernel(*refs, first):
    if first:
        x_ref, mod_ref, g_ref, w_ref, q_ref, k_ref, v_ref = refs
    else:
        x_ref, mod_ref, g_ref, w_ref, _, _, q_ref, k_ref, v_ref = refs
    h = _rms(x_ref[...], g_ref[...]) * (1.0 + mod_ref[0, 1:2, :]) + mod_ref[0, 0:1, :]
    qkv = _dot(h.astype(BF16), w_ref[...])
    q_ref[...] = qkv[:, :D_MODEL].astype(BF16)
    nb, n_slots, s, _ = k_ref.shape
    k_ref[:, 0] = qkv[:, D_MODEL:2 * D_MODEL].reshape(nb, s, D_MODEL)
    v_ref[:, 0] = qkv[:, 2 * D_MODEL:].reshape(nb, s, D_MODEL)
    for slot in range(1, n_slots):
        k_ref[:, slot] = jnp.zeros((nb, s, D_MODEL), F32)
        v_ref[:, slot] = jnp.zeros((nb, s, D_MODEL), F32)


def _premix_nat_cache(x, mod, g, w_qkv, k_buf, v_buf, slot, n_slots, n_batch, tm):
    n = x.shape[0]
    s = n // n_batch
    nb = tm // s
    first = k_buf is None
    row = lambda i: (i, 0)
    full = lambda i: (0, 0)
    in_specs = [pl.BlockSpec((tm, D_MODEL), row), _mod_spec(None), pl.BlockSpec((1, D_MODEL), full),
                pl.BlockSpec(w_qkv.shape, full)]
    args = [x, mod, g, w_qkv]
    if first:
        assert slot == 0
        kv_spec = pl.BlockSpec((nb, n_slots, s, D_MODEL), lambda i: (i, 0, 0, 0))
        aliases = {}
    else:
        in_specs += [pl.BlockSpec(memory_space=pl.ANY)] * 2
        args += [k_buf, v_buf]
        kv_spec = pl.BlockSpec((nb, 1, s, D_MODEL), lambda i: (i, slot, 0, 0))
        aliases = {4: 1, 5: 2}
    kv_shape = jax.ShapeDtypeStruct((n_batch, n_slots, s, D_MODEL), F32)
    return pl.pallas_call(
        functools.partial(_premix_nat_cache_kernel, first=first),
        grid=(n // tm,),
        in_specs=in_specs,
        out_specs=[pl.BlockSpec((tm, D_MODEL), row), kv_spec, kv_spec],
        out_shape=[jax.ShapeDtypeStruct((n, D_MODEL), BF16), kv_shape, kv_shape],
        input_output_aliases=aliases,
        compiler_params=_cparams("parallel"),
        name="premix_nat_cache",
    )(*args)


def _dense_attn_kernel(q_ref, k_ref, v_ref, o_ref):
    for p in range(HEAD_PAIRS):
        sl = slice(p * LANES, (p + 1) * LANES)
        q2 = _stack_heads(q_ref[:, sl], [_lane_mask(NAT_DH, i, BF16) for i in range(2)])
        k = k_ref[:, sl].astype(BF16)
        v = v_ref[:, sl].astype(BF16)
        o2 = _softmax_pv([_dot_nt(q2, k)], [v], NAT_SCALE)
        o_ref[:, sl] = _unstack_heads(o2, NAT_DH).astype(BF16)


def _dense_attention(q, k_buf, v_buf, slot):
    n = q.shape[0]
    n_batch, _, s, _ = k_buf.shape
    blk = pl.BlockSpec((s, D_MODEL), lambda b: (b, 0))
    kv = pl.BlockSpec((None, None, s, D_MODEL), lambda b: (b, slot, 0, 0))
    return pl.pallas_call(
        _dense_attn_kernel,
        grid=(n_batch,),
        in_specs=[blk, kv, kv],
        out_specs=blk,
        out_shape=jax.ShapeDtypeStruct((n, D_MODEL), BF16),
        compiler_params=_cparams("parallel"),
        name="dense_attention",
    )(q, k_buf, v_buf)


_NAT_QBLK = NAT_QROWS * GRID_W
_NAT_KBLK = NAT_KROWS * GRID_W


def _nat_block_plan(rows):
    assert rows % NAT_QROWS == 0 and rows >= NAT_KROWS and NAT_KROWS % 2 == 0
    plan, variants = [], []
    for r0 in range(0, rows, NAT_QROWS):
        ks = min(max(r0 - WIN_H // 2, 0), rows - NAT_KROWS)
        r = r0 + np.arange(NAT_QROWS)
        kr = ks + np.arange(NAT_KROWS)
        rs = np.clip(r - WIN_H // 2, 0, rows - WIN_H)
        valid_row = (kr[None, :] >= rs[:, None]) & (kr[None, :] < rs[:, None] + WIN_H)
        d0 = ks - r + (WIN_H - 1)
        for vi, (d0_v, valid_v) in enumerate(variants):
            if np.array_equal(d0, d0_v) and np.array_equal(valid_row, valid_v):
                break
        else:
            vi = len(variants)
            variants.append((d0, valid_row))
        plan.append((ks, vi))
    return plan, variants


def _nat_bias_row_range(variants):
    lo = min(int(d0.min()) for d0, _ in variants)
    hi = max(int(d0.max()) for d0, _ in variants) + NAT_KROWS
    return lo, hi


def _nat_bias_pairs(rpb, rows):
    n_heads, n_dr, n_dc = rpb.shape
    _, variants = _nat_block_plan(rows)
    lo, hi = _nat_bias_row_range(variants)
    c = np.arange(GRID_W)
    cs = np.clip(c - WIN_W // 2, 0, GRID_W - WIN_W)
    valid_col = (c[None, :] >= cs[:, None]) & (c[None, :] < cs[:, None] + WIN_W)
    d_col = c[None, :] - c[:, None] + (WIN_W - 1)
    sel = (d_col[None] == np.arange(n_dc)[:, None, None]) & valid_col[None]
    toep = jnp.einsum('hdj,jck->hdck', rpb, jnp.asarray(sel, F32), precision=lax.Precision.HIGHEST)
    toep = jnp.where(jnp.asarray(valid_col)[None, None], toep, NEG_INF)
    toep = jnp.pad(toep, ((0, 0), (max(-lo, 0), max(hi + 1 - n_dr, 0)), (0, 0), (0, 0)),
                   constant_values=NEG_INF)
    toep = toep[:, max(lo, 0):]
    return jnp.concatenate([toep[:, :-1], toep[:, 1:]], axis=-1)


def _nat_block_bias(tp_ref, head, d0, valid_row, row_lo):
    neg = jnp.full((GRID_W, 2 * GRID_W), NEG_INF, F32)
    left = lax.broadcasted_iota(jnp.int32, (1, 2 * GRID_W), 1) < GRID_W
    rows_out = []
    for dr in range(NAT_QROWS):
        pieces = []
        for a in range(0, NAT_KROWS, 2):
            ok0, ok1 = bool(valid_row[dr, a]), bool(valid_row[dr, a + 1])
            if not (ok0 or ok1):
                pieces.append(neg)
                continue
            piece = tp_ref[head, int(d0[dr]) + a - row_lo]
            if ok0 and not ok1:
                piece = jnp.where(left, piece, NEG_INF)
            elif ok1 and not ok0:
                piece = jnp.where(left, NEG_INF, piece)
            pieces.append(piece)
        rows_out.append(jnp.concatenate(pieces, axis=-1))
    return jnp.concatenate(rows_out, axis=0)


def _nat_attn_kernel(q_ref, k_ref, v_ref, kc_ref, vc_ref, tp_ref, o_ref, *, plan, variants, row_lo):
    kc = kc_ref[0].astype(BF16)
    vc = vc_ref[0].astype(BF16)
    assert NAT_SCALE == 2.0 ** round(np.log2(NAT_SCALE))
    masks = [_lane_mask(NAT_DH, i, BF16) * NAT_SCALE for i in range(2)]
    for bi, (ks, var) in enumerate(plan):
        q2 = _stack_heads(q_ref[bi * _NAT_QBLK:(bi + 1) * _NAT_QBLK, :], masks)
        k = k_ref[ks * GRID_W:ks * GRID_W + _NAT_KBLK, :]
        v = v_ref[ks * GRID_W:ks * GRID_W + _NAT_KBLK, :]
        bias2 = jnp.concatenate([_nat_block_bias(tp_ref, i, *variants[var], row_lo) for i in range(2)], axis=0)
        o2 = _softmax_pv([_dot_nt(q2, k) + bias2, _dot_nt(q2, kc)], [v, vc])
        o_ref[bi * _NAT_QBLK:(bi + 1) * _NAT_QBLK, :] = _unstack_heads(o2, NAT_DH).astype(BF16)


def _nat_attention(q, k, v, cache_k, cache_v, bias, n_batch):
    n = q.shape[0]
    s = n // n_batch
    plan, variants = _nat_block_plan(s // GRID_W)
    row_lo, _ = _nat_bias_row_range(variants)
    own = pl.BlockSpec((s, LANES), lambda p, b: (b, p))
    cache = pl.BlockSpec((1, cache_k.shape[1], LANES), lambda p, b: (b, 0, p))
    return pl.pallas_call(
        functools.partial(_nat_attn_kernel, plan=plan, variants=variants, row_lo=row_lo),
        grid=(HEAD_PAIRS, n_batch),
        in_specs=[own, own, own, cache, cache,
                  pl.BlockSpec((2,) + bias.shape[1:], lambda p, b: (p, 0, 0, 0))],
        out_specs=own,
        out_shape=jax.ShapeDtypeStruct((n, D_MODEL), BF16),
        compiler_params=_cparams("parallel", "parallel"),
        name="nat_attention",
    )(q, k, v, cache_k, cache_v, bias)


def _proj_res_kernel(x_ref, a_ref, w_ref, mod_ref, o_ref):
    o_ref[...] = x_ref[...] + mod_ref[0, 2:3, :] * _dot(a_ref[...], w_ref[...])


def _proj_residual(x, a, w, mod, rows_per_cond, tm):
    n = x.shape[0]
    row = lambda i: (i, 0)
    return pl.pallas_call(
        _proj_res_kernel,
        grid=(n // tm,),
        in_specs=[
            pl.BlockSpec((tm, D_MODEL), row),
            pl.BlockSpec((tm, D_MODEL), row),
            pl.BlockSpec(w.shape, lambda i: (0, 0)),
            _mod_spec(None if rows_per_cond is None else rows_per_cond // tm),
        ],
        out_specs=pl.BlockSpec((tm, D_MODEL), row),
        out_shape=jax.ShapeDtypeStruct((n, D_MODEL), F32),
        compiler_params=_cparams("parallel"),
        name="proj_residual",
    )(x, a, w, mod)


def _top2_sum(a, b, c, d):
    hi1, lo1 = jnp.maximum(a, b), jnp.minimum(a, b)
    hi2, lo2 = jnp.maximum(c, d), jnp.minimum(c, d)
    return jnp.maximum(hi1, hi2) + jnp.maximum(jnp.minimum(hi1, hi2), jnp.maximum(lo1, lo2))


def _route(scores, biased):
    sc = [scores[e:e + 1, :] for e in range(N_EXPERTS)]
    bs = [biased[e:e + 1, :] for e in range(N_EXPERTS)]
    gscore = [_top2_sum(*bs[EPG * g:EPG * (g + 1)]) for g in range(N_GROUPS)]
    best, gidx = gscore[0], jnp.zeros_like(gscore[0], dtype=jnp.int32)
    for g in range(1, N_GROUPS):
        better = gscore[g] > best
        gidx = jnp.where(better, g, gidx)
        best = jnp.where(better, gscore[g], best)
    cb = [functools.reduce(lambda a, b: a + b,
                           [jnp.where(gidx == g, bs[EPG * g + i], 0.0) for g in range(N_GROUPS)])
          for i in range(EPG)]
    cs = [functools.reduce(lambda a, b: a + b,
                           [jnp.where(gidx == g, sc[EPG * g + i], 0.0) for g in range(N_GROUPS)])
          for i in range(EPG)]
    b1, i1 = cb[0], jnp.zeros_like(gidx)
    for i in range(1, EPG):
        better = cb[i] > b1
        i1 = jnp.where(better, i, i1)
        b1 = jnp.where(better, cb[i], b1)
    b2, i2 = jnp.full_like(b1, -jnp.inf), jnp.full_like(i1, -1)
    for i in range(EPG):
        better = (i1 != i) & (cb[i] > b2)
        i2 = jnp.where(better, i, i2)
        b2 = jnp.where(better, cb[i], b2)
    sel = [(i1 == i) | (i2 == i) for i in range(EPG)]
    w = [jnp.where(sel[i], cs[i], 0.0) for i in range(EPG)]
    tot = w[0] + w[1] + w[2] + w[3]
    return gidx, [w[i] / tot for i in range(EPG)]


MOE_SRC = 256
MOE_DST = 256
_LOG2_DST = MOE_DST.bit_length() - 1
assert MOE_DST == 1 << _LOG2_DST and MOE_SRC <= MOE_DST
_INFO_GID = EPG


def _group_tiles(cum_ref):
    n_src = cum_ref.shape[0]
    tot = [cum_ref[n_src - 1, g] for g in range(N_GROUPS)]
    ends, acc = [], 0
    for t in tot:
        acc = acc + lax.shift_right_logical(t + (MOE_DST - 1), _LOG2_DST)
        ends.append(acc)
    return tot, ends


def _dest_tile(d, cum_ref):
    tot, ends = _group_tiles(cum_ref)
    g = ((d >= ends[0]).astype(jnp.int32) + (d >= ends[1]).astype(jnp.int32)
         + (d >= ends[2]).astype(jnp.int32))
    first = jnp.where(g == 0, 0, jnp.where(g == 1, ends[0], jnp.where(g == 2, ends[1], ends[2])))
    tot_g = jnp.where(g == 0, tot[0], jnp.where(g == 1, tot[1], jnp.where(g == 2, tot[2], tot[3])))
    k0 = (d - first) * MOE_DST
    n_valid = jnp.clip(tot_g - k0, 0, MOE_DST)
    return g, k0, n_valid


def _counts_before(cum_ref, s, g):
    return jnp.where(s > 0, cum_ref[jnp.maximum(s - 1, 0), g], 0)


MOE_ALIGN = 16
_LOG2_ALIGN = MOE_ALIGN.bit_length() - 1
_LOC_ROWS = MOE_SRC + N_GROUPS * MOE_ALIGN
_INFO_LRANK = EPG + 1
H_EXT = D_MODEL + LANES
_PIECES = [MOE_SRC >> i for i in range((MOE_SRC // MOE_ALIGN).bit_length())]


def _moe_sort_tile(s, x_ref, mod_ref, g_ref, wr_ref, rb_ref, info_ref, cum_ref, lsort, seg_s, base_s):
    h = _rms(x_ref[...], g_ref[...]) * (1.0 + mod_ref[0, 4:5, :]) + mod_ref[0, 3:4, :]
    h_hi = h.astype(BF16)
    h_lo = (h - h_hi.astype(F32)).astype(BF16)
    hi_w = _dot(h_hi, wr_ref[...])
    logits = hi_w[:, :LANES] + (_dot(h_lo, wr_ref[:, :LANES]) + hi_w[:, LANES:])
    scores = _sigmoid(logits)
    gidx, gates = _route(scores.T[:N_EXPERTS], (scores + rb_ref[...]).T[:N_EXPERTS])
    tm = h.shape[0]
    sub = lax.broadcasted_iota(jnp.int32, (8, 1), 0)
    onehot = (sub == gidx).astype(F32)
    tri = (lax.broadcasted_iota(jnp.int32, (tm, tm), 0)
           <= lax.broadcasted_iota(jnp.int32, (tm, tm), 1)).astype(BF16)
    prefix = _dot(onehot.astype(BF16), tri)
    lrank = jnp.sum((prefix - 1.0) * onehot, axis=0, keepdims=True)

    count = [jnp.sum(onehot[g:g + 1, :]).astype(jnp.int32) for g in range(N_GROUPS)]
    padded = [lax.shift_left(lax.shift_right_logical(c + (MOE_ALIGN - 1), _LOG2_ALIGN), _LOG2_ALIGN)
              for c in count]
    loff, acc = [], 0
    for p in padded:
        loff.append(acc)
        acc = acc + p
    lpos = lrank + functools.reduce(
        lambda a, b: a + b, [jnp.where(gidx == g, jnp.asarray(v, jnp.int32).astype(F32), 0.0)
                             for g, v in enumerate(loff)])
    infot = jnp.concatenate(gates + [gidx.astype(F32), lrank, jnp.zeros((LANES - EPG - 2, tm), F32)], axis=0)
    info_ref[...] = infot.T
    g_hi = [gt.astype(BF16).astype(F32) for gt in gates]
    g_lo = [gt - gh for gt, gh in zip(gates, g_hi)]
    gext = jnp.concatenate(g_hi + g_lo + [jnp.zeros((LANES - 2 * EPG, tm), F32)], axis=0)
    h_ext = jnp.concatenate([h_hi, gext.T.astype(BF16)], axis=-1)
    perm = (lax.broadcasted_iota(jnp.int32, (_LOC_ROWS, 1), 0).astype(F32) == lpos).astype(BF16)
    lsort[s] = _dot(perm, h_ext).astype(BF16)

    new_base = []
    for g in range(N_GROUPS):
        seg_s[s, g] = padded[g]
        new_base.append(base_s[g] + padded[g])
        base_s[g] = new_base[g]
    cum = functools.reduce(lambda a, b: a + b,
                           [jnp.where(sub == g, new_base[g], 0) for g in range(N_GROUPS)])
    cum_ref[0] = jnp.broadcast_to(cum, (8, LANES))


def _copy_pieces(length, make_copy, wait):
    done = jnp.int32(0)
    for rows in _PIECES:
        take = (length & rows) != 0 if rows < MOE_SRC else length >= rows

        @pl.when(take)
        def _():
            cp = make_copy(pl.multiple_of(done, MOE_ALIGN), rows)
            cp.wait() if wait else cp.start()
        done = done + jnp.where(take, rows, 0)


def _moe_sort_kernel(x_ref, mod_ref, g_ref, wr_ref, rb_ref, info_ref, cum_ref, srt_ref,
                     lsort, zeros_v, seg_s, base_s, sem):
    s = pl.program_id(0)
    n_src = lsort.shape[0]
    n_rows = srt_ref.shape[0]

    @pl.when(s == 0)
    def _():
        for g in range(N_GROUPS):
            base_s[g] = 0
        zeros_v[...] = jnp.zeros_like(zeros_v)

    @pl.when(s < n_src)
    def _():
        _moe_sort_tile(s, x_ref, mod_ref, g_ref, wr_ref, rb_ref, info_ref, cum_ref, lsort, seg_s, base_s)

    @pl.when(s == n_src)
    def _():
        starts, acc = [], jnp.int32(0)
        for g in range(N_GROUPS):
            starts.append(acc)
            acc = acc + lax.shift_left(
                lax.shift_right_logical(base_s[g] + (MOE_DST - 1), _LOG2_DST), _LOG2_DST)
        end = acc

        def all_copies(wait):
            def tile_copies(t, dst):
                src = jnp.int32(0)
                new_dst = []
                for g in range(N_GROUPS):
                    n = seg_s[t, g]
                    _copy_pieces(n, lambda off, rows, src=src, g=g: pltpu.make_async_copy(
                        lsort.at[t, pl.ds(pl.multiple_of(src + off, MOE_ALIGN), rows)],
                        srt_ref.at[pl.ds(pl.multiple_of(dst[g] + off, MOE_ALIGN), rows)], sem.at[0]), wait)
                    src = src + n
                    new_dst.append(dst[g] + n)
                return tuple(new_dst)

            dst = lax.fori_loop(0, n_src, tile_copies, tuple(starts))
            for g in range(N_GROUPS):
                gap_end = starts[g + 1] if g + 1 < N_GROUPS else end
                _copy_pieces(gap_end - dst[g], lambda off, rows, g=g: pltpu.make_async_copy(
                    zeros_v.at[pl.ds(0, rows)],
                    srt_ref.at[pl.ds(pl.multiple_of(dst[g] + off, MOE_ALIGN), rows)], sem.at[0]), wait)

            def zero_tile(r, carry):
                cp = pltpu.make_async_copy(
                    zeros_v, srt_ref.at[pl.ds(pl.multiple_of(end + r * MOE_DST, MOE_DST), MOE_DST)], sem.at[0])
                cp.wait() if wait else cp.start()
                return carry

            lax.fori_loop(0, lax.shift_right_logical(n_rows - end, _LOG2_DST), zero_tile, 0)

        all_copies(False)
        all_copies(True)


def _moe_ffn2_kernel(cum_ref, xs_ref, wg_ref, wu_ref, wd_ref, y_ref):
    _, ends = _group_tiles(cum_ref)

    @pl.when(pl.program_id(0) >= ends[-1])
    def _():
        y_ref[...] = jnp.zeros_like(y_ref)

    @pl.when(pl.program_id(0) < ends[-1])
    def _():
        xs = xs_ref[:, :D_MODEL]
        gext = xs_ref[:, D_MODEL:].astype(F32)
        acts = []
        for i in range(EPG):
            a = _dot(xs, wg_ref[i])
            u = _dot(xs, wu_ref[i])
            gate = gext[:, i:i + 1] + gext[:, EPG + i:EPG + i + 1]
            acts.append(((a * _sigmoid(a)) * u * gate).astype(BF16))
        act = jnp.concatenate(acts, axis=-1)
        y_ref[...] = _dot(act, wd_ref[...].reshape(EPG * D_EXPERT, D_MODEL)).astype(BF16)


def _segment_row(cum_ref, s, g):
    _, ends = _group_tiles(cum_ref)
    first_tile = ends[g - 1] if g > 0 else 0
    return first_tile * MOE_DST + _counts_before(cum_ref, s, g)


def _unsort_window(cum_ref, s, g, n_rows):
    return pl.multiple_of(jnp.minimum(_segment_row(cum_ref, s, g), n_rows - MOE_DST), MOE_ALIGN)


def _moe_unsort2_kernel(*refs, final_norm):
    if final_norm:
        cum_ref, x_ref, mod_ref, info_ref, y_ref, gf_ref, o_ref = refs
    else:
        cum_ref, x_ref, mod_ref, info_ref, y_ref, o_ref = refs
    s = pl.program_id(0)
    n_rows = y_ref.shape[0]
    inf = info_ref[...]
    gid = inf[:, _INFO_GID:_INFO_GID + 1]
    lrank = inf[:, _INFO_LRANK:_INFO_LRANK + 1]
    col = lax.broadcasted_iota(jnp.int32, (1, MOE_DST), 1).astype(F32)
    perms, wins = [], []
    for g in range(N_GROUPS):
        win = _unsort_window(cum_ref, s, g, n_rows)
        shift = _segment_row(cum_ref, s, g) - win
        rel = jnp.where(gid == float(g), lrank + shift.astype(F32), -1.0)
        perms.append((rel == col).astype(BF16))
        wins.append(y_ref[pl.ds(win, MOE_DST), :])
    out = x_ref[...] + mod_ref[0, 5:6, :] * _dot(jnp.concatenate(perms, axis=-1),
                                                 jnp.concatenate(wins, axis=0))
    o_ref[...] = _rms(out, gf_ref[...]) if final_norm else out


def _moe_sorted(x, mod, g, w_router, router_bias, wg, wu, wd, layer, rows_per_cond, final_g=None):
    n = x.shape[0]
    n_src = n // MOE_SRC
    n_dst = (n + N_GROUPS * n_src * (MOE_ALIGN - 1)) // MOE_DST + N_GROUPS
    n_rows = n_dst * MOE_DST
    last = n_src - 1
    row = lambda i: (jnp.minimum(i, last), 0)
    full = lambda i: (0, 0)
    info, cum, srt = pl.pallas_call(
        _moe_sort_kernel,
        grid=(n_src + 1,),
        in_specs=[
            pl.BlockSpec((MOE_SRC, D_MODEL), row),
            pl.BlockSpec((1, 6, D_MODEL), (lambda i: (0, 0, 0)) if rows_per_cond is None else
                         (lambda i: (1 + jnp.minimum(i, last) // (rows_per_cond // MOE_SRC), 0, 0))),
            pl.BlockSpec((1, D_MODEL), full),
            pl.BlockSpec(w_router.shape, full),
            pl.BlockSpec(router_bias.shape, full),
        ],
        out_specs=[pl.BlockSpec((MOE_SRC, LANES), row),
                   pl.BlockSpec((1, 8, LANES), lambda i: (jnp.minimum(i, last), 0, 0)),
                   pl.BlockSpec(memory_space=pl.ANY)],
        out_shape=[jax.ShapeDtypeStruct((n, LANES), F32),
                   jax.ShapeDtypeStruct((n_src, 8, LANES), jnp.int32),
                   jax.ShapeDtypeStruct((n_rows, H_EXT), BF16)],
        scratch_shapes=[pltpu.VMEM((n_src, _LOC_ROWS, H_EXT), BF16),
                        pltpu.VMEM((MOE_DST, H_EXT), BF16),
                        pltpu.SMEM((n_src, N_GROUPS), jnp.int32),
                        pltpu.SMEM((N_GROUPS,), jnp.int32),
                        pltpu.SemaphoreType.DMA((1,))],
        compiler_params=_cparams("arbitrary"),
        name="moe_sort",
    )(x, mod, g, w_router, router_bias)
    cum = cum[:, :N_GROUPS, 0]

    blk = lambda width: pl.BlockSpec((MOE_DST, width), lambda d, c: (d, 0))
    expert = lambda shape: pl.BlockSpec(shape, lambda d, c: (layer, _dest_tile(d, c)[0], 0, 0))
    y = pl.pallas_call(
        _moe_ffn2_kernel,
        grid_spec=pltpu.PrefetchScalarGridSpec(
            num_scalar_prefetch=1,
            grid=(n_dst,),
            in_specs=[blk(H_EXT),
                      expert((None, EPG, D_MODEL, D_EXPERT)),
                      expert((None, EPG, D_MODEL, D_EXPERT)),
                      expert((None, EPG, D_EXPERT, D_MODEL))],
            out_specs=blk(D_MODEL),
        ),
        out_shape=jax.ShapeDtypeStruct((n_rows, D_MODEL), BF16),
        compiler_params=_cparams("arbitrary"),
        name="moe_experts",
    )(cum, srt, wg, wu, wd)

    in_specs = [pl.BlockSpec((MOE_SRC, D_MODEL), lambda i, c: (i, 0)),
                _mod_spec(None if rows_per_cond is None else rows_per_cond // MOE_SRC),
                pl.BlockSpec((MOE_SRC, LANES), lambda i, c: (i, 0)),
                pl.BlockSpec(y.shape, lambda i, c: (0, 0), pipeline_mode=pl.Buffered(1))]
    args = [cum, x, mod, info, y]
    if final_g is not None:
        in_specs.append(pl.BlockSpec((1, D_MODEL), lambda i, c: (0, 0)))
        args.append(final_g)
    return pl.pallas_call(
        functools.partial(_moe_unsort2_kernel, final_norm=final_g is not None),
        grid_spec=pltpu.PrefetchScalarGridSpec(
            num_scalar_prefetch=1,
            grid=(n_src,),
            in_specs=in_specs,
            out_specs=pl.BlockSpec((MOE_SRC, D_MODEL), lambda i, c: (i, 0)),
        ),
        out_shape=jax.ShapeDtypeStruct((n, D_MODEL), F32),
        compiler_params=_cparams("parallel"),
        name="moe_unsort",
    )(*args)


def _rope_tables(n_lat):
    t = np.arange(n_lat)
    n_freq = MLA_ROPE // 4
    inv_freq = jnp.asarray(ROPE_THETA, F32) ** (-jnp.arange(n_freq, dtype=F32) / n_freq)
    ar = jnp.asarray(t // GRID_W, F32)[:, None] * inv_freq
    ac = jnp.asarray(t % GRID_W, F32)[:, None] * inv_freq
    cos = jnp.concatenate([jnp.cos(ar), jnp.cos(ar), jnp.cos(ac), jnp.cos(ac)], axis=-1)
    sin = jnp.concatenate([-jnp.sin(ar), jnp.sin(ar), -jnp.sin(ac), jnp.sin(ac)], axis=-1)
    return (jnp.tile(cos, (1, MLA_HEADS)), jnp.tile(sin, (1, MLA_HEADS)),
            jnp.tile(cos, (1, LANES // MLA_ROPE)), jnp.tile(sin, (1, LANES // MLA_ROPE)))


_ROPE_SWAP = np.concatenate([np.arange(8, 16), np.arange(0, 8), np.arange(24, 32), np.arange(16, 24)])


def _mla_weights(w_in, w_uq, w_ukv):
    o = MLA_Q_LORA + MLA_KV_LORA
    kr = w_in[:, o:]
    rep = LANES // MLA_ROPE
    w_in_x = jnp.concatenate([w_in[:, :o], jnp.tile(kr, (1, rep)), jnp.tile(kr[:, _ROPE_SWAP], (1, rep))],
                             axis=-1).astype(BF16)
    uq = w_uq.reshape(MLA_Q_LORA, MLA_HEADS, MLA_NOPE + MLA_ROPE)
    q_nope = uq[:, :, :MLA_NOPE].reshape(MLA_Q_LORA, -1)
    q_rope = uq[:, :, MLA_NOPE:]
    w_uq_x = jnp.concatenate([q_nope, q_rope.reshape(MLA_Q_LORA, -1),
                              q_rope[:, :, _ROPE_SWAP].reshape(MLA_Q_LORA, -1)], axis=-1).astype(BF16)
    ukv = w_ukv.reshape(MLA_KV_LORA, MLA_HEADS, MLA_NOPE + MLA_V)
    w_uk = ukv[:, :, :MLA_NOPE].reshape(MLA_KV_LORA, -1).astype(BF16)
    w_uv = ukv[:, :, MLA_NOPE:].reshape(MLA_KV_LORA, -1).astype(BF16)
    return w_in_x, w_uq_x, w_uk, w_uv


def kernel(x_prompt, x_sample, cache_mla_ckv, cache_mla_krope, cache_nat_k, cache_nat_v, c, c_ctx,
           w_ada, b_ada, norm_mix, norm_ffn, norm_final, mla_w_in, mla_q_norm, mla_w_uq, mla_kv_norm,
           mla_w_ukv, mla_w_o, nat_w_qkv, nat_rpb, nat_w_o, w_router, router_bias,
           moe_w_gate, moe_w_up, moe_w_down):
    B, S, D = x_prompt.shape
    Bd, Sd, _ = x_sample.shape
    assert D == D_MODEL and Bd + 1 <= 8
    tm_c, tm_s = 512, 512

    xc = x_prompt.reshape(B * S, D)
    xs = x_sample.reshape(Bd * Sd, D)
    cond8 = jnp.concatenate([c_ctx[None, :], c, jnp.zeros((8 - 1 - Bd, D), F32)], axis=0)
    mod_all = _ada_modulation(cond8, w_ada, b_ada).reshape(DEPTH, 8, 6, D)

    rope_tabs = _rope_tables(Sd)
    wr = jnp.pad(w_router, ((0, 0), (0, LANES - N_EXPERTS)))
    wr_hi = wr.astype(BF16)
    wr = jnp.concatenate([wr_hi, (wr - wr_hi.astype(F32)).astype(BF16)], axis=1)
    rb = jnp.pad(router_bias, (0, LANES - N_EXPERTS)).reshape(1, LANES)
    wg = moe_w_gate.astype(BF16)
    wu = moe_w_up.astype(BF16)
    wd = moe_w_down.astype(BF16)

    new_ckv, new_krope = [], []
    k_buf = v_buf = None
    for layer in range(DEPTH):
        mod = mod_all[layer]
        g_mix = norm_mix[layer][None, :]
        j = layer // 2
        if layer % 2 == 0:
            w_in_x, w_uq_x, w_uk, w_uv = _mla_weights(mla_w_in[j], mla_w_uq[j], mla_w_ukv[j])
            w_in_c = w_in_x[:, :MLA_Q_LORA + MLA_KV_LORA + LANES]
            w_uq_c = w_uq_x[:, :MLA_HEADS * (MLA_NOPE + MLA_ROPE)]
            qg = mla_q_norm[j][None, :]
            kvg = mla_kv_norm[j][None, :]
            w_o = mla_w_o[j].astype(BF16)
            qn, qr, ckv, kr, kr4 = _premix_mla(xc, mod, g_mix, w_in_c, qg, w_uq_c, kvg, None, None, tm_c)
            new_ckv.append(ckv.reshape(B, S, MLA_KV_LORA))
            new_krope.append(kr.reshape(B, S, MLA_ROPE))
            att_c = _mla_attention(qn, qr, ckv, kr4, None, None, w_uk, w_uv, B, S)
            qn, qr, ckv, kr, kr4 = _premix_mla(xs, mod, g_mix, w_in_x, qg, w_uq_x, kvg, rope_tabs, Sd, tm_s)
            cache_kr4 = jnp.tile(cache_mla_krope[:, j], (1, 1, LANES // MLA_ROPE))
            att_s = _mla_attention(qn, qr, ckv, kr4, cache_mla_ckv[:, j], cache_kr4, w_uk, w_uv, Bd, 256)
        else:
            w_qkv = nat_w_qkv[j].astype(BF16)
            w_o = nat_w_o[j].astype(BF16)
            q, k_buf, v_buf = _premix_nat_cache(xc, mod, g_mix, w_qkv, k_buf, v_buf, j, DEPTH // 2, B, tm_c)
            att_c = _dense_attention(q, k_buf, v_buf, j)
            q, k, v = _premix_nat(xs, mod, g_mix, w_qkv, Sd, tm_s, BF16)
            bias = _nat_bias_pairs(nat_rpb[j], Sd // GRID_W)
            att_s = _nat_attention(q, k, v,
                                   cache_nat_k[:, j].reshape(Bd, -1, D), cache_nat_v[:, j].reshape(Bd, -1, D),
                                   bias, Bd)
        xc = _proj_residual(xc, att_c, w_o, mod, None, tm_c)
        xs = _proj_residual(xs, att_s, w_o, mod, Sd, tm_s)
        g_ffn = norm_ffn[layer][None, :]
        final_g = norm_final[None, :] if layer == DEPTH - 1 else None
        xc = _moe_sorted(xc, mod, g_ffn, wr, rb, wg, wu, wd, layer, None, final_g)
        xs = _moe_sorted(xs, mod, g_ffn, wr, rb, wg, wu, wd, layer, Sd, final_g)

    y_prompt = xc.reshape(B, S, D)
    y_sample = xs.reshape(Bd, Sd, D)
    kv_shape = (B, DEPTH // 2, S, NAT_HEADS, NAT_DH)
    return (y_prompt, y_sample, jnp.stack(new_ckv, axis=1), jnp.stack(new_krope, axis=1),
            k_buf.reshape(kv_shape), v_buf.reshape(kv_shape))
```

```python
import functools

import numpy as np
import jax
import jax.numpy as jnp
from jax import lax
from jax.experimental import pallas as pl
from jax.experimental.pallas import tpu as pltpu

F32 = jnp.float32
BF16 = jnp.bfloat16

D_MODEL = 1024
DEPTH = 4
GRID_W = 64
LANES = 128
MLA_HEADS = 16
MLA_NOPE = 64
MLA_ROPE = 32
MLA_V = 64
MLA_Q_LORA = 384
MLA_KV_LORA = 256
MLA_SCALE = (MLA_NOPE + MLA_ROPE) ** -0.5
ROPE_THETA = 10000.0
NAT_HEADS = 16
NAT_DH = 64
NAT_SCALE = NAT_DH ** -0.5
WIN_H = 8
WIN_W = 16
NAT_QROWS = 4
NAT_KROWS = NAT_QROWS + WIN_H
N_EXPERTS = 16
N_GROUPS = 4
EPG = N_EXPERTS // N_GROUPS
D_EXPERT = 256
NORM_EPS = 1e-6
NEG_INF = -1e30

HEAD_PAIRS = MLA_HEADS // 2
VMEM_LIMIT = 56 * 1024 * 1024


def _cparams(*sem):
    return pltpu.CompilerParams(dimension_semantics=sem, vmem_limit_bytes=VMEM_LIMIT)


def _sigmoid(x):
    return 1.0 / (1.0 + jnp.exp(-x))


def _rms(x, g):
    ms = jnp.mean(x * x, axis=-1, keepdims=True)
    return x * lax.rsqrt(ms + NORM_EPS) * g


def _dot(a, b):
    return jnp.dot(a, b, preferred_element_type=F32)


def _dot_nt(a, b):
    return lax.dot_general(a, b, (((1,), (1,)), ((), ())), preferred_element_type=F32)


def _lane_mask(width, idx, dtype):
    lane = lax.broadcasted_iota(jnp.int32, (1, LANES), 1)
    return ((lane >= idx * width) & (lane < (idx + 1) * width)).astype(dtype)


def _ada_kernel(cond_ref, w_ref, b_ref, o_ref):
    c = cond_ref[...]
    s = (c * _sigmoid(c)).astype(BF16)
    o_ref[0] = _dot(s, w_ref[0].astype(BF16)) + b_ref[0]


def _ada_modulation(cond8, w_ada, b_ada):
    n_chunk = 6
    return pl.pallas_call(
        _ada_kernel,
        grid=(DEPTH, n_chunk),
        in_specs=[
            pl.BlockSpec((8, D_MODEL), lambda l, j: (0, 0)),
            pl.BlockSpec((1, D_MODEL, D_MODEL), lambda l, j: (l, 0, j)),
            pl.BlockSpec((1, 1, D_MODEL), lambda l, j: (l, 0, j)),
        ],
        out_specs=pl.BlockSpec((1, 8, D_MODEL), lambda l, j: (l, 0, j)),
        out_shape=jax.ShapeDtypeStruct((DEPTH, 8, 6 * D_MODEL), F32),
        compiler_params=_cparams("parallel", "parallel"),
        name="ada_modulation",
    )(cond8, w_ada, b_ada.reshape(DEPTH, 1, 6 * D_MODEL))


def _mod_spec(rows_per_cond):
    if rows_per_cond is None:
        return pl.BlockSpec((1, 6, D_MODEL), lambda i, *_: (0, 0, 0))
    return pl.BlockSpec((1, 6, D_MODEL), lambda i, *_: (1 + i // rows_per_cond, 0, 0))


def _premix_mla_kernel(*refs, rope):
    if rope:
        (x_ref, mod_ref, g_ref, w_in_ref, qg_ref, w_uq_ref, kvg_ref,
         cosq_ref, sinq_ref, cosk_ref, sink_ref,
         qn_ref, qr_ref, ckv_ref, kr_ref, kr4_ref) = refs
    else:
        (x_ref, mod_ref, g_ref, w_in_ref, qg_ref, w_uq_ref, kvg_ref,
         qn_ref, qr_ref, ckv_ref, kr_ref, kr4_ref) = refs
    h = _rms(x_ref[...], g_ref[...]) * (1.0 + mod_ref[0, 1:2, :]) + mod_ref[0, 0:1, :]
    lat = _dot(h.astype(BF16), w_in_ref[...])
    c_q = lat[:, :MLA_Q_LORA]
    c_kv = lat[:, MLA_Q_LORA:MLA_Q_LORA + MLA_KV_LORA]
    o = MLA_Q_LORA + MLA_KV_LORA
    kr4 = lat[:, o:o + LANES]
    q = _dot(_rms(c_q, qg_ref[...]).astype(BF16), w_uq_ref[...])
    n_nope = MLA_HEADS * MLA_NOPE
    n_rope = MLA_HEADS * MLA_ROPE
    qr = q[:, n_nope:n_nope + n_rope]
    if rope:
        qr = qr * cosq_ref[...] + q[:, n_nope + n_rope:] * sinq_ref[...]
        kr4 = kr4 * cosk_ref[...] + lat[:, o + LANES:o + 2 * LANES] * sink_ref[...]
    qn_ref[...] = q[:, :n_nope].astype(BF16)
    qr_ref[...] = qr.astype(BF16)
    ckv_ref[...] = _rms(c_kv, kvg_ref[...])
    kr_ref[...] = kr4[:, :MLA_ROPE]
    kr4_ref[...] = kr4.astype(BF16)


def _premix_mla(x, mod, g, w_in, qg, w_uq, kvg, rope_tabs, rows_per_cond, tm):
    n = x.shape[0]
    rope = rope_tabs is not None
    row = lambda i: (i, 0)
    full = lambda i: (0, 0)
    in_specs = [
        pl.BlockSpec((tm, D_MODEL), row),
        _mod_spec(None if rows_per_cond is None else rows_per_cond // tm),
        pl.BlockSpec((1, D_MODEL), full),
        pl.BlockSpec(w_in.shape, full),
        pl.BlockSpec((1, MLA_Q_LORA), full),
        pl.BlockSpec(w_uq.shape, full),
        pl.BlockSpec((1, MLA_KV_LORA), full),
    ]
    args = [x, mod, g, w_in, qg, w_uq, kvg]
    if rope:
        nblk = rope_tabs[0].shape[0] // tm
        pos = lambda i: (i % nblk, 0)
        for t in rope_tabs:
            in_specs.append(pl.BlockSpec((tm, t.shape[1]), pos))
            args.append(t)
    widths = (MLA_HEADS * MLA_NOPE, MLA_HEADS * MLA_ROPE, MLA_KV_LORA, MLA_ROPE, LANES)
    dtypes = (BF16, BF16, F32, F32, BF16)
    return pl.pallas_call(
        functools.partial(_premix_mla_kernel, rope=rope),
        grid=(n // tm,),
        in_specs=in_specs,
        out_specs=[pl.BlockSpec((tm, w), row) for w in widths],
        out_shape=[jax.ShapeDtypeStruct((n, w), d) for w, d in zip(widths, dtypes)],
        compiler_params=_cparams("parallel"),
        name="premix_mla_rope" if rope else "premix_mla",
    )(*args)


LOG2E = 1.4426950408889634


def _softmax_pv(s_list, v_list, scale=1.0):
    m = functools.reduce(jnp.maximum, [jnp.max(s, axis=-1, keepdims=True) for s in s_list])
    e_list = [jnp.exp2((s - m) * (scale * LOG2E)) for s in s_list]
    l = functools.reduce(lambda a, b: a + b, [jnp.sum(e, axis=-1, keepdims=True) for e in e_list])
    o = functools.reduce(lambda a, b: a + b,
                         [_dot(e.astype(BF16), v) for e, v in zip(e_list, v_list)])
    return o / l


def _stack_heads(q, masks):
    return jnp.concatenate([q * m for m in masks], axis=0)


def _unstack_heads(o2, width):
    tq = o2.shape[0] // 2
    lane = lax.broadcasted_iota(jnp.int32, (1, LANES), 1)
    return jnp.where(lane < width, o2[:tq], o2[tq:])


def _mla_attn_kernel(*refs, cached):
    if cached:
        (qn_ref, qr_ref, ckv_ref, kr4_ref, cckv_ref, ckr4_ref, w_uk_ref, w_uv_ref,
         o_ref, kn_s, v_s, kr_s) = refs
    else:
        (qn_ref, qr_ref, ckv_ref, kr4_ref, w_uk_ref, w_uv_ref, o_ref, kn_s, v_s, kr_s) = refs
    t_own = ckv_ref.shape[0]

    @pl.when(pl.program_id(1) == 0)
    def _():
        c = ckv_ref[...].astype(BF16)
        kn_s[0:t_own, :] = _dot(c, w_uk_ref[...]).astype(BF16)
        v_s[0:t_own, :] = _dot(c, w_uv_ref[...]).astype(BF16)
        kr_s[0:t_own, :] = kr4_ref[...]
        if cached:
            cc = cckv_ref[0].astype(BF16)
            kn_s[t_own:, :] = _dot(cc, w_uk_ref[...]).astype(BF16)
            v_s[t_own:, :] = _dot(cc, w_uv_ref[...]).astype(BF16)
            kr_s[t_own:, :] = ckr4_ref[0].astype(BF16)

    kr4 = kr_s[...]
    for p in range(HEAD_PAIRS):
        sl = slice(p * LANES, (p + 1) * LANES)
        qn = qn_ref[:, sl]
        qr = qr_ref[:, (p // 2) * LANES:(p // 2 + 1) * LANES]
        k_cat = jnp.concatenate([kn_s[:, sl], kr4], axis=-1)
        q_cat = jnp.concatenate(
            [_stack_heads(qn, [_lane_mask(MLA_NOPE, i, BF16) for i in range(2)]),
             _stack_heads(qr, [_lane_mask(MLA_ROPE, (2 * p + i) % 4, BF16) for i in range(2)])], axis=-1)
        o2 = _softmax_pv([_dot_nt(q_cat, k_cat)], [v_s[:, sl]], MLA_SCALE)
        o_ref[:, sl] = _unstack_heads(o2, MLA_V).astype(BF16)


def _mla_attention(qn, qr, ckv, kr4, cache_ckv, cache_kr4, w_uk, w_uv, n_batch, tq):
    n = qn.shape[0]
    s_own = n // n_batch
    nq = s_own // tq
    cached = cache_ckv is not None
    t_all = s_own + (cache_ckv.shape[1] if cached else 0)
    qrow = lambda b, j: (b * nq + j, 0)
    own = lambda b, j: (b, 0)
    full = lambda b, j: (0, 0)
    in_specs = [
        pl.BlockSpec((tq, qn.shape[1]), qrow),
        pl.BlockSpec((tq, qr.shape[1]), qrow),
        pl.BlockSpec((s_own, MLA_KV_LORA), own),
        pl.BlockSpec((s_own, LANES), own),
    ]
    args = [qn, qr, ckv, kr4]
    if cached:
        in_specs += [pl.BlockSpec((1,) + cache_ckv.shape[1:], lambda b, j: (b, 0, 0)),
                     pl.BlockSpec((1,) + cache_kr4.shape[1:], lambda b, j: (b, 0, 0))]
        args += [cache_ckv, cache_kr4]
    in_specs += [pl.BlockSpec(w_uk.shape, full), pl.BlockSpec(w_uv.shape, full)]
    args += [w_uk, w_uv]
    return pl.pallas_call(
        functools.partial(_mla_attn_kernel, cached=cached),
        grid=(n_batch, nq),
        in_specs=in_specs,
        out_specs=pl.BlockSpec((tq, D_MODEL), qrow),
        out_shape=jax.ShapeDtypeStruct((n, D_MODEL), BF16),
        scratch_shapes=[pltpu.VMEM((t_all, D_MODEL), BF16), pltpu.VMEM((t_all, D_MODEL), BF16),
                        pltpu.VMEM((t_all, LANES), BF16)],
        compiler_params=_cparams("parallel", "arbitrary"),
        name="mla_attention_cached" if cached else "mla_attention",
    )(*args)


def _premix_nat_kernel(x_ref, mod_ref, g_ref, w_ref, q_ref, k_ref, v_ref):
    h = _rms(x_ref[...], g_ref[...]) * (1.0 + mod_ref[0, 1:2, :]) + mod_ref[0, 0:1, :]
    qkv = _dot(h.astype(BF16), w_ref[...])
    q_ref[...] = qkv[:, :D_MODEL].astype(q_ref.dtype)
    k_ref[...] = qkv[:, D_MODEL:2 * D_MODEL].astype(k_ref.dtype)
    v_ref[...] = qkv[:, 2 * D_MODEL:].astype(v_ref.dtype)


def _premix_nat(x, mod, g, w_qkv, rows_per_cond, tm, kv_dtype):
    n = x.shape[0]
    row = lambda i: (i, 0)
    full = lambda i: (0, 0)
    return pl.pallas_call(
        _premix_nat_kernel,
        grid=(n // tm,),
        in_specs=[
            pl.BlockSpec((tm, D_MODEL), row),
            _mod_spec(None if rows_per_cond is None else rows_per_cond // tm),
            pl.BlockSpec((1, D_MODEL), full),
            pl.BlockSpec(w_qkv.shape, full),
        ],
        out_specs=[pl.BlockSpec((tm, D_MODEL), row)] * 3,
        out_shape=[jax.ShapeDtypeStruct((n, D_MODEL), BF16),
                   jax.ShapeDtypeStruct((n, D_MODEL), kv_dtype),
                   jax.ShapeDtypeStruct((n, D_MODEL), kv_dtype)],
        compiler_params=_cparams("parallel"),
        name="premix_nat",
    )(x, mod, g, w_qkv)


def _premix_nat_cache_kernel(*refs, first):
    if first:
        x_ref, mod_ref, g_ref, w_ref, q_ref, k_ref, v_ref = refs
    else:
        x_ref, mod_ref, g_ref, w_ref, _, _, q_ref, k_ref, v_ref = refs
    h = _rms(x_ref[...], g_ref[...]) * (1.0 + mod_ref[0, 1:2, :]) + mod_ref[0, 0:1, :]
    qkv = _dot(h.astype(BF16), w_ref[...])
    q_ref[...] = qkv[:, :D_MODEL].astype(BF16)
    nb, n_slots, s, _ = k_ref.shape
    k_ref[:, 0] = qkv[:, D_MODEL:2 * D_MODEL].reshape(nb, s, D_MODEL)
    v_ref[:, 0] = qkv[:, 2 * D_MODEL:].reshape(nb, s, D_MODEL)
    for slot in range(1, n_slots):
        k_ref[:, slot] = jnp.zeros((nb, s, D_MODEL), F32)
        v_ref[:, slot] = jnp.zeros((nb, s, D_MODEL), F32)


def _premix_nat_cache(x, mod, g, w_qkv, k_buf, v_buf, slot, n_slots, n_batch, tm):
    n = x.shape[0]
    s = n // n_batch
    nb = tm // s
    first = k_buf is None
    row = lambda i: (i, 0)
    full = lambda i: (0, 0)
    in_specs = [pl.BlockSpec((tm, D_MODEL), row), _mod_spec(None), pl.BlockSpec((1, D_MODEL), full),
                pl.BlockSpec(w_qkv.shape, full)]
    args = [x, mod, g, w_qkv]
    if first:
        assert slot == 0
        kv_spec = pl.BlockSpec((nb, n_slots, s, D_MODEL), lambda i: (i, 0, 0, 0))
        aliases = {}
    else:
        in_specs += [pl.BlockSpec(memory_space=pl.ANY)] * 2
        args += [k_buf, v_buf]
        kv_spec = pl.BlockSpec((nb, 1, s, D_MODEL), lambda i: (i, slot, 0, 0))
        aliases = {4: 1, 5: 2}
    kv_shape = jax.ShapeDtypeStruct((n_batch, n_slots, s, D_MODEL), F32)
    return pl.pallas_call(
        functools.partial(_premix_nat_cache_kernel, first=first),
        grid=(n // tm,),
        in_specs=in_specs,
        out_specs=[pl.BlockSpec((tm, D_MODEL), row), kv_spec, kv_spec],
        out_shape=[jax.ShapeDtypeStruct((n, D_MODEL), BF16), kv_shape, kv_shape],
        input_output_aliases=aliases,
        compiler_params=_cparams("parallel"),
        name="premix_nat_cache",
    )(*args)


def _dense_attn_kernel(q_ref, k_ref, v_ref, o_ref):
    for p in range(HEAD_PAIRS):
        sl = slice(p * LANES, (p + 1) * LANES)
        q2 = _stack_heads(q_ref[:, sl], [_lane_mask(NAT_DH, i, BF16) for i in range(2)])
        k = k_ref[:, sl].astype(BF16)
        v = v_ref[:, sl].astype(BF16)
        o2 = _softmax_pv([_dot_nt(q2, k)], [v], NAT_SCALE)
        o_ref[:, sl] = _unstack_heads(o2, NAT_DH).astype(BF16)


def _dense_attention(q, k_buf, v_buf, slot):
    n = q.shape[0]
    n_batch, _, s, _ = k_buf.shape
    blk = pl.BlockSpec((s, D_MODEL), lambda b: (b, 0))
    kv = pl.BlockSpec((None, None, s, D_MODEL), lambda b: (b, slot, 0, 0))
    return pl.pallas_call(
        _dense_attn_kernel,
        grid=(n_batch,),
        in_specs=[blk, kv, kv],
        out_specs=blk,
        out_shape=jax.ShapeDtypeStruct((n, D_MODEL), BF16),
        compiler_params=_cparams("parallel"),
        name="dense_attention",
    )(q, k_buf, v_buf)


_NAT_QBLK = NAT_QROWS * GRID_W
_NAT_KBLK = NAT_KROWS * GRID_W


def _nat_block_plan(rows):
    assert rows % NAT_QROWS == 0 and rows >= NAT_KROWS and NAT_KROWS % 2 == 0
    plan, variants = [], []
    for r0 in range(0, rows, NAT_QROWS):
        ks = min(max(r0 - WIN_H // 2, 0), rows - NAT_KROWS)
        r = r0 + np.arange(NAT_QROWS)
        kr = ks + np.arange(NAT_KROWS)
        rs = np.clip(r - WIN_H // 2, 0, rows - WIN_H)
        valid_row = (kr[None, :] >= rs[:, None]) & (kr[None, :] < rs[:, None] + WIN_H)
        d0 = ks - r + (WIN_H - 1)
        for vi, (d0_v, valid_v) in enumerate(variants):
            if np.array_equal(d0, d0_v) and np.array_equal(valid_row, valid_v):
                break
        else:
            vi = len(variants)
            variants.append((d0, valid_row))
        plan.append((ks, vi))
    return plan, variants


def _nat_bias_row_range(variants):
    lo = min(int(d0.min()) for d0, _ in variants)
    hi = max(int(d0.max()) for d0, _ in variants) + NAT_KROWS
    return lo, hi


def _nat_bias_pairs(rpb, rows):
    n_heads, n_dr, n_dc = rpb.shape
    _, variants = _nat_block_plan(rows)
    lo, hi = _nat_bias_row_range(variants)
    c = np.arange(GRID_W)
    cs = np.clip(c - WIN_W // 2, 0, GRID_W - WIN_W)
    valid_col = (c[None, :] >= cs[:, None]) & (c[None, :] < cs[:, None] + WIN_W)
    d_col = c[None, :] - c[:, None] + (WIN_W - 1)
    sel = (d_col[None] == np.arange(n_dc)[:, None, None]) & valid_col[None]
    toep = jnp.einsum('hdj,jck->hdck', rpb, jnp.asarray(sel, F32), precision=lax.Precision.HIGHEST)
    toep = jnp.where(jnp.asarray(valid_col)[None, None], toep, NEG_INF)
    toep = jnp.pad(toep, ((0, 0), (max(-lo, 0), max(hi + 1 - n_dr, 0)), (0, 0), (0, 0)),
                   constant_values=NEG_INF)
    toep = toep[:, max(lo, 0):]
    return jnp.concatenate([toep[:, :-1], toep[:, 1:]], axis=-1)


def _nat_block_bias(tp_ref, head, d0, valid_row, row_lo):
    neg = jnp.full((GRID_W, 2 * GRID_W), NEG_INF, F32)
    left = lax.broadcasted_iota(jnp.int32, (1, 2 * GRID_W), 1) < GRID_W
    rows_out = []
    for dr in range(NAT_QROWS):
        pieces = []
        for a in range(0, NAT_KROWS, 2):
            ok0, ok1 = bool(valid_row[dr, a]), bool(valid_row[dr, a + 1])
            if not (ok0 or ok1):
                pieces.append(neg)
                continue
            piece = tp_ref[head, int(d0[dr]) + a - row_lo]
            if ok0 and not ok1:
                piece = jnp.where(left, piece, NEG_INF)
            elif ok1 and not ok0:
                piece = jnp.where(left, NEG_INF, piece)
            pieces.append(piece)
        rows_out.append(jnp.concatenate(pieces, axis=-1))
    return jnp.concatenate(rows_out, axis=0)


def _nat_attn_kernel(q_ref, k_ref, v_ref, kc_ref, vc_ref, tp_ref, o_ref, *, plan, variants, row_lo):
    kc = kc_ref[0].astype(BF16)
    vc = vc_ref[0].astype(BF16)
    assert NAT_SCALE == 2.0 ** round(np.log2(NAT_SCALE))
    masks = [_lane_mask(NAT_DH, i, BF16) * NAT_SCALE for i in range(2)]
    for bi, (ks, var) in enumerate(plan):
        q2 = _stack_heads(q_ref[bi * _NAT_QBLK:(bi + 1) * _NAT_QBLK, :], masks)
        k = k_ref[ks * GRID_W:ks * GRID_W + _NAT_KBLK, :]
        v = v_ref[ks * GRID_W:ks * GRID_W + _NAT_KBLK, :]
        bias2 = jnp.concatenate([_nat_block_bias(tp_ref, i, *variants[var], row_lo) for i in range(2)], axis=0)
        o2 = _softmax_pv([_dot_nt(q2, k) + bias2, _dot_nt(q2, kc)], [v, vc])
        o_ref[bi * _NAT_QBLK:(bi + 1) * _NAT_QBLK, :] = _unstack_heads(o2, NAT_DH).astype(BF16)


def _nat_attention(q, k, v, cache_k, cache_v, bias, n_batch):
    n = q.shape[0]
    s = n // n_batch
    plan, variants = _nat_block_plan(s // GRID_W)
    row_lo, _ = _nat_bias_row_range(variants)
    own = pl.BlockSpec((s, LANES), lambda p, b: (b, p))
    cache = pl.BlockSpec((1, cache_k.shape[1], LANES), lambda p, b: (b, 0, p))
    return pl.pallas_call(
        functools.partial(_nat_attn_kernel, plan=plan, variants=variants, row_lo=row_lo),
        grid=(HEAD_PAIRS, n_batch),
        in_specs=[own, own, own, cache, cache,
                  pl.BlockSpec((2,) + bias.shape[1:], lambda p, b: (p, 0, 0, 0))],
        out_specs=own,
        out_shape=jax.ShapeDtypeStruct((n, D_MODEL), BF16),
        compiler_params=_cparams("parallel", "parallel"),
        name="nat_attention",
    )(q, k, v, cache_k, cache_v, bias)


def _top2_sum(a, b, c, d):
    hi1, lo1 = jnp.maximum(a, b), jnp.minimum(a, b)
    hi2, lo2 = jnp.maximum(c, d), jnp.minimum(c, d)
    return jnp.maximum(hi1, hi2) + jnp.maximum(jnp.minimum(hi1, hi2), jnp.maximum(lo1, lo2))


def _route(scores, biased):
    sc = [scores[e:e + 1, :] for e in range(N_EXPERTS)]
    bs = [biased[e:e + 1, :] for e in range(N_EXPERTS)]
    gscore = [_top2_sum(*bs[EPG * g:EPG * (g + 1)]) for g in range(N_GROUPS)]
    best, gidx = gscore[0], jnp.zeros_like(gscore[0], dtype=jnp.int32)
    for g in range(1, N_GROUPS):
        better = gscore[g] > best
        gidx = jnp.where(better, g, gidx)
        best = jnp.where(better, gscore[g], best)
    cb = [functools.reduce(lambda a, b: a + b,
                           [jnp.where(gidx == g, bs[EPG * g + i], 0.0) for g in range(N_GROUPS)])
          for i in range(EPG)]
    cs = [functools.reduce(lambda a, b: a + b,
                           [jnp.where(gidx == g, sc[EPG * g + i], 0.0) for g in range(N_GROUPS)])
          for i in range(EPG)]
    b1, i1 = cb[0], jnp.zeros_like(gidx)
    for i in range(1, EPG):
        better = cb[i] > b1
        i1 = jnp.where(better, i, i1)
        b1 = jnp.where(better, cb[i], b1)
    b2, i2 = jnp.full_like(b1, -jnp.inf), jnp.full_like(i1, -1)
    for i in range(EPG):
        better = (i1 != i) & (cb[i] > b2)
        i2 = jnp.where(better, i, i2)
        b2 = jnp.where(better, cb[i], b2)
    sel = [(i1 == i) | (i2 == i) for i in range(EPG)]
    w = [jnp.where(sel[i], cs[i], 0.0) for i in range(EPG)]
    tot = w[0] + w[1] + w[2] + w[3]
    return gidx, [w[i] / tot for i in range(EPG)]


MOE_SRC = 256
MOE_DST = 256
_LOG2_DST = MOE_DST.bit_length() - 1
assert MOE_DST == 1 << _LOG2_DST and MOE_SRC <= MOE_DST
_INFO_GID = EPG


def _group_tiles(cum_ref):
    n_src = cum_ref.shape[0]
    tot = [cum_ref[n_src - 1, g] for g in range(N_GROUPS)]
    ends, acc = [], 0
    for t in tot:
        acc = acc + lax.shift_right_logical(t + (MOE_DST - 1), _LOG2_DST)
        ends.append(acc)
    return tot, ends


def _dest_tile(d, cum_ref):
    tot, ends = _group_tiles(cum_ref)
    g = ((d >= ends[0]).astype(jnp.int32) + (d >= ends[1]).astype(jnp.int32)
         + (d >= ends[2]).astype(jnp.int32))
    first = jnp.where(g == 0, 0, jnp.where(g == 1, ends[0], jnp.where(g == 2, ends[1], ends[2])))
    tot_g = jnp.where(g == 0, tot[0], jnp.where(g == 1, tot[1], jnp.where(g == 2, tot[2], tot[3])))
    k0 = (d - first) * MOE_DST
    n_valid = jnp.clip(tot_g - k0, 0, MOE_DST)
    return g, k0, n_valid


def _counts_before(cum_ref, s, g):
    return jnp.where(s > 0, cum_ref[jnp.maximum(s - 1, 0), g], 0)


MOE_ALIGN = 16
_LOG2_ALIGN = MOE_ALIGN.bit_length() - 1
_LOC_ROWS = MOE_SRC + N_GROUPS * MOE_ALIGN
_INFO_LRANK = EPG + 1
H_EXT = D_MODEL + LANES
_PIECES = [MOE_SRC >> i for i in range((MOE_SRC // MOE_ALIGN).bit_length())]


def _moe_sort_tile(s, x_ref, a_ref, wo_ref, mod_ref, g_ref, wr_ref, rb_ref, xo_ref, info_ref, cum_ref,
                   lsort, seg_s, base_s):
    x = x_ref[...] + mod_ref[0, 2:3, :] * _dot(a_ref[...], wo_ref[...])
    xo_ref[...] = x
    h = _rms(x, g_ref[...]) * (1.0 + mod_ref[0, 4:5, :]) + mod_ref[0, 3:4, :]
    h_hi = h.astype(BF16)
    h_lo = (h - h_hi.astype(F32)).astype(BF16)
    hi_w = _dot(h_hi, wr_ref[...])
    logits = hi_w[:, :LANES] + (_dot(h_lo, wr_ref[:, :LANES]) + hi_w[:, LANES:])
    scores = _sigmoid(logits)
    gidx, gates = _route(scores.T[:N_EXPERTS], (scores + rb_ref[...]).T[:N_EXPERTS])
    tm = h.shape[0]
    sub = lax.broadcasted_iota(jnp.int32, (8, 1), 0)
    onehot = (sub == gidx).astype(F32)
    tri = (lax.broadcasted_iota(jnp.int32, (tm, tm), 0)
           <= lax.broadcasted_iota(jnp.int32, (tm, tm), 1)).astype(BF16)
    prefix = _dot(onehot.astype(BF16), tri)
    lrank = jnp.sum((prefix - 1.0) * onehot, axis=0, keepdims=True)

    count = [jnp.sum(onehot[g:g + 1, :]).astype(jnp.int32) for g in range(N_GROUPS)]
    padded = [lax.shift_left(lax.shift_right_logical(c + (MOE_ALIGN - 1), _LOG2_ALIGN), _LOG2_ALIGN)
              for c in count]
    loff, acc = [], 0
    for p in padded:
        loff.append(acc)
        acc = acc + p
    lpos = lrank + functools.reduce(
        lambda a, b: a + b, [jnp.where(gidx == g, jnp.asarray(v, jnp.int32).astype(F32), 0.0)
                             for g, v in enumerate(loff)])
    infot = jnp.concatenate(gates + [gidx.astype(F32), lrank, jnp.zeros((LANES - EPG - 2, tm), F32)], axis=0)
    info_ref[...] = infot.T
    g_hi = [gt.astype(BF16).astype(F32) for gt in gates]
    g_lo = [gt - gh for gt, gh in zip(gates, g_hi)]
    gext = jnp.concatenate(g_hi + g_lo + [jnp.zeros((LANES - 2 * EPG, tm), F32)], axis=0)
    h_ext = jnp.concatenate([h_hi, gext.T.astype(BF16)], axis=-1)
    perm = (lax.broadcasted_iota(jnp.int32, (_LOC_ROWS, 1), 0).astype(F32) == lpos).astype(BF16)
    lsort[s] = _dot(perm, h_ext).astype(BF16)

    new_base = []
    for g in range(N_GROUPS):
        seg_s[s, g] = padded[g]
        new_base.append(base_s[g] + padded[g])
        base_s[g] = new_base[g]
    cum = functools.reduce(lambda a, b: a + b,
                           [jnp.where(sub == g, new_base[g], 0) for g in range(N_GROUPS)])
    cum_ref[0] = jnp.broadcast_to(cum, (8, LANES))


def _copy_pieces(length, make_copy, wait):
    done = jnp.int32(0)
    for rows in _PIECES:
        take = (length & rows) != 0 if rows < MOE_SRC else length >= rows

        @pl.when(take)
        def _():
            cp = make_copy(pl.multiple_of(done, MOE_ALIGN), rows)
            cp.wait() if wait else cp.start()
        done = done + jnp.where(take, rows, 0)


def _moe_sort_kernel(x_ref, a_ref, wo_ref, mod_ref, g_ref, wr_ref, rb_ref, xo_ref, info_ref, cum_ref, srt_ref,
                     lsort, zeros_v, seg_s, base_s, sem):
    s = pl.program_id(0)
    n_src = lsort.shape[0]
    n_rows = srt_ref.shape[0]

    @pl.when(s == 0)
    def _():
        for g in range(N_GROUPS):
            base_s[g] = 0
        zeros_v[...] = jnp.zeros_like(zeros_v)

    @pl.when(s < n_src)
    def _():
        _moe_sort_tile(s, x_ref, a_ref, wo_ref, mod_ref, g_ref, wr_ref, rb_ref, xo_ref, info_ref, cum_ref,
                       lsort, seg_s, base_s)

    @pl.when(s == n_src)
    def _():
        starts, acc = [], jnp.int32(0)
        for g in range(N_GROUPS):
            starts.append(acc)
            acc = acc + lax.shift_left(
                lax.shift_right_logical(base_s[g] + (MOE_DST - 1), _LOG2_DST), _LOG2_DST)
        end = acc

        def all_copies(wait):
            def tile_copies(t, dst):
                src = jnp.int32(0)
                new_dst = []
                for g in range(N_GROUPS):
                    n = seg_s[t, g]
                    _copy_pieces(n, lambda off, rows, src=src, g=g: pltpu.make_async_copy(
                        lsort.at[t, pl.ds(pl.multiple_of(src + off, MOE_ALIGN), rows)],
                        srt_ref.at[pl.ds(pl.multiple_of(dst[g] + off, MOE_ALIGN), rows)], sem.at[0]), wait)
                    src = src + n
                    new_dst.append(dst[g] + n)
                return tuple(new_dst)

            dst = lax.fori_loop(0, n_src, tile_copies, tuple(starts))
            for g in range(N_GROUPS):
                gap_end = starts[g + 1] if g + 1 < N_GROUPS else end
                _copy_pieces(gap_end - dst[g], lambda off, rows, g=g: pltpu.make_async_copy(
                    zeros_v.at[pl.ds(0, rows)],
                    srt_ref.at[pl.ds(pl.multiple_of(dst[g] + off, MOE_ALIGN), rows)], sem.at[0]), wait)

            def zero_tile(r, carry):
                cp = pltpu.make_async_copy(
                    zeros_v, srt_ref.at[pl.ds(pl.multiple_of(end + r * MOE_DST, MOE_DST), MOE_DST)], sem.at[0])
                cp.wait() if wait else cp.start()
                return carry

            lax.fori_loop(0, lax.shift_right_logical(n_rows - end, _LOG2_DST), zero_tile, 0)

        all_copies(False)
        all_copies(True)


def _moe_ffn2_kernel(cum_ref, xs_ref, wg_ref, wu_ref, wd_ref, y_ref):
    _, ends = _group_tiles(cum_ref)

    @pl.when(pl.program_id(0) >= ends[-1])
    def _():
        y_ref[...] = jnp.zeros_like(y_ref)

    @pl.when(pl.program_id(0) < ends[-1])
    def _():
        xs = xs_ref[:, :D_MODEL]
        gext = xs_ref[:, D_MODEL:].astype(F32)
        acts = []
        for i in range(EPG):
            a = _dot(xs, wg_ref[i])
            u = _dot(xs, wu_ref[i])
            gate = gext[:, i:i + 1] + gext[:, EPG + i:EPG + i + 1]
            acts.append(((a * _sigmoid(a)) * u * gate).astype(BF16))
        act = jnp.concatenate(acts, axis=-1)
        y_ref[...] = _dot(act, wd_ref[...].reshape(EPG * D_EXPERT, D_MODEL)).astype(BF16)


def _segment_row(cum_ref, s, g):
    _, ends = _group_tiles(cum_ref)
    first_tile = ends[g - 1] if g > 0 else 0
    return first_tile * MOE_DST + _counts_before(cum_ref, s, g)


def _unsort_window(cum_ref, s, g, n_rows):
    return pl.multiple_of(jnp.minimum(_segment_row(cum_ref, s, g), n_rows - MOE_DST), MOE_ALIGN)


def _moe_unsort2_kernel(*refs, final_norm):
    if final_norm:
        cum_ref, x_ref, mod_ref, info_ref, y_ref, gf_ref, o_ref = refs
    else:
        cum_ref, x_ref, mod_ref, info_ref, y_ref, o_ref = refs
    s = pl.program_id(0)
    n_rows = y_ref.shape[0]
    inf = info_ref[...]
    gid = inf[:, _INFO_GID:_INFO_GID + 1]
    lrank = inf[:, _INFO_LRANK:_INFO_LRANK + 1]
    col = lax.broadcasted_iota(jnp.int32, (1, MOE_DST), 1).astype(F32)
    perms, wins = [], []
    for g in range(N_GROUPS):
        win = _unsort_window(cum_ref, s, g, n_rows)
        shift = _segment_row(cum_ref, s, g) - win
        rel = jnp.where(gid == float(g), lrank + shift.astype(F32), -1.0)
        perms.append((rel == col).astype(BF16))
        wins.append(y_ref[pl.ds(win, MOE_DST), :])
    out = x_ref[...] + mod_ref[0, 5:6, :] * _dot(jnp.concatenate(perms, axis=-1),
                                                 jnp.concatenate(wins, axis=0))
    o_ref[...] = _rms(out, gf_ref[...]) if final_norm else out


def _mixer_out_and_moe(x, att, w_o, mod, g, w_router, router_bias, wg, wu, wd, layer, rows_per_cond,
                       final_g=None):
    n = x.shape[0]
    n_src = n // MOE_SRC
    n_dst = (n + N_GROUPS * n_src * (MOE_ALIGN - 1)) // MOE_DST + N_GROUPS
    n_rows = n_dst * MOE_DST
    last = n_src - 1
    row = lambda i: (jnp.minimum(i, last), 0)
    full = lambda i: (0, 0)
    x, info, cum, srt = pl.pallas_call(
        _moe_sort_kernel,
        grid=(n_src + 1,),
        in_specs=[
            pl.BlockSpec((MOE_SRC, D_MODEL), row),
            pl.BlockSpec((MOE_SRC, D_MODEL), row),
            pl.BlockSpec(w_o.shape, full),
            pl.BlockSpec((1, 6, D_MODEL), (lambda i: (0, 0, 0)) if rows_per_cond is None else
                         (lambda i: (1 + jnp.minimum(i, last) // (rows_per_cond // MOE_SRC), 0, 0))),
            pl.BlockSpec((1, D_MODEL), full),
            pl.BlockSpec(w_router.shape, full),
            pl.BlockSpec(router_bias.shape, full),
        ],
        out_specs=[pl.BlockSpec((MOE_SRC, D_MODEL), row),
                   pl.BlockSpec((MOE_SRC, LANES), row),
                   pl.BlockSpec((1, 8, LANES), lambda i: (jnp.minimum(i, last), 0, 0)),
                   pl.BlockSpec(memory_space=pl.ANY)],
        out_shape=[jax.ShapeDtypeStruct((n, D_MODEL), F32),
                   jax.ShapeDtypeStruct((n, LANES), F32),
                   jax.ShapeDtypeStruct((n_src, 8, LANES), jnp.int32),
                   jax.ShapeDtypeStruct((n_rows, H_EXT), BF16)],
        scratch_shapes=[pltpu.VMEM((n_src, _LOC_ROWS, H_EXT), BF16),
                        pltpu.VMEM((MOE_DST, H_EXT), BF16),
                        pltpu.SMEM((n_src, N_GROUPS), jnp.int32),
                        pltpu.SMEM((N_GROUPS,), jnp.int32),
                        pltpu.SemaphoreType.DMA((1,))],
        compiler_params=_cparams("arbitrary"),
        name="moe_sort",
    )(x, att, w_o, mod, g, w_router, router_bias)
    cum = cum[:, :N_GROUPS, 0]

    blk = lambda width: pl.BlockSpec((MOE_DST, width), lambda d, c: (d, 0))
    expert = lambda shape: pl.BlockSpec(shape, lambda d, c: (layer, _dest_tile(d, c)[0], 0, 0))
    y = pl.pallas_call(
        _moe_ffn2_kernel,
        grid_spec=pltpu.PrefetchScalarGridSpec(
            num_scalar_prefetch=1,
            grid=(n_dst,),
            in_specs=[blk(H_EXT),
                      expert((None, EPG, D_MODEL, D_EXPERT)),
                      expert((None, EPG, D_MODEL, D_EXPERT)),
                      expert((None, EPG, D_EXPERT, D_MODEL))],
            out_specs=blk(D_MODEL),
        ),
        out_shape=jax.ShapeDtypeStruct((n_rows, D_MODEL), BF16),
        compiler_params=_cparams("arbitrary"),
        name="moe_experts",
    )(cum, srt, wg, wu, wd)

    in_specs = [pl.BlockSpec((MOE_SRC, D_MODEL), lambda i, c: (i, 0)),
                _mod_spec(None if rows_per_cond is None else rows_per_cond // MOE_SRC),
                pl.BlockSpec((MOE_SRC, LANES), lambda i, c: (i, 0)),
                pl.BlockSpec(y.shape, lambda i, c: (0, 0), pipeline_mode=pl.Buffered(1))]
    args = [cum, x, mod, info, y]
    if final_g is not None:
        in_specs.append(pl.BlockSpec((1, D_MODEL), lambda i, c: (0, 0)))
        args.append(final_g)
    return pl.pallas_call(
        functools.partial(_moe_unsort2_kernel, final_norm=final_g is not None),
        grid_spec=pltpu.PrefetchScalarGridSpec(
            num_scalar_prefetch=1,
            grid=(n_src,),
            in_specs=in_specs,
            out_specs=pl.BlockSpec((MOE_SRC, D_MODEL), lambda i, c: (i, 0)),
        ),
        out_shape=jax.ShapeDtypeStruct((n, D_MODEL), F32),
        compiler_params=_cparams("parallel"),
        name="moe_unsort",
    )(*args)


def _rope_tables(n_lat):
    t = np.arange(n_lat)
    n_freq = MLA_ROPE // 4
    inv_freq = jnp.asarray(ROPE_THETA, F32) ** (-jnp.arange(n_freq, dtype=F32) / n_freq)
    ar = jnp.asarray(t // GRID_W, F32)[:, None] * inv_freq
    ac = jnp.asarray(t % GRID_W, F32)[:, None] * inv_freq
    cos = jnp.concatenate([jnp.cos(ar), jnp.cos(ar), jnp.cos(ac), jnp.cos(ac)], axis=-1)
    sin = jnp.concatenate([-jnp.sin(ar), jnp.sin(ar), -jnp.sin(ac), jnp.sin(ac)], axis=-1)
    return (jnp.tile(cos, (1, MLA_HEADS)), jnp.tile(sin, (1, MLA_HEADS)),
            jnp.tile(cos, (1, LANES // MLA_ROPE)), jnp.tile(sin, (1, LANES // MLA_ROPE)))


_ROPE_SWAP = np.concatenate([np.arange(8, 16), np.arange(0, 8), np.arange(24, 32), np.arange(16, 24)])


def _mla_weights(w_in, w_uq, w_ukv):
    o = MLA_Q_LORA + MLA_KV_LORA
    kr = w_in[:, o:]
    rep = LANES // MLA_ROPE
    w_in_x = jnp.concatenate([w_in[:, :o], jnp.tile(kr, (1, rep)), jnp.tile(kr[:, _ROPE_SWAP], (1, rep))],
                             axis=-1).astype(BF16)
    uq = w_uq.reshape(MLA_Q_LORA, MLA_HEADS, MLA_NOPE + MLA_ROPE)
    q_nope = uq[:, :, :MLA_NOPE].reshape(MLA_Q_LORA, -1)
    q_rope = uq[:, :, MLA_NOPE:]
    w_uq_x = jnp.concatenate([q_nope, q_rope.reshape(MLA_Q_LORA, -1),
                              q_rope[:, :, _ROPE_SWAP].reshape(MLA_Q_LORA, -1)], axis=-1).astype(BF16)
    ukv = w_ukv.reshape(MLA_KV_LORA, MLA_HEADS, MLA_NOPE + MLA_V)
    w_uk = ukv[:, :, :MLA_NOPE].reshape(MLA_KV_LORA, -1).astype(BF16)
    w_uv = ukv[:, :, MLA_NOPE:].reshape(MLA_KV_LORA, -1).astype(BF16)
    return w_in_x, w_uq_x, w_uk, w_uv


def kernel(x_prompt, x_sample, cache_mla_ckv, cache_mla_krope, cache_nat_k, cache_nat_v, c, c_ctx,
           w_ada, b_ada, norm_mix, norm_ffn, norm_final, mla_w_in, mla_q_norm, mla_w_uq, mla_kv_norm,
           mla_w_ukv, mla_w_o, nat_w_qkv, nat_rpb, nat_w_o, w_router, router_bias,
           moe_w_gate, moe_w_up, moe_w_down):
    B, S, D = x_prompt.shape
    Bd, Sd, _ = x_sample.shape
    assert D == D_MODEL and Bd + 1 <= 8
    tm_c, tm_s = 512, 512

    xc = x_prompt.reshape(B * S, D)
    xs = x_sample.reshape(Bd * Sd, D)
    cond8 = jnp.concatenate([c_ctx[None, :], c, jnp.zeros((8 - 1 - Bd, D), F32)], axis=0)
    mod_all = _ada_modulation(cond8, w_ada, b_ada).reshape(DEPTH, 8, 6, D)

    rope_tabs = _rope_tables(Sd)
    wr = jnp.pad(w_router, ((0, 0), (0, LANES - N_EXPERTS)))
    wr_hi = wr.astype(BF16)
    wr = jnp.concatenate([wr_hi, (wr - wr_hi.astype(F32)).astype(BF16)], axis=1)
    rb = jnp.pad(router_bias, (0, LANES - N_EXPERTS)).reshape(1, LANES)
    wg = moe_w_gate.astype(BF16)
    wu = moe_w_up.astype(BF16)
    wd = moe_w_down.astype(BF16)

    new_ckv, new_krope = [], []
    k_buf = v_buf = None
    for layer in range(DEPTH):
        mod = mod_all[layer]
        g_mix = norm_mix[layer][None, :]
        j = layer // 2
        if layer % 2 == 0:
            w_in_x, w_uq_x, w_uk, w_uv = _mla_weights(mla_w_in[j], mla_w_uq[j], mla_w_ukv[j])
            w_in_c = w_in_x[:, :MLA_Q_LORA + MLA_KV_LORA + LANES]
            w_uq_c = w_uq_x[:, :MLA_HEADS * (MLA_NOPE + MLA_ROPE)]
            qg = mla_q_norm[j][None, :]
            kvg = mla_kv_norm[j][None, :]
            w_o = mla_w_o[j].astype(BF16)
            qn, qr, ckv, kr, kr4 = _premix_mla(xc, mod, g_mix, w_in_c, qg, w_uq_c, kvg, None, None, tm_c)
            new_ckv.append(ckv.reshape(B, S, MLA_KV_LORA))
            new_krope.append(kr.reshape(B, S, MLA_ROPE))
            att_c = _mla_attention(qn, qr, ckv, kr4, None, None, w_uk, w_uv, B, S)
            qn, qr, ckv, kr, kr4 = _premix_mla(xs, mod, g_mix, w_in_x, qg, w_uq_x, kvg, rope_tabs, Sd, tm_s)
            cache_kr4 = jnp.tile(cache_mla_krope[:, j], (1, 1, LANES // MLA_ROPE))
            att_s = _mla_attention(qn, qr, ckv, kr4, cache_mla_ckv[:, j], cache_kr4, w_uk, w_uv, Bd, 256)
        else:
            w_qkv = nat_w_qkv[j].astype(BF16)
            w_o = nat_w_o[j].astype(BF16)
            q, k_buf, v_buf = _premix_nat_cache(xc, mod, g_mix, w_qkv, k_buf, v_buf, j, DEPTH // 2, B, tm_c)
            att_c = _dense_attention(q, k_buf, v_buf, j)
            q, k, v = _premix_nat(xs, mod, g_mix, w_qkv, Sd, tm_s, BF16)
            bias = _nat_bias_pairs(nat_rpb[j], Sd // GRID_W)
            att_s = _nat_attention(q, k, v,
                                   cache_nat_k[:, j].reshape(Bd, -1, D), cache_nat_v[:, j].reshape(Bd, -1, D),
                                   bias, Bd)
        g_ffn = norm_ffn[layer][None, :]
        final_g = norm_final[None, :] if layer == DEPTH - 1 else None
        xc = _mixer_out_and_moe(xc, att_c, w_o, mod, g_ffn, wr, rb, wg, wu, wd, layer, None, final_g)
        xs = _mixer_out_and_moe(xs, att_s, w_o, mod, g_ffn, wr, rb, wg, wu, wd, layer, Sd, final_g)

    y_prompt = xc.reshape(B, S, D)
    y_sample = xs.reshape(Bd, Sd, D)
    kv_shape = (B, DEPTH // 2, S, NAT_HEADS, NAT_DH)
    return (y_prompt, y_sample, jnp.stack(new_ckv, axis=1), jnp.stack(new_krope, axis=1),
            k_buf.reshape(kv_shape), v_buf.reshape(kv_shape))
```

```python
import functools

import numpy as np
import jax
import jax.numpy as jnp
from jax import lax
from jax.experimental import pallas as pl
from jax.experimental.pallas import tpu as pltpu

F32 = jnp.float32
BF16 = jnp.bfloat16

D_MODEL = 1024
DEPTH = 4
GRID_W = 64
LANES = 128
MLA_HEADS = 16
MLA_NOPE = 64
MLA_ROPE = 32
MLA_V = 64
MLA_Q_LORA = 384
MLA_KV_LORA = 256
MLA_SCALE = (MLA_NOPE + MLA_ROPE) ** -0.5
ROPE_THETA = 10000.0
NAT_HEADS = 16
NAT_DH = 64
NAT_SCALE = NAT_DH ** -0.5
WIN_H = 8
WIN_W = 16
NAT_QROWS = 4
NAT_KROWS = NAT_QROWS + WIN_H
N_EXPERTS = 16
N_GROUPS = 4
EPG = N_EXPERTS // N_GROUPS
D_EXPERT = 256
NORM_EPS = 1e-6
NEG_INF = -1e30

HEAD_PAIRS = MLA_HEADS // 2
VMEM_LIMIT = 56 * 1024 * 1024


def _cparams(*sem):
    return pltpu.CompilerParams(dimension_semantics=sem, vmem_limit_bytes=VMEM_LIMIT)


def _sigmoid(x):
    return 1.0 / (1.0 + jnp.exp(-x))


def _rms(x, g):
    ms = jnp.mean(x * x, axis=-1, keepdims=True)
    return x * lax.rsqrt(ms + NORM_EPS) * g


def _dot(a, b):
    return jnp.dot(a, b, preferred_element_type=F32)


def _dot_nt(a, b):
    return lax.dot_general(a, b, (((1,), (1,)), ((), ())), preferred_element_type=F32)


def _lane_mask(width, idx, dtype):
    lane = lax.broadcasted_iota(jnp.int32, (1, LANES), 1)
    return ((lane >= idx * width) & (lane < (idx + 1) * width)).astype(dtype)


def _ada_kernel(cond_ref, w_ref, b_ref, o_ref):
    c = cond_ref[...]
    s = (c * _sigmoid(c)).astype(BF16)
    o_ref[0] = _dot(s, w_ref[0].astype(BF16)) + b_ref[0]


def _ada_modulation(cond8, w_ada, b_ada):
    n_chunk = 6
    return pl.pallas_call(
        _ada_kernel,
        grid=(DEPTH, n_chunk),
        in_specs=[
            pl.BlockSpec((8, D_MODEL), lambda l, j: (0, 0)),
            pl.BlockSpec((1, D_MODEL, D_MODEL), lambda l, j: (l, 0, j)),
            pl.BlockSpec((1, 1, D_MODEL), lambda l, j: (l, 0, j)),
        ],
        out_specs=pl.BlockSpec((1, 8, D_MODEL), lambda l, j: (l, 0, j)),
        out_shape=jax.ShapeDtypeStruct((DEPTH, 8, 6 * D_MODEL), F32),
        compiler_params=_cparams("parallel", "parallel"),
        name="ada_modulation",
    )(cond8, w_ada, b_ada.reshape(DEPTH, 1, 6 * D_MODEL))


def _mod_spec(rows_per_cond):
    if rows_per_cond is None:
        return pl.BlockSpec((1, 6, D_MODEL), lambda i, *_: (0, 0, 0))
    return pl.BlockSpec((1, 6, D_MODEL), lambda i, *_: (1 + i // rows_per_cond, 0, 0))


def _premix_mla_kernel(*refs, rope):
    if rope:
        (x_ref, mod_ref, g_ref, w_in_ref, qg_ref, w_uq_ref, kvg_ref,
         cosq_ref, sinq_ref, cosk_ref, sink_ref,
         qn_ref, qr_ref, ckv_ref, kr_ref, kr4_ref) = refs
    else:
        (x_ref, mod_ref, g_ref, w_in_ref, qg_ref, w_uq_ref, kvg_ref,
         qn_ref, qr_ref, ckv_ref, kr_ref, kr4_ref) = refs
    h = _rms(x_ref[...], g_ref[...]) * (1.0 + mod_ref[0, 1:2, :]) + mod_ref[0, 0:1, :]
    lat = _dot(h.astype(BF16), w_in_ref[...])
    c_q = lat[:, :MLA_Q_LORA]
    c_kv = lat[:, MLA_Q_LORA:MLA_Q_LORA + MLA_KV_LORA]
    o = MLA_Q_LORA + MLA_KV_LORA
    kr4 = lat[:, o:o + LANES]
    q = _dot(_rms(c_q, qg_ref[...]).astype(BF16), w_uq_ref[...])
    n_nope = MLA_HEADS * MLA_NOPE
    n_rope = MLA_HEADS * MLA_ROPE
    qr = q[:, n_nope:n_nope + n_rope]
    if rope:
        qr = qr * cosq_ref[...] + q[:, n_nope + n_rope:] * sinq_ref[...]
        kr4 = kr4 * cosk_ref[...] + lat[:, o + LANES:o + 2 * LANES] * sink_ref[...]
    qn_ref[...] = q[:, :n_nope].astype(BF16)
    qr_ref[...] = qr.astype(BF16)
    ckv_ref[...] = _rms(c_kv, kvg_ref[...])
    kr_ref[...] = kr4[:, :MLA_ROPE]
    kr4_ref[...] = kr4.astype(BF16)


def _premix_mla(x, mod, g, w_in, qg, w_uq, kvg, rope_tabs, rows_per_cond, tm):
    n = x.shape[0]
    rope = rope_tabs is not None
    row = lambda i: (i, 0)
    full = lambda i: (0, 0)
    in_specs = [
        pl.BlockSpec((tm, D_MODEL), row),
        _mod_spec(None if rows_per_cond is None else rows_per_cond // tm),
        pl.BlockSpec((1, D_MODEL), full),
        pl.BlockSpec(w_in.shape, full),
        pl.BlockSpec((1, MLA_Q_LORA), full),
        pl.BlockSpec(w_uq.shape, full),
        pl.BlockSpec((1, MLA_KV_LORA), full),
    ]
    args = [x, mod, g, w_in, qg, w_uq, kvg]
    if rope:
        nblk = rope_tabs[0].shape[0] // tm
        pos = lambda i: (i % nblk, 0)
        for t in rope_tabs:
            in_specs.append(pl.BlockSpec((tm, t.shape[1]), pos))
            args.append(t)
    widths = (MLA_HEADS * MLA_NOPE, MLA_HEADS * MLA_ROPE, MLA_KV_LORA, MLA_ROPE, LANES)
    dtypes = (BF16, BF16, F32, F32, BF16)
    return pl.pallas_call(
        functools.partial(_premix_mla_kernel, rope=rope),
        grid=(n // tm,),
        in_specs=in_specs,
        out_specs=[pl.BlockSpec((tm, w), row) for w in widths],
        out_shape=[jax.ShapeDtypeStruct((n, w), d) for w, d in zip(widths, dtypes)],
        compiler_params=_cparams("parallel"),
        name="premix_mla_rope" if rope else "premix_mla",
    )(*args)


LOG2E = 1.4426950408889634


def _softmax_pv(s_list, v_list, scale=1.0):
    m = functools.reduce(jnp.maximum, [jnp.max(s, axis=-1, keepdims=True) for s in s_list])
    e_list = [jnp.exp2((s - m) * (scale * LOG2E)) for s in s_list]
    l = functools.reduce(lambda a, b: a + b, [jnp.sum(e, axis=-1, keepdims=True) for e in e_list])
    o = functools.reduce(lambda a, b: a + b,
                         [_dot(e.astype(BF16), v) for e, v in zip(e_list, v_list)])
    return o / l


def _stack_heads(q, masks):
    return jnp.concatenate([q * m for m in masks], axis=0)


def _unstack_heads(o2, width):
    tq = o2.shape[0] // 2
    lane = lax.broadcasted_iota(jnp.int32, (1, LANES), 1)
    return jnp.where(lane < width, o2[:tq], o2[tq:])


def _mla_attn_kernel(*refs, cached):
    if cached:
        (qn_ref, qr_ref, ckv_ref, kr4_ref, cckv_ref, ckr4_ref, w_uk_ref, w_uv_ref,
         o_ref, kn_s, v_s, kr_s) = refs
    else:
        (qn_ref, qr_ref, ckv_ref, kr4_ref, w_uk_ref, w_uv_ref, o_ref, kn_s, v_s, kr_s) = refs
    t_own = ckv_ref.shape[0]

    @pl.when(pl.program_id(1) == 0)
    def _():
        c = ckv_ref[...].astype(BF16)
        kn_s[0:t_own, :] = _dot(c, w_uk_ref[...]).astype(BF16)
        v_s[0:t_own, :] = _dot(c, w_uv_ref[...]).astype(BF16)
        kr_s[0:t_own, :] = kr4_ref[...]
        if cached:
            cc = cckv_ref[0].astype(BF16)
            kn_s[t_own:, :] = _dot(cc, w_uk_ref[...]).astype(BF16)
            v_s[t_own:, :] = _dot(cc, w_uv_ref[...]).astype(BF16)
            kr_s[t_own:, :] = ckr4_ref[0].astype(BF16)

    kr4 = kr_s[...]
    for p in range(HEAD_PAIRS):
        sl = slice(p * LANES, (p + 1) * LANES)
        qn = qn_ref[:, sl]
        qr = qr_ref[:, (p // 2) * LANES:(p // 2 + 1) * LANES]
        k_cat = jnp.concatenate([kn_s[:, sl], kr4], axis=-1)
        q_cat = jnp.concatenate(
            [_stack_heads(qn, [_lane_mask(MLA_NOPE, i, BF16) for i in range(2)]),
             _stack_heads(qr, [_lane_mask(MLA_ROPE, (2 * p + i) % 4, BF16) for i in range(2)])], axis=-1)
        o2 = _softmax_pv([_dot_nt(q_cat, k_cat)], [v_s[:, sl]], MLA_SCALE)
        o_ref[:, sl] = _unstack_heads(o2, MLA_V).astype(BF16)


def _mla_attention(qn, qr, ckv, kr4, cache_ckv, cache_kr4, w_uk, w_uv, n_batch, tq):
    n = qn.shape[0]
    s_own = n // n_batch
    nq = s_own // tq
    cached = cache_ckv is not None
    t_all = s_own + (cache_ckv.shape[1] if cached else 0)
    qrow = lambda b, j: (b * nq + j, 0)
    own = lambda b, j: (b, 0)
    full = lambda b, j: (0, 0)
    in_specs = [
        pl.BlockSpec((tq, qn.shape[1]), qrow),
        pl.BlockSpec((tq, qr.shape[1]), qrow),
        pl.BlockSpec((s_own, MLA_KV_LORA), own),
        pl.BlockSpec((s_own, LANES), own),
    ]
    args = [qn, qr, ckv, kr4]
    if cached:
        in_specs += [pl.BlockSpec((1,) + cache_ckv.shape[1:], lambda b, j: (b, 0, 0)),
                     pl.BlockSpec((1,) + cache_kr4.shape[1:], lambda b, j: (b, 0, 0))]
        args += [cache_ckv, cache_kr4]
    in_specs += [pl.BlockSpec(w_uk.shape, full), pl.BlockSpec(w_uv.shape, full)]
    args += [w_uk, w_uv]
    return pl.pallas_call(
        functools.partial(_mla_attn_kernel, cached=cached),
        grid=(n_batch, nq),
        in_specs=in_specs,
        out_specs=pl.BlockSpec((tq, D_MODEL), qrow),
        out_shape=jax.ShapeDtypeStruct((n, D_MODEL), BF16),
        scratch_shapes=[pltpu.VMEM((t_all, D_MODEL), BF16), pltpu.VMEM((t_all, D_MODEL), BF16),
                        pltpu.VMEM((t_all, LANES), BF16)],
        compiler_params=_cparams("parallel", "arbitrary"),
        name="mla_attention_cached" if cached else "mla_attention",
    )(*args)


def _premix_nat_kernel(x_ref, mod_ref, g_ref, w_ref, q_ref, k_ref, v_ref):
    h = _rms(x_ref[...], g_ref[...]) * (1.0 + mod_ref[0, 1:2, :]) + mod_ref[0, 0:1, :]
    qkv = _dot(h.astype(BF16), w_ref[...])
    q_ref[...] = qkv[:, :D_MODEL].astype(q_ref.dtype)
    k_ref[...] = qkv[:, D_MODEL:2 * D_MODEL].astype(k_ref.dtype)
    v_ref[...] = qkv[:, 2 * D_MODEL:].astype(v_ref.dtype)


def _premix_nat(x, mod, g, w_qkv, rows_per_cond, tm, kv_dtype):
    n = x.shape[0]
    row = lambda i: (i, 0)
    full = lambda i: (0, 0)
    return pl.pallas_call(
        _premix_nat_kernel,
        grid=(n // tm,),
        in_specs=[
            pl.BlockSpec((tm, D_MODEL), row),
            _mod_spec(None if rows_per_cond is None else rows_per_cond // tm),
            pl.BlockSpec((1, D_MODEL), full),
            pl.BlockSpec(w_qkv.shape, full),
        ],
        out_specs=[pl.BlockSpec((tm, D_MODEL), row)] * 3,
        out_shape=[jax.ShapeDtypeStruct((n, D_MODEL), BF16),
                   jax.ShapeDtypeStruct((n, D_MODEL), kv_dtype),
                   jax.ShapeDtypeStruct((n, D_MODEL), kv_dtype)],
        compiler_params=_cparams("parallel"),
        name="premix_nat",
    )(x, mod, g, w_qkv)


def _premix_nat_cache_kernel(*refs, first):
    if first:
        x_ref, mod_ref, g_ref, w_ref, q_ref, k_ref, v_ref = refs
    else:
        x_ref, mod_ref, g_ref, w_ref, _, _, q_ref, k_ref, v_ref = refs
    h = _rms(x_ref[...], g_ref[...]) * (1.0 + mod_ref[0, 1:2, :]) + mod_ref[0, 0:1, :]
    qkv = _dot(h.astype(BF16), w_ref[...])
    q_ref[...] = qkv[:, :D_MODEL].astype(BF16)
    nb, n_slots, s, _ = k_ref.shape
    k_ref[:, 0] = qkv[:, D_MODEL:2 * D_MODEL].reshape(nb, s, D_MODEL)
    v_ref[:, 0] = qkv[:, 2 * D_MODEL:].reshape(nb, s, D_MODEL)
    for slot in range(1, n_slots):
        k_ref[:, slot] = jnp.zeros((nb, s, D_MODEL), F32)
        v_ref[:, slot] = jnp.zeros((nb, s, D_MODEL), F32)


def _premix_nat_cache(x, mod, g, w_qkv, k_buf, v_buf, slot, n_slots, n_batch, tm):
    n = x.shape[0]
    s = n // n_batch
    nb = tm // s
    first = k_buf is None
    row = lambda i: (i, 0)
    full = lambda i: (0, 0)
    in_specs = [pl.BlockSpec((tm, D_MODEL), row), _mod_spec(None), pl.BlockSpec((1, D_MODEL), full),
                pl.BlockSpec(w_qkv.shape, full)]
    args = [x, mod, g, w_qkv]
    if first:
        assert slot == 0
        kv_spec = pl.BlockSpec((nb, n_slots, s, D_MODEL), lambda i: (i, 0, 0, 0))
        aliases = {}
    else:
        in_specs += [pl.BlockSpec(memory_space=pl.ANY)] * 2
        args += [k_buf, v_buf]
        kv_spec = pl.BlockSpec((nb, 1, s, D_MODEL), lambda i: (i, slot, 0, 0))
        aliases = {4: 1, 5: 2}
    kv_shape = jax.ShapeDtypeStruct((n_batch, n_slots, s, D_MODEL), F32)
    return pl.pallas_call(
        functools.partial(_premix_nat_cache_kernel, first=first),
        grid=(n // tm,),
        in_specs=in_specs,
        out_specs=[pl.BlockSpec((tm, D_MODEL), row), kv_spec, kv_spec],
        out_shape=[jax.ShapeDtypeStruct((n, D_MODEL), BF16), kv_shape, kv_shape],
        input_output_aliases=aliases,
        compiler_params=_cparams("parallel"),
        name="premix_nat_cache",
    )(*args)


def _dense_attn_kernel(q_ref, k_ref, v_ref, o_ref):
    for p in range(HEAD_PAIRS):
        sl = slice(p * LANES, (p + 1) * LANES)
        q2 = _stack_heads(q_ref[:, sl], [_lane_mask(NAT_DH, i, BF16) for i in range(2)])
        k = k_ref[:, sl].astype(BF16)
        v = v_ref[:, sl].astype(BF16)
        o2 = _softmax_pv([_dot_nt(q2, k)], [v], NAT_SCALE)
        o_ref[:, sl] = _unstack_heads(o2, NAT_DH).astype(BF16)


def _dense_attention(q, k_buf, v_buf, slot):
    n = q.shape[0]
    n_batch, _, s, _ = k_buf.shape
    blk = pl.BlockSpec((s, D_MODEL), lambda b: (b, 0))
    kv = pl.BlockSpec((None, None, s, D_MODEL), lambda b: (b, slot, 0, 0))
    return pl.pallas_call(
        _dense_attn_kernel,
        grid=(n_batch,),
        in_specs=[blk, kv, kv],
        out_specs=blk,
        out_shape=jax.ShapeDtypeStruct((n, D_MODEL), BF16),
        compiler_params=_cparams("parallel"),
        name="dense_attention",
    )(q, k_buf, v_buf)


_NAT_QBLK = NAT_QROWS * GRID_W
_NAT_KBLK = NAT_KROWS * GRID_W


def _nat_block_plan(rows):
    assert rows % NAT_QROWS == 0 and rows >= NAT_KROWS and NAT_KROWS % 2 == 0
    plan, variants = [], []
    for r0 in range(0, rows, NAT_QROWS):
        ks = min(max(r0 - WIN_H // 2, 0), rows - NAT_KROWS)
        r = r0 + np.arange(NAT_QROWS)
        kr = ks + np.arange(NAT_KROWS)
        rs = np.clip(r - WIN_H // 2, 0, rows - WIN_H)
        valid_row = (kr[None, :] >= rs[:, None]) & (kr[None, :] < rs[:, None] + WIN_H)
        d0 = ks - r + (WIN_H - 1)
        for vi, (d0_v, valid_v) in enumerate(variants):
            if np.array_equal(d0, d0_v) and np.array_equal(valid_row, valid_v):
                break
        else:
            vi = len(variants)
            variants.append((d0, valid_row))
        plan.append((ks, vi))
    return plan, variants


def _nat_bias_row_range(variants):
    lo = min(int(d0.min()) for d0, _ in variants)
    hi = max(int(d0.max()) for d0, _ in variants) + NAT_KROWS
    return lo, hi


def _nat_bias_pairs(rpb, rows):
    n_heads, n_dr, n_dc = rpb.shape
    _, variants = _nat_block_plan(rows)
    lo, hi = _nat_bias_row_range(variants)
    c = np.arange(GRID_W)
    cs = np.clip(c - WIN_W // 2, 0, GRID_W - WIN_W)
    valid_col = (c[None, :] >= cs[:, None]) & (c[None, :] < cs[:, None] + WIN_W)
    d_col = c[None, :] - c[:, None] + (WIN_W - 1)
    sel = (d_col[None] == np.arange(n_dc)[:, None, None]) & valid_col[None]
    toep = jnp.einsum('hdj,jck->hdck', rpb, jnp.asarray(sel, F32), precision=lax.Precision.HIGHEST)
    toep = jnp.where(jnp.asarray(valid_col)[None, None], toep, NEG_INF)
    toep = jnp.pad(toep, ((0, 0), (max(-lo, 0), max(hi + 1 - n_dr, 0)), (0, 0), (0, 0)),
                   constant_values=NEG_INF)
    toep = toep[:, max(lo, 0):]
    return jnp.concatenate([toep[:, :-1], toep[:, 1:]], axis=-1)


def _nat_block_bias(tp_ref, head, d0, valid_row, row_lo):
    neg = jnp.full((GRID_W, 2 * GRID_W), NEG_INF, F32)
    left = lax.broadcasted_iota(jnp.int32, (1, 2 * GRID_W), 1) < GRID_W
    rows_out = []
    for dr in range(NAT_QROWS):
        pieces = []
        for a in range(0, NAT_KROWS, 2):
            ok0, ok1 = bool(valid_row[dr, a]), bool(valid_row[dr, a + 1])
            if not (ok0 or ok1):
                pieces.append(neg)
                continue
            piece = tp_ref[head, int(d0[dr]) + a - row_lo]
            if ok0 and not ok1:
                piece = jnp.where(left, piece, NEG_INF)
            elif ok1 and not ok0:
                piece = jnp.where(left, NEG_INF, piece)
            pieces.append(piece)
        rows_out.append(jnp.concatenate(pieces, axis=-1))
    return jnp.concatenate(rows_out, axis=0)


def _nat_attn_kernel(q_ref, k_ref, v_ref, kc_ref, vc_ref, tp_ref, o_ref, *, plan, variants, row_lo):
    kc = kc_ref[0].astype(BF16)
    vc = vc_ref[0].astype(BF16)
    assert NAT_SCALE == 2.0 ** round(np.log2(NAT_SCALE))
    masks = [_lane_mask(NAT_DH, i, BF16) * NAT_SCALE for i in range(2)]
    for bi, (ks, var) in enumerate(plan):
        q2 = _stack_heads(q_ref[bi * _NAT_QBLK:(bi + 1) * _NAT_QBLK, :], masks)
        k = k_ref[ks * GRID_W:ks * GRID_W + _NAT_KBLK, :]
        v = v_ref[ks * GRID_W:ks * GRID_W + _NAT_KBLK, :]
        bias2 = jnp.concatenate([_nat_block_bias(tp_ref, i, *variants[var], row_lo) for i in range(2)], axis=0)
        o2 = _softmax_pv([_dot_nt(q2, k) + bias2, _dot_nt(q2, kc)], [v, vc])
        o_ref[bi * _NAT_QBLK:(bi + 1) * _NAT_QBLK, :] = _unstack_heads(o2, NAT_DH).astype(BF16)


def _nat_attention(q, k, v, cache_k, cache_v, bias, n_batch):
    n = q.shape[0]
    s = n // n_batch
    plan, variants = _nat_block_plan(s // GRID_W)
    row_lo, _ = _nat_bias_row_range(variants)
    own = pl.BlockSpec((s, LANES), lambda p, b: (b, p))
    cache = pl.BlockSpec((1, cache_k.shape[1], LANES), lambda p, b: (b, 0, p))
    return pl.pallas_call(
        functools.partial(_nat_attn_kernel, plan=plan, variants=variants, row_lo=row_lo),
        grid=(HEAD_PAIRS, n_batch),
        in_specs=[own, own, own, cache, cache,
                  pl.BlockSpec((2,) + bias.shape[1:], lambda p, b: (p, 0, 0, 0))],
        out_specs=own,
        out_shape=jax.ShapeDtypeStruct((n, D_MODEL), BF16),
        compiler_params=_cparams("parallel", "parallel"),
        name="nat_attention",
    )(q, k, v, cache_k, cache_v, bias)


def _top2_sum(a, b, c, d):
    hi1, lo1 = jnp.maximum(a, b), jnp.minimum(a, b)
    hi2, lo2 = jnp.maximum(c, d), jnp.minimum(c, d)
    return jnp.maximum(hi1, hi2) + jnp.maximum(jnp.minimum(hi1, hi2), jnp.maximum(lo1, lo2))


def _route(scores, biased):
    sc = [scores[e:e + 1, :] for e in range(N_EXPERTS)]
    bs = [biased[e:e + 1, :] for e in range(N_EXPERTS)]
    gscore = [_top2_sum(*bs[EPG * g:EPG * (g + 1)]) for g in range(N_GROUPS)]
    best, gidx = gscore[0], jnp.zeros_like(gscore[0], dtype=jnp.int32)
    for g in range(1, N_GROUPS):
        better = gscore[g] > best
        gidx = jnp.where(better, g, gidx)
        best = jnp.where(better, gscore[g], best)
    cb = [functools.reduce(lambda a, b: a + b,
                           [jnp.where(gidx == g, bs[EPG * g + i], 0.0) for g in range(N_GROUPS)])
          for i in range(EPG)]
    cs = [functools.reduce(lambda a, b: a + b,
                           [jnp.where(gidx == g, sc[EPG * g + i], 0.0) for g in range(N_GROUPS)])
          for i in range(EPG)]
    b1, i1 = cb[0], jnp.zeros_like(gidx)
    for i in range(1, EPG):
        better = cb[i] > b1
        i1 = jnp.where(better, i, i1)
        b1 = jnp.where(better, cb[i], b1)
    b2, i2 = jnp.full_like(b1, -jnp.inf), jnp.full_like(i1, -1)
    for i in range(EPG):
        better = (i1 != i) & (cb[i] > b2)
        i2 = jnp.where(better, i, i2)
        b2 = jnp.where(better, cb[i], b2)
    sel = [(i1 == i) | (i2 == i) for i in range(EPG)]
    w = [jnp.where(sel[i], cs[i], 0.0) for i in range(EPG)]
    tot = w[0] + w[1] + w[2] + w[3]
    return gidx, [w[i] / tot for i in range(EPG)]


MOE_SRC = 256
MOE_DST = 512
_LOG2_DST = MOE_DST.bit_length() - 1
assert MOE_DST == 1 << _LOG2_DST and MOE_SRC <= MOE_DST <= 2 * MOE_SRC
_INFO_GID = EPG


def _group_tiles(cum_ref):
    n_src = cum_ref.shape[0]
    tot = [cum_ref[n_src - 1, g] for g in range(N_GROUPS)]
    ends, acc = [], 0
    for t in tot:
        acc = acc + lax.shift_right_logical(t + (MOE_DST - 1), _LOG2_DST)
        ends.append(acc)
    return tot, ends


def _dest_tile(d, cum_ref):
    tot, ends = _group_tiles(cum_ref)
    g = ((d >= ends[0]).astype(jnp.int32) + (d >= ends[1]).astype(jnp.int32)
         + (d >= ends[2]).astype(jnp.int32))
    first = jnp.where(g == 0, 0, jnp.where(g == 1, ends[0], jnp.where(g == 2, ends[1], ends[2])))
    tot_g = jnp.where(g == 0, tot[0], jnp.where(g == 1, tot[1], jnp.where(g == 2, tot[2], tot[3])))
    k0 = (d - first) * MOE_DST
    n_valid = jnp.clip(tot_g - k0, 0, MOE_DST)
    return g, k0, n_valid


def _counts_before(cum_ref, s, g):
    return jnp.where(s > 0, cum_ref[jnp.maximum(s - 1, 0), g], 0)


MOE_ALIGN = 16
_LOG2_ALIGN = MOE_ALIGN.bit_length() - 1
_LOC_ROWS = MOE_SRC + N_GROUPS * MOE_ALIGN
_INFO_LRANK = EPG + 1
H_EXT = D_MODEL + LANES
_PIECES = [MOE_SRC >> i for i in range((MOE_SRC // MOE_ALIGN).bit_length())]


def _moe_sort_tile(s, x_ref, a_ref, wo_ref, mod_ref, g_ref, wr_ref, rb_ref, xo_ref, info_ref, cum_ref,
                   lsort, seg_s, base_s):
    x = x_ref[...] + mod_ref[0, 2:3, :] * _dot(a_ref[...], wo_ref[...])
    xo_ref[...] = x
    h = _rms(x, g_ref[...]) * (1.0 + mod_ref[0, 4:5, :]) + mod_ref[0, 3:4, :]
    h_hi = h.astype(BF16)
    h_lo = (h - h_hi.astype(F32)).astype(BF16)
    hi_w = _dot(h_hi, wr_ref[...])
    logits = hi_w[:, :LANES] + (_dot(h_lo, wr_ref[:, :LANES]) + hi_w[:, LANES:])
    scores = _sigmoid(logits)
    gidx, gates = _route(scores.T[:N_EXPERTS], (scores + rb_ref[...]).T[:N_EXPERTS])
    tm = h.shape[0]
    sub = lax.broadcasted_iota(jnp.int32, (8, 1), 0)
    onehot = (sub == gidx).astype(F32)
    tri = (lax.broadcasted_iota(jnp.int32, (tm, tm), 0)
           <= lax.broadcasted_iota(jnp.int32, (tm, tm), 1)).astype(BF16)
    prefix = _dot(onehot.astype(BF16), tri)
    lrank = jnp.sum((prefix - 1.0) * onehot, axis=0, keepdims=True)

    count = [jnp.sum(onehot[g:g + 1, :]).astype(jnp.int32) for g in range(N_GROUPS)]
    padded = [lax.shift_left(lax.shift_right_logical(c + (MOE_ALIGN - 1), _LOG2_ALIGN), _LOG2_ALIGN)
              for c in count]
    loff, acc = [], 0
    for p in padded:
        loff.append(acc)
        acc = acc + p
    lpos = lrank + functools.reduce(
        lambda a, b: a + b, [jnp.where(gidx == g, jnp.asarray(v, jnp.int32).astype(F32), 0.0)
                             for g, v in enumerate(loff)])
    infot = jnp.concatenate(gates + [gidx.astype(F32), lrank, jnp.zeros((LANES - EPG - 2, tm), F32)], axis=0)
    info_ref[...] = infot.T
    g_hi = [gt.astype(BF16).astype(F32) for gt in gates]
    g_lo = [gt - gh for gt, gh in zip(gates, g_hi)]
    gext = jnp.concatenate(g_hi + g_lo + [jnp.zeros((LANES - 2 * EPG, tm), F32)], axis=0)
    h_ext = jnp.concatenate([h_hi, gext.T.astype(BF16)], axis=-1)
    perm = (lax.broadcasted_iota(jnp.int32, (_LOC_ROWS, 1), 0).astype(F32) == lpos).astype(BF16)
    lsort[s] = _dot(perm, h_ext).astype(BF16)

    new_base = []
    for g in range(N_GROUPS):
        seg_s[s, g] = padded[g]
        new_base.append(base_s[g] + padded[g])
        base_s[g] = new_base[g]
    cum = functools.reduce(lambda a, b: a + b,
                           [jnp.where(sub == g, new_base[g], 0) for g in range(N_GROUPS)])
    cum_ref[0] = jnp.broadcast_to(cum, (8, LANES))


def _copy_pieces(length, make_copy, wait):
    done = jnp.int32(0)
    for rows in _PIECES:
        take = (length & rows) != 0

        @pl.when(take)
        def _():
            cp = make_copy(pl.multiple_of(done, MOE_ALIGN), rows)
            cp.wait() if wait else cp.start()
        done = done + jnp.where(take, rows, 0)


def _moe_sort_kernel(x_ref, a_ref, wo_ref, mod_ref, g_ref, wr_ref, rb_ref, xo_ref, info_ref, cum_ref, srt_ref,
                     lsort, zeros_v, seg_s, base_s, sem):
    s = pl.program_id(0)
    n_src = lsort.shape[0]
    n_rows = srt_ref.shape[0]

    @pl.when(s == 0)
    def _():
        for g in range(N_GROUPS):
            base_s[g] = 0
        zeros_v[...] = jnp.zeros_like(zeros_v)

    @pl.when(s < n_src)
    def _():
        _moe_sort_tile(s, x_ref, a_ref, wo_ref, mod_ref, g_ref, wr_ref, rb_ref, xo_ref, info_ref, cum_ref,
                       lsort, seg_s, base_s)

    @pl.when(s == n_src)
    def _():
        starts, acc = [], jnp.int32(0)
        for g in range(N_GROUPS):
            starts.append(acc)
            acc = acc + lax.shift_left(
                lax.shift_right_logical(base_s[g] + (MOE_DST - 1), _LOG2_DST), _LOG2_DST)
        end = acc

        def all_copies(wait):
            def tile_copies(t, dst):
                src = jnp.int32(0)
                new_dst = []
                for g in range(N_GROUPS):
                    n = seg_s[t, g]
                    _copy_pieces(n, lambda off, rows, src=src, g=g: pltpu.make_async_copy(
                        lsort.at[t, pl.ds(pl.multiple_of(src + off, MOE_ALIGN), rows)],
                        srt_ref.at[pl.ds(pl.multiple_of(dst[g] + off, MOE_ALIGN), rows)], sem.at[0]), wait)
                    src = src + n
                    new_dst.append(dst[g] + n)
                return tuple(new_dst)

            dst = lax.fori_loop(0, n_src, tile_copies, tuple(starts))
            for g in range(N_GROUPS):
                gap_end = starts[g + 1] if g + 1 < N_GROUPS else end
                _copy_pieces(gap_end - dst[g], lambda off, rows, g=g: pltpu.make_async_copy(
                    zeros_v.at[pl.ds(0, rows)],
                    srt_ref.at[pl.ds(pl.multiple_of(dst[g] + off, MOE_ALIGN), rows)], sem.at[0]), wait)

            def zero_tile(r, carry):
                cp = pltpu.make_async_copy(
                    zeros_v, srt_ref.at[pl.ds(pl.multiple_of(end + r * MOE_DST, MOE_DST), MOE_DST)], sem.at[0])
                cp.wait() if wait else cp.start()
                return carry

            lax.fori_loop(0, lax.shift_right_logical(n_rows - end, _LOG2_DST), zero_tile, 0)

        all_copies(False)
        all_copies(True)


def _moe_ffn2_kernel(cum_ref, xs_ref, wg_ref, wu_ref, wd_ref, y_ref):
    _, ends = _group_tiles(cum_ref)

    @pl.when(pl.program_id(0) >= ends[-1])
    def _():
        y_ref[...] = jnp.zeros_like(y_ref)

    @pl.when(pl.program_id(0) < ends[-1])
    def _():
        xs = xs_ref[:, :D_MODEL]
        gext = xs_ref[:, D_MODEL:].astype(F32)
        acts = []
        for i in range(EPG):
            a = _dot(xs, wg_ref[i])
            u = _dot(xs, wu_ref[i])
            gate = gext[:, i:i + 1] + gext[:, EPG + i:EPG + i + 1]
            acts.append(((a * _sigmoid(a)) * u * gate).astype(BF16))
        act = jnp.concatenate(acts, axis=-1)
        y_ref[...] = _dot(act, wd_ref[...].reshape(EPG * D_EXPERT, D_MODEL)).astype(BF16)


def _segment_row(cum_ref, s, g):
    _, ends = _group_tiles(cum_ref)
    first_tile = ends[g - 1] if g > 0 else 0
    return first_tile * MOE_DST + _counts_before(cum_ref, s, g)


def _unsort_window(cum_ref, s, g, n_rows):
    return pl.multiple_of(jnp.minimum(_segment_row(cum_ref, s, g), n_rows - MOE_SRC), MOE_ALIGN)


def _moe_unsort2_kernel(*refs, final_norm):
    if final_norm:
        cum_ref, x_ref, mod_ref, info_ref, y_ref, gf_ref, o_ref = refs
    else:
        cum_ref, x_ref, mod_ref, info_ref, y_ref, o_ref = refs
    s = pl.program_id(0)
    n_rows = y_ref.shape[0]
    inf = info_ref[...]
    gid = inf[:, _INFO_GID:_INFO_GID + 1]
    lrank = inf[:, _INFO_LRANK:_INFO_LRANK + 1]
    col = lax.broadcasted_iota(jnp.int32, (1, MOE_SRC), 1).astype(F32)
    perms, wins = [], []
    for g in range(N_GROUPS):
        win = _unsort_window(cum_ref, s, g, n_rows)
        shift = _segment_row(cum_ref, s, g) - win
        rel = jnp.where(gid == float(g), lrank + shift.astype(F32), -1.0)
        perms.append((rel == col).astype(BF16))
        wins.append(y_ref[pl.ds(win, MOE_SRC), :])
    out = x_ref[...] + mod_ref[0, 5:6, :] * _dot(jnp.concatenate(perms, axis=-1),
                                                 jnp.concatenate(wins, axis=0))
    o_ref[...] = _rms(out, gf_ref[...]) if final_norm else out


def _mixer_out_and_moe(x, att, w_o, mod, g, w_router, router_bias, wg, wu, wd, layer, rows_per_cond,
                       final_g=None):
    n = x.shape[0]
    n_src = n // MOE_SRC
    n_dst = (n + N_GROUPS * n_src * (MOE_ALIGN - 1)) // MOE_DST + N_GROUPS
    n_rows = n_dst * MOE_DST
    last = n_src - 1
    row = lambda i: (jnp.minimum(i, last), 0)
    full = lambda i: (0, 0)
    x, info, cum, srt = pl.pallas_call(
        _moe_sort_kernel,
        grid=(n_src + 1,),
        in_specs=[
            pl.BlockSpec((MOE_SRC, D_MODEL), row),
            pl.BlockSpec((MOE_SRC, D_MODEL), row),
            pl.BlockSpec(w_o.shape, full),
            pl.BlockSpec((1, 6, D_MODEL), (lambda i: (0, 0, 0)) if rows_per_cond is None else
                         (lambda i: (1 + jnp.minimum(i, last) // (rows_per_cond // MOE_SRC), 0, 0))),
            pl.BlockSpec((1, D_MODEL), full),
            pl.BlockSpec(w_router.shape, full),
            pl.BlockSpec(router_bias.shape, full),
        ],
        out_specs=[pl.BlockSpec((MOE_SRC, D_MODEL), row),
                   pl.BlockSpec((MOE_SRC, LANES), row),
                   pl.BlockSpec((1, 8, LANES), lambda i: (jnp.minimum(i, last), 0, 0)),
                   pl.BlockSpec(memory_space=pl.ANY)],
        out_shape=[jax.ShapeDtypeStruct((n, D_MODEL), F32),
                   jax.ShapeDtypeStruct((n, LANES), F32),
                   jax.ShapeDtypeStruct((n_src, 8, LANES), jnp.int32),
                   jax.ShapeDtypeStruct((n_rows, H_EXT), BF16)],
        scratch_shapes=[pltpu.VMEM((n_src, _LOC_ROWS, H_EXT), BF16),
                        pltpu.VMEM((MOE_DST, H_EXT), BF16),
                        pltpu.SMEM((n_src, N_GROUPS), jnp.int32),
                        pltpu.SMEM((N_GROUPS,), jnp.int32),
                        pltpu.SemaphoreType.DMA((1,))],
        compiler_params=_cparams("arbitrary"),
        name="moe_sort",
    )(x, att, w_o, mod, g, w_router, router_bias)
    cum = cum[:, :N_GROUPS, 0]

    blk = lambda width: pl.BlockSpec((MOE_DST, width), lambda d, c: (d, 0))
    expert = lambda shape: pl.BlockSpec(shape, lambda d, c: (layer, _dest_tile(d, c)[0], 0, 0))
    y = pl.pallas_call(
        _moe_ffn2_kernel,
        grid_spec=pltpu.PrefetchScalarGridSpec(
            num_scalar_prefetch=1,
            grid=(n_dst,),
            in_specs=[blk(H_EXT),
                      expert((None, EPG, D_MODEL, D_EXPERT)),
                      expert((None, EPG, D_MODEL, D_EXPERT)),
                      expert((None, EPG, D_EXPERT, D_MODEL))],
            out_specs=blk(D_MODEL),
        ),
        out_shape=jax.ShapeDtypeStruct((n_rows, D_MODEL), BF16),
        compiler_params=_cparams("arbitrary"),
        name="moe_experts",
    )(cum, srt, wg, wu, wd)

    in_specs = [pl.BlockSpec((MOE_SRC, D_MODEL), lambda i, c: (i, 0)),
                _mod_spec(None if rows_per_cond is None else rows_per_cond // MOE_SRC),
                pl.BlockSpec((MOE_SRC, LANES), lambda i, c: (i, 0)),
                pl.BlockSpec(y.shape, lambda i, c: (0, 0), pipeline_mode=pl.Buffered(1))]
    args = [cum, x, mod, info, y]
    if final_g is not None:
        in_specs.append(pl.BlockSpec((1, D_MODEL), lambda i, c: (0, 0)))
        args.append(final_g)
    return pl.pallas_call(
        functools.partial(_moe_unsort2_kernel, final_norm=final_g is not None),
        grid_spec=pltpu.PrefetchScalarGridSpec(
            num_scalar_prefetch=1,
            grid=(n_src,),
            in_specs=in_specs,
            out_specs=pl.BlockSpec((MOE_SRC, D_MODEL), lambda i, c: (i, 0)),
        ),
        out_shape=jax.ShapeDtypeStruct((n, D_MODEL), F32),
        compiler_params=_cparams("parallel"),
        name="moe_unsort",
    )(*args)


def _rope_tables(n_lat):
    t = np.arange(n_lat)
    n_freq = MLA_ROPE // 4
    inv_freq = jnp.asarray(ROPE_THETA, F32) ** (-jnp.arange(n_freq, dtype=F32) / n_freq)
    ar = jnp.asarray(t // GRID_W, F32)[:, None] * inv_freq
    ac = jnp.asarray(t % GRID_W, F32)[:, None] * inv_freq
    cos = jnp.concatenate([jnp.cos(ar), jnp.cos(ar), jnp.cos(ac), jnp.cos(ac)], axis=-1)
    sin = jnp.concatenate([-jnp.sin(ar), jnp.sin(ar), -jnp.sin(ac), jnp.sin(ac)], axis=-1)
    return (jnp.tile(cos, (1, MLA_HEADS)), jnp.tile(sin, (1, MLA_HEADS)),
            jnp.tile(cos, (1, LANES // MLA_ROPE)), jnp.tile(sin, (1, LANES // MLA_ROPE)))


_ROPE_SWAP = np.concatenate([np.arange(8, 16), np.arange(0, 8), np.arange(24, 32), np.arange(16, 24)])


def _mla_weights(w_in, w_uq, w_ukv):
    o = MLA_Q_LORA + MLA_KV_LORA
    kr = w_in[:, o:]
    rep = LANES // MLA_ROPE
    w_in_x = jnp.concatenate([w_in[:, :o], jnp.tile(kr, (1, rep)), jnp.tile(kr[:, _ROPE_SWAP], (1, rep))],
                             axis=-1).astype(BF16)
    uq = w_uq.reshape(MLA_Q_LORA, MLA_HEADS, MLA_NOPE + MLA_ROPE)
    q_nope = uq[:, :, :MLA_NOPE].reshape(MLA_Q_LORA, -1)
    q_rope = uq[:, :, MLA_NOPE:]
    w_uq_x = jnp.concatenate([q_nope, q_rope.reshape(MLA_Q_LORA, -1),
                              q_rope[:, :, _ROPE_SWAP].reshape(MLA_Q_LORA, -1)], axis=-1).astype(BF16)
    ukv = w_ukv.reshape(MLA_KV_LORA, MLA_HEADS, MLA_NOPE + MLA_V)
    w_uk = ukv[:, :, :MLA_NOPE].reshape(MLA_KV_LORA, -1).astype(BF16)
    w_uv = ukv[:, :, MLA_NOPE:].reshape(MLA_KV_LORA, -1).astype(BF16)
    return w_in_x, w_uq_x, w_uk, w_uv


def kernel(x_prompt, x_sample, cache_mla_ckv, cache_mla_krope, cache_nat_k, cache_nat_v, c, c_ctx,
           w_ada, b_ada, norm_mix, norm_ffn, norm_final, mla_w_in, mla_q_norm, mla_w_uq, mla_kv_norm,
           mla_w_ukv, mla_w_o, nat_w_qkv, nat_rpb, nat_w_o, w_router, router_bias,
           moe_w_gate, moe_w_up, moe_w_down):
    B, S, D = x_prompt.shape
    Bd, Sd, _ = x_sample.shape
    assert D == D_MODEL and Bd + 1 <= 8
    tm_c, tm_s = 512, 512

    xc = x_prompt.reshape(B * S, D)
    xs = x_sample.reshape(Bd * Sd, D)
    cond8 = jnp.concatenate([c_ctx[None, :], c, jnp.zeros((8 - 1 - Bd, D), F32)], axis=0)
    mod_all = _ada_modulation(cond8, w_ada, b_ada).reshape(DEPTH, 8, 6, D)

    rope_tabs = _rope_tables(Sd)
    wr = jnp.pad(w_router, ((0, 0), (0, LANES - N_EXPERTS)))
    wr_hi = wr.astype(BF16)
    wr = jnp.concatenate([wr_hi, (wr - wr_hi.astype(F32)).astype(BF16)], axis=1)
    rb = jnp.pad(router_bias, (0, LANES - N_EXPERTS)).reshape(1, LANES)
    wg = moe_w_gate.astype(BF16)
    wu = moe_w_up.astype(BF16)
    wd = moe_w_down.astype(BF16)

    new_ckv, new_krope = [], []
    k_buf = v_buf = None
    for layer in range(DEPTH):
        mod = mod_all[layer]
        g_mix = norm_mix[layer][None, :]
        j = layer // 2
        if layer % 2 == 0:
            w_in_x, w_uq_x, w_uk, w_uv = _mla_weights(mla_w_in[j], mla_w_uq[j], mla_w_ukv[j])
            w_in_c = w_in_x[:, :MLA_Q_LORA + MLA_KV_LORA + LANES]
            w_uq_c = w_uq_x[:, :MLA_HEADS * (MLA_NOPE + MLA_ROPE)]
            qg = mla_q_norm[j][None, :]
            kvg = mla_kv_norm[j][None, :]
            w_o = mla_w_o[j].astype(BF16)
            qn, qr, ckv, kr, kr4 = _premix_mla(xc, mod, g_mix, w_in_c, qg, w_uq_c, kvg, None, None, tm_c)
            new_ckv.append(ckv.reshape(B, S, MLA_KV_LORA))
            new_krope.append(kr.reshape(B, S, MLA_ROPE))
            att_c = _mla_attention(qn, qr, ckv, kr4, None, None, w_uk, w_uv, B, S)
            qn, qr, ckv, kr, kr4 = _premix_mla(xs, mod, g_mix, w_in_x, qg, w_uq_x, kvg, rope_tabs, Sd, tm_s)
            cache_kr4 = jnp.tile(cache_mla_krope[:, j], (1, 1, LANES // MLA_ROPE))
            att_s = _mla_attention(qn, qr, ckv, kr4, cache_mla_ckv[:, j], cache_kr4, w_uk, w_uv, Bd, 256)
        else:
            w_qkv = nat_w_qkv[j].astype(BF16)
            w_o = nat_w_o[j].astype(BF16)
            q, k_buf, v_buf = _premix_nat_cache(xc, mod, g_mix, w_qkv, k_buf, v_buf, j, DEPTH // 2, B, tm_c)
            att_c = _dense_attention(q, k_buf, v_buf, j)
            q, k, v = _premix_nat(xs, mod, g_mix, w_qkv, Sd, tm_s, BF16)
            bias = _nat_bias_pairs(nat_rpb[j], Sd // GRID_W)
            att_s = _nat_attention(q, k, v,
                                   cache_nat_k[:, j].reshape(Bd, -1, D), cache_nat_v[:, j].reshape(Bd, -1, D),
                                   bias, Bd)
        g_ffn = norm_ffn[layer][None, :]
        final_g = norm_final[None, :] if layer == DEPTH - 1 else None
        xc = _mixer_out_and_moe(xc, att_c, w_o, mod, g_ffn, wr, rb, wg, wu, wd, layer, None, final_g)
        xs = _mixer_out_and_moe(xs, att_s, w_o, mod, g_ffn, wr, rb, wg, wu, wd, layer, Sd, final_g)

    y_prompt = xc.reshape(B, S, D)
    y_sample = xs.reshape(Bd, Sd, D)
    kv_shape = (B, DEPTH // 2, S, NAT_HEADS, NAT_DH)
    return (y_prompt, y_sample, jnp.stack(new_ckv, axis=1), jnp.stack(new_krope, axis=1),
            k_buf.reshape(kv_shape), v_buf.reshape(kv_shape))
```

```python
import functools

import numpy as np
import jax
import jax.numpy as jnp
from jax import lax
from jax.experimental import pallas as pl
from jax.experimental.pallas import tpu as pltpu

F32 = jnp.float32
BF16 = jnp.bfloat16

D_MODEL = 1024
DEPTH = 4
GRID_W = 64
LANES = 128
MLA_HEADS = 16
MLA_NOPE = 64
MLA_ROPE = 32
MLA_V = 64
MLA_Q_LORA = 384
MLA_KV_LORA = 256
MLA_SCALE = (MLA_NOPE + MLA_ROPE) ** -0.5
ROPE_THETA = 10000.0
NAT_HEADS = 16
NAT_DH = 64
NAT_SCALE = NAT_DH ** -0.5
WIN_H = 8
WIN_W = 16
NAT_QROWS = 4
NAT_KROWS = NAT_QROWS + WIN_H
N_EXPERTS = 16
N_GROUPS = 4
EPG = N_EXPERTS // N_GROUPS
D_EXPERT = 256
NORM_EPS = 1e-6
NEG_INF = -1e30

HEAD_PAIRS = MLA_HEADS // 2
VMEM_LIMIT = 56 * 1024 * 1024


def _cparams(*sem):
    return pltpu.CompilerParams(dimension_semantics=sem, vmem_limit_bytes=VMEM_LIMIT)


def _sigmoid(x):
    return 1.0 / (1.0 + jnp.exp(-x))


def _rms(x, g):
    ms = jnp.mean(x * x, axis=-1, keepdims=True)
    return x * lax.rsqrt(ms + NORM_EPS) * g


def _dot(a, b):
    return jnp.dot(a, b, preferred_element_type=F32)


def _dot_nt(a, b):
    return lax.dot_general(a, b, (((1,), (1,)), ((), ())), preferred_element_type=F32)


def _lane_mask(width, idx, dtype):
    lane = lax.broadcasted_iota(jnp.int32, (1, LANES), 1)
    return ((lane >= idx * width) & (lane < (idx + 1) * width)).astype(dtype)


def _ada_kernel(cond_ref, w_ref, b_ref, o_ref):
    c = cond_ref[...]
    s = (c * _sigmoid(c)).astype(BF16)
    o_ref[0] = _dot(s, w_ref[0].astype(BF16)) + b_ref[0]


def _ada_modulation(cond8, w_ada, b_ada):
    n_chunk = 6
    return pl.pallas_call(
        _ada_kernel,
        grid=(DEPTH, n_chunk),
        in_specs=[
            pl.BlockSpec((8, D_MODEL), lambda l, j: (0, 0)),
            pl.BlockSpec((1, D_MODEL, D_MODEL), lambda l, j: (l, 0, j)),
            pl.BlockSpec((1, 1, D_MODEL), lambda l, j: (l, 0, j)),
        ],
        out_specs=pl.BlockSpec((1, 8, D_MODEL), lambda l, j: (l, 0, j)),
        out_shape=jax.ShapeDtypeStruct((DEPTH, 8, 6 * D_MODEL), F32),
        compiler_params=_cparams("parallel", "parallel"),
        name="ada_modulation",
    )(cond8, w_ada, b_ada.reshape(DEPTH, 1, 6 * D_MODEL))


def _mod_spec(rows_per_cond):
    if rows_per_cond is None:
        return pl.BlockSpec((1, 6, D_MODEL), lambda i, *_: (0, 0, 0))
    return pl.BlockSpec((1, 6, D_MODEL), lambda i, *_: (1 + i // rows_per_cond, 0, 0))


def _premix_mla_kernel(*refs, rope):
    if rope:
        (x_ref, mod_ref, g_ref, w_in_ref, qg_ref, w_uq_ref, kvg_ref,
         cosq_ref, sinq_ref, cosk_ref, sink_ref,
         qn_ref, qr_ref, ckv_ref, kr_ref, kr4_ref) = refs
    else:
        (x_ref, mod_ref, g_ref, w_in_ref, qg_ref, w_uq_ref, kvg_ref,
         qn_ref, qr_ref, ckv_ref, kr_ref, kr4_ref) = refs
    h = _rms(x_ref[...], g_ref[...]) * (1.0 + mod_ref[0, 1:2, :]) + mod_ref[0, 0:1, :]
    lat = _dot(h.astype(BF16), w_in_ref[...])
    c_q = lat[:, :MLA_Q_LORA]
    c_kv = lat[:, MLA_Q_LORA:MLA_Q_LORA + MLA_KV_LORA]
    o = MLA_Q_LORA + MLA_KV_LORA
    kr4 = lat[:, o:o + LANES]
    q = _dot(_rms(c_q, qg_ref[...]).astype(BF16), w_uq_ref[...])
    n_nope = MLA_HEADS * MLA_NOPE
    n_rope = MLA_HEADS * MLA_ROPE
    qr = q[:, n_nope:n_nope + n_rope]
    if rope:
        qr = qr * cosq_ref[...] + q[:, n_nope + n_rope:] * sinq_ref[...]
        kr4 = kr4 * cosk_ref[...] + lat[:, o + LANES:o + 2 * LANES] * sink_ref[...]
    qn_ref[...] = q[:, :n_nope].astype(BF16)
    qr_ref[...] = qr.astype(BF16)
    ckv_ref[...] = _rms(c_kv, kvg_ref[...])
    kr_ref[...] = kr4[:, :MLA_ROPE]
    kr4_ref[...] = kr4.astype(BF16)


def _premix_mla(x, mod, g, w_in, qg, w_uq, kvg, rope_tabs, rows_per_cond, tm):
    n = x.shape[0]
    rope = rope_tabs is not None
    row = lambda i: (i, 0)
    full = lambda i: (0, 0)
    in_specs = [
        pl.BlockSpec((tm, D_MODEL), row),
        _mod_spec(None if rows_per_cond is None else rows_per_cond // tm),
        pl.BlockSpec((1, D_MODEL), full),
        pl.BlockSpec(w_in.shape, full),
        pl.BlockSpec((1, MLA_Q_LORA), full),
        pl.BlockSpec(w_uq.shape, full),
        pl.BlockSpec((1, MLA_KV_LORA), full),
    ]
    args = [x, mod, g, w_in, qg, w_uq, kvg]
    if rope:
        nblk = rope_tabs[0].shape[0] // tm
        pos = lambda i: (i % nblk, 0)
        for t in rope_tabs:
            in_specs.append(pl.BlockSpec((tm, t.shape[1]), pos))
            args.append(t)
    widths = (MLA_HEADS * MLA_NOPE, MLA_HEADS * MLA_ROPE, MLA_KV_LORA, MLA_ROPE, LANES)
    dtypes = (BF16, BF16, F32, F32, BF16)
    return pl.pallas_call(
        functools.partial(_premix_mla_kernel, rope=rope),
        grid=(n // tm,),
        in_specs=in_specs,
        out_specs=[pl.BlockSpec((tm, w), row) for w in widths],
        out_shape=[jax.ShapeDtypeStruct((n, w), d) for w, d in zip(widths, dtypes)],
        compiler_params=_cparams("parallel"),
        name="premix_mla_rope" if rope else "premix_mla",
    )(*args)


LOG2E = 1.4426950408889634


def _softmax_pv(s_list, v_list, scale=1.0):
    m = functools.reduce(jnp.maximum, [jnp.max(s, axis=-1, keepdims=True) for s in s_list])
    e_list = [jnp.exp2((s - m) * (scale * LOG2E)) for s in s_list]
    l = functools.reduce(lambda a, b: a + b, [jnp.sum(e, axis=-1, keepdims=True) for e in e_list])
    o = functools.reduce(lambda a, b: a + b,
                         [_dot(e.astype(BF16), v) for e, v in zip(e_list, v_list)])
    return o / l


def _stack_heads(q, masks):
    return jnp.concatenate([q * m for m in masks], axis=0)


def _unstack_heads(o2, width):
    tq = o2.shape[0] // 2
    lane = lax.broadcasted_iota(jnp.int32, (1, LANES), 1)
    return jnp.where(lane < width, o2[:tq], o2[tq:])


def _mla_attn_kernel(*refs, cached):
    if cached:
        (qn_ref, qr_ref, ckv_ref, kr4_ref, cckv_ref, ckr4_ref, w_uk_ref, w_uv_ref,
         o_ref, kn_s, v_s, kr_s) = refs
    else:
        (qn_ref, qr_ref, ckv_ref, kr4_ref, w_uk_ref, w_uv_ref, o_ref, kn_s, v_s, kr_s) = refs
    t_own = ckv_ref.shape[0]

    @pl.when(pl.program_id(1) == 0)
    def _():
        c = ckv_ref[...].astype(BF16)
        kn_s[0:t_own, :] = _dot(c, w_uk_ref[...]).astype(BF16)
        v_s[0:t_own, :] = _dot(c, w_uv_ref[...]).astype(BF16)
        kr_s[0:t_own, :] = kr4_ref[...]
        if cached:
            cc = cckv_ref[0].astype(BF16)
            kn_s[t_own:, :] = _dot(cc, w_uk_ref[...]).astype(BF16)
            v_s[t_own:, :] = _dot(cc, w_uv_ref[...]).astype(BF16)
            kr_s[t_own:, :] = ckr4_ref[0].astype(BF16)

    kr4 = kr_s[...]
    for p in range(HEAD_PAIRS):
        sl = slice(p * LANES, (p + 1) * LANES)
        qn = qn_ref[:, sl]
        qr = qr_ref[:, (p // 2) * LANES:(p // 2 + 1) * LANES]
        k_cat = jnp.concatenate([kn_s[:, sl], kr4], axis=-1)
        q_cat = jnp.concatenate(
            [_stack_heads(qn, [_lane_mask(MLA_NOPE, i, BF16) for i in range(2)]),
             _stack_heads(qr, [_lane_mask(MLA_ROPE, (2 * p + i) % 4, BF16) for i in range(2)])], axis=-1)
        o2 = _softmax_pv([_dot_nt(q_cat, k_cat)], [v_s[:, sl]], MLA_SCALE)
        o_ref[:, sl] = _unstack_heads(o2, MLA_V).astype(BF16)


def _mla_attention(qn, qr, ckv, kr4, cache_ckv, cache_kr4, w_uk, w_uv, n_batch, tq):
    n = qn.shape[0]
    s_own = n // n_batch
    nq = s_own // tq
    cached = cache_ckv is not None
    t_all = s_own + (cache_ckv.shape[1] if cached else 0)
    qrow = lambda b, j: (b * nq + j, 0)
    own = lambda b, j: (b, 0)
    full = lambda b, j: (0, 0)
    in_specs = [
        pl.BlockSpec((tq, qn.shape[1]), qrow),
        pl.BlockSpec((tq, qr.shape[1]), qrow),
        pl.BlockSpec((s_own, MLA_KV_LORA), own),
        pl.BlockSpec((s_own, LANES), own),
    ]
    args = [qn, qr, ckv, kr4]
    if cached:
        in_specs += [pl.BlockSpec((1,) + cache_ckv.shape[1:], lambda b, j: (b, 0, 0)),
                     pl.BlockSpec((1,) + cache_kr4.shape[1:], lambda b, j: (b, 0, 0))]
        args += [cache_ckv, cache_kr4]
    in_specs += [pl.BlockSpec(w_uk.shape, full), pl.BlockSpec(w_uv.shape, full)]
    args += [w_uk, w_uv]
    return pl.pallas_call(
        functools.partial(_mla_attn_kernel, cached=cached),
        grid=(n_batch, nq),
        in_specs=in_specs,
        out_specs=pl.BlockSpec((tq, D_MODEL), qrow),
        out_shape=jax.ShapeDtypeStruct((n, D_MODEL), BF16),
        scratch_shapes=[pltpu.VMEM((t_all, D_MODEL), BF16), pltpu.VMEM((t_all, D_MODEL), BF16),
                        pltpu.VMEM((t_all, LANES), BF16)],
        compiler_params=_cparams("parallel", "arbitrary"),
        name="mla_attention_cached" if cached else "mla_attention",
    )(*args)


def _premix_nat_kernel(x_ref, mod_ref, g_ref, w_ref, q_ref, k_ref, v_ref):
    h = _rms(x_ref[...], g_ref[...]) * (1.0 + mod_ref[0, 1:2, :]) + mod_ref[0, 0:1, :]
    qkv = _dot(h.astype(BF16), w_ref[...])
    q_ref[...] = qkv[:, :D_MODEL].astype(q_ref.dtype)
    k_ref[...] = qkv[:, D_MODEL:2 * D_MODEL].astype(k_ref.dtype)
    v_ref[...] = qkv[:, 2 * D_MODEL:].astype(v_ref.dtype)


def _premix_nat(x, mod, g, w_qkv, rows_per_cond, tm, kv_dtype):
    n = x.shape[0]
    row = lambda i: (i, 0)
    full = lambda i: (0, 0)
    return pl.pallas_call(
        _premix_nat_kernel,
        grid=(n // tm,),
        in_specs=[
            pl.BlockSpec((tm, D_MODEL), row),
            _mod_spec(None if rows_per_cond is None else rows_per_cond // tm),
            pl.BlockSpec((1, D_MODEL), full),
            pl.BlockSpec(w_qkv.shape, full),
        ],
        out_specs=[pl.BlockSpec((tm, D_MODEL), row)] * 3,
        out_shape=[jax.ShapeDtypeStruct((n, D_MODEL), BF16),
                   jax.ShapeDtypeStruct((n, D_MODEL), kv_dtype),
                   jax.ShapeDtypeStruct((n, D_MODEL), kv_dtype)],
        compiler_params=_cparams("parallel"),
        name="premix_nat",
    )(x, mod, g, w_qkv)


def _premix_nat_cache_kernel(*refs, first):
    if first:
        x_ref, mod_ref, g_ref, w_ref, q_ref, k_ref, v_ref = refs
    else:
        x_ref, mod_ref, g_ref, w_ref, _, _, q_ref, k_ref, v_ref = refs
    h = _rms(x_ref[...], g_ref[...]) * (1.0 + mod_ref[0, 1:2, :]) + mod_ref[0, 0:1, :]
    qkv = _dot(h.astype(BF16), w_ref[...])
    q_ref[...] = qkv[:, :D_MODEL].astype(BF16)
    nb, n_slots, s, _ = k_ref.shape
    k_ref[:, 0] = qkv[:, D_MODEL:2 * D_MODEL].reshape(nb, s, D_MODEL)
    v_ref[:, 0] = qkv[:, 2 * D_MODEL:].reshape(nb, s, D_MODEL)
    for slot in range(1, n_slots):
        k_ref[:, slot] = jnp.zeros((nb, s, D_MODEL), F32)
        v_ref[:, slot] = jnp.zeros((nb, s, D_MODEL), F32)


def _premix_nat_cache(x, mod, g, w_qkv, k_buf, v_buf, slot, n_slots, n_batch, tm):
    n = x.shape[0]
    s = n // n_batch
    nb = tm // s
    first = k_buf is None
    row = lambda i: (i, 0)
    full = lambda i: (0, 0)
    in_specs = [pl.BlockSpec((tm, D_MODEL), row), _mod_spec(None), pl.BlockSpec((1, D_MODEL), full),
                pl.BlockSpec(w_qkv.shape, full)]
    args = [x, mod, g, w_qkv]
    if first:
        assert slot == 0
        kv_spec = pl.BlockSpec((nb, n_slots, s, D_MODEL), lambda i: (i, 0, 0, 0))
        aliases = {}
    else:
        in_specs += [pl.BlockSpec(memory_space=pl.ANY)] * 2
        args += [k_buf, v_buf]
        kv_spec = pl.BlockSpec((nb, 1, s, D_MODEL), lambda i: (i, slot, 0, 0))
        aliases = {4: 1, 5: 2}
    kv_shape = jax.ShapeDtypeStruct((n_batch, n_slots, s, D_MODEL), F32)
    return pl.pallas_call(
        functools.partial(_premix_nat_cache_kernel, first=first),
        grid=(n // tm,),
        in_specs=in_specs,
        out_specs=[pl.BlockSpec((tm, D_MODEL), row), kv_spec, kv_spec],
        out_shape=[jax.ShapeDtypeStruct((n, D_MODEL), BF16), kv_shape, kv_shape],
        input_output_aliases=aliases,
        compiler_params=_cparams("parallel"),
        name="premix_nat_cache",
    )(*args)


DENSE_ATTN_BATCHES = 2


def _dense_attn_kernel(q_ref, k_ref, v_ref, o_ref):
    nb, s, _ = k_ref.shape
    for b in range(nb):
        rows = slice(b * s, (b + 1) * s)
        for p in range(HEAD_PAIRS):
            sl = slice(p * LANES, (p + 1) * LANES)
            q2 = _stack_heads(q_ref[rows, sl], [_lane_mask(NAT_DH, i, BF16) for i in range(2)])
            k = k_ref[b, :, sl].astype(BF16)
            v = v_ref[b, :, sl].astype(BF16)
            o2 = _softmax_pv([_dot_nt(q2, k)], [v], NAT_SCALE)
            o_ref[rows, sl] = _unstack_heads(o2, NAT_DH).astype(BF16)


def _dense_attention(q, k_buf, v_buf, slot):
    n = q.shape[0]
    n_batch, _, s, _ = k_buf.shape
    nb = DENSE_ATTN_BATCHES
    blk = pl.BlockSpec((nb * s, D_MODEL), lambda b: (b, 0))
    kv = pl.BlockSpec((nb, None, s, D_MODEL), lambda b: (b, slot, 0, 0))
    return pl.pallas_call(
        _dense_attn_kernel,
        grid=(n_batch // nb,),
        in_specs=[blk, kv, kv],
        out_specs=blk,
        out_shape=jax.ShapeDtypeStruct((n, D_MODEL), BF16),
        compiler_params=_cparams("parallel"),
        name="dense_attention",
    )(q, k_buf, v_buf)


_NAT_QBLK = NAT_QROWS * GRID_W
_NAT_KBLK = NAT_KROWS * GRID_W


def _nat_block_plan(rows):
    assert rows % NAT_QROWS == 0 and rows >= NAT_KROWS and NAT_KROWS % 2 == 0
    plan, variants = [], []
    for r0 in range(0, rows, NAT_QROWS):
        ks = min(max(r0 - WIN_H // 2, 0), rows - NAT_KROWS)
        r = r0 + np.arange(NAT_QROWS)
        kr = ks + np.arange(NAT_KROWS)
        rs = np.clip(r - WIN_H // 2, 0, rows - WIN_H)
        valid_row = (kr[None, :] >= rs[:, None]) & (kr[None, :] < rs[:, None] + WIN_H)
        d0 = ks - r + (WIN_H - 1)
        for vi, (d0_v, valid_v) in enumerate(variants):
            if np.array_equal(d0, d0_v) and np.array_equal(valid_row, valid_v):
                break
        else:
            vi = len(variants)
            variants.append((d0, valid_row))
        plan.append((ks, vi))
    return plan, variants


def _nat_bias_row_range(variants):
    lo = min(int(d0.min()) for d0, _ in variants)
    hi = max(int(d0.max()) for d0, _ in variants) + NAT_KROWS
    return lo, hi


def _nat_bias_pairs(rpb, rows):
    n_heads, n_dr, n_dc = rpb.shape
    _, variants = _nat_block_plan(rows)
    lo, hi = _nat_bias_row_range(variants)
    c = np.arange(GRID_W)
    cs = np.clip(c - WIN_W // 2, 0, GRID_W - WIN_W)
    valid_col = (c[None, :] >= cs[:, None]) & (c[None, :] < cs[:, None] + WIN_W)
    d_col = c[None, :] - c[:, None] + (WIN_W - 1)
    sel = (d_col[None] == np.arange(n_dc)[:, None, None]) & valid_col[None]
    toep = jnp.einsum('hdj,jck->hdck', rpb, jnp.asarray(sel, F32), precision=lax.Precision.HIGHEST)
    toep = jnp.where(jnp.asarray(valid_col)[None, None], toep, NEG_INF)
    toep = jnp.pad(toep, ((0, 0), (max(-lo, 0), max(hi + 1 - n_dr, 0)), (0, 0), (0, 0)),
                   constant_values=NEG_INF)
    toep = toep[:, max(lo, 0):]
    return jnp.concatenate([toep[:, :-1], toep[:, 1:]], axis=-1)


def _nat_block_bias(tp_ref, head, d0, valid_row, row_lo):
    neg = jnp.full((GRID_W, 2 * GRID_W), NEG_INF, F32)
    left = lax.broadcasted_iota(jnp.int32, (1, 2 * GRID_W), 1) < GRID_W
    rows_out = []
    for dr in range(NAT_QROWS):
        pieces = []
        for a in range(0, NAT_KROWS, 2):
            ok0, ok1 = bool(valid_row[dr, a]), bool(valid_row[dr, a + 1])
            if not (ok0 or ok1):
                pieces.append(neg)
                continue
            piece = tp_ref[head, int(d0[dr]) + a - row_lo]
            if ok0 and not ok1:
                piece = jnp.where(left, piece, NEG_INF)
            elif ok1 and not ok0:
                piece = jnp.where(left, NEG_INF, piece)
            pieces.append(piece)
        rows_out.append(jnp.concatenate(pieces, axis=-1))
    return jnp.concatenate(rows_out, axis=0)


def _nat_attn_kernel(q_ref, k_ref, v_ref, kc_ref, vc_ref, tp_ref, o_ref, *, plan, variants, row_lo):
    kc = kc_ref[0].astype(BF16)
    vc = vc_ref[0].astype(BF16)
    assert NAT_SCALE == 2.0 ** round(np.log2(NAT_SCALE))
    masks = [_lane_mask(NAT_DH, i, BF16) * NAT_SCALE for i in range(2)]
    for bi, (ks, var) in enumerate(plan):
        q2 = _stack_heads(q_ref[bi * _NAT_QBLK:(bi + 1) * _NAT_QBLK, :], masks)
        k = k_ref[ks * GRID_W:ks * GRID_W + _NAT_KBLK, :]
        v = v_ref[ks * GRID_W:ks * GRID_W + _NAT_KBLK, :]
        bias2 = jnp.concatenate([_nat_block_bias(tp_ref, i, *variants[var], row_lo) for i in range(2)], axis=0)
        o2 = _softmax_pv([_dot_nt(q2, k) + bias2, _dot_nt(q2, kc)], [v, vc])
        o_ref[bi * _NAT_QBLK:(bi + 1) * _NAT_QBLK, :] = _unstack_heads(o2, NAT_DH).astype(BF16)


def _nat_attention(q, k, v, cache_k, cache_v, bias, n_batch):
    n = q.shape[0]
    s = n // n_batch
    plan, variants = _nat_block_plan(s // GRID_W)
    row_lo, _ = _nat_bias_row_range(variants)
    own = pl.BlockSpec((s, LANES), lambda p, b: (b, p))
    cache = pl.BlockSpec((1, cache_k.shape[1], LANES), lambda p, b: (b, 0, p))
    return pl.pallas_call(
        functools.partial(_nat_attn_kernel, plan=plan, variants=variants, row_lo=row_lo),
        grid=(HEAD_PAIRS, n_batch),
        in_specs=[own, own, own, cache, cache,
                  pl.BlockSpec((2,) + bias.shape[1:], lambda p, b: (p, 0, 0, 0))],
        out_specs=own,
        out_shape=jax.ShapeDtypeStruct((n, D_MODEL), BF16),
        compiler_params=_cparams("parallel", "parallel"),
        name="nat_attention",
    )(q, k, v, cache_k, cache_v, bias)


def _top2_sum(a, b, c, d):
    hi1, lo1 = jnp.maximum(a, b), jnp.minimum(a, b)
    hi2, lo2 = jnp.maximum(c, d), jnp.minimum(c, d)
    return jnp.maximum(hi1, hi2) + jnp.maximum(jnp.minimum(hi1, hi2), jnp.maximum(lo1, lo2))


def _route(scores, biased):
    sc = [scores[e:e + 1, :] for e in range(N_EXPERTS)]
    bs = [biased[e:e + 1, :] for e in range(N_EXPERTS)]
    gscore = [_top2_sum(*bs[EPG * g:EPG * (g + 1)]) for g in range(N_GROUPS)]
    best, gidx = gscore[0], jnp.zeros_like(gscore[0], dtype=jnp.int32)
    for g in range(1, N_GROUPS):
        better = gscore[g] > best
        gidx = jnp.where(better, g, gidx)
        best = jnp.where(better, gscore[g], best)
    cb = [functools.reduce(lambda a, b: a + b,
                           [jnp.where(gidx == g, bs[EPG * g + i], 0.0) for g in range(N_GROUPS)])
          for i in range(EPG)]
    cs = [functools.reduce(lambda a, b: a + b,
                           [jnp.where(gidx == g, sc[EPG * g + i], 0.0) for g in range(N_GROUPS)])
          for i in range(EPG)]
    b1, i1 = cb[0], jnp.zeros_like(gidx)
    for i in range(1, EPG):
        better = cb[i] > b1
        i1 = jnp.where(better, i, i1)
        b1 = jnp.where(better, cb[i], b1)
    b2, i2 = jnp.full_like(b1, -jnp.inf), jnp.full_like(i1, -1)
    for i in range(EPG):
        better = (i1 != i) & (cb[i] > b2)
        i2 = jnp.where(better, i, i2)
        b2 = jnp.where(better, cb[i], b2)
    sel = [(i1 == i) | (i2 == i) for i in range(EPG)]
    w = [jnp.where(sel[i], cs[i], 0.0) for i in range(EPG)]
    tot = w[0] + w[1] + w[2] + w[3]
    return gidx, [w[i] / tot for i in range(EPG)]


MOE_SRC = 256
MOE_DST = 512
MOE_UNSORT_TILES = 2
_LOG2_DST = MOE_DST.bit_length() - 1
assert MOE_DST == 1 << _LOG2_DST and MOE_SRC <= MOE_DST <= 2 * MOE_SRC
_INFO_GID = EPG


def _group_tiles(cum_ref):
    n_src = cum_ref.shape[0]
    tot = [cum_ref[n_src - 1, g] for g in range(N_GROUPS)]
    ends, acc = [], 0
    for t in tot:
        acc = acc + lax.shift_right_logical(t + (MOE_DST - 1), _LOG2_DST)
        ends.append(acc)
    return tot, ends


def _dest_tile(d, cum_ref):
    tot, ends = _group_tiles(cum_ref)
    g = ((d >= ends[0]).astype(jnp.int32) + (d >= ends[1]).astype(jnp.int32)
         + (d >= ends[2]).astype(jnp.int32))
    first = jnp.where(g == 0, 0, jnp.where(g == 1, ends[0], jnp.where(g == 2, ends[1], ends[2])))
    tot_g = jnp.where(g == 0, tot[0], jnp.where(g == 1, tot[1], jnp.where(g == 2, tot[2], tot[3])))
    k0 = (d - first) * MOE_DST
    n_valid = jnp.clip(tot_g - k0, 0, MOE_DST)
    return g, k0, n_valid


def _counts_before(cum_ref, s, g):
    return jnp.where(s > 0, cum_ref[jnp.maximum(s - 1, 0), g], 0)


MOE_ALIGN = 16
_LOG2_ALIGN = MOE_ALIGN.bit_length() - 1
_LOC_ROWS = MOE_SRC + N_GROUPS * MOE_ALIGN
_INFO_LRANK = EPG + 1
H_EXT = D_MODEL + LANES
_PIECES = [MOE_SRC >> i for i in range((MOE_SRC // MOE_ALIGN).bit_length())]


def _moe_sort_tile(s, x_ref, a_ref, wo_ref, mod_ref, g_ref, wr_ref, rb_ref, xo_ref, info_ref, cum_ref,
                   lsort, seg_s, base_s):
    x = x_ref[...] + mod_ref[0, 2:3, :] * _dot(a_ref[...], wo_ref[...])
    xo_ref[...] = x
    h = _rms(x, g_ref[...]) * (1.0 + mod_ref[0, 4:5, :]) + mod_ref[0, 3:4, :]
    h_hi = h.astype(BF16)
    h_lo = (h - h_hi.astype(F32)).astype(BF16)
    hi_w = _dot(h_hi, wr_ref[...])
    logits = hi_w[:, :LANES] + (_dot(h_lo, wr_ref[:, :LANES]) + hi_w[:, LANES:])
    scores = _sigmoid(logits)
    gidx, gates = _route(scores.T[:N_EXPERTS], (scores + rb_ref[...]).T[:N_EXPERTS])
    tm = h.shape[0]
    sub = lax.broadcasted_iota(jnp.int32, (8, 1), 0)
    onehot = (sub == gidx).astype(F32)
    tri = (lax.broadcasted_iota(jnp.int32, (tm, tm), 0)
           <= lax.broadcasted_iota(jnp.int32, (tm, tm), 1)).astype(BF16)
    prefix = _dot(onehot.astype(BF16), tri)
    lrank = jnp.sum((prefix - 1.0) * onehot, axis=0, keepdims=True)

    count = [jnp.sum(onehot[g:g + 1, :]).astype(jnp.int32) for g in range(N_GROUPS)]
    padded = [lax.shift_left(lax.shift_right_logical(c + (MOE_ALIGN - 1), _LOG2_ALIGN), _LOG2_ALIGN)
              for c in count]
    loff, acc = [], 0
    for p in padded:
        loff.append(acc)
        acc = acc + p
    lpos = lrank + functools.reduce(
        lambda a, b: a + b, [jnp.where(gidx == g, jnp.asarray(v, jnp.int32).astype(F32), 0.0)
                             for g, v in enumerate(loff)])
    infot = jnp.concatenate(gates + [gidx.astype(F32), lrank, jnp.zeros((LANES - EPG - 2, tm), F32)], axis=0)
    info_ref[...] = infot.T
    g_hi = [gt.astype(BF16).astype(F32) for gt in gates]
    g_lo = [gt - gh for gt, gh in zip(gates, g_hi)]
    gext = jnp.concatenate(g_hi + g_lo + [jnp.zeros((LANES - 2 * EPG, tm), F32)], axis=0)
    h_ext = jnp.concatenate([h_hi, gext.T.astype(BF16)], axis=-1)
    perm = (lax.broadcasted_iota(jnp.int32, (_LOC_ROWS, 1), 0).astype(F32) == lpos).astype(BF16)
    lsort[s] = _dot(perm, h_ext).astype(BF16)

    new_base = []
    for g in range(N_GROUPS):
        seg_s[s, g] = padded[g]
        new_base.append(base_s[g] + padded[g])
        base_s[g] = new_base[g]
    cum = functools.reduce(lambda a, b: a + b,
                           [jnp.where(sub == g, new_base[g], 0) for g in range(N_GROUPS)])
    cum_ref[0] = jnp.broadcast_to(cum, (8, LANES))


def _copy_pieces(length, make_copy, wait):
    done = jnp.int32(0)
    for rows in _PIECES:
        take = (length & rows) != 0

        @pl.when(take)
        def _():
            cp = make_copy(pl.multiple_of(done, MOE_ALIGN), rows)
            cp.wait() if wait else cp.start()
        done = done + jnp.where(take, rows, 0)


def _moe_sort_kernel(x_ref, a_ref, wo_ref, mod_ref, g_ref, wr_ref, rb_ref, xo_ref, info_ref, cum_ref, srt_ref,
                     lsort, zeros_v, seg_s, base_s, sem):
    s = pl.program_id(0)
    n_src = lsort.shape[0]
    n_rows = srt_ref.shape[0]

    @pl.when(s == 0)
    def _():
        for g in range(N_GROUPS):
            base_s[g] = 0
        zeros_v[...] = jnp.zeros_like(zeros_v)

    @pl.when(s < n_src)
    def _():
        _moe_sort_tile(s, x_ref, a_ref, wo_ref, mod_ref, g_ref, wr_ref, rb_ref, xo_ref, info_ref, cum_ref,
                       lsort, seg_s, base_s)

    @pl.when(s == n_src)
    def _():
        starts, acc = [], jnp.int32(0)
        for g in range(N_GROUPS):
            starts.append(acc)
            acc = acc + lax.shift_left(
                lax.shift_right_logical(base_s[g] + (MOE_DST - 1), _LOG2_DST), _LOG2_DST)
        end = acc

        def all_copies(wait):
            def tile_copies(t, dst):
                src = jnp.int32(0)
                new_dst = []
                for g in range(N_GROUPS):
                    n = seg_s[t, g]
                    _copy_pieces(n, lambda off, rows, src=src, g=g: pltpu.make_async_copy(
                        lsort.at[t, pl.ds(pl.multiple_of(src + off, MOE_ALIGN), rows)],
                        srt_ref.at[pl.ds(pl.multiple_of(dst[g] + off, MOE_ALIGN), rows)], sem.at[0]), wait)
                    src = src + n
                    new_dst.append(dst[g] + n)
                return tuple(new_dst)

            dst = lax.fori_loop(0, n_src, tile_copies, tuple(starts))
            for g in range(N_GROUPS):
                gap_end = starts[g + 1] if g + 1 < N_GROUPS else end
                _copy_pieces(gap_end - dst[g], lambda off, rows, g=g: pltpu.make_async_copy(
                    zeros_v.at[pl.ds(0, rows)],
                    srt_ref.at[pl.ds(pl.multiple_of(dst[g] + off, MOE_ALIGN), rows)], sem.at[0]), wait)

            def zero_tile(r, carry):
                cp = pltpu.make_async_copy(
                    zeros_v, srt_ref.at[pl.ds(pl.multiple_of(end + r * MOE_DST, MOE_DST), MOE_DST)], sem.at[0])
                cp.wait() if wait else cp.start()
                return carry

            lax.fori_loop(0, lax.shift_right_logical(n_rows - end, _LOG2_DST), zero_tile, 0)

        all_copies(False)
        all_copies(True)


def _moe_ffn2_kernel(cum_ref, xs_ref, wg_ref, wu_ref, wd_ref, y_ref):
    _, ends = _group_tiles(cum_ref)

    @pl.when(pl.program_id(0) >= ends[-1])
    def _():
        y_ref[...] = jnp.zeros_like(y_ref)

    @pl.when(pl.program_id(0) < ends[-1])
    def _():
        xs = xs_ref[:, :D_MODEL]
        gext = xs_ref[:, D_MODEL:].astype(F32)
        acts = []
        for i in range(EPG):
            a = _dot(xs, wg_ref[i])
            u = _dot(xs, wu_ref[i])
            gate = gext[:, i:i + 1] + gext[:, EPG + i:EPG + i + 1]
            acts.append(((a * _sigmoid(a)) * u * gate).astype(BF16))
        act = jnp.concatenate(acts, axis=-1)
        y_ref[...] = _dot(act, wd_ref[...].reshape(EPG * D_EXPERT, D_MODEL)).astype(BF16)


def _segment_row(cum_ref, s, g):
    _, ends = _group_tiles(cum_ref)
    first_tile = ends[g - 1] if g > 0 else 0
    return first_tile * MOE_DST + _counts_before(cum_ref, s, g)


def _unsort_window(cum_ref, s, g, n_rows):
    return pl.multiple_of(jnp.minimum(_segment_row(cum_ref, s, g), n_rows - MOE_SRC), MOE_ALIGN)


def _moe_unsort2_kernel(*refs, final_norm):
    if final_norm:
        cum_ref, x_ref, mod_ref, info_ref, y_ref, gf_ref, o_ref = refs
    else:
        cum_ref, x_ref, mod_ref, info_ref, y_ref, o_ref = refs
    n_rows = y_ref.shape[0]
    col = lax.broadcasted_iota(jnp.int32, (1, MOE_SRC), 1).astype(F32)
    for j in range(MOE_UNSORT_TILES):
        s = pl.program_id(0) * MOE_UNSORT_TILES + j
        rows = slice(j * MOE_SRC, (j + 1) * MOE_SRC)
        inf = info_ref[rows, :]
        gid = inf[:, _INFO_GID:_INFO_GID + 1]
        lrank = inf[:, _INFO_LRANK:_INFO_LRANK + 1]
        perms, wins = [], []
        for g in range(N_GROUPS):
            win = _unsort_window(cum_ref, s, g, n_rows)
            shift = _segment_row(cum_ref, s, g) - win
            rel = jnp.where(gid == float(g), lrank + shift.astype(F32), -1.0)
            perms.append((rel == col).astype(BF16))
            wins.append(y_ref[pl.ds(win, MOE_SRC), :])
        out = x_ref[rows, :] + mod_ref[0, 5:6, :] * _dot(jnp.concatenate(perms, axis=-1),
                                                         jnp.concatenate(wins, axis=0))
        o_ref[rows, :] = _rms(out, gf_ref[...]) if final_norm else out


def _mixer_out_and_moe(x, att, w_o, mod, g, w_router, router_bias, wg, wu, wd, layer, rows_per_cond,
                       final_g=None):
    n = x.shape[0]
    n_src = n // MOE_SRC
    n_dst = (n + N_GROUPS * n_src * (MOE_ALIGN - 1)) // MOE_DST + N_GROUPS
    n_rows = n_dst * MOE_DST
    last = n_src - 1
    row = lambda i: (jnp.minimum(i, last), 0)
    full = lambda i: (0, 0)
    x, info, cum, srt = pl.pallas_call(
        _moe_sort_kernel,
        grid=(n_src + 1,),
        in_specs=[
            pl.BlockSpec((MOE_SRC, D_MODEL), row),
            pl.BlockSpec((MOE_SRC, D_MODEL), row),
            pl.BlockSpec(w_o.shape, full),
            pl.BlockSpec((1, 6, D_MODEL), (lambda i: (0, 0, 0)) if rows_per_cond is None else
                         (lambda i: (1 + jnp.minimum(i, last) // (rows_per_cond // MOE_SRC), 0, 0))),
            pl.BlockSpec((1, D_MODEL), full),
            pl.BlockSpec(w_router.shape, full),
            pl.BlockSpec(router_bias.shape, full),
        ],
        out_specs=[pl.BlockSpec((MOE_SRC, D_MODEL), row),
                   pl.BlockSpec((MOE_SRC, LANES), row),
                   pl.BlockSpec((1, 8, LANES), lambda i: (jnp.minimum(i, last), 0, 0)),
                   pl.BlockSpec(memory_space=pl.ANY)],
        out_shape=[jax.ShapeDtypeStruct((n, D_MODEL), F32),
                   jax.ShapeDtypeStruct((n, LANES), F32),
                   jax.ShapeDtypeStruct((n_src, 8, LANES), jnp.int32),
                   jax.ShapeDtypeStruct((n_rows, H_EXT), BF16)],
        scratch_shapes=[pltpu.VMEM((n_src, _LOC_ROWS, H_EXT), BF16),
                        pltpu.VMEM((MOE_DST, H_EXT), BF16),
                        pltpu.SMEM((n_src, N_GROUPS), jnp.int32),
                        pltpu.SMEM((N_GROUPS,), jnp.int32),
                        pltpu.SemaphoreType.DMA((1,))],
        compiler_params=_cparams("arbitrary"),
        name="moe_sort",
    )(x, att, w_o, mod, g, w_router, router_bias)
    cum = cum[:, :N_GROUPS, 0]

    blk = lambda width: pl.BlockSpec((MOE_DST, width), lambda d, c: (d, 0))
    expert = lambda shape: pl.BlockSpec(shape, lambda d, c: (layer, _dest_tile(d, c)[0], 0, 0))
    y = pl.pallas_call(
        _moe_ffn2_kernel,
        grid_spec=pltpu.PrefetchScalarGridSpec(
            num_scalar_prefetch=1,
            grid=(n_dst,),
            in_specs=[blk(H_EXT),
                      expert((None, EPG, D_MODEL, D_EXPERT)),
                      expert((None, EPG, D_MODEL, D_EXPERT)),
                      expert((None, EPG, D_EXPERT, D_MODEL))],
            out_specs=blk(D_MODEL),
        ),
        out_shape=jax.ShapeDtypeStruct((n_rows, D_MODEL), BF16),
        compiler_params=_cparams("arbitrary"),
        name="moe_experts",
    )(cum, srt, wg, wu, wd)

    tu = MOE_UNSORT_TILES * MOE_SRC
    in_specs = [pl.BlockSpec((tu, D_MODEL), lambda i, c: (i, 0)),
                _mod_spec(None if rows_per_cond is None else rows_per_cond // tu),
                pl.BlockSpec((tu, LANES), lambda i, c: (i, 0)),
                pl.BlockSpec(y.shape, lambda i, c: (0, 0), pipeline_mode=pl.Buffered(1))]
    args = [cum, x, mod, info, y]
    if final_g is not None:
        in_specs.append(pl.BlockSpec((1, D_MODEL), lambda i, c: (0, 0)))
        args.append(final_g)
    return pl.pallas_call(
        functools.partial(_moe_unsort2_kernel, final_norm=final_g is not None),
        grid_spec=pltpu.PrefetchScalarGridSpec(
            num_scalar_prefetch=1,
            grid=(n_src // MOE_UNSORT_TILES,),
            in_specs=in_specs,
            out_specs=pl.BlockSpec((tu, D_MODEL), lambda i, c: (i, 0)),
        ),
        out_shape=jax.ShapeDtypeStruct((n, D_MODEL), F32),
        compiler_params=_cparams("parallel"),
        name="moe_unsort",
    )(*args)


def _rope_tables(n_lat):
    t = np.arange(n_lat)
    n_freq = MLA_ROPE // 4
    inv_freq = jnp.asarray(ROPE_THETA, F32) ** (-jnp.arange(n_freq, dtype=F32) / n_freq)
    ar = jnp.asarray(t // GRID_W, F32)[:, None] * inv_freq
    ac = jnp.asarray(t % GRID_W, F32)[:, None] * inv_freq
    cos = jnp.concatenate([jnp.cos(ar), jnp.cos(ar), jnp.cos(ac), jnp.cos(ac)], axis=-1)
    sin = jnp.concatenate([-jnp.sin(ar), jnp.sin(ar), -jnp.sin(ac), jnp.sin(ac)], axis=-1)
    return (jnp.tile(cos, (1, MLA_HEADS)), jnp.tile(sin, (1, MLA_HEADS)),
            jnp.tile(cos, (1, LANES // MLA_ROPE)), jnp.tile(sin, (1, LANES // MLA_ROPE)))


_ROPE_SWAP = np.concatenate([np.arange(8, 16), np.arange(0, 8), np.arange(24, 32), np.arange(16, 24)])


def _mla_weights(w_in, w_uq, w_ukv):
    o = MLA_Q_LORA + MLA_KV_LORA
    kr = w_in[:, o:]
    rep = LANES // MLA_ROPE
    w_in_x = jnp.concatenate([w_in[:, :o], jnp.tile(kr, (1, rep)), jnp.tile(kr[:, _ROPE_SWAP], (1, rep))],
                             axis=-1).astype(BF16)
    uq = w_uq.reshape(MLA_Q_LORA, MLA_HEADS, MLA_NOPE + MLA_ROPE)
    q_nope = uq[:, :, :MLA_NOPE].reshape(MLA_Q_LORA, -1)
    q_rope = uq[:, :, MLA_NOPE:]
    w_uq_x = jnp.concatenate([q_nope, q_rope.reshape(MLA_Q_LORA, -1),
                              q_rope[:, :, _ROPE_SWAP].reshape(MLA_Q_LORA, -1)], axis=-1).astype(BF16)
    ukv = w_ukv.reshape(MLA_KV_LORA, MLA_HEADS, MLA_NOPE + MLA_V)
    w_uk = ukv[:, :, :MLA_NOPE].reshape(MLA_KV_LORA, -1).astype(BF16)
    w_uv = ukv[:, :, MLA_NOPE:].reshape(MLA_KV_LORA, -1).astype(BF16)
    return w_in_x, w_uq_x, w_uk, w_uv


def kernel(x_prompt, x_sample, cache_mla_ckv, cache_mla_krope, cache_nat_k, cache_nat_v, c, c_ctx,
           w_ada, b_ada, norm_mix, norm_ffn, norm_final, mla_w_in, mla_q_norm, mla_w_uq, mla_kv_norm,
           mla_w_ukv, mla_w_o, nat_w_qkv, nat_rpb, nat_w_o, w_router, router_bias,
           moe_w_gate, moe_w_up, moe_w_down):
    B, S, D = x_prompt.shape
    Bd, Sd, _ = x_sample.shape
    assert D == D_MODEL and Bd + 1 <= 8
    tm_c, tm_s = 512, 512

    xc = x_prompt.reshape(B * S, D)
    xs = x_sample.reshape(Bd * Sd, D)
    cond8 = jnp.concatenate([c_ctx[None, :], c, jnp.zeros((8 - 1 - Bd, D), F32)], axis=0)
    mod_all = _ada_modulation(cond8, w_ada, b_ada).reshape(DEPTH, 8, 6, D)

    rope_tabs = _rope_tables(Sd)
    wr = jnp.pad(w_router, ((0, 0), (0, LANES - N_EXPERTS)))
    wr_hi = wr.astype(BF16)
    wr = jnp.concatenate([wr_hi, (wr - wr_hi.astype(F32)).astype(BF16)], axis=1)
    rb = jnp.pad(router_bias, (0, LANES - N_EXPERTS)).reshape(1, LANES)
    wg = moe_w_gate.astype(BF16)
    wu = moe_w_up.astype(BF16)
    wd = moe_w_down.astype(BF16)

    new_ckv, new_krope = [], []
    k_buf = v_buf = None
    for layer in range(DEPTH):
        mod = mod_all[layer]
        g_mix = norm_mix[layer][None, :]
        j = layer // 2
        if layer % 2 == 0:
            w_in_x, w_uq_x, w_uk, w_uv = _mla_weights(mla_w_in[j], mla_w_uq[j], mla_w_ukv[j])
            w_in_c = w_in_x[:, :MLA_Q_LORA + MLA_KV_LORA + LANES]
            w_uq_c = w_uq_x[:, :MLA_HEADS * (MLA_NOPE + MLA_ROPE)]
            qg = mla_q_norm[j][None, :]
            kvg = mla_kv_norm[j][None, :]
            w_o = mla_w_o[j].astype(BF16)
            qn, qr, ckv, kr, kr4 = _premix_mla(xc, mod, g_mix, w_in_c, qg, w_uq_c, kvg, None, None, tm_c)
            new_ckv.append(ckv.reshape(B, S, MLA_KV_LORA))
            new_krope.append(kr.reshape(B, S, MLA_ROPE))
            att_c = _mla_attention(qn, qr, ckv, kr4, None, None, w_uk, w_uv, B, S)
            qn, qr, ckv, kr, kr4 = _premix_mla(xs, mod, g_mix, w_in_x, qg, w_uq_x, kvg, rope_tabs, Sd, tm_s)
            cache_kr4 = jnp.tile(cache_mla_krope[:, j], (1, 1, LANES // MLA_ROPE))
            att_s = _mla_attention(qn, qr, ckv, kr4, cache_mla_ckv[:, j], cache_kr4, w_uk, w_uv, Bd, 256)
        else:
            w_qkv = nat_w_qkv[j].astype(BF16)
            w_o = nat_w_o[j].astype(BF16)
            q, k_buf, v_buf = _premix_nat_cache(xc, mod, g_mix, w_qkv, k_buf, v_buf, j, DEPTH // 2, B, tm_c)
            att_c = _dense_attention(q, k_buf, v_buf, j)
            q, k, v = _premix_nat(xs, mod, g_mix, w_qkv, Sd, tm_s, BF16)
            bias = _nat_bias_pairs(nat_rpb[j], Sd // GRID_W)
            att_s = _nat_attention(q, k, v,
                                   cache_nat_k[:, j].reshape(Bd, -1, D), cache_nat_v[:, j].reshape(Bd, -1, D),
                                   bias, Bd)
        g_ffn = norm_ffn[layer][None, :]
        final_g = norm_final[None, :] if layer == DEPTH - 1 else None
        xc = _mixer_out_and_moe(xc, att_c, w_o, mod, g_ffn, wr, rb, wg, wu, wd, layer, None, final_g)
        xs = _mixer_out_and_moe(xs, att_s, w_o, mod, g_ffn, wr, rb, wg, wu, wd, layer, Sd, final_g)

    y_prompt = xc.reshape(B, S, D)
    y_sample = xs.reshape(Bd, Sd, D)
    kv_shape = (B, DEPTH // 2, S, NAT_HEADS, NAT_DH)
    return (y_prompt, y_sample, jnp.stack(new_ckv, axis=1), jnp.stack(new_krope, axis=1),
            k_buf.reshape(kv_shape), v_buf.reshape(kv_shape))
```

```python
import functools

import numpy as np
import jax
import jax.numpy as jnp
from jax import lax
from jax.experimental import pallas as pl
from jax.experimental.pallas import tpu as pltpu

F32 = jnp.float32
BF16 = jnp.bfloat16

D_MODEL = 1024
DEPTH = 4
GRID_W = 64
LANES = 128
MLA_HEADS = 16
MLA_NOPE = 64
MLA_ROPE = 32
MLA_V = 64
MLA_Q_LORA = 384
MLA_KV_LORA = 256
MLA_SCALE = (MLA_NOPE + MLA_ROPE) ** -0.5
ROPE_THETA = 10000.0
NAT_HEADS = 16
NAT_DH = 64
NAT_SCALE = NAT_DH ** -0.5
WIN_H = 8
WIN_W = 16
NAT_QROWS = 4
NAT_KROWS = NAT_QROWS + WIN_H
N_EXPERTS = 16
N_GROUPS = 4
EPG = N_EXPERTS // N_GROUPS
D_EXPERT = 256
NORM_EPS = 1e-6
NEG_INF = -1e30

HEAD_PAIRS = MLA_HEADS // 2
VMEM_LIMIT = 56 * 1024 * 1024


def _cparams(*sem):
    return pltpu.CompilerParams(dimension_semantics=sem, vmem_limit_bytes=VMEM_LIMIT)


def _sigmoid(x):
    return 1.0 / (1.0 + jnp.exp(-x))


def _rms(x, g):
    ms = jnp.mean(x * x, axis=-1, keepdims=True)
    return x * lax.rsqrt(ms + NORM_EPS) * g


def _dot(a, b):
    return jnp.dot(a, b, preferred_element_type=F32)


def _dot_nt(a, b):
    return lax.dot_general(a, b, (((1,), (1,)), ((), ())), preferred_element_type=F32)


def _lane_mask(width, idx, dtype):
    lane = lax.broadcasted_iota(jnp.int32, (1, LANES), 1)
    return ((lane >= idx * width) & (lane < (idx + 1) * width)).astype(dtype)


def _ada_kernel(cond_ref, w_ref, b_ref, o_ref):
    c = cond_ref[...]
    s = (c * _sigmoid(c)).astype(BF16)
    o_ref[0] = _dot(s, w_ref[0].astype(BF16)) + b_ref[0]


def _ada_modulation(cond8, w_ada, b_ada):
    n_chunk = 6
    return pl.pallas_call(
        _ada_kernel,
        grid=(DEPTH, n_chunk),
        in_specs=[
            pl.BlockSpec((8, D_MODEL), lambda l, j: (0, 0)),
            pl.BlockSpec((1, D_MODEL, D_MODEL), lambda l, j: (l, 0, j)),
            pl.BlockSpec((1, 1, D_MODEL), lambda l, j: (l, 0, j)),
        ],
        out_specs=pl.BlockSpec((1, 8, D_MODEL), lambda l, j: (l, 0, j)),
        out_shape=jax.ShapeDtypeStruct((DEPTH, 8, 6 * D_MODEL), F32),
        compiler_params=_cparams("parallel", "parallel"),
        name="ada_modulation",
    )(cond8, w_ada, b_ada.reshape(DEPTH, 1, 6 * D_MODEL))


def _mod_spec(rows_per_cond):
    if rows_per_cond is None:
        return pl.BlockSpec((1, 6, D_MODEL), lambda i, *_: (0, 0, 0))
    return pl.BlockSpec((1, 6, D_MODEL), lambda i, *_: (1 + i // rows_per_cond, 0, 0))


def _premix_mla_kernel(*refs, rope):
    if rope:
        (x_ref, mod_ref, g_ref, w_in_ref, qg_ref, w_uq_ref, kvg_ref,
         cosq_ref, sinq_ref, cosk_ref, sink_ref,
         qn_ref, qr_ref, ckv_ref, kr_ref, kr4_ref) = refs
    else:
        (x_ref, mod_ref, g_ref, w_in_ref, qg_ref, w_uq_ref, kvg_ref,
         qn_ref, qr_ref, ckv_ref, kr_ref, kr4_ref) = refs
    h = _rms(x_ref[...], g_ref[...]) * (1.0 + mod_ref[0, 1:2, :]) + mod_ref[0, 0:1, :]
    lat = _dot(h.astype(BF16), w_in_ref[...])
    c_q = lat[:, :MLA_Q_LORA]
    c_kv = lat[:, MLA_Q_LORA:MLA_Q_LORA + MLA_KV_LORA]
    o = MLA_Q_LORA + MLA_KV_LORA
    kr4 = lat[:, o:o + LANES]
    q = _dot(_rms(c_q, qg_ref[...]).astype(BF16), w_uq_ref[...])
    n_nope = MLA_HEADS * MLA_NOPE
    n_rope = MLA_HEADS * MLA_ROPE
    qr = q[:, n_nope:n_nope + n_rope]
    if rope:
        qr = qr * cosq_ref[...] + q[:, n_nope + n_rope:] * sinq_ref[...]
        kr4 = kr4 * cosk_ref[...] + lat[:, o + LANES:o + 2 * LANES] * sink_ref[...]
    qn_ref[...] = q[:, :n_nope].astype(BF16)
    qr_ref[...] = qr.astype(BF16)
    ckv_ref[...] = _rms(c_kv, kvg_ref[...])
    kr_ref[...] = kr4[:, :MLA_ROPE]
    kr4_ref[...] = kr4.astype(BF16)


def _premix_mla(x, mod, g, w_in, qg, w_uq, kvg, rope_tabs, rows_per_cond, tm):
    n = x.shape[0]
    rope = rope_tabs is not None
    row = lambda i: (i, 0)
    full = lambda i: (0, 0)
    in_specs = [
        pl.BlockSpec((tm, D_MODEL), row),
        _mod_spec(None if rows_per_cond is None else rows_per_cond // tm),
        pl.BlockSpec((1, D_MODEL), full),
        pl.BlockSpec(w_in.shape, full),
        pl.BlockSpec((1, MLA_Q_LORA), full),
        pl.BlockSpec(w_uq.shape, full),
        pl.BlockSpec((1, MLA_KV_LORA), full),
    ]
    args = [x, mod, g, w_in, qg, w_uq, kvg]
    if rope:
        nblk = rope_tabs[0].shape[0] // tm
        pos = lambda i: (i % nblk, 0)
        for t in rope_tabs:
            in_specs.append(pl.BlockSpec((tm, t.shape[1]), pos))
            args.append(t)
    widths = (MLA_HEADS * MLA_NOPE, MLA_HEADS * MLA_ROPE, MLA_KV_LORA, MLA_ROPE, LANES)
    dtypes = (BF16, BF16, F32, F32, BF16)
    return pl.pallas_call(
        functools.partial(_premix_mla_kernel, rope=rope),
        grid=(n // tm,),
        in_specs=in_specs,
        out_specs=[pl.BlockSpec((tm, w), row) for w in widths],
        out_shape=[jax.ShapeDtypeStruct((n, w), d) for w, d in zip(widths, dtypes)],
        compiler_params=_cparams("parallel"),
        name="premix_mla_rope" if rope else "premix_mla",
    )(*args)


LOG2E = 1.4426950408889634


def _softmax_pv(s_list, v_list, scale=1.0):
    m = functools.reduce(jnp.maximum, [jnp.max(s, axis=-1, keepdims=True) for s in s_list])
    e_list = [jnp.exp2((s - m) * (scale * LOG2E)) for s in s_list]
    l = functools.reduce(lambda a, b: a + b, [jnp.sum(e, axis=-1, keepdims=True) for e in e_list])
    o = functools.reduce(lambda a, b: a + b,
                         [_dot(e.astype(BF16), v) for e, v in zip(e_list, v_list)])
    return o / l


def _stack_heads(q, masks):
    return jnp.concatenate([q * m for m in masks], axis=0)


def _unstack_heads(o2, width):
    tq = o2.shape[0] // 2
    lane = lax.broadcasted_iota(jnp.int32, (1, LANES), 1)
    return jnp.where(lane < width, o2[:tq], o2[tq:])


MLA_ATTN_BATCHES = 2


def _mla_attn_kernel(*refs, cached, nb):
    if cached:
        (qn_ref, qr_ref, ckv_ref, kr4_ref, cckv_ref, ckr4_ref, w_uk_ref, w_uv_ref,
         o_ref, kn_s, v_s, kr_s) = refs
    else:
        (qn_ref, qr_ref, ckv_ref, kr4_ref, w_uk_ref, w_uv_ref, o_ref, kn_s, v_s, kr_s) = refs
    t_own = ckv_ref.shape[0] // nb

    for b in range(nb):
        own = slice(b * t_own, (b + 1) * t_own)

        def fill_kv():
            c = ckv_ref[own, :].astype(BF16)
            kn_s[0:t_own, :] = _dot(c, w_uk_ref[...]).astype(BF16)
            v_s[0:t_own, :] = _dot(c, w_uv_ref[...]).astype(BF16)
            kr_s[0:t_own, :] = kr4_ref[own, :]
            if cached:
                cc = cckv_ref[0].astype(BF16)
                kn_s[t_own:, :] = _dot(cc, w_uk_ref[...]).astype(BF16)
                v_s[t_own:, :] = _dot(cc, w_uv_ref[...]).astype(BF16)
                kr_s[t_own:, :] = ckr4_ref[0].astype(BF16)

        if cached:
            pl.when(pl.program_id(1) == 0)(fill_kv)
            qrows = slice(None)
        else:
            fill_kv()
            qrows = own

        kr4 = kr_s[...]
        for p in range(HEAD_PAIRS):
            sl = slice(p * LANES, (p + 1) * LANES)
            qn = qn_ref[qrows, sl]
            qr = qr_ref[qrows, (p // 2) * LANES:(p // 2 + 1) * LANES]
            k_cat = jnp.concatenate([kn_s[:, sl], kr4], axis=-1)
            q_cat = jnp.concatenate(
                [_stack_heads(qn, [_lane_mask(MLA_NOPE, i, BF16) for i in range(2)]),
                 _stack_heads(qr, [_lane_mask(MLA_ROPE, (2 * p + i) % 4, BF16) for i in range(2)])], axis=-1)
            o2 = _softmax_pv([_dot_nt(q_cat, k_cat)], [v_s[:, sl]], MLA_SCALE)
            o_ref[qrows, sl] = _unstack_heads(o2, MLA_V).astype(BF16)


def _mla_attention(qn, qr, ckv, kr4, cache_ckv, cache_kr4, w_uk, w_uv, n_batch, tq):
    n = qn.shape[0]
    s_own = n // n_batch
    nq = s_own // tq
    cached = cache_ckv is not None
    t_all = s_own + (cache_ckv.shape[1] if cached else 0)
    nb = 1 if cached else MLA_ATTN_BATCHES
    assert cached or nq == 1
    tq = tq * nb
    qrow = lambda b, j: (b * nq + j, 0)
    own = lambda b, j: (b, 0)
    full = lambda b, j: (0, 0)
    in_specs = [
        pl.BlockSpec((tq, qn.shape[1]), qrow),
        pl.BlockSpec((tq, qr.shape[1]), qrow),
        pl.BlockSpec((nb * s_own, MLA_KV_LORA), own),
        pl.BlockSpec((nb * s_own, LANES), own),
    ]
    args = [qn, qr, ckv, kr4]
    if cached:
        in_specs += [pl.BlockSpec((1,) + cache_ckv.shape[1:], lambda b, j: (b, 0, 0)),
                     pl.BlockSpec((1,) + cache_kr4.shape[1:], lambda b, j: (b, 0, 0))]
        args += [cache_ckv, cache_kr4]
    in_specs += [pl.BlockSpec(w_uk.shape, full), pl.BlockSpec(w_uv.shape, full)]
    args += [w_uk, w_uv]
    return pl.pallas_call(
        functools.partial(_mla_attn_kernel, cached=cached, nb=nb),
        grid=(n_batch // nb, nq),
        in_specs=in_specs,
        out_specs=pl.BlockSpec((tq, D_MODEL), qrow),
        out_shape=jax.ShapeDtypeStruct((n, D_MODEL), BF16),
        scratch_shapes=[pltpu.VMEM((t_all, D_MODEL), BF16), pltpu.VMEM((t_all, D_MODEL), BF16),
                        pltpu.VMEM((t_all, LANES), BF16)],
        compiler_params=_cparams("parallel", "arbitrary"),
        name="mla_attention_cached" if cached else "mla_attention",
    )(*args)


def _premix_nat_kernel(x_ref, mod_ref, g_ref, w_ref, q_ref, k_ref, v_ref):
    h = _rms(x_ref[...], g_ref[...]) * (1.0 + mod_ref[0, 1:2, :]) + mod_ref[0, 0:1, :]
    qkv = _dot(h.astype(BF16), w_ref[...])
    q_ref[...] = qkv[:, :D_MODEL].astype(q_ref.dtype)
    k_ref[...] = qkv[:, D_MODEL:2 * D_MODEL].astype(k_ref.dtype)
    v_ref[...] = qkv[:, 2 * D_MODEL:].astype(v_ref.dtype)


def _premix_nat(x, mod, g, w_qkv, rows_per_cond, tm, kv_dtype):
    n = x.shape[0]
    row = lambda i: (i, 0)
    full = lambda i: (0, 0)
    return pl.pallas_call(
        _premix_nat_kernel,
        grid=(n // tm,),
        in_specs=[
            pl.BlockSpec((tm, D_MODEL), row),
            _mod_spec(None if rows_per_cond is None else rows_per_cond // tm),
            pl.BlockSpec((1, D_MODEL), full),
            pl.BlockSpec(w_qkv.shape, full),
        ],
        out_specs=[pl.BlockSpec((tm, D_MODEL), row)] * 3,
        out_shape=[jax.ShapeDtypeStruct((n, D_MODEL), BF16),
                   jax.ShapeDtypeStruct((n, D_MODEL), kv_dtype),
                   jax.ShapeDtypeStruct((n, D_MODEL), kv_dtype)],
        compiler_params=_cparams("parallel"),
        name="premix_nat",
    )(x, mod, g, w_qkv)


def _premix_nat_cache_kernel(*refs, first):
    if first:
        x_ref, mod_ref, g_ref, w_ref, q_ref, k_ref, v_ref = refs
    else:
        x_ref, mod_ref, g_ref, w_ref, _, _, q_ref, k_ref, v_ref = refs
    h = _rms(x_ref[...], g_ref[...]) * (1.0 + mod_ref[0, 1:2, :]) + mod_ref[0, 0:1, :]
    qkv = _dot(h.astype(BF16), w_ref[...])
    q_ref[...] = qkv[:, :D_MODEL].astype(BF16)
    nb, n_slots, s, _ = k_ref.shape
    k_ref[:, 0] = qkv[:, D_MODEL:2 * D_MODEL].reshape(nb, s, D_MODEL)
    v_ref[:, 0] = qkv[:, 2 * D_MODEL:].reshape(nb, s, D_MODEL)
    for slot in range(1, n_slots):
        k_ref[:, slot] = jnp.zeros((nb, s, D_MODEL), F32)
        v_ref[:, slot] = jnp.zeros((nb, s, D_MODEL), F32)


def _premix_nat_cache(x, mod, g, w_qkv, k_buf, v_buf, slot, n_slots, n_batch, tm):
    n = x.shape[0]
    s = n // n_batch
    nb = tm // s
    first = k_buf is None
    row = lambda i: (i, 0)
    full = lambda i: (0, 0)
    in_specs = [pl.BlockSpec((tm, D_MODEL), row), _mod_spec(None), pl.BlockSpec((1, D_MODEL), full),
                pl.BlockSpec(w_qkv.shape, full)]
    args = [x, mod, g, w_qkv]
    if first:
        assert slot == 0
        kv_spec = pl.BlockSpec((nb, n_slots, s, D_MODEL), lambda i: (i, 0, 0, 0))
        aliases = {}
    else:
        in_specs += [pl.BlockSpec(memory_space=pl.ANY)] * 2
        args += [k_buf, v_buf]
        kv_spec = pl.BlockSpec((nb, 1, s, D_MODEL), lambda i: (i, slot, 0, 0))
        aliases = {4: 1, 5: 2}
    kv_shape = jax.ShapeDtypeStruct((n_batch, n_slots, s, D_MODEL), F32)
    return pl.pallas_call(
        functools.partial(_premix_nat_cache_kernel, first=first),
        grid=(n // tm,),
        in_specs=in_specs,
        out_specs=[pl.BlockSpec((tm, D_MODEL), row), kv_spec, kv_spec],
        out_shape=[jax.ShapeDtypeStruct((n, D_MODEL), BF16), kv_shape, kv_shape],
        input_output_aliases=aliases,
        compiler_params=_cparams("parallel"),
        name="premix_nat_cache",
    )(*args)


DENSE_ATTN_BATCHES = 4


def _dense_attn_kernel(q_ref, k_ref, v_ref, o_ref):
    nb, s, _ = k_ref.shape
    for b in range(nb):
        rows = slice(b * s, (b + 1) * s)
        for p in range(HEAD_PAIRS):
            sl = slice(p * LANES, (p + 1) * LANES)
            q2 = _stack_heads(q_ref[rows, sl], [_lane_mask(NAT_DH, i, BF16) for i in range(2)])
            k = k_ref[b, :, sl].astype(BF16)
            v = v_ref[b, :, sl].astype(BF16)
            o2 = _softmax_pv([_dot_nt(q2, k)], [v], NAT_SCALE)
            o_ref[rows, sl] = _unstack_heads(o2, NAT_DH).astype(BF16)


def _dense_attention(q, k_buf, v_buf, slot):
    n = q.shape[0]
    n_batch, _, s, _ = k_buf.shape
    nb = DENSE_ATTN_BATCHES
    blk = pl.BlockSpec((nb * s, D_MODEL), lambda b: (b, 0))
    kv = pl.BlockSpec((nb, None, s, D_MODEL), lambda b: (b, slot, 0, 0))
    return pl.pallas_call(
        _dense_attn_kernel,
        grid=(n_batch // nb,),
        in_specs=[blk, kv, kv],
        out_specs=blk,
        out_shape=jax.ShapeDtypeStruct((n, D_MODEL), BF16),
        compiler_params=_cparams("parallel"),
        name="dense_attention",
    )(q, k_buf, v_buf)


_NAT_QBLK = NAT_QROWS * GRID_W
_NAT_KBLK = NAT_KROWS * GRID_W


def _nat_block_plan(rows):
    assert rows % NAT_QROWS == 0 and rows >= NAT_KROWS and NAT_KROWS % 2 == 0
    plan, variants = [], []
    for r0 in range(0, rows, NAT_QROWS):
        ks = min(max(r0 - WIN_H // 2, 0), rows - NAT_KROWS)
        r = r0 + np.arange(NAT_QROWS)
        kr = ks + np.arange(NAT_KROWS)
        rs = np.clip(r - WIN_H // 2, 0, rows - WIN_H)
        valid_row = (kr[None, :] >= rs[:, None]) & (kr[None, :] < rs[:, None] + WIN_H)
        d0 = ks - r + (WIN_H - 1)
        for vi, (d0_v, valid_v) in enumerate(variants):
            if np.array_equal(d0, d0_v) and np.array_equal(valid_row, valid_v):
                break
        else:
            vi = len(variants)
            variants.append((d0, valid_row))
        plan.append((ks, vi))
    return plan, variants


def _nat_bias_row_range(variants):
    lo = min(int(d0.min()) for d0, _ in variants)
    hi = max(int(d0.max()) for d0, _ in variants) + NAT_KROWS
    return lo, hi


def _nat_bias_pairs(rpb, rows):
    n_heads, n_dr, n_dc = rpb.shape
    _, variants = _nat_block_plan(rows)
    lo, hi = _nat_bias_row_range(variants)
    c = np.arange(GRID_W)
    cs = np.clip(c - WIN_W // 2, 0, GRID_W - WIN_W)
    valid_col = (c[None, :] >= cs[:, None]) & (c[None, :] < cs[:, None] + WIN_W)
    d_col = c[None, :] - c[:, None] + (WIN_W - 1)
    sel = (d_col[None] == np.arange(n_dc)[:, None, None]) & valid_col[None]
    toep = jnp.einsum('hdj,jck->hdck', rpb, jnp.asarray(sel, F32), precision=lax.Precision.HIGHEST)
    toep = jnp.where(jnp.asarray(valid_col)[None, None], toep, NEG_INF)
    toep = jnp.pad(toep, ((0, 0), (max(-lo, 0), max(hi + 1 - n_dr, 0)), (0, 0), (0, 0)),
                   constant_values=NEG_INF)
    toep = toep[:, max(lo, 0):]
    return jnp.concatenate([toep[:, :-1], toep[:, 1:]], axis=-1)


def _nat_block_bias(tp_ref, head, d0, valid_row, row_lo):
    neg = jnp.full((GRID_W, 2 * GRID_W), NEG_INF, F32)
    left = lax.broadcasted_iota(jnp.int32, (1, 2 * GRID_W), 1) < GRID_W
    rows_out = []
    for dr in range(NAT_QROWS):
        pieces = []
        for a in range(0, NAT_KROWS, 2):
            ok0, ok1 = bool(valid_row[dr, a]), bool(valid_row[dr, a + 1])
            if not (ok0 or ok1):
                pieces.append(neg)
                continue
            piece = tp_ref[head, int(d0[dr]) + a - row_lo]
            if ok0 and not ok1:
                piece = jnp.where(left, piece, NEG_INF)
            elif ok1 and not ok0:
                piece = jnp.where(left, NEG_INF, piece)
            pieces.append(piece)
        rows_out.append(jnp.concatenate(pieces, axis=-1))
    return jnp.concatenate(rows_out, axis=0)


def _nat_attn_kernel(q_ref, k_ref, v_ref, kc_ref, vc_ref, tp_ref, o_ref, *, plan, variants, row_lo):
    kc = kc_ref[0].astype(BF16)
    vc = vc_ref[0].astype(BF16)
    assert NAT_SCALE == 2.0 ** round(np.log2(NAT_SCALE))
    masks = [_lane_mask(NAT_DH, i, BF16) * NAT_SCALE for i in range(2)]
    for bi, (ks, var) in enumerate(plan):
        q2 = _stack_heads(q_ref[bi * _NAT_QBLK:(bi + 1) * _NAT_QBLK, :], masks)
        k = k_ref[ks * GRID_W:ks * GRID_W + _NAT_KBLK, :]
        v = v_ref[ks * GRID_W:ks * GRID_W + _NAT_KBLK, :]
        bias2 = jnp.concatenate([_nat_block_bias(tp_ref, i, *variants[var], row_lo) for i in range(2)], axis=0)
        o2 = _softmax_pv([_dot_nt(q2, k) + bias2, _dot_nt(q2, kc)], [v, vc])
        o_ref[bi * _NAT_QBLK:(bi + 1) * _NAT_QBLK, :] = _unstack_heads(o2, NAT_DH).astype(BF16)


def _nat_attention(q, k, v, cache_k, cache_v, bias, n_batch):
    n = q.shape[0]
    s = n // n_batch
    plan, variants = _nat_block_plan(s // GRID_W)
    row_lo, _ = _nat_bias_row_range(variants)
    own = pl.BlockSpec((s, LANES), lambda p, b: (b, p))
    cache = pl.BlockSpec((1, cache_k.shape[1], LANES), lambda p, b: (b, 0, p))
    return pl.pallas_call(
        functools.partial(_nat_attn_kernel, plan=plan, variants=variants, row_lo=row_lo),
        grid=(HEAD_PAIRS, n_batch),
        in_specs=[own, own, own, cache, cache,
                  pl.BlockSpec((2,) + bias.shape[1:], lambda p, b: (p, 0, 0, 0))],
        out_specs=own,
        out_shape=jax.ShapeDtypeStruct((n, D_MODEL), BF16),
        compiler_params=_cparams("parallel", "parallel"),
        name="nat_attention",
    )(q, k, v, cache_k, cache_v, bias)


def _top2_sum(a, b, c, d):
    hi1, lo1 = jnp.maximum(a, b), jnp.minimum(a, b)
    hi2, lo2 = jnp.maximum(c, d), jnp.minimum(c, d)
    return jnp.maximum(hi1, hi2) + jnp.maximum(jnp.minimum(hi1, hi2), jnp.maximum(lo1, lo2))


def _route(scores, biased):
    sc = [scores[e:e + 1, :] for e in range(N_EXPERTS)]
    bs = [biased[e:e + 1, :] for e in range(N_EXPERTS)]
    gscore = [_top2_sum(*bs[EPG * g:EPG * (g + 1)]) for g in range(N_GROUPS)]
    best, gidx = gscore[0], jnp.zeros_like(gscore[0], dtype=jnp.int32)
    for g in range(1, N_GROUPS):
        better = gscore[g] > best
        gidx = jnp.where(better, g, gidx)
        best = jnp.where(better, gscore[g], best)
    cb = [functools.reduce(lambda a, b: a + b,
                           [jnp.where(gidx == g, bs[EPG * g + i], 0.0) for g in range(N_GROUPS)])
          for i in range(EPG)]
    cs = [functools.reduce(lambda a, b: a + b,
                           [jnp.where(gidx == g, sc[EPG * g + i], 0.0) for g in range(N_GROUPS)])
          for i in range(EPG)]
    b1, i1 = cb[0], jnp.zeros_like(gidx)
    for i in range(1, EPG):
        better = cb[i] > b1
        i1 = jnp.where(better, i, i1)
        b1 = jnp.where(better, cb[i], b1)
    b2, i2 = jnp.full_like(b1, -jnp.inf), jnp.full_like(i1, -1)
    for i in range(EPG):
        better = (i1 != i) & (cb[i] > b2)
        i2 = jnp.where(better, i, i2)
        b2 = jnp.where(better, cb[i], b2)
    sel = [(i1 == i) | (i2 == i) for i in range(EPG)]
    w = [jnp.where(sel[i], cs[i], 0.0) for i in range(EPG)]
    tot = w[0] + w[1] + w[2] + w[3]
    return gidx, [w[i] / tot for i in range(EPG)]


MOE_SRC = 256
MOE_DST = 512
MOE_SORT_TILES = 2
MOE_UNSORT_TILES = 4
_LOG2_DST = MOE_DST.bit_length() - 1
assert MOE_DST == 1 << _LOG2_DST and MOE_SRC <= MOE_DST <= 2 * MOE_SRC
_INFO_GID = EPG


def _group_tiles(cum_ref):
    n_src = cum_ref.shape[0]
    tot = [cum_ref[n_src - 1, g] for g in range(N_GROUPS)]
    ends, acc = [], 0
    for t in tot:
        acc = acc + lax.shift_right_logical(t + (MOE_DST - 1), _LOG2_DST)
        ends.append(acc)
    return tot, ends


def _dest_tile(d, cum_ref):
    tot, ends = _group_tiles(cum_ref)
    g = ((d >= ends[0]).astype(jnp.int32) + (d >= ends[1]).astype(jnp.int32)
         + (d >= ends[2]).astype(jnp.int32))
    first = jnp.where(g == 0, 0, jnp.where(g == 1, ends[0], jnp.where(g == 2, ends[1], ends[2])))
    tot_g = jnp.where(g == 0, tot[0], jnp.where(g == 1, tot[1], jnp.where(g == 2, tot[2], tot[3])))
    k0 = (d - first) * MOE_DST
    n_valid = jnp.clip(tot_g - k0, 0, MOE_DST)
    return g, k0, n_valid


def _counts_before(cum_ref, s, g):
    return jnp.where(s > 0, cum_ref[jnp.maximum(s - 1, 0), g], 0)


MOE_ALIGN = 16
_LOG2_ALIGN = MOE_ALIGN.bit_length() - 1
_LOC_ROWS = MOE_SRC + N_GROUPS * MOE_ALIGN
_INFO_LRANK = EPG + 1
H_EXT = D_MODEL + LANES
_PIECES = [MOE_SRC >> i for i in range((MOE_SRC // MOE_ALIGN).bit_length())]


def _moe_sort_tile(s, x_ref, a_ref, wo_ref, mod_ref, g_ref, wr_ref, rb_ref, xo_ref, info_ref, cum_ref,
                   lsort, seg_s, base_s):
    x = x_ref[...] + mod_ref[0, 2:3, :] * _dot(a_ref[...], wo_ref[...])
    xo_ref[...] = x
    h = _rms(x, g_ref[...]) * (1.0 + mod_ref[0, 4:5, :]) + mod_ref[0, 3:4, :]
    h_hi = h.astype(BF16)
    h_lo = (h - h_hi.astype(F32)).astype(BF16)
    hi_w = _dot(h_hi, wr_ref[...])
    logits = hi_w[:, :LANES] + (_dot(h_lo, wr_ref[:, :LANES]) + hi_w[:, LANES:])
    scores = _sigmoid(logits)
    gidx, gates = _route(scores.T[:N_EXPERTS], (scores + rb_ref[...]).T[:N_EXPERTS])
    tm = h.shape[0]
    sub = lax.broadcasted_iota(jnp.int32, (8, 1), 0)
    onehot = (sub == gidx).astype(F32)
    tri = (lax.broadcasted_iota(jnp.int32, (tm, tm), 0)
           <= lax.broadcasted_iota(jnp.int32, (tm, tm), 1)).astype(BF16)
    prefix = _dot(onehot.astype(BF16), tri)
    lrank = jnp.sum((prefix - 1.0) * onehot, axis=0, keepdims=True)

    count = [jnp.sum(onehot[g:g + 1, :]).astype(jnp.int32) for g in range(N_GROUPS)]
    padded = [lax.shift_left(lax.shift_right_logical(c + (MOE_ALIGN - 1), _LOG2_ALIGN), _LOG2_ALIGN)
              for c in count]
    loff, acc = [], 0
    for p in padded:
        loff.append(acc)
        acc = acc + p
    lpos = lrank + functools.reduce(
        lambda a, b: a + b, [jnp.where(gidx == g, jnp.asarray(v, jnp.int32).astype(F32), 0.0)
                             for g, v in enumerate(loff)])
    infot = jnp.concatenate(gates + [gidx.astype(F32), lrank, jnp.zeros((LANES - EPG - 2, tm), F32)], axis=0)
    info_ref[...] = infot.T
    g_hi = [gt.astype(BF16).astype(F32) for gt in gates]
    g_lo = [gt - gh for gt, gh in zip(gates, g_hi)]
    gext = jnp.concatenate(g_hi + g_lo + [jnp.zeros((LANES - 2 * EPG, tm), F32)], axis=0)
    h_ext = jnp.concatenate([h_hi, gext.T.astype(BF16)], axis=-1)
    perm = (lax.broadcasted_iota(jnp.int32, (_LOC_ROWS, 1), 0).astype(F32) == lpos).astype(BF16)
    lsort[s] = _dot(perm, h_ext).astype(BF16)

    new_base = []
    for g in range(N_GROUPS):
        seg_s[s, g] = padded[g]
        new_base.append(base_s[g] + padded[g])
        base_s[g] = new_base[g]
    cum = functools.reduce(lambda a, b: a + b,
                           [jnp.where(sub == g, new_base[g], 0) for g in range(N_GROUPS)])
    cum_ref[0] = jnp.broadcast_to(cum, (8, LANES))


def _copy_pieces(length, make_copy, wait):
    done = jnp.int32(0)
    for rows in _PIECES:
        take = (length & rows) != 0

        @pl.when(take)
        def _():
            cp = make_copy(pl.multiple_of(done, MOE_ALIGN), rows)
            cp.wait() if wait else cp.start()
        done = done + jnp.where(take, rows, 0)


def _moe_sort_kernel(x_ref, a_ref, wo_ref, mod_ref, g_ref, wr_ref, rb_ref, xo_ref, info_ref, cum_ref, srt_ref,
                     lsort, zeros_v, seg_s, base_s, sem):
    step = pl.program_id(0)
    n_src = lsort.shape[0]
    n_rows = srt_ref.shape[0]

    @pl.when(step == 0)
    def _():
        for g in range(N_GROUPS):
            base_s[g] = 0
        zeros_v[...] = jnp.zeros_like(zeros_v)

    @pl.when(step < n_src // MOE_SORT_TILES)
    def _():
        for j in range(MOE_SORT_TILES):
            rows = pl.ds(j * MOE_SRC, MOE_SRC)
            _moe_sort_tile(step * MOE_SORT_TILES + j, x_ref.at[rows], a_ref.at[rows], wo_ref, mod_ref, g_ref,
                           wr_ref, rb_ref, xo_ref.at[rows], info_ref.at[rows], cum_ref.at[pl.ds(j, 1)],
                           lsort, seg_s, base_s)

    @pl.when(step == n_src // MOE_SORT_TILES)
    def _():
        starts, acc = [], jnp.int32(0)
        for g in range(N_GROUPS):
            starts.append(acc)
            acc = acc + lax.shift_left(
                lax.shift_right_logical(base_s[g] + (MOE_DST - 1), _LOG2_DST), _LOG2_DST)
        end = acc

        def all_copies(wait):
            def tile_copies(t, dst):
                src = jnp.int32(0)
                new_dst = []
                for g in range(N_GROUPS):
                    n = seg_s[t, g]
                    _copy_pieces(n, lambda off, rows, src=src, g=g: pltpu.make_async_copy(
                        lsort.at[t, pl.ds(pl.multiple_of(src + off, MOE_ALIGN), rows)],
                        srt_ref.at[pl.ds(pl.multiple_of(dst[g] + off, MOE_ALIGN), rows)], sem.at[0]), wait)
                    src = src + n
                    new_dst.append(dst[g] + n)
                return tuple(new_dst)

            dst = lax.fori_loop(0, n_src, tile_copies, tuple(starts))
            for g in range(N_GROUPS):
                gap_end = starts[g + 1] if g + 1 < N_GROUPS else end
                _copy_pieces(gap_end - dst[g], lambda off, rows, g=g: pltpu.make_async_copy(
                    zeros_v.at[pl.ds(0, rows)],
                    srt_ref.at[pl.ds(pl.multiple_of(dst[g] + off, MOE_ALIGN), rows)], sem.at[0]), wait)

            def zero_tile(r, carry):
                cp = pltpu.make_async_copy(
                    zeros_v, srt_ref.at[pl.ds(pl.multiple_of(end + r * MOE_DST, MOE_DST), MOE_DST)], sem.at[0])
                cp.wait() if wait else cp.start()
                return carry

            lax.fori_loop(0, lax.shift_right_logical(n_rows - end, _LOG2_DST), zero_tile, 0)

        all_copies(False)
        all_copies(True)


def _moe_ffn2_kernel(cum_ref, xs_ref, wg_ref, wu_ref, wd_ref, y_ref):
    _, ends = _group_tiles(cum_ref)

    @pl.when(pl.program_id(0) >= ends[-1])
    def _():
        y_ref[...] = jnp.zeros_like(y_ref)

    @pl.when(pl.program_id(0) < ends[-1])
    def _():
        xs = xs_ref[:, :D_MODEL]
        gext = xs_ref[:, D_MODEL:].astype(F32)
        acts = []
        for i in range(EPG):
            a = _dot(xs, wg_ref[i])
            u = _dot(xs, wu_ref[i])
            gate = gext[:, i:i + 1] + gext[:, EPG + i:EPG + i + 1]
            acts.append(((a * _sigmoid(a)) * u * gate).astype(BF16))
        act = jnp.concatenate(acts, axis=-1)
        y_ref[...] = _dot(act, wd_ref[...].reshape(EPG * D_EXPERT, D_MODEL)).astype(BF16)


def _segment_row(cum_ref, s, g):
    _, ends = _group_tiles(cum_ref)
    first_tile = ends[g - 1] if g > 0 else 0
    return first_tile * MOE_DST + _counts_before(cum_ref, s, g)


def _unsort_window(cum_ref, s, g, n_rows):
    return pl.multiple_of(jnp.minimum(_segment_row(cum_ref, s, g), n_rows - MOE_SRC), MOE_ALIGN)


def _moe_unsort2_kernel(*refs, final_norm):
    if final_norm:
        cum_ref, x_ref, mod_ref, info_ref, y_ref, gf_ref, o_ref = refs
    else:
        cum_ref, x_ref, mod_ref, info_ref, y_ref, o_ref = refs
    n_rows = y_ref.shape[0]
    col = lax.broadcasted_iota(jnp.int32, (1, MOE_SRC), 1).astype(F32)
    for j in range(MOE_UNSORT_TILES):
        s = pl.program_id(0) * MOE_UNSORT_TILES + j
        rows = slice(j * MOE_SRC, (j + 1) * MOE_SRC)
        inf = info_ref[rows, :]
        gid = inf[:, _INFO_GID:_INFO_GID + 1]
        lrank = inf[:, _INFO_LRANK:_INFO_LRANK + 1]
        perms, wins = [], []
        for g in range(N_GROUPS):
            win = _unsort_window(cum_ref, s, g, n_rows)
            shift = _segment_row(cum_ref, s, g) - win
            rel = jnp.where(gid == float(g), lrank + shift.astype(F32), -1.0)
            perms.append((rel == col).astype(BF16))
            wins.append(y_ref[pl.ds(win, MOE_SRC), :])
        out = x_ref[rows, :] + mod_ref[0, 5:6, :] * _dot(jnp.concatenate(perms, axis=-1),
                                                         jnp.concatenate(wins, axis=0))
        o_ref[rows, :] = _rms(out, gf_ref[...]) if final_norm else out


def _mixer_out_and_moe(x, att, w_o, mod, g, w_router, router_bias, wg, wu, wd, layer, rows_per_cond,
                       final_g=None):
    n = x.shape[0]
    n_src = n // MOE_SRC
    n_dst = (n + N_GROUPS * n_src * (MOE_ALIGN - 1)) // MOE_DST + N_GROUPS
    n_rows = n_dst * MOE_DST
    ts = MOE_SORT_TILES * MOE_SRC
    last = n // ts - 1
    row = lambda i: (jnp.minimum(i, last), 0)
    full = lambda i: (0, 0)
    x, info, cum, srt = pl.pallas_call(
        _moe_sort_kernel,
        grid=(n // ts + 1,),
        in_specs=[
            pl.BlockSpec((ts, D_MODEL), row),
            pl.BlockSpec((ts, D_MODEL), row),
            pl.BlockSpec(w_o.shape, full),
            pl.BlockSpec((1, 6, D_MODEL), (lambda i: (0, 0, 0)) if rows_per_cond is None else
                         (lambda i: (1 + jnp.minimum(i, last) // (rows_per_cond // ts), 0, 0))),
            pl.BlockSpec((1, D_MODEL), full),
            pl.BlockSpec(w_router.shape, full),
            pl.BlockSpec(router_bias.shape, full),
        ],
        out_specs=[pl.BlockSpec((ts, D_MODEL), row),
                   pl.BlockSpec((ts, LANES), row),
                   pl.BlockSpec((MOE_SORT_TILES, 8, LANES), lambda i: (jnp.minimum(i, last), 0, 0)),
                   pl.BlockSpec(memory_space=pl.ANY)],
        out_shape=[jax.ShapeDtypeStruct((n, D_MODEL), F32),
                   jax.ShapeDtypeStruct((n, LANES), F32),
                   jax.ShapeDtypeStruct((n_src, 8, LANES), jnp.int32),
                   jax.ShapeDtypeStruct((n_rows, H_EXT), BF16)],
        scratch_shapes=[pltpu.VMEM((n_src, _LOC_ROWS, H_EXT), BF16),
                        pltpu.VMEM((MOE_DST, H_EXT), BF16),
                        pltpu.SMEM((n_src, N_GROUPS), jnp.int32),
                        pltpu.SMEM((N_GROUPS,), jnp.int32),
                        pltpu.SemaphoreType.DMA((1,))],
        compiler_params=_cparams("arbitrary"),
        name="moe_sort",
    )(x, att, w_o, mod, g, w_router, router_bias)
    cum = cum[:, :N_GROUPS, 0]

    blk = lambda width: pl.BlockSpec((MOE_DST, width), lambda d, c: (d, 0))
    expert = lambda shape: pl.BlockSpec(shape, lambda d, c: (layer, _dest_tile(d, c)[0], 0, 0))
    y = pl.pallas_call(
        _moe_ffn2_kernel,
        grid_spec=pltpu.PrefetchScalarGridSpec(
            num_scalar_prefetch=1,
            grid=(n_dst,),
            in_specs=[blk(H_EXT),
                      expert((None, EPG, D_MODEL, D_EXPERT)),
                      expert((None, EPG, D_MODEL, D_EXPERT)),
                      expert((None, EPG, D_EXPERT, D_MODEL))],
            out_specs=blk(D_MODEL),
        ),
        out_shape=jax.ShapeDtypeStruct((n_rows, D_MODEL), BF16),
        compiler_params=_cparams("arbitrary"),
        name="moe_experts",
    )(cum, srt, wg, wu, wd)

    tu = MOE_UNSORT_TILES * MOE_SRC
    in_specs = [pl.BlockSpec((tu, D_MODEL), lambda i, c: (i, 0)),
                _mod_spec(None if rows_per_cond is None else rows_per_cond // tu),
                pl.BlockSpec((tu, LANES), lambda i, c: (i, 0)),
                pl.BlockSpec(y.shape, lambda i, c: (0, 0), pipeline_mode=pl.Buffered(1))]
    args = [cum, x, mod, info, y]
    if final_g is not None:
        in_specs.append(pl.BlockSpec((1, D_MODEL), lambda i, c: (0, 0)))
        args.append(final_g)
    return pl.pallas_call(
        functools.partial(_moe_unsort2_kernel, final_norm=final_g is not None),
        grid_spec=pltpu.PrefetchScalarGridSpec(
            num_scalar_prefetch=1,
            grid=(n_src // MOE_UNSORT_TILES,),
            in_specs=in_specs,
            out_specs=pl.BlockSpec((tu, D_MODEL), lambda i, c: (i, 0)),
        ),
        out_shape=jax.ShapeDtypeStruct((n, D_MODEL), F32),
        compiler_params=_cparams("parallel"),
        name="moe_unsort",
    )(*args)


def _rope_tables(n_lat):
    t = np.arange(n_lat)
    n_freq = MLA_ROPE // 4
    inv_freq = jnp.asarray(ROPE_THETA, F32) ** (-jnp.arange(n_freq, dtype=F32) / n_freq)
    ar = jnp.asarray(t // GRID_W, F32)[:, None] * inv_freq
    ac = jnp.asarray(t % GRID_W, F32)[:, None] * inv_freq
    cos = jnp.concatenate([jnp.cos(ar), jnp.cos(ar), jnp.cos(ac), jnp.cos(ac)], axis=-1)
    sin = jnp.concatenate([-jnp.sin(ar), jnp.sin(ar), -jnp.sin(ac), jnp.sin(ac)], axis=-1)
    return (jnp.tile(cos, (1, MLA_HEADS)), jnp.tile(sin, (1, MLA_HEADS)),
            jnp.tile(cos, (1, LANES // MLA_ROPE)), jnp.tile(sin, (1, LANES // MLA_ROPE)))


_ROPE_SWAP = np.concatenate([np.arange(8, 16), np.arange(0, 8), np.arange(24, 32), np.arange(16, 24)])


def _mla_weights(w_in, w_uq, w_ukv):
    o = MLA_Q_LORA + MLA_KV_LORA
    kr = w_in[:, o:]
    rep = LANES // MLA_ROPE
    w_in_x = jnp.concatenate([w_in[:, :o], jnp.tile(kr, (1, rep)), jnp.tile(kr[:, _ROPE_SWAP], (1, rep))],
                             axis=-1).astype(BF16)
    uq = w_uq.reshape(MLA_Q_LORA, MLA_HEADS, MLA_NOPE + MLA_ROPE)
    q_nope = uq[:, :, :MLA_NOPE].reshape(MLA_Q_LORA, -1)
    q_rope = uq[:, :, MLA_NOPE:]
    w_uq_x = jnp.concatenate([q_nope, q_rope.reshape(MLA_Q_LORA, -1),
                              q_rope[:, :, _ROPE_SWAP].reshape(MLA_Q_LORA, -1)], axis=-1).astype(BF16)
    ukv = w_ukv.reshape(MLA_KV_LORA, MLA_HEADS, MLA_NOPE + MLA_V)
    w_uk = ukv[:, :, :MLA_NOPE].reshape(MLA_KV_LORA, -1).astype(BF16)
    w_uv = ukv[:, :, MLA_NOPE:].reshape(MLA_KV_LORA, -1).astype(BF16)
    return w_in_x, w_uq_x, w_uk, w_uv


def kernel(x_prompt, x_sample, cache_mla_ckv, cache_mla_krope, cache_nat_k, cache_nat_v, c, c_ctx,
           w_ada, b_ada, norm_mix, norm_ffn, norm_final, mla_w_in, mla_q_norm, mla_w_uq, mla_kv_norm,
           mla_w_ukv, mla_w_o, nat_w_qkv, nat_rpb, nat_w_o, w_router, router_bias,
           moe_w_gate, moe_w_up, moe_w_down):
    B, S, D = x_prompt.shape
    Bd, Sd, _ = x_sample.shape
    assert D == D_MODEL and Bd + 1 <= 8
    tm_c, tm_s = 512, 512

    xc = x_prompt.reshape(B * S, D)
    xs = x_sample.reshape(Bd * Sd, D)
    cond8 = jnp.concatenate([c_ctx[None, :], c, jnp.zeros((8 - 1 - Bd, D), F32)], axis=0)
    mod_all = _ada_modulation(cond8, w_ada, b_ada).reshape(DEPTH, 8, 6, D)

    rope_tabs = _rope_tables(Sd)
    wr = jnp.pad(w_router, ((0, 0), (0, LANES - N_EXPERTS)))
    wr_hi = wr.astype(BF16)
    wr = jnp.concatenate([wr_hi, (wr - wr_hi.astype(F32)).astype(BF16)], axis=1)
    rb = jnp.pad(router_bias, (0, LANES - N_EXPERTS)).reshape(1, LANES)
    wg = moe_w_gate.astype(BF16)
    wu = moe_w_up.astype(BF16)
    wd = moe_w_down.astype(BF16)

    new_ckv, new_krope = [], []
    k_buf = v_buf = None
    for layer in range(DEPTH):
        mod = mod_all[layer]
        g_mix = norm_mix[layer][None, :]
        j = layer // 2
        if layer % 2 == 0:
            w_in_x, w_uq_x, w_uk, w_uv = _mla_weights(mla_w_in[j], mla_w_uq[j], mla_w_ukv[j])
            w_in_c = w_in_x[:, :MLA_Q_LORA + MLA_KV_LORA + LANES]
            w_uq_c = w_uq_x[:, :MLA_HEADS * (MLA_NOPE + MLA_ROPE)]
            qg = mla_q_norm[j][None, :]
            kvg = mla_kv_norm[j][None, :]
            w_o = mla_w_o[j].astype(BF16)
            qn, qr, ckv, kr, kr4 = _premix_mla(xc, mod, g_mix, w_in_c, qg, w_uq_c, kvg, None, None, tm_c)
            new_ckv.append(ckv.reshape(B, S, MLA_KV_LORA))
            new_krope.append(kr.reshape(B, S, MLA_ROPE))
            att_c = _mla_attention(qn, qr, ckv, kr4, None, None, w_uk, w_uv, B, S)
            qn, qr, ckv, kr, kr4 = _premix_mla(xs, mod, g_mix, w_in_x, qg, w_uq_x, kvg, rope_tabs, Sd, tm_s)
            cache_kr4 = jnp.tile(cache_mla_krope[:, j], (1, 1, LANES // MLA_ROPE))
            att_s = _mla_attention(qn, qr, ckv, kr4, cache_mla_ckv[:, j], cache_kr4, w_uk, w_uv, Bd, 256)
        else:
            w_qkv = nat_w_qkv[j].astype(BF16)
            w_o = nat_w_o[j].astype(BF16)
            q, k_buf, v_buf = _premix_nat_cache(xc, mod, g_mix, w_qkv, k_buf, v_buf, j, DEPTH // 2, B, tm_c)
            att_c = _dense_attention(q, k_buf, v_buf, j)
            q, k, v = _premix_nat(xs, mod, g_mix, w_qkv, Sd, tm_s, BF16)
            bias = _nat_bias_pairs(nat_rpb[j], Sd // GRID_W)
            att_s = _nat_attention(q, k, v,
                                   cache_nat_k[:, j].reshape(Bd, -1, D), cache_nat_v[:, j].reshape(Bd, -1, D),
                                   bias, Bd)
        g_ffn = norm_ffn[layer][None, :]
        final_g = norm_final[None, :] if layer == DEPTH - 1 else None
        xc = _mixer_out_and_moe(xc, att_c, w_o, mod, g_ffn, wr, rb, wg, wu, wd, layer, None, final_g)
        xs = _mixer_out_and_moe(xs, att_s, w_o, mod, g_ffn, wr, rb, wg, wu, wd, layer, Sd, final_g)

    y_prompt = xc.reshape(B, S, D)
    y_sample = xs.reshape(Bd, Sd, D)
    kv_shape = (B, DEPTH // 2, S, NAT_HEADS, NAT_DH)
    return (y_prompt, y_sample, jnp.stack(new_ckv, axis=1), jnp.stack(new_krope, axis=1),
            k_buf.reshape(kv_shape), v_buf.reshape(kv_shape))
```

```python
import functools

import numpy as np
import jax
import jax.numpy as jnp
from jax import lax
from jax.experimental import pallas as pl
from jax.experimental.pallas import tpu as pltpu

F32 = jnp.float32
BF16 = jnp.bfloat16

D_MODEL = 1024
DEPTH = 4
GRID_W = 64
LANES = 128
MLA_HEADS = 16
MLA_NOPE = 64
MLA_ROPE = 32
MLA_V = 64
MLA_Q_LORA = 384
MLA_KV_LORA = 256
MLA_SCALE = (MLA_NOPE + MLA_ROPE) ** -0.5
ROPE_THETA = 10000.0
NAT_HEADS = 16
NAT_DH = 64
NAT_SCALE = NAT_DH ** -0.5
WIN_H = 8
WIN_W = 16
NAT_QROWS = 4
NAT_KROWS = NAT_QROWS + WIN_H
N_EXPERTS = 16
N_GROUPS = 4
EPG = N_EXPERTS // N_GROUPS
D_EXPERT = 256
NORM_EPS = 1e-6
NEG_INF = -1e30

HEAD_PAIRS = MLA_HEADS // 2
VMEM_LIMIT = 56 * 1024 * 1024


def _cparams(*sem):
    return pltpu.CompilerParams(dimension_semantics=sem, vmem_limit_bytes=VMEM_LIMIT)


def _sigmoid(x):
    return 1.0 / (1.0 + jnp.exp(-x))


def _rms(x, g):
    ms = jnp.mean(x * x, axis=-1, keepdims=True)
    return x * lax.rsqrt(ms + NORM_EPS) * g


def _dot(a, b):
    return jnp.dot(a, b, preferred_element_type=F32)


def _dot_nt(a, b):
    return lax.dot_general(a, b, (((1,), (1,)), ((), ())), preferred_element_type=F32)


def _lane_mask(width, idx, dtype):
    lane = lax.broadcasted_iota(jnp.int32, (1, LANES), 1)
    return ((lane >= idx * width) & (lane < (idx + 1) * width)).astype(dtype)


def _ada_kernel(cond_ref, w_ref, b_ref, o_ref):
    c = cond_ref[...]
    s = (c * _sigmoid(c)).astype(BF16)
    o_ref[0] = _dot(s, w_ref[0].astype(BF16)) + b_ref[0]


def _ada_modulation(cond8, w_ada, b_ada):
    n_chunk = 6
    return pl.pallas_call(
        _ada_kernel,
        grid=(DEPTH, n_chunk),
        in_specs=[
            pl.BlockSpec((8, D_MODEL), lambda l, j: (0, 0)),
            pl.BlockSpec((1, D_MODEL, D_MODEL), lambda l, j: (l, 0, j)),
            pl.BlockSpec((1, 1, D_MODEL), lambda l, j: (l, 0, j)),
        ],
        out_specs=pl.BlockSpec((1, 8, D_MODEL), lambda l, j: (l, 0, j)),
        out_shape=jax.ShapeDtypeStruct((DEPTH, 8, 6 * D_MODEL), F32),
        compiler_params=_cparams("parallel", "parallel"),
        name="ada_modulation",
    )(cond8, w_ada, b_ada.reshape(DEPTH, 1, 6 * D_MODEL))


def _mod_spec(rows_per_cond):
    if rows_per_cond is None:
        return pl.BlockSpec((1, 6, D_MODEL), lambda i, *_: (0, 0, 0))
    return pl.BlockSpec((1, 6, D_MODEL), lambda i, *_: (1 + i // rows_per_cond, 0, 0))


def _premix_mla_kernel(*refs, rope):
    if rope:
        (x_ref, mod_ref, g_ref, w_in_ref, qg_ref, w_uq_ref, kvg_ref,
         cosq_ref, sinq_ref, cosk_ref, sink_ref,
         qn_ref, qr_ref, ckv_ref, kr_ref, kr4_ref) = refs
    else:
        (x_ref, mod_ref, g_ref, w_in_ref, qg_ref, w_uq_ref, kvg_ref,
         qn_ref, qr_ref, ckv_ref, kr_ref, kr4_ref) = refs
    h = _rms(x_ref[...], g_ref[...]) * (1.0 + mod_ref[0, 1:2, :]) + mod_ref[0, 0:1, :]
    lat = _dot(h.astype(BF16), w_in_ref[...])
    c_q = lat[:, :MLA_Q_LORA]
    c_kv = lat[:, MLA_Q_LORA:MLA_Q_LORA + MLA_KV_LORA]
    o = MLA_Q_LORA + MLA_KV_LORA
    kr4 = lat[:, o:o + LANES]
    q = _dot(_rms(c_q, qg_ref[...]).astype(BF16), w_uq_ref[...])
    n_nope = MLA_HEADS * MLA_NOPE
    n_rope = MLA_HEADS * MLA_ROPE
    qr = q[:, n_nope:n_nope + n_rope]
    if rope:
        qr = qr * cosq_ref[...] + q[:, n_nope + n_rope:] * sinq_ref[...]
        kr4 = kr4 * cosk_ref[...] + lat[:, o + LANES:o + 2 * LANES] * sink_ref[...]
    qn_ref[...] = q[:, :n_nope].astype(BF16)
    qr_ref[...] = qr.astype(BF16)
    ckv_ref[...] = _rms(c_kv, kvg_ref[...])
    kr_ref[...] = kr4[:, :MLA_ROPE]
    kr4_ref[...] = kr4.astype(BF16)


def _premix_mla(x, mod, g, w_in, qg, w_uq, kvg, rope_tabs, rows_per_cond, tm):
    n = x.shape[0]
    rope = rope_tabs is not None
    row = lambda i: (i, 0)
    full = lambda i: (0, 0)
    in_specs = [
        pl.BlockSpec((tm, D_MODEL), row),
        _mod_spec(None if rows_per_cond is None else rows_per_cond // tm),
        pl.BlockSpec((1, D_MODEL), full),
        pl.BlockSpec(w_in.shape, full),
        pl.BlockSpec((1, MLA_Q_LORA), full),
        pl.BlockSpec(w_uq.shape, full),
        pl.BlockSpec((1, MLA_KV_LORA), full),
    ]
    args = [x, mod, g, w_in, qg, w_uq, kvg]
    if rope:
        nblk = rope_tabs[0].shape[0] // tm
        pos = lambda i: (i % nblk, 0)
        for t in rope_tabs:
            in_specs.append(pl.BlockSpec((tm, t.shape[1]), pos))
            args.append(t)
    widths = (MLA_HEADS * MLA_NOPE, MLA_HEADS * MLA_ROPE, MLA_KV_LORA, MLA_ROPE, LANES)
    dtypes = (BF16, BF16, F32, F32, BF16)
    return pl.pallas_call(
        functools.partial(_premix_mla_kernel, rope=rope),
        grid=(n // tm,),
        in_specs=in_specs,
        out_specs=[pl.BlockSpec((tm, w), row) for w in widths],
        out_shape=[jax.ShapeDtypeStruct((n, w), d) for w, d in zip(widths, dtypes)],
        compiler_params=_cparams("parallel"),
        name="premix_mla_rope" if rope else "premix_mla",
    )(*args)


LOG2E = 1.4426950408889634


def _softmax_pv(s_list, v_list, scale=1.0):
    m = functools.reduce(jnp.maximum, [jnp.max(s, axis=-1, keepdims=True) for s in s_list])
    e_list = [jnp.exp2((s - m) * (scale * LOG2E)) for s in s_list]
    l = functools.reduce(lambda a, b: a + b, [jnp.sum(e, axis=-1, keepdims=True) for e in e_list])
    o = functools.reduce(lambda a, b: a + b,
                         [_dot(e.astype(BF16), v) for e, v in zip(e_list, v_list)])
    return o / l


def _stack_heads(q, masks):
    return jnp.concatenate([q * m for m in masks], axis=0)


def _unstack_heads(o2, width):
    tq = o2.shape[0] // 2
    lane = lax.broadcasted_iota(jnp.int32, (1, LANES), 1)
    return jnp.where(lane < width, o2[:tq], o2[tq:])


MLA_ATTN_BATCHES = 2


def _mla_attn_kernel(*refs, cached, nb):
    if cached:
        (qn_ref, qr_ref, ckv_ref, kr4_ref, cckv_ref, ckr4_ref, w_uk_ref, w_uv_ref,
         o_ref, kn_s, v_s, kr_s) = refs
    else:
        (qn_ref, qr_ref, ckv_ref, kr4_ref, w_uk_ref, w_uv_ref, o_ref, kn_s, v_s, kr_s) = refs
    t_own = ckv_ref.shape[0] // nb

    for b in range(nb):
        own = slice(b * t_own, (b + 1) * t_own)

        def fill_kv():
            c = ckv_ref[own, :].astype(BF16)
            kn_s[0:t_own, :] = _dot(c, w_uk_ref[...]).astype(BF16)
            v_s[0:t_own, :] = _dot(c, w_uv_ref[...]).astype(BF16)
            kr_s[0:t_own, :] = kr4_ref[own, :]
            if cached:
                cc = cckv_ref[0].astype(BF16)
                kn_s[t_own:, :] = _dot(cc, w_uk_ref[...]).astype(BF16)
                v_s[t_own:, :] = _dot(cc, w_uv_ref[...]).astype(BF16)
                kr_s[t_own:, :] = ckr4_ref[0].astype(BF16)

        if cached:
            pl.when(pl.program_id(1) == 0)(fill_kv)
            qrows = slice(None)
        else:
            fill_kv()
            qrows = own

        kr4 = kr_s[...]
        for p in range(HEAD_PAIRS):
            sl = slice(p * LANES, (p + 1) * LANES)
            qn = qn_ref[qrows, sl]
            qr = qr_ref[qrows, (p // 2) * LANES:(p // 2 + 1) * LANES]
            k_cat = jnp.concatenate([kn_s[:, sl], kr4], axis=-1)
            q_cat = jnp.concatenate(
                [_stack_heads(qn, [_lane_mask(MLA_NOPE, i, BF16) for i in range(2)]),
                 _stack_heads(qr, [_lane_mask(MLA_ROPE, (2 * p + i) % 4, BF16) for i in range(2)])], axis=-1)
            o2 = _softmax_pv([_dot_nt(q_cat, k_cat)], [v_s[:, sl]], MLA_SCALE)
            o_ref[qrows, sl] = _unstack_heads(o2, MLA_V).astype(BF16)


def _mla_attention(qn, qr, ckv, kr4, cache_ckv, cache_kr4, w_uk, w_uv, n_batch, tq):
    n = qn.shape[0]
    s_own = n // n_batch
    nq = s_own // tq
    cached = cache_ckv is not None
    t_all = s_own + (cache_ckv.shape[1] if cached else 0)
    nb = 1 if cached else MLA_ATTN_BATCHES
    assert cached or nq == 1
    tq = tq * nb
    qrow = lambda b, j: (b * nq + j, 0)
    own = lambda b, j: (b, 0)
    full = lambda b, j: (0, 0)
    in_specs = [
        pl.BlockSpec((tq, qn.shape[1]), qrow),
        pl.BlockSpec((tq, qr.shape[1]), qrow),
        pl.BlockSpec((nb * s_own, MLA_KV_LORA), own),
        pl.BlockSpec((nb * s_own, LANES), own),
    ]
    args = [qn, qr, ckv, kr4]
    if cached:
        in_specs += [pl.BlockSpec((1,) + cache_ckv.shape[1:], lambda b, j: (b, 0, 0)),
                     pl.BlockSpec((1,) + cache_kr4.shape[1:], lambda b, j: (b, 0, 0))]
        args += [cache_ckv, cache_kr4]
    in_specs += [pl.BlockSpec(w_uk.shape, full), pl.BlockSpec(w_uv.shape, full)]
    args += [w_uk, w_uv]
    return pl.pallas_call(
        functools.partial(_mla_attn_kernel, cached=cached, nb=nb),
        grid=(n_batch // nb, nq),
        in_specs=in_specs,
        out_specs=pl.BlockSpec((tq, D_MODEL), qrow),
        out_shape=jax.ShapeDtypeStruct((n, D_MODEL), BF16),
        scratch_shapes=[pltpu.VMEM((t_all, D_MODEL), BF16), pltpu.VMEM((t_all, D_MODEL), BF16),
                        pltpu.VMEM((t_all, LANES), BF16)],
        compiler_params=_cparams("parallel", "arbitrary"),
        name="mla_attention_cached" if cached else "mla_attention",
    )(*args)


def _premix_nat_kernel(x_ref, mod_ref, g_ref, w_ref, q_ref, k_ref, v_ref):
    h = _rms(x_ref[...], g_ref[...]) * (1.0 + mod_ref[0, 1:2, :]) + mod_ref[0, 0:1, :]
    qkv = _dot(h.astype(BF16), w_ref[...])
    q_ref[...] = qkv[:, :D_MODEL].astype(q_ref.dtype)
    k_ref[...] = qkv[:, D_MODEL:2 * D_MODEL].astype(k_ref.dtype)
    v_ref[...] = qkv[:, 2 * D_MODEL:].astype(v_ref.dtype)


def _premix_nat(x, mod, g, w_qkv, rows_per_cond, tm, kv_dtype):
    n = x.shape[0]
    row = lambda i: (i, 0)
    full = lambda i: (0, 0)
    return pl.pallas_call(
        _premix_nat_kernel,
        grid=(n // tm,),
        in_specs=[
            pl.BlockSpec((tm, D_MODEL), row),
            _mod_spec(None if rows_per_cond is None else rows_per_cond // tm),
            pl.BlockSpec((1, D_MODEL), full),
            pl.BlockSpec(w_qkv.shape, full),
        ],
        out_specs=[pl.BlockSpec((tm, D_MODEL), row)] * 3,
        out_shape=[jax.ShapeDtypeStruct((n, D_MODEL), BF16),
                   jax.ShapeDtypeStruct((n, D_MODEL), kv_dtype),
                   jax.ShapeDtypeStruct((n, D_MODEL), kv_dtype)],
        compiler_params=_cparams("parallel"),
        name="premix_nat",
    )(x, mod, g, w_qkv)


def _premix_nat_cache_kernel(*refs, first):
    if first:
        x_ref, mod_ref, g_ref, w_ref, q_ref, k_ref, v_ref = refs
    else:
        x_ref, mod_ref, g_ref, w_ref, _, _, q_ref, k_ref, v_ref = refs
    h = _rms(x_ref[...], g_ref[...]) * (1.0 + mod_ref[0, 1:2, :]) + mod_ref[0, 0:1, :]
    qkv = _dot(h.astype(BF16), w_ref[...])
    q_ref[...] = qkv[:, :D_MODEL].astype(BF16)
    nb, n_slots, s, _ = k_ref.shape
    k_ref[:, 0] = qkv[:, D_MODEL:2 * D_MODEL].reshape(nb, s, D_MODEL)
    v_ref[:, 0] = qkv[:, 2 * D_MODEL:].reshape(nb, s, D_MODEL)
    for slot in range(1, n_slots):
        k_ref[:, slot] = jnp.zeros((nb, s, D_MODEL), F32)
        v_ref[:, slot] = jnp.zeros((nb, s, D_MODEL), F32)


def _premix_nat_cache(x, mod, g, w_qkv, k_buf, v_buf, slot, n_slots, n_batch, tm):
    n = x.shape[0]
    s = n // n_batch
    nb = tm // s
    first = k_buf is None
    row = lambda i: (i, 0)
    full = lambda i: (0, 0)
    in_specs = [pl.BlockSpec((tm, D_MODEL), row), _mod_spec(None), pl.BlockSpec((1, D_MODEL), full),
                pl.BlockSpec(w_qkv.shape, full)]
    args = [x, mod, g, w_qkv]
    if first:
        assert slot == 0
        kv_spec = pl.BlockSpec((nb, n_slots, s, D_MODEL), lambda i: (i, 0, 0, 0))
        aliases = {}
    else:
        in_specs += [pl.BlockSpec(memory_space=pl.ANY)] * 2
        args += [k_buf, v_buf]
        kv_spec = pl.BlockSpec((nb, 1, s, D_MODEL), lambda i: (i, slot, 0, 0))
        aliases = {4: 1, 5: 2}
    kv_shape = jax.ShapeDtypeStruct((n_batch, n_slots, s, D_MODEL), F32)
    return pl.pallas_call(
        functools.partial(_premix_nat_cache_kernel, first=first),
        grid=(n // tm,),
        in_specs=in_specs,
        out_specs=[pl.BlockSpec((tm, D_MODEL), row), kv_spec, kv_spec],
        out_shape=[jax.ShapeDtypeStruct((n, D_MODEL), BF16), kv_shape, kv_shape],
        input_output_aliases=aliases,
        compiler_params=_cparams("parallel"),
        name="premix_nat_cache",
    )(*args)


DENSE_ATTN_BATCHES = 4


def _dense_attn_kernel(q_ref, k_ref, v_ref, o_ref):
    nb, s, _ = k_ref.shape
    for b in range(nb):
        rows = slice(b * s, (b + 1) * s)
        for p in range(HEAD_PAIRS):
            sl = slice(p * LANES, (p + 1) * LANES)
            q2 = _stack_heads(q_ref[rows, sl], [_lane_mask(NAT_DH, i, BF16) for i in range(2)])
            k = k_ref[b, :, sl].astype(BF16)
            v = v_ref[b, :, sl].astype(BF16)
            o2 = _softmax_pv([_dot_nt(q2, k)], [v], NAT_SCALE)
            o_ref[rows, sl] = _unstack_heads(o2, NAT_DH).astype(BF16)


def _dense_attention(q, k_buf, v_buf, slot):
    n = q.shape[0]
    n_batch, _, s, _ = k_buf.shape
    nb = DENSE_ATTN_BATCHES
    blk = pl.BlockSpec((nb * s, D_MODEL), lambda b: (b, 0))
    kv = pl.BlockSpec((nb, None, s, D_MODEL), lambda b: (b, slot, 0, 0))
    return pl.pallas_call(
        _dense_attn_kernel,
        grid=(n_batch // nb,),
        in_specs=[blk, kv, kv],
        out_specs=blk,
        out_shape=jax.ShapeDtypeStruct((n, D_MODEL), BF16),
        compiler_params=_cparams("parallel"),
        name="dense_attention",
    )(q, k_buf, v_buf)


_NAT_QBLK = NAT_QROWS * GRID_W
_NAT_KBLK = NAT_KROWS * GRID_W


def _nat_block_plan(rows):
    assert rows % NAT_QROWS == 0 and rows >= NAT_KROWS and NAT_KROWS % 2 == 0
    plan, variants = [], []
    for r0 in range(0, rows, NAT_QROWS):
        ks = min(max(r0 - WIN_H // 2, 0), rows - NAT_KROWS)
        r = r0 + np.arange(NAT_QROWS)
        kr = ks + np.arange(NAT_KROWS)
        rs = np.clip(r - WIN_H // 2, 0, rows - WIN_H)
        valid_row = (kr[None, :] >= rs[:, None]) & (kr[None, :] < rs[:, None] + WIN_H)
        d0 = ks - r + (WIN_H - 1)
        for vi, (d0_v, valid_v) in enumerate(variants):
            if np.array_equal(d0, d0_v) and np.array_equal(valid_row, valid_v):
                break
        else:
            vi = len(variants)
            variants.append((d0, valid_row))
        plan.append((ks, vi))
    return plan, variants


def _nat_bias_row_range(variants):
    lo = min(int(d0.min()) for d0, _ in variants)
    hi = max(int(d0.max()) for d0, _ in variants) + NAT_KROWS
    return lo, hi


def _nat_bias_pairs(rpb, rows):
    n_heads, n_dr, n_dc = rpb.shape
    _, variants = _nat_block_plan(rows)
    lo, hi = _nat_bias_row_range(variants)
    c = np.arange(GRID_W)
    cs = np.clip(c - WIN_W // 2, 0, GRID_W - WIN_W)
    valid_col = (c[None, :] >= cs[:, None]) & (c[None, :] < cs[:, None] + WIN_W)
    d_col = c[None, :] - c[:, None] + (WIN_W - 1)
    sel = (d_col[None] == np.arange(n_dc)[:, None, None]) & valid_col[None]
    toep = jnp.einsum('hdj,jck->hdck', rpb, jnp.asarray(sel, F32), precision=lax.Precision.HIGHEST)
    toep = jnp.where(jnp.asarray(valid_col)[None, None], toep, NEG_INF)
    toep = jnp.pad(toep, ((0, 0), (max(-lo, 0), max(hi + 1 - n_dr, 0)), (0, 0), (0, 0)),
                   constant_values=NEG_INF)
    toep = toep[:, max(lo, 0):]
    return jnp.concatenate([toep[:, :-1], toep[:, 1:]], axis=-1)


def _nat_block_bias(tp_ref, head, d0, valid_row, row_lo):
    neg = jnp.full((GRID_W, 2 * GRID_W), NEG_INF, F32)
    left = lax.broadcasted_iota(jnp.int32, (1, 2 * GRID_W), 1) < GRID_W
    rows_out = []
    for dr in range(NAT_QROWS):
        pieces = []
        for a in range(0, NAT_KROWS, 2):
            ok0, ok1 = bool(valid_row[dr, a]), bool(valid_row[dr, a + 1])
            if not (ok0 or ok1):
                pieces.append(neg)
                continue
            piece = tp_ref[head, int(d0[dr]) + a - row_lo]
            if ok0 and not ok1:
                piece = jnp.where(left, piece, NEG_INF)
            elif ok1 and not ok0:
                piece = jnp.where(left, NEG_INF, piece)
            pieces.append(piece)
        rows_out.append(jnp.concatenate(pieces, axis=-1))
    return jnp.concatenate(rows_out, axis=0)


NAT_ATTN_BATCHES = 2


def _nat_attn_kernel(q_ref, k_ref, v_ref, kc_ref, vc_ref, tp_ref, o_ref, *, plan, variants, row_lo):
    nb = kc_ref.shape[0]
    s = q_ref.shape[0] // nb
    assert NAT_SCALE == 2.0 ** round(np.log2(NAT_SCALE))
    masks = [_lane_mask(NAT_DH, i, BF16) * NAT_SCALE for i in range(2)]
    for bi, (ks, var) in enumerate(plan):
        bias2 = jnp.concatenate([_nat_block_bias(tp_ref, i, *variants[var], row_lo) for i in range(2)], axis=0)
        for b in range(nb):
            rows = slice(b * s + bi * _NAT_QBLK, b * s + (bi + 1) * _NAT_QBLK)
            keys = slice(b * s + ks * GRID_W, b * s + ks * GRID_W + _NAT_KBLK)
            q2 = _stack_heads(q_ref[rows, :], masks)
            kc = kc_ref[b].astype(BF16)
            vc = vc_ref[b].astype(BF16)
            o2 = _softmax_pv([_dot_nt(q2, k_ref[keys, :]) + bias2, _dot_nt(q2, kc)], [v_ref[keys, :], vc])
            o_ref[rows, :] = _unstack_heads(o2, NAT_DH).astype(BF16)


def _nat_attention(q, k, v, cache_k, cache_v, bias, n_batch):
    n = q.shape[0]
    s = n // n_batch
    plan, variants = _nat_block_plan(s // GRID_W)
    row_lo, _ = _nat_bias_row_range(variants)
    nb = NAT_ATTN_BATCHES
    own = pl.BlockSpec((nb * s, LANES), lambda p, b: (b, p))
    cache = pl.BlockSpec((nb, cache_k.shape[1], LANES), lambda p, b: (b, 0, p))
    return pl.pallas_call(
        functools.partial(_nat_attn_kernel, plan=plan, variants=variants, row_lo=row_lo),
        grid=(HEAD_PAIRS, n_batch // nb),
        in_specs=[own, own, own, cache, cache,
                  pl.BlockSpec((2,) + bias.shape[1:], lambda p, b: (p, 0, 0, 0))],
        out_specs=own,
        out_shape=jax.ShapeDtypeStruct((n, D_MODEL), BF16),
        compiler_params=_cparams("parallel", "parallel"),
        name="nat_attention",
    )(q, k, v, cache_k, cache_v, bias)


def _top2_sum(a, b, c, d):
    hi1, lo1 = jnp.maximum(a, b), jnp.minimum(a, b)
    hi2, lo2 = jnp.maximum(c, d), jnp.minimum(c, d)
    return jnp.maximum(hi1, hi2) + jnp.maximum(jnp.minimum(hi1, hi2), jnp.maximum(lo1, lo2))


def _route(scores, biased):
    sc = [scores[e:e + 1, :] for e in range(N_EXPERTS)]
    bs = [biased[e:e + 1, :] for e in range(N_EXPERTS)]
    gscore = [_top2_sum(*bs[EPG * g:EPG * (g + 1)]) for g in range(N_GROUPS)]
    best, gidx = gscore[0], jnp.zeros_like(gscore[0], dtype=jnp.int32)
    for g in range(1, N_GROUPS):
        better = gscore[g] > best
        gidx = jnp.where(better, g, gidx)
        best = jnp.where(better, gscore[g], best)
    cb = [functools.reduce(lambda a, b: a + b,
                           [jnp.where(gidx == g, bs[EPG * g + i], 0.0) for g in range(N_GROUPS)])
          for i in range(EPG)]
    cs = [functools.reduce(lambda a, b: a + b,
                           [jnp.where(gidx == g, sc[EPG * g + i], 0.0) for g in range(N_GROUPS)])
          for i in range(EPG)]
    b1, i1 = cb[0], jnp.zeros_like(gidx)
    for i in range(1, EPG):
        better = cb[i] > b1
        i1 = jnp.where(better, i, i1)
        b1 = jnp.where(better, cb[i], b1)
    b2, i2 = jnp.full_like(b1, -jnp.inf), jnp.full_like(i1, -1)
    for i in range(EPG):
        better = (i1 != i) & (cb[i] > b2)
        i2 = jnp.where(better, i, i2)
        b2 = jnp.where(better, cb[i], b2)
    sel = [(i1 == i) | (i2 == i) for i in range(EPG)]
    w = [jnp.where(sel[i], cs[i], 0.0) for i in range(EPG)]
    tot = w[0] + w[1] + w[2] + w[3]
    return gidx, [w[i] / tot for i in range(EPG)]


MOE_SRC = 256
MOE_DST = 512
MOE_SORT_TILES = 2
MOE_UNSORT_TILES = 4
_LOG2_DST = MOE_DST.bit_length() - 1
assert MOE_DST == 1 << _LOG2_DST and MOE_SRC <= MOE_DST <= 2 * MOE_SRC
_INFO_GID = EPG


def _group_tiles(cum_ref):
    n_src = cum_ref.shape[0]
    tot = [cum_ref[n_src - 1, g] for g in range(N_GROUPS)]
    ends, acc = [], 0
    for t in tot:
        acc = acc + lax.shift_right_logical(t + (MOE_DST - 1), _LOG2_DST)
        ends.append(acc)
    return tot, ends


def _dest_tile(d, cum_ref):
    tot, ends = _group_tiles(cum_ref)
    g = ((d >= ends[0]).astype(jnp.int32) + (d >= ends[1]).astype(jnp.int32)
         + (d >= ends[2]).astype(jnp.int32))
    first = jnp.where(g == 0, 0, jnp.where(g == 1, ends[0], jnp.where(g == 2, ends[1], ends[2])))
    tot_g = jnp.where(g == 0, tot[0], jnp.where(g == 1, tot[1], jnp.where(g == 2, tot[2], tot[3])))
    k0 = (d - first) * MOE_DST
    n_valid = jnp.clip(tot_g - k0, 0, MOE_DST)
    return g, k0, n_valid


def _counts_before(cum_ref, s, g):
    return jnp.where(s > 0, cum_ref[jnp.maximum(s - 1, 0), g], 0)


MOE_ALIGN = 16
_LOG2_ALIGN = MOE_ALIGN.bit_length() - 1
_LOC_ROWS = MOE_SRC + N_GROUPS * MOE_ALIGN
_INFO_LRANK = EPG + 1
H_EXT = D_MODEL + LANES
_PIECES = [MOE_SRC >> i for i in range((MOE_SRC // MOE_ALIGN).bit_length())]


def _moe_sort_tile(s, x_ref, a_ref, wo_ref, mod_ref, g_ref, wr_ref, rb_ref, xo_ref, info_ref, cum_ref,
                   lsort, seg_s, base_s):
    x = x_ref[...] + mod_ref[0, 2:3, :] * _dot(a_ref[...], wo_ref[...])
    xo_ref[...] = x
    h = _rms(x, g_ref[...]) * (1.0 + mod_ref[0, 4:5, :]) + mod_ref[0, 3:4, :]
    h_hi = h.astype(BF16)
    h_lo = (h - h_hi.astype(F32)).astype(BF16)
    hi_w = _dot(h_hi, wr_ref[...])
    logits = hi_w[:, :LANES] + (_dot(h_lo, wr_ref[:, :LANES]) + hi_w[:, LANES:])
    scores = _sigmoid(logits)
    gidx, gates = _route(scores.T[:N_EXPERTS], (scores + rb_ref[...]).T[:N_EXPERTS])
    tm = h.shape[0]
    sub = lax.broadcasted_iota(jnp.int32, (8, 1), 0)
    onehot = (sub == gidx).astype(F32)
    tri = (lax.broadcasted_iota(jnp.int32, (tm, tm), 0)
           <= lax.broadcasted_iota(jnp.int32, (tm, tm), 1)).astype(BF16)
    prefix = _dot(onehot.astype(BF16), tri)
    lrank = jnp.sum((prefix - 1.0) * onehot, axis=0, keepdims=True)

    count = [jnp.sum(onehot[g:g + 1, :]).astype(jnp.int32) for g in range(N_GROUPS)]
    padded = [lax.shift_left(lax.shift_right_logical(c + (MOE_ALIGN - 1), _LOG2_ALIGN), _LOG2_ALIGN)
              for c in count]
    loff, acc = [], 0
    for p in padded:
        loff.append(acc)
        acc = acc + p
    lpos = lrank + functools.reduce(
        lambda a, b: a + b, [jnp.where(gidx == g, jnp.asarray(v, jnp.int32).astype(F32), 0.0)
                             for g, v in enumerate(loff)])
    infot = jnp.concatenate(gates + [gidx.astype(F32), lrank, jnp.zeros((LANES - EPG - 2, tm), F32)], axis=0)
    info_ref[...] = infot.T
    g_hi = [gt.astype(BF16).astype(F32) for gt in gates]
    g_lo = [gt - gh for gt, gh in zip(gates, g_hi)]
    gext = jnp.concatenate(g_hi + g_lo + [jnp.zeros((LANES - 2 * EPG, tm), F32)], axis=0)
    h_ext = jnp.concatenate([h_hi, gext.T.astype(BF16)], axis=-1)
    perm = (lax.broadcasted_iota(jnp.int32, (_LOC_ROWS, 1), 0).astype(F32) == lpos).astype(BF16)
    lsort[s] = _dot(perm, h_ext).astype(BF16)

    new_base = []
    for g in range(N_GROUPS):
        seg_s[s, g] = padded[g]
        new_base.append(base_s[g] + padded[g])
        base_s[g] = new_base[g]
    cum = functools.reduce(lambda a, b: a + b,
                           [jnp.where(sub == g, new_base[g], 0) for g in range(N_GROUPS)])
    cum_ref[0] = jnp.broadcast_to(cum, (8, LANES))


def _copy_pieces(length, make_copy, wait):
    done = jnp.int32(0)
    for rows in _PIECES:
        take = (length & rows) != 0

        @pl.when(take)
        def _():
            cp = make_copy(pl.multiple_of(done, MOE_ALIGN), rows)
            cp.wait() if wait else cp.start()
        done = done + jnp.where(take, rows, 0)


def _moe_sort_kernel(x_ref, a_ref, wo_ref, mod_ref, g_ref, wr_ref, rb_ref, xo_ref, info_ref, cum_ref, srt_ref,
                     lsort, zeros_v, seg_s, base_s, sem):
    step = pl.program_id(0)
    n_src = lsort.shape[0]
    n_rows = srt_ref.shape[0]

    @pl.when(step == 0)
    def _():
        for g in range(N_GROUPS):
            base_s[g] = 0
        zeros_v[...] = jnp.zeros_like(zeros_v)

    @pl.when(step < n_src // MOE_SORT_TILES)
    def _():
        for j in range(MOE_SORT_TILES):
            rows = pl.ds(j * MOE_SRC, MOE_SRC)
            _moe_sort_tile(step * MOE_SORT_TILES + j, x_ref.at[rows], a_ref.at[rows], wo_ref, mod_ref, g_ref,
                           wr_ref, rb_ref, xo_ref.at[rows], info_ref.at[rows], cum_ref.at[pl.ds(j, 1)],
                           lsort, seg_s, base_s)

    @pl.when(step == n_src // MOE_SORT_TILES)
    def _():
        starts, acc = [], jnp.int32(0)
        for g in range(N_GROUPS):
            starts.append(acc)
            acc = acc + lax.shift_left(
                lax.shift_right_logical(base_s[g] + (MOE_DST - 1), _LOG2_DST), _LOG2_DST)
        end = acc

        def all_copies(wait):
            def tile_copies(t, dst):
                src = jnp.int32(0)
                new_dst = []
                for g in range(N_GROUPS):
                    n = seg_s[t, g]
                    _copy_pieces(n, lambda off, rows, src=src, g=g: pltpu.make_async_copy(
                        lsort.at[t, pl.ds(pl.multiple_of(src + off, MOE_ALIGN), rows)],
                        srt_ref.at[pl.ds(pl.multiple_of(dst[g] + off, MOE_ALIGN), rows)], sem.at[0]), wait)
                    src = src + n
                    new_dst.append(dst[g] + n)
                return tuple(new_dst)

            dst = lax.fori_loop(0, n_src, tile_copies, tuple(starts))
            for g in range(N_GROUPS):
                gap_end = starts[g + 1] if g + 1 < N_GROUPS else end
                _copy_pieces(gap_end - dst[g], lambda off, rows, g=g: pltpu.make_async_copy(
                    zeros_v.at[pl.ds(0, rows)],
                    srt_ref.at[pl.ds(pl.multiple_of(dst[g] + off, MOE_ALIGN), rows)], sem.at[0]), wait)

            def zero_tile(r, carry):
                cp = pltpu.make_async_copy(
                    zeros_v, srt_ref.at[pl.ds(pl.multiple_of(end + r * MOE_DST, MOE_DST), MOE_DST)], sem.at[0])
                cp.wait() if wait else cp.start()
                return carry

            lax.fori_loop(0, lax.shift_right_logical(n_rows - end, _LOG2_DST), zero_tile, 0)

        all_copies(False)
        all_copies(True)


def _moe_ffn2_kernel(cum_ref, xs_ref, wg_ref, wu_ref, wd_ref, y_ref):
    _, ends = _group_tiles(cum_ref)

    @pl.when(pl.program_id(0) >= ends[-1])
    def _():
        y_ref[...] = jnp.zeros_like(y_ref)

    @pl.when(pl.program_id(0) < ends[-1])
    def _():
        xs = xs_ref[:, :D_MODEL]
        gext = xs_ref[:, D_MODEL:].astype(F32)
        acts = []
        for i in range(EPG):
            a = _dot(xs, wg_ref[i])
            u = _dot(xs, wu_ref[i])
            gate = gext[:, i:i + 1] + gext[:, EPG + i:EPG + i + 1]
            acts.append(((a * _sigmoid(a)) * u * gate).astype(BF16))
        act = jnp.concatenate(acts, axis=-1)
        y_ref[...] = _dot(act, wd_ref[...].reshape(EPG * D_EXPERT, D_MODEL)).astype(BF16)


def _segment_row(cum_ref, s, g):
    _, ends = _group_tiles(cum_ref)
    first_tile = ends[g - 1] if g > 0 else 0
    return first_tile * MOE_DST + _counts_before(cum_ref, s, g)


def _unsort_window(cum_ref, s, g, n_rows):
    return pl.multiple_of(jnp.minimum(_segment_row(cum_ref, s, g), n_rows - MOE_SRC), MOE_ALIGN)


def _moe_unsort2_kernel(*refs, final_norm):
    if final_norm:
        cum_ref, x_ref, mod_ref, info_ref, y_ref, gf_ref, o_ref = refs
    else:
        cum_ref, x_ref, mod_ref, info_ref, y_ref, o_ref = refs
    n_rows = y_ref.shape[0]
    col = lax.broadcasted_iota(jnp.int32, (1, MOE_SRC), 1).astype(F32)
    for j in range(MOE_UNSORT_TILES):
        s = pl.program_id(0) * MOE_UNSORT_TILES + j
        rows = slice(j * MOE_SRC, (j + 1) * MOE_SRC)
        inf = info_ref[rows, :]
        gid = inf[:, _INFO_GID:_INFO_GID + 1]
        lrank = inf[:, _INFO_LRANK:_INFO_LRANK + 1]
        perms, wins = [], []
        for g in range(N_GROUPS):
            win = _unsort_window(cum_ref, s, g, n_rows)
            shift = _segment_row(cum_ref, s, g) - win
            rel = jnp.where(gid == float(g), lrank + shift.astype(F32), -1.0)
            perms.append((rel == col).astype(BF16))
            wins.append(y_ref[pl.ds(win, MOE_SRC), :])
        out = x_ref[rows, :] + mod_ref[0, 5:6, :] * _dot(jnp.concatenate(perms, axis=-1),
                                                         jnp.concatenate(wins, axis=0))
        o_ref[rows, :] = _rms(out, gf_ref[...]) if final_norm else out


def _mixer_out_and_moe(x, att, w_o, mod, g, w_router, router_bias, wg, wu, wd, layer, rows_per_cond,
                       final_g=None):
    n = x.shape[0]
    n_src = n // MOE_SRC
    n_dst = (n + N_GROUPS * n_src * (MOE_ALIGN - 1)) // MOE_DST + N_GROUPS
    n_rows = n_dst * MOE_DST
    ts = MOE_SORT_TILES * MOE_SRC
    last = n // ts - 1
    row = lambda i: (jnp.minimum(i, last), 0)
    full = lambda i: (0, 0)
    x, info, cum, srt = pl.pallas_call(
        _moe_sort_kernel,
        grid=(n // ts + 1,),
        in_specs=[
            pl.BlockSpec((ts, D_MODEL), row),
            pl.BlockSpec((ts, D_MODEL), row),
            pl.BlockSpec(w_o.shape, full),
            pl.BlockSpec((1, 6, D_MODEL), (lambda i: (0, 0, 0)) if rows_per_cond is None else
                         (lambda i: (1 + jnp.minimum(i, last) // (rows_per_cond // ts), 0, 0))),
            pl.BlockSpec((1, D_MODEL), full),
            pl.BlockSpec(w_router.shape, full),
            pl.BlockSpec(router_bias.shape, full),
        ],
        out_specs=[pl.BlockSpec((ts, D_MODEL), row),
                   pl.BlockSpec((ts, LANES), row),
                   pl.BlockSpec((MOE_SORT_TILES, 8, LANES), lambda i: (jnp.minimum(i, last), 0, 0)),
                   pl.BlockSpec(memory_space=pl.ANY)],
        out_shape=[jax.ShapeDtypeStruct((n, D_MODEL), F32),
                   jax.ShapeDtypeStruct((n, LANES), F32),
                   jax.ShapeDtypeStruct((n_src, 8, LANES), jnp.int32),
                   jax.ShapeDtypeStruct((n_rows, H_EXT), BF16)],
        scratch_shapes=[pltpu.VMEM((n_src, _LOC_ROWS, H_EXT), BF16),
                        pltpu.VMEM((MOE_DST, H_EXT), BF16),
                        pltpu.SMEM((n_src, N_GROUPS), jnp.int32),
                        pltpu.SMEM((N_GROUPS,), jnp.int32),
                        pltpu.SemaphoreType.DMA((1,))],
        compiler_params=_cparams("arbitrary"),
        name="moe_sort",
    )(x, att, w_o, mod, g, w_router, router_bias)
    cum = cum[:, :N_GROUPS, 0]

    blk = lambda width: pl.BlockSpec((MOE_DST, width), lambda d, c: (d, 0))
    expert = lambda shape: pl.BlockSpec(shape, lambda d, c: (layer, _dest_tile(d, c)[0], 0, 0))
    y = pl.pallas_call(
        _moe_ffn2_kernel,
        grid_spec=pltpu.PrefetchScalarGridSpec(
            num_scalar_prefetch=1,
            grid=(n_dst,),
            in_specs=[blk(H_EXT),
                      expert((None, EPG, D_MODEL, D_EXPERT)),
                      expert((None, EPG, D_MODEL, D_EXPERT)),
                      expert((None, EPG, D_EXPERT, D_MODEL))],
            out_specs=blk(D_MODEL),
        ),
        out_shape=jax.ShapeDtypeStruct((n_rows, D_MODEL), BF16),
        compiler_params=_cparams("arbitrary"),
        name="moe_experts",
    )(cum, srt, wg, wu, wd)

    tu = MOE_UNSORT_TILES * MOE_SRC
    in_specs = [pl.BlockSpec((tu, D_MODEL), lambda i, c: (i, 0)),
                _mod_spec(None if rows_per_cond is None else rows_per_cond // tu),
                pl.BlockSpec((tu, LANES), lambda i, c: (i, 0)),
                pl.BlockSpec(y.shape, lambda i, c: (0, 0), pipeline_mode=pl.Buffered(1))]
    args = [cum, x, mod, info, y]
    if final_g is not None:
        in_specs.append(pl.BlockSpec((1, D_MODEL), lambda i, c: (0, 0)))
        args.append(final_g)
    return pl.pallas_call(
        functools.partial(_moe_unsort2_kernel, final_norm=final_g is not None),
        grid_spec=pltpu.PrefetchScalarGridSpec(
            num_scalar_prefetch=1,
            grid=(n_src // MOE_UNSORT_TILES,),
            in_specs=in_specs,
            out_specs=pl.BlockSpec((tu, D_MODEL), lambda i, c: (i, 0)),
        ),
        out_shape=jax.ShapeDtypeStruct((n, D_MODEL), F32),
        compiler_params=_cparams("parallel"),
        name="moe_unsort",
    )(*args)


def _rope_tables(n_lat):
    t = np.arange(n_lat)
    n_freq = MLA_ROPE // 4
    inv_freq = jnp.asarray(ROPE_THETA, F32) ** (-jnp.arange(n_freq, dtype=F32) / n_freq)
    ar = jnp.asarray(t // GRID_W, F32)[:, None] * inv_freq
    ac = jnp.asarray(t % GRID_W, F32)[:, None] * inv_freq
    cos = jnp.concatenate([jnp.cos(ar), jnp.cos(ar), jnp.cos(ac), jnp.cos(ac)], axis=-1)
    sin = jnp.concatenate([-jnp.sin(ar), jnp.sin(ar), -jnp.sin(ac), jnp.sin(ac)], axis=-1)
    return (jnp.tile(cos, (1, MLA_HEADS)), jnp.tile(sin, (1, MLA_HEADS)),
            jnp.tile(cos, (1, LANES // MLA_ROPE)), jnp.tile(sin, (1, LANES // MLA_ROPE)))


_ROPE_SWAP = np.concatenate([np.arange(8, 16), np.arange(0, 8), np.arange(24, 32), np.arange(16, 24)])


def _mla_weights(w_in, w_uq, w_ukv):
    o = MLA_Q_LORA + MLA_KV_LORA
    kr = w_in[:, o:]
    rep = LANES // MLA_ROPE
    w_in_x = jnp.concatenate([w_in[:, :o], jnp.tile(kr, (1, rep)), jnp.tile(kr[:, _ROPE_SWAP], (1, rep))],
                             axis=-1).astype(BF16)
    uq = w_uq.reshape(MLA_Q_LORA, MLA_HEADS, MLA_NOPE + MLA_ROPE)
    q_nope = uq[:, :, :MLA_NOPE].reshape(MLA_Q_LORA, -1)
    q_rope = uq[:, :, MLA_NOPE:]
    w_uq_x = jnp.concatenate([q_nope, q_rope.reshape(MLA_Q_LORA, -1),
                              q_rope[:, :, _ROPE_SWAP].reshape(MLA_Q_LORA, -1)], axis=-1).astype(BF16)
    ukv = w_ukv.reshape(MLA_KV_LORA, MLA_HEADS, MLA_NOPE + MLA_V)
    w_uk = ukv[:, :, :MLA_NOPE].reshape(MLA_KV_LORA, -1).astype(BF16)
    w_uv = ukv[:, :, MLA_NOPE:].reshape(MLA_KV_LORA, -1).astype(BF16)
    return w_in_x, w_uq_x, w_uk, w_uv


def kernel(x_prompt, x_sample, cache_mla_ckv, cache_mla_krope, cache_nat_k, cache_nat_v, c, c_ctx,
           w_ada, b_ada, norm_mix, norm_ffn, norm_final, mla_w_in, mla_q_norm, mla_w_uq, mla_kv_norm,
           mla_w_ukv, mla_w_o, nat_w_qkv, nat_rpb, nat_w_o, w_router, router_bias,
           moe_w_gate, moe_w_up, moe_w_down):
    B, S, D = x_prompt.shape
    Bd, Sd, _ = x_sample.shape
    assert D == D_MODEL and Bd + 1 <= 8
    tm_c, tm_s = 512, 512

    xc = x_prompt.reshape(B * S, D)
    xs = x_sample.reshape(Bd * Sd, D)
    cond8 = jnp.concatenate([c_ctx[None, :], c, jnp.zeros((8 - 1 - Bd, D), F32)], axis=0)
    mod_all = _ada_modulation(cond8, w_ada, b_ada).reshape(DEPTH, 8, 6, D)

    rope_tabs = _rope_tables(Sd)
    wr = jnp.pad(w_router, ((0, 0), (0, LANES - N_EXPERTS)))
    wr_hi = wr.astype(BF16)
    wr = jnp.concatenate([wr_hi, (wr - wr_hi.astype(F32)).astype(BF16)], axis=1)
    rb = jnp.pad(router_bias, (0, LANES - N_EXPERTS)).reshape(1, LANES)
    wg = moe_w_gate.astype(BF16)
    wu = moe_w_up.astype(BF16)
    wd = moe_w_down.astype(BF16)

    new_ckv, new_krope = [], []
    k_buf = v_buf = None
    for layer in range(DEPTH):
        mod = mod_all[layer]
        g_mix = norm_mix[layer][None, :]
        j = layer // 2
        if layer % 2 == 0:
            w_in_x, w_uq_x, w_uk, w_uv = _mla_weights(mla_w_in[j], mla_w_uq[j], mla_w_ukv[j])
            w_in_c = w_in_x[:, :MLA_Q_LORA + MLA_KV_LORA + LANES]
            w_uq_c = w_uq_x[:, :MLA_HEADS * (MLA_NOPE + MLA_ROPE)]
            qg = mla_q_norm[j][None, :]
            kvg = mla_kv_norm[j][None, :]
            w_o = mla_w_o[j].astype(BF16)
            qn, qr, ckv, kr, kr4 = _premix_mla(xc, mod, g_mix, w_in_c, qg, w_uq_c, kvg, None, None, tm_c)
            new_ckv.append(ckv.reshape(B, S, MLA_KV_LORA))
            new_krope.append(kr.reshape(B, S, MLA_ROPE))
            att_c = _mla_attention(qn, qr, ckv, kr4, None, None, w_uk, w_uv, B, S)
            qn, qr, ckv, kr, kr4 = _premix_mla(xs, mod, g_mix, w_in_x, qg, w_uq_x, kvg, rope_tabs, Sd, tm_s)
            cache_kr4 = jnp.tile(cache_mla_krope[:, j], (1, 1, LANES // MLA_ROPE))
            att_s = _mla_attention(qn, qr, ckv, kr4, cache_mla_ckv[:, j], cache_kr4, w_uk, w_uv, Bd, 512)
        else:
            w_qkv = nat_w_qkv[j].astype(BF16)
            w_o = nat_w_o[j].astype(BF16)
            q, k_buf, v_buf = _premix_nat_cache(xc, mod, g_mix, w_qkv, k_buf, v_buf, j, DEPTH // 2, B, tm_c)
            att_c = _dense_attention(q, k_buf, v_buf, j)
            q, k, v = _premix_nat(xs, mod, g_mix, w_qkv, Sd, tm_s, BF16)
            bias = _nat_bias_pairs(nat_rpb[j], Sd // GRID_W)
            att_s = _nat_attention(q, k, v,
                                   cache_nat_k[:, j].reshape(Bd, -1, D), cache_nat_v[:, j].reshape(Bd, -1, D),
                                   bias, Bd)
        g_ffn = norm_ffn[layer][None, :]
        final_g = norm_final[None, :] if layer == DEPTH - 1 else None
        xc = _mixer_out_and_moe(xc, att_c, w_o, mod, g_ffn, wr, rb, wg, wu, wd, layer, None, final_g)
        xs = _mixer_out_and_moe(xs, att_s, w_o, mod, g_ffn, wr, rb, wg, wu, wd, layer, Sd, final_g)

    y_prompt = xc.reshape(B, S, D)
    y_sample = xs.reshape(Bd, Sd, D)
    kv_shape = (B, DEPTH // 2, S, NAT_HEADS, NAT_DH)
    return (y_prompt, y_sample, jnp.stack(new_ckv, axis=1), jnp.stack(new_krope, axis=1),
            k_buf.reshape(kv_shape), v_buf.reshape(kv_shape))
```

```python
import functools

import numpy as np
import jax
import jax.numpy as jnp
from jax import lax
from jax.experimental import pallas as pl
from jax.experimental.pallas import tpu as pltpu

F32 = jnp.float32
BF16 = jnp.bfloat16

D_MODEL = 1024
DEPTH = 4
GRID_W = 64
LANES = 128
MLA_HEADS = 16
MLA_NOPE = 64
MLA_ROPE = 32
MLA_V = 64
MLA_Q_LORA = 384
MLA_KV_LORA = 256
MLA_SCALE = (MLA_NOPE + MLA_ROPE) ** -0.5
ROPE_THETA = 10000.0
NAT_HEADS = 16
NAT_DH = 64
NAT_SCALE = NAT_DH ** -0.5
WIN_H = 8
WIN_W = 16
NAT_QROWS = 4
NAT_KROWS = NAT_QROWS + WIN_H
N_EXPERTS = 16
N_GROUPS = 4
EPG = N_EXPERTS // N_GROUPS
D_EXPERT = 256
NORM_EPS = 1e-6
NEG_INF = -1e30

HEAD_PAIRS = MLA_HEADS // 2
VMEM_LIMIT = 56 * 1024 * 1024


def _cparams(*sem):
    return pltpu.CompilerParams(dimension_semantics=sem, vmem_limit_bytes=VMEM_LIMIT)


def _sigmoid(x):
    return 1.0 / (1.0 + jnp.exp(-x))


def _rms(x, g):
    ms = jnp.mean(x * x, axis=-1, keepdims=True)
    return x * lax.rsqrt(ms + NORM_EPS) * g


def _dot(a, b):
    return jnp.dot(a, b, preferred_element_type=F32)


def _dot_nt(a, b):
    return lax.dot_general(a, b, (((1,), (1,)), ((), ())), preferred_element_type=F32)


def _lane_mask(width, idx, dtype):
    lane = lax.broadcasted_iota(jnp.int32, (1, LANES), 1)
    return ((lane >= idx * width) & (lane < (idx + 1) * width)).astype(dtype)


def _ada_kernel(cond_ref, w_ref, b_ref, o_ref):
    c = cond_ref[...]
    s = (c * _sigmoid(c)).astype(BF16)
    o_ref[0] = _dot(s, w_ref[0].astype(BF16)) + b_ref[0]


def _ada_modulation(cond8, w_ada, b_ada):
    n_chunk = 6
    return pl.pallas_call(
        _ada_kernel,
        grid=(DEPTH, n_chunk),
        in_specs=[
            pl.BlockSpec((8, D_MODEL), lambda l, j: (0, 0)),
            pl.BlockSpec((1, D_MODEL, D_MODEL), lambda l, j: (l, 0, j)),
            pl.BlockSpec((1, 1, D_MODEL), lambda l, j: (l, 0, j)),
        ],
        out_specs=pl.BlockSpec((1, 8, D_MODEL), lambda l, j: (l, 0, j)),
        out_shape=jax.ShapeDtypeStruct((DEPTH, 8, 6 * D_MODEL), F32),
        compiler_params=_cparams("parallel", "parallel"),
        name="ada_modulation",
    )(cond8, w_ada, b_ada.reshape(DEPTH, 1, 6 * D_MODEL))


def _mod_spec(rows_per_cond):
    if rows_per_cond is None:
        return pl.BlockSpec((1, 6, D_MODEL), lambda i, *_: (0, 0, 0))
    return pl.BlockSpec((1, 6, D_MODEL), lambda i, *_: (1 + i // rows_per_cond, 0, 0))


def _premix_mla_kernel(*refs, rope, cache):
    if rope:
        (x_ref, mod_ref, g_ref, w_in_ref, qg_ref, w_uq_ref, kvg_ref,
         cosq_ref, sinq_ref, cosk_ref, sink_ref,
         qn_ref, qr_ref, ckv_ref, kr_ref, kr4_ref) = refs
    elif cache == "update":
        (x_ref, mod_ref, g_ref, w_in_ref, qg_ref, w_uq_ref, kvg_ref, _, _,
         qn_ref, qr_ref, ckv_ref, kr_ref, kr4_ref) = refs
    else:
        (x_ref, mod_ref, g_ref, w_in_ref, qg_ref, w_uq_ref, kvg_ref,
         qn_ref, qr_ref, ckv_ref, kr_ref, kr4_ref) = refs
    h = _rms(x_ref[...], g_ref[...]) * (1.0 + mod_ref[0, 1:2, :]) + mod_ref[0, 0:1, :]
    lat = _dot(h.astype(BF16), w_in_ref[...])
    c_q = lat[:, :MLA_Q_LORA]
    c_kv = lat[:, MLA_Q_LORA:MLA_Q_LORA + MLA_KV_LORA]
    o = MLA_Q_LORA + MLA_KV_LORA
    kr4 = lat[:, o:o + LANES]
    q = _dot(_rms(c_q, qg_ref[...]).astype(BF16), w_uq_ref[...])
    n_nope = MLA_HEADS * MLA_NOPE
    n_rope = MLA_HEADS * MLA_ROPE
    qr = q[:, n_nope:n_nope + n_rope]
    if rope:
        qr = qr * cosq_ref[...] + q[:, n_nope + n_rope:] * sinq_ref[...]
        kr4 = kr4 * cosk_ref[...] + lat[:, o + LANES:o + 2 * LANES] * sink_ref[...]
    qn_ref[...] = q[:, :n_nope].astype(BF16)
    qr_ref[...] = qr.astype(BF16)
    kr4_ref[...] = kr4.astype(BF16)
    ckv = _rms(c_kv, kvg_ref[...])
    if cache is None:
        ckv_ref[...] = ckv
        kr_ref[...] = kr4[:, :MLA_ROPE]
    else:
        nb, n_slots, s, _ = ckv_ref.shape
        ckv_ref[:, 0] = ckv.reshape(nb, s, MLA_KV_LORA)
        kr_ref[:, 0] = kr4[:, :MLA_ROPE].reshape(nb, s, MLA_ROPE)
        for slot in range(1, n_slots):
            ckv_ref[:, slot] = jnp.zeros((nb, s, MLA_KV_LORA), F32)
            kr_ref[:, slot] = jnp.zeros((nb, s, MLA_ROPE), F32)


def _premix_mla(x, mod, g, w_in, qg, w_uq, kvg, rope_tabs, rows_per_cond, tm, cache=None):
    n = x.shape[0]
    rope = rope_tabs is not None
    row = lambda i: (i, 0)
    full = lambda i: (0, 0)
    in_specs = [
        pl.BlockSpec((tm, D_MODEL), row),
        _mod_spec(None if rows_per_cond is None else rows_per_cond // tm),
        pl.BlockSpec((1, D_MODEL), full),
        pl.BlockSpec(w_in.shape, full),
        pl.BlockSpec((1, MLA_Q_LORA), full),
        pl.BlockSpec(w_uq.shape, full),
        pl.BlockSpec((1, MLA_KV_LORA), full),
    ]
    args = [x, mod, g, w_in, qg, w_uq, kvg]
    if rope:
        nblk = rope_tabs[0].shape[0] // tm
        pos = lambda i: (i % nblk, 0)
        for t in rope_tabs:
            in_specs.append(pl.BlockSpec((tm, t.shape[1]), pos))
            args.append(t)
    widths = (MLA_HEADS * MLA_NOPE, MLA_HEADS * MLA_ROPE, MLA_KV_LORA, MLA_ROPE, LANES)
    dtypes = (BF16, BF16, F32, F32, BF16)
    out_specs = [pl.BlockSpec((tm, w), row) for w in widths]
    out_shape = [jax.ShapeDtypeStruct((n, w), d) for w, d in zip(widths, dtypes)]
    mode, aliases = None, {}
    if cache is not None:
        assert not rope
        ckv_buf, kr_buf, slot, n_slots, n_batch = cache
        s = n // n_batch
        nb = tm // s
        mode = "first" if ckv_buf is None else "update"
        if mode == "first":
            assert slot == 0
            spec = lambda w: pl.BlockSpec((nb, n_slots, s, w), lambda i: (i, 0, 0, 0))
        else:
            spec = lambda w: pl.BlockSpec((nb, 1, s, w), lambda i: (i, slot, 0, 0))
            aliases = {len(args): 2, len(args) + 1: 3}
            in_specs += [pl.BlockSpec(memory_space=pl.ANY)] * 2
            args += [ckv_buf, kr_buf]
        for idx in (2, 3):
            out_specs[idx] = spec(widths[idx])
            out_shape[idx] = jax.ShapeDtypeStruct((n_batch, n_slots, s, widths[idx]), F32)
    return pl.pallas_call(
        functools.partial(_premix_mla_kernel, rope=rope, cache=mode),
        grid=(n // tm,),
        in_specs=in_specs,
        out_specs=out_specs,
        out_shape=out_shape,
        input_output_aliases=aliases,
        compiler_params=_cparams("parallel"),
        name="premix_mla_rope" if rope else "premix_mla",
    )(*args)


LOG2E = 1.4426950408889634


def _softmax_pv(s_list, v_list, scale=1.0):
    m = functools.reduce(jnp.maximum, [jnp.max(s, axis=-1, keepdims=True) for s in s_list])
    e_list = [jnp.exp2((s - m) * (scale * LOG2E)) for s in s_list]
    l = functools.reduce(lambda a, b: a + b, [jnp.sum(e, axis=-1, keepdims=True) for e in e_list])
    o = functools.reduce(lambda a, b: a + b,
                         [_dot(e.astype(BF16), v) for e, v in zip(e_list, v_list)])
    return o / l


def _stack_heads(q, masks):
    return jnp.concatenate([q * m for m in masks], axis=0)


def _unstack_heads(o2, width):
    tq = o2.shape[0] // 2
    lane = lax.broadcasted_iota(jnp.int32, (1, LANES), 1)
    return jnp.where(lane < width, o2[:tq], o2[tq:])


MLA_ATTN_BATCHES = 2


def _mla_attn_kernel(*refs, cached, nb):
    if cached:
        (qn_ref, qr_ref, ckv_ref, kr4_ref, cckv_ref, ckr4_ref, w_uk_ref, w_uv_ref,
         o_ref, kn_s, v_s, kr_s) = refs
    else:
        (qn_ref, qr_ref, ckv_ref, kr4_ref, w_uk_ref, w_uv_ref, o_ref, kn_s, v_s, kr_s) = refs
    from_slot = len(ckv_ref.shape) == 3
    t_own = ckv_ref.shape[1] if from_slot else ckv_ref.shape[0] // nb

    for b in range(nb):
        own = slice(b * t_own, (b + 1) * t_own)

        def fill_kv():
            c = (ckv_ref[b] if from_slot else ckv_ref[own, :]).astype(BF16)
            kn_s[0:t_own, :] = _dot(c, w_uk_ref[...]).astype(BF16)
            v_s[0:t_own, :] = _dot(c, w_uv_ref[...]).astype(BF16)
            kr_s[0:t_own, :] = kr4_ref[own, :]
            if cached:
                cc = cckv_ref[0].astype(BF16)
                kn_s[t_own:, :] = _dot(cc, w_uk_ref[...]).astype(BF16)
                v_s[t_own:, :] = _dot(cc, w_uv_ref[...]).astype(BF16)
                kr_s[t_own:, :] = ckr4_ref[0].astype(BF16)

        if cached:
            pl.when(pl.program_id(1) == 0)(fill_kv)
            qrows = slice(None)
        else:
            fill_kv()
            qrows = own

        kr4 = kr_s[...]
        for p in range(HEAD_PAIRS):
            sl = slice(p * LANES, (p + 1) * LANES)
            qn = qn_ref[qrows, sl]
            qr = qr_ref[qrows, (p // 2) * LANES:(p // 2 + 1) * LANES]
            k_cat = jnp.concatenate([kn_s[:, sl], kr4], axis=-1)
            q_cat = jnp.concatenate(
                [_stack_heads(qn, [_lane_mask(MLA_NOPE, i, BF16) for i in range(2)]),
                 _stack_heads(qr, [_lane_mask(MLA_ROPE, (2 * p + i) % 4, BF16) for i in range(2)])], axis=-1)
            o2 = _softmax_pv([_dot_nt(q_cat, k_cat)], [v_s[:, sl]], MLA_SCALE)
            o_ref[qrows, sl] = _unstack_heads(o2, MLA_V).astype(BF16)


def _mla_attention(qn, qr, ckv, kr4, cache_ckv, cache_kr4, w_uk, w_uv, n_batch, tq, ckv_slot=None):
    n = qn.shape[0]
    s_own = n // n_batch
    nq = s_own // tq
    cached = cache_ckv is not None
    t_all = s_own + (cache_ckv.shape[1] if cached else 0)
    nb = 1 if cached else MLA_ATTN_BATCHES
    assert cached or nq == 1
    tq = tq * nb
    qrow = lambda b, j: (b * nq + j, 0)
    own = lambda b, j: (b, 0)
    full = lambda b, j: (0, 0)
    in_specs = [
        pl.BlockSpec((tq, qn.shape[1]), qrow),
        pl.BlockSpec((tq, qr.shape[1]), qrow),
        pl.BlockSpec((nb * s_own, MLA_KV_LORA), own) if ckv_slot is None else
        pl.BlockSpec((nb, None, s_own, MLA_KV_LORA), lambda b, j: (b, ckv_slot, 0, 0)),
        pl.BlockSpec((nb * s_own, LANES), own),
    ]
    args = [qn, qr, ckv, kr4]
    if cached:
        in_specs += [pl.BlockSpec((1,) + cache_ckv.shape[1:], lambda b, j: (b, 0, 0)),
                     pl.BlockSpec((1,) + cache_kr4.shape[1:], lambda b, j: (b, 0, 0))]
        args += [cache_ckv, cache_kr4]
    in_specs += [pl.BlockSpec(w_uk.shape, full), pl.BlockSpec(w_uv.shape, full)]
    args += [w_uk, w_uv]
    return pl.pallas_call(
        functools.partial(_mla_attn_kernel, cached=cached, nb=nb),
        grid=(n_batch // nb, nq),
        in_specs=in_specs,
        out_specs=pl.BlockSpec((tq, D_MODEL), qrow),
        out_shape=jax.ShapeDtypeStruct((n, D_MODEL), BF16),
        scratch_shapes=[pltpu.VMEM((t_all, D_MODEL), BF16), pltpu.VMEM((t_all, D_MODEL), BF16),
                        pltpu.VMEM((t_all, LANES), BF16)],
        compiler_params=_cparams("parallel", "arbitrary"),
        name="mla_attention_cached" if cached else "mla_attention",
    )(*args)


def _premix_nat_kernel(x_ref, mod_ref, g_ref, w_ref, q_ref, k_ref, v_ref):
    h = _rms(x_ref[...], g_ref[...]) * (1.0 + mod_ref[0, 1:2, :]) + mod_ref[0, 0:1, :]
    qkv = _dot(h.astype(BF16), w_ref[...])
    q_ref[...] = qkv[:, :D_MODEL].astype(q_ref.dtype)
    k_ref[...] = qkv[:, D_MODEL:2 * D_MODEL].astype(k_ref.dtype)
    v_ref[...] = qkv[:, 2 * D_MODEL:].astype(v_ref.dtype)


def _premix_nat(x, mod, g, w_qkv, rows_per_cond, tm, kv_dtype):
    n = x.shape[0]
    row = lambda i: (i, 0)
    full = lambda i: (0, 0)
    return pl.pallas_call(
        _premix_nat_kernel,
        grid=(n // tm,),
        in_specs=[
            pl.BlockSpec((tm, D_MODEL), row),
            _mod_spec(None if rows_per_cond is None else rows_per_cond // tm),
            pl.BlockSpec((1, D_MODEL), full),
            pl.BlockSpec(w_qkv.shape, full),
        ],
        out_specs=[pl.BlockSpec((tm, D_MODEL), row)] * 3,
        out_shape=[jax.ShapeDtypeStruct((n, D_MODEL), BF16),
                   jax.ShapeDtypeStruct((n, D_MODEL), kv_dtype),
                   jax.ShapeDtypeStruct((n, D_MODEL), kv_dtype)],
        compiler_params=_cparams("parallel"),
        name="premix_nat",
    )(x, mod, g, w_qkv)


def _premix_nat_cache_kernel(*refs, first):
    if first:
        x_ref, mod_ref, g_ref, w_ref, q_ref, k_ref, v_ref = refs
    else:
        x_ref, mod_ref, g_ref, w_ref, _, _, q_ref, k_ref, v_ref = refs
    h = _rms(x_ref[...], g_ref[...]) * (1.0 + mod_ref[0, 1:2, :]) + mod_ref[0, 0:1, :]
    qkv = _dot(h.astype(BF16), w_ref[...])
    q_ref[...] = qkv[:, :D_MODEL].astype(BF16)
    nb, n_slots, s, _ = k_ref.shape
    k_ref[:, 0] = qkv[:, D_MODEL:2 * D_MODEL].reshape(nb, s, D_MODEL)
    v_ref[:, 0] = qkv[:, 2 * D_MODEL:].reshape(nb, s, D_MODEL)
    for slot in range(1, n_slots):
        k_ref[:, slot] = jnp.zeros((nb, s, D_MODEL), F32)
        v_ref[:, slot] = jnp.zeros((nb, s, D_MODEL), F32)


def _premix_nat_cache(x, mod, g, w_qkv, k_buf, v_buf, slot, n_slots, n_batch, tm):
    n = x.shape[0]
    s = n // n_batch
    nb = tm // s
    first = k_buf is None
    row = lambda i: (i, 0)
    full = lambda i: (0, 0)
    in_specs = [pl.BlockSpec((tm, D_MODEL), row), _mod_spec(None), pl.BlockSpec((1, D_MODEL), full),
                pl.BlockSpec(w_qkv.shape, full)]
    args = [x, mod, g, w_qkv]
    if first:
        assert slot == 0
        kv_spec = pl.BlockSpec((nb, n_slots, s, D_MODEL), lambda i: (i, 0, 0, 0))
        aliases = {}
    else:
        in_specs += [pl.BlockSpec(memory_space=pl.ANY)] * 2
        args += [k_buf, v_buf]
        kv_spec = pl.BlockSpec((nb, 1, s, D_MODEL), lambda i: (i, slot, 0, 0))
        aliases = {4: 1, 5: 2}
    kv_shape = jax.ShapeDtypeStruct((n_batch, n_slots, s, D_MODEL), F32)
    return pl.pallas_call(
        functools.partial(_premix_nat_cache_kernel, first=first),
        grid=(n // tm,),
        in_specs=in_specs,
        out_specs=[pl.BlockSpec((tm, D_MODEL), row), kv_spec, kv_spec],
        out_shape=[jax.ShapeDtypeStruct((n, D_MODEL), BF16), kv_shape, kv_shape],
        input_output_aliases=aliases,
        compiler_params=_cparams("parallel"),
        name="premix_nat_cache",
    )(*args)


DENSE_ATTN_BATCHES = 4


def _dense_attn_kernel(q_ref, k_ref, v_ref, o_ref):
    nb, s, _ = k_ref.shape
    for b in range(nb):
        rows = slice(b * s, (b + 1) * s)
        for p in range(HEAD_PAIRS):
            sl = slice(p * LANES, (p + 1) * LANES)
            q2 = _stack_heads(q_ref[rows, sl], [_lane_mask(NAT_DH, i, BF16) for i in range(2)])
            k = k_ref[b, :, sl].astype(BF16)
            v = v_ref[b, :, sl].astype(BF16)
            o2 = _softmax_pv([_dot_nt(q2, k)], [v], NAT_SCALE)
            o_ref[rows, sl] = _unstack_heads(o2, NAT_DH).astype(BF16)


def _dense_attention(q, k_buf, v_buf, slot):
    n = q.shape[0]
    n_batch, _, s, _ = k_buf.shape
    nb = DENSE_ATTN_BATCHES
    blk = pl.BlockSpec((nb * s, D_MODEL), lambda b: (b, 0))
    kv = pl.BlockSpec((nb, None, s, D_MODEL), lambda b: (b, slot, 0, 0))
    return pl.pallas_call(
        _dense_attn_kernel,
        grid=(n_batch // nb,),
        in_specs=[blk, kv, kv],
        out_specs=blk,
        out_shape=jax.ShapeDtypeStruct((n, D_MODEL), BF16),
        compiler_params=_cparams("parallel"),
        name="dense_attention",
    )(q, k_buf, v_buf)


_NAT_QBLK = NAT_QROWS * GRID_W
_NAT_KBLK = NAT_KROWS * GRID_W


def _nat_block_plan(rows):
    assert rows % NAT_QROWS == 0 and rows >= NAT_KROWS and NAT_KROWS % 2 == 0
    plan, variants = [], []
    for r0 in range(0, rows, NAT_QROWS):
        ks = min(max(r0 - WIN_H // 2, 0), rows - NAT_KROWS)
        r = r0 + np.arange(NAT_QROWS)
        kr = ks + np.arange(NAT_KROWS)
        rs = np.clip(r - WIN_H // 2, 0, rows - WIN_H)
        valid_row = (kr[None, :] >= rs[:, None]) & (kr[None, :] < rs[:, None] + WIN_H)
        d0 = ks - r + (WIN_H - 1)
        for vi, (d0_v, valid_v) in enumerate(variants):
            if np.array_equal(d0, d0_v) and np.array_equal(valid_row, valid_v):
                break
        else:
            vi = len(variants)
            variants.append((d0, valid_row))
        plan.append((ks, vi))
    return plan, variants


def _nat_bias_row_range(variants):
    lo = min(int(d0.min()) for d0, _ in variants)
    hi = max(int(d0.max()) for d0, _ in variants) + NAT_KROWS
    return lo, hi


def _nat_bias_pairs(rpb, rows):
    n_heads, n_dr, n_dc = rpb.shape
    _, variants = _nat_block_plan(rows)
    lo, hi = _nat_bias_row_range(variants)
    c = np.arange(GRID_W)
    cs = np.clip(c - WIN_W // 2, 0, GRID_W - WIN_W)
    valid_col = (c[None, :] >= cs[:, None]) & (c[None, :] < cs[:, None] + WIN_W)
    d_col = c[None, :] - c[:, None] + (WIN_W - 1)
    sel = (d_col[None] == np.arange(n_dc)[:, None, None]) & valid_col[None]
    toep = jnp.einsum('hdj,jck->hdck', rpb, jnp.asarray(sel, F32), precision=lax.Precision.HIGHEST)
    toep = jnp.where(jnp.asarray(valid_col)[None, None], toep, NEG_INF)
    toep = jnp.pad(toep, ((0, 0), (max(-lo, 0), max(hi + 1 - n_dr, 0)), (0, 0), (0, 0)),
                   constant_values=NEG_INF)
    toep = toep[:, max(lo, 0):]
    return jnp.concatenate([toep[:, :-1], toep[:, 1:]], axis=-1)


def _nat_block_bias(tp_ref, head, d0, valid_row, row_lo):
    neg = jnp.full((GRID_W, 2 * GRID_W), NEG_INF, F32)
    left = lax.broadcasted_iota(jnp.int32, (1, 2 * GRID_W), 1) < GRID_W
    rows_out = []
    for dr in range(NAT_QROWS):
        pieces = []
        for a in range(0, NAT_KROWS, 2):
            ok0, ok1 = bool(valid_row[dr, a]), bool(valid_row[dr, a + 1])
            if not (ok0 or ok1):
                pieces.append(neg)
                continue
            piece = tp_ref[head, int(d0[dr]) + a - row_lo]
            if ok0 and not ok1:
                piece = jnp.where(left, piece, NEG_INF)
            elif ok1 and not ok0:
                piece = jnp.where(left, NEG_INF, piece)
            pieces.append(piece)
        rows_out.append(jnp.concatenate(pieces, axis=-1))
    return jnp.concatenate(rows_out, axis=0)


NAT_ATTN_BATCHES = 2


def _nat_attn_kernel(q_ref, k_ref, v_ref, kc_ref, vc_ref, tp_ref, o_ref, *, plan, variants, row_lo):
    nb = kc_ref.shape[0]
    s = q_ref.shape[0] // nb
    assert NAT_SCALE == 2.0 ** round(np.log2(NAT_SCALE))
    masks = [_lane_mask(NAT_DH, i, BF16) * NAT_SCALE for i in range(2)]
    for bi, (ks, var) in enumerate(plan):
        bias2 = jnp.concatenate([_nat_block_bias(tp_ref, i, *variants[var], row_lo) for i in range(2)], axis=0)
        for b in range(nb):
            rows = slice(b * s + bi * _NAT_QBLK, b * s + (bi + 1) * _NAT_QBLK)
            keys = slice(b * s + ks * GRID_W, b * s + ks * GRID_W + _NAT_KBLK)
            q2 = _stack_heads(q_ref[rows, :], masks)
            kc = kc_ref[b].astype(BF16)
            vc = vc_ref[b].astype(BF16)
            o2 = _softmax_pv([_dot_nt(q2, k_ref[keys, :]) + bias2, _dot_nt(q2, kc)], [v_ref[keys, :], vc])
            o_ref[rows, :] = _unstack_heads(o2, NAT_DH).astype(BF16)


def _nat_attention(q, k, v, cache_k, cache_v, bias, n_batch):
    n = q.shape[0]
    s = n // n_batch
    plan, variants = _nat_block_plan(s // GRID_W)
    row_lo, _ = _nat_bias_row_range(variants)
    nb = NAT_ATTN_BATCHES
    own = pl.BlockSpec((nb * s, LANES), lambda p, b: (b, p))
    cache = pl.BlockSpec((nb, cache_k.shape[1], LANES), lambda p, b: (b, 0, p))
    return pl.pallas_call(
        functools.partial(_nat_attn_kernel, plan=plan, variants=variants, row_lo=row_lo),
        grid=(HEAD_PAIRS, n_batch // nb),
        in_specs=[own, own, own, cache, cache,
                  pl.BlockSpec((2,) + bias.shape[1:], lambda p, b: (p, 0, 0, 0))],
        out_specs=own,
        out_shape=jax.ShapeDtypeStruct((n, D_MODEL), BF16),
        compiler_params=_cparams("parallel", "parallel"),
        name="nat_attention",
    )(q, k, v, cache_k, cache_v, bias)


def _top2_sum(a, b, c, d):
    hi1, lo1 = jnp.maximum(a, b), jnp.minimum(a, b)
    hi2, lo2 = jnp.maximum(c, d), jnp.minimum(c, d)
    return jnp.maximum(hi1, hi2) + jnp.maximum(jnp.minimum(hi1, hi2), jnp.maximum(lo1, lo2))


def _route(scores, biased):
    sc = [scores[e:e + 1, :] for e in range(N_EXPERTS)]
    bs = [biased[e:e + 1, :] for e in range(N_EXPERTS)]
    gscore = [_top2_sum(*bs[EPG * g:EPG * (g + 1)]) for g in range(N_GROUPS)]
    best, gidx = gscore[0], jnp.zeros_like(gscore[0], dtype=jnp.int32)
    for g in range(1, N_GROUPS):
        better = gscore[g] > best
        gidx = jnp.where(better, g, gidx)
        best = jnp.where(better, gscore[g], best)
    cb = [functools.reduce(lambda a, b: a + b,
                           [jnp.where(gidx == g, bs[EPG * g + i], 0.0) for g in range(N_GROUPS)])
          for i in range(EPG)]
    cs = [functools.reduce(lambda a, b: a + b,
                           [jnp.where(gidx == g, sc[EPG * g + i], 0.0) for g in range(N_GROUPS)])
          for i in range(EPG)]
    b1, i1 = cb[0], jnp.zeros_like(gidx)
    for i in range(1, EPG):
        better = cb[i] > b1
        i1 = jnp.where(better, i, i1)
        b1 = jnp.where(better, cb[i], b1)
    b2, i2 = jnp.full_like(b1, -jnp.inf), jnp.full_like(i1, -1)
    for i in range(EPG):
        better = (i1 != i) & (cb[i] > b2)
        i2 = jnp.where(better, i, i2)
        b2 = jnp.where(better, cb[i], b2)
    sel = [(i1 == i) | (i2 == i) for i in range(EPG)]
    w = [jnp.where(sel[i], cs[i], 0.0) for i in range(EPG)]
    tot = w[0] + w[1] + w[2] + w[3]
    return gidx, [w[i] / tot for i in range(EPG)]


MOE_SRC = 256
MOE_DST = 512
MOE_SORT_TILES = 2
MOE_UNSORT_TILES = 4
_LOG2_DST = MOE_DST.bit_length() - 1
assert MOE_DST == 1 << _LOG2_DST and MOE_SRC <= MOE_DST <= 2 * MOE_SRC
_INFO_GID = EPG


def _group_tiles(cum_ref):
    n_src = cum_ref.shape[0]
    tot = [cum_ref[n_src - 1, g] for g in range(N_GROUPS)]
    ends, acc = [], 0
    for t in tot:
        acc = acc + lax.shift_right_logical(t + (MOE_DST - 1), _LOG2_DST)
        ends.append(acc)
    return tot, ends


def _dest_tile(d, cum_ref):
    tot, ends = _group_tiles(cum_ref)
    g = ((d >= ends[0]).astype(jnp.int32) + (d >= ends[1]).astype(jnp.int32)
         + (d >= ends[2]).astype(jnp.int32))
    first = jnp.where(g == 0, 0, jnp.where(g == 1, ends[0], jnp.where(g == 2, ends[1], ends[2])))
    tot_g = jnp.where(g == 0, tot[0], jnp.where(g == 1, tot[1], jnp.where(g == 2, tot[2], tot[3])))
    k0 = (d - first) * MOE_DST
    n_valid = jnp.clip(tot_g - k0, 0, MOE_DST)
    return g, k0, n_valid


def _counts_before(cum_ref, s, g):
    return jnp.where(s > 0, cum_ref[jnp.maximum(s - 1, 0), g], 0)


MOE_ALIGN = 16
_LOG2_ALIGN = MOE_ALIGN.bit_length() - 1
_LOC_ROWS = MOE_SRC + N_GROUPS * MOE_ALIGN
_INFO_LRANK = EPG + 1
H_EXT = D_MODEL + LANES
_PIECES = [MOE_SRC >> i for i in range((MOE_SRC // MOE_ALIGN).bit_length())]


def _moe_sort_tile(s, x_ref, a_ref, wo_ref, mod_ref, g_ref, wr_ref, rb_ref, xo_ref, info_ref, cum_ref,
                   lsort, seg_s, base_s):
    x = x_ref[...] + mod_ref[0, 2:3, :] * _dot(a_ref[...], wo_ref[...])
    xo_ref[...] = x
    h = _rms(x, g_ref[...]) * (1.0 + mod_ref[0, 4:5, :]) + mod_ref[0, 3:4, :]
    h_hi = h.astype(BF16)
    h_lo = (h - h_hi.astype(F32)).astype(BF16)
    hi_w = _dot(h_hi, wr_ref[...])
    logits = hi_w[:, :LANES] + (_dot(h_lo, wr_ref[:, :LANES]) + hi_w[:, LANES:])
    scores = _sigmoid(logits)
    gidx, gates = _route(scores.T[:N_EXPERTS], (scores + rb_ref[...]).T[:N_EXPERTS])
    tm = h.shape[0]
    sub = lax.broadcasted_iota(jnp.int32, (8, 1), 0)
    onehot = (sub == gidx).astype(F32)
    tri = (lax.broadcasted_iota(jnp.int32, (tm, tm), 0)
           <= lax.broadcasted_iota(jnp.int32, (tm, tm), 1)).astype(BF16)
    prefix = _dot(onehot.astype(BF16), tri)
    lrank = jnp.sum((prefix - 1.0) * onehot, axis=0, keepdims=True)

    count = [jnp.sum(onehot[g:g + 1, :]).astype(jnp.int32) for g in range(N_GROUPS)]
    padded = [lax.shift_left(lax.shift_right_logical(c + (MOE_ALIGN - 1), _LOG2_ALIGN), _LOG2_ALIGN)
              for c in count]
    loff, acc = [], 0
    for p in padded:
        loff.append(acc)
        acc = acc + p
    lpos = lrank + functools.reduce(
        lambda a, b: a + b, [jnp.where(gidx == g, jnp.asarray(v, jnp.int32).astype(F32), 0.0)
                             for g, v in enumerate(loff)])
    infot = jnp.concatenate(gates + [gidx.astype(F32), lrank, jnp.zeros((LANES - EPG - 2, tm), F32)], axis=0)
    info_ref[...] = infot.T
    g_hi = [gt.astype(BF16).astype(F32) for gt in gates]
    g_lo = [gt - gh for gt, gh in zip(gates, g_hi)]
    gext = jnp.concatenate(g_hi + g_lo + [jnp.zeros((LANES - 2 * EPG, tm), F32)], axis=0)
    h_ext = jnp.concatenate([h_hi, gext.T.astype(BF16)], axis=-1)
    perm = (lax.broadcasted_iota(jnp.int32, (_LOC_ROWS, 1), 0).astype(F32) == lpos).astype(BF16)
    lsort[s] = _dot(perm, h_ext).astype(BF16)

    new_base = []
    for g in range(N_GROUPS):
        seg_s[s, g] = padded[g]
        new_base.append(base_s[g] + padded[g])
        base_s[g] = new_base[g]
    cum = functools.reduce(lambda a, b: a + b,
                           [jnp.where(sub == g, new_base[g], 0) for g in range(N_GROUPS)])
    cum_ref[0] = jnp.broadcast_to(cum, (8, LANES))


def _copy_pieces(length, make_copy, wait):
    done = jnp.int32(0)
    for rows in _PIECES:
        take = (length & rows) != 0

        @pl.when(take)
        def _():
            cp = make_copy(pl.multiple_of(done, MOE_ALIGN), rows)
            cp.wait() if wait else cp.start()
        done = done + jnp.where(take, rows, 0)


def _moe_sort_kernel(x_ref, a_ref, wo_ref, mod_ref, g_ref, wr_ref, rb_ref, xo_ref, info_ref, cum_ref, srt_ref,
                     lsort, zeros_v, seg_s, base_s, sem):
    step = pl.program_id(0)
    n_src = lsort.shape[0]
    n_rows = srt_ref.shape[0]

    @pl.when(step == 0)
    def _():
        for g in range(N_GROUPS):
            base_s[g] = 0
        zeros_v[...] = jnp.zeros_like(zeros_v)

    @pl.when(step < n_src // MOE_SORT_TILES)
    def _():
        for j in range(MOE_SORT_TILES):
            rows = pl.ds(j * MOE_SRC, MOE_SRC)
            _moe_sort_tile(step * MOE_SORT_TILES + j, x_ref.at[rows], a_ref.at[rows], wo_ref, mod_ref, g_ref,
                           wr_ref, rb_ref, xo_ref.at[rows], info_ref.at[rows], cum_ref.at[pl.ds(j, 1)],
                           lsort, seg_s, base_s)

    @pl.when(step == n_src // MOE_SORT_TILES)
    def _():
        starts, acc = [], jnp.int32(0)
        for g in range(N_GROUPS):
            starts.append(acc)
            acc = acc + lax.shift_left(
                lax.shift_right_logical(base_s[g] + (MOE_DST - 1), _LOG2_DST), _LOG2_DST)
        end = acc

        def all_copies(wait):
            def tile_copies(t, dst):
                src = jnp.int32(0)
                new_dst = []
                for g in range(N_GROUPS):
                    n = seg_s[t, g]
                    _copy_pieces(n, lambda off, rows, src=src, g=g: pltpu.make_async_copy(
                        lsort.at[t, pl.ds(pl.multiple_of(src + off, MOE_ALIGN), rows)],
                        srt_ref.at[pl.ds(pl.multiple_of(dst[g] + off, MOE_ALIGN), rows)], sem.at[0]), wait)
                    src = src + n
                    new_dst.append(dst[g] + n)
                return tuple(new_dst)

            dst = lax.fori_loop(0, n_src, tile_copies, tuple(starts))
            for g in range(N_GROUPS):
                gap_end = starts[g + 1] if g + 1 < N_GROUPS else end
                _copy_pieces(gap_end - dst[g], lambda off, rows, g=g: pltpu.make_async_copy(
                    zeros_v.at[pl.ds(0, rows)],
                    srt_ref.at[pl.ds(pl.multiple_of(dst[g] + off, MOE_ALIGN), rows)], sem.at[0]), wait)

            def zero_tile(r, carry):
                cp = pltpu.make_async_copy(
                    zeros_v, srt_ref.at[pl.ds(pl.multiple_of(end + r * MOE_DST, MOE_DST), MOE_DST)], sem.at[0])
                cp.wait() if wait else cp.start()
                return carry

            lax.fori_loop(0, lax.shift_right_logical(n_rows - end, _LOG2_DST), zero_tile, 0)

        all_copies(False)
        all_copies(True)


def _moe_ffn2_kernel(cum_ref, xs_ref, wg_ref, wu_ref, wd_ref, y_ref):
    _, ends = _group_tiles(cum_ref)

    @pl.when(pl.program_id(0) >= ends[-1])
    def _():
        y_ref[...] = jnp.zeros_like(y_ref)

    @pl.when(pl.program_id(0) < ends[-1])
    def _():
        xs = xs_ref[:, :D_MODEL]
        gext = xs_ref[:, D_MODEL:].astype(F32)
        acts = []
        for i in range(EPG):
            a = _dot(xs, wg_ref[i])
            u = _dot(xs, wu_ref[i])
            gate = gext[:, i:i + 1] + gext[:, EPG + i:EPG + i + 1]
            acts.append(((a * _sigmoid(a)) * u * gate).astype(BF16))
        act = jnp.concatenate(acts, axis=-1)
        y_ref[...] = _dot(act, wd_ref[...].reshape(EPG * D_EXPERT, D_MODEL)).astype(BF16)


def _segment_row(cum_ref, s, g):
    _, ends = _group_tiles(cum_ref)
    first_tile = ends[g - 1] if g > 0 else 0
    return first_tile * MOE_DST + _counts_before(cum_ref, s, g)


def _unsort_window(cum_ref, s, g, n_rows):
    return pl.multiple_of(jnp.minimum(_segment_row(cum_ref, s, g), n_rows - MOE_SRC), MOE_ALIGN)


def _moe_unsort2_kernel(*refs, final_norm):
    if final_norm:
        cum_ref, x_ref, mod_ref, info_ref, y_ref, gf_ref, o_ref = refs
    else:
        cum_ref, x_ref, mod_ref, info_ref, y_ref, o_ref = refs
    n_rows = y_ref.shape[0]
    col = lax.broadcasted_iota(jnp.int32, (1, MOE_SRC), 1).astype(F32)
    for j in range(MOE_UNSORT_TILES):
        s = pl.program_id(0) * MOE_UNSORT_TILES + j
        rows = slice(j * MOE_SRC, (j + 1) * MOE_SRC)
        inf = info_ref[rows, :]
        gid = inf[:, _INFO_GID:_INFO_GID + 1]
        lrank = inf[:, _INFO_LRANK:_INFO_LRANK + 1]
        perms, wins = [], []
        for g in range(N_GROUPS):
            win = _unsort_window(cum_ref, s, g, n_rows)
            shift = _segment_row(cum_ref, s, g) - win
            rel = jnp.where(gid == float(g), lrank + shift.astype(F32), -1.0)
            perms.append((rel == col).astype(BF16))
            wins.append(y_ref[pl.ds(win, MOE_SRC), :])
        out = x_ref[rows, :] + mod_ref[0, 5:6, :] * _dot(jnp.concatenate(perms, axis=-1),
                                                         jnp.concatenate(wins, axis=0))
        o_ref[rows, :] = _rms(out, gf_ref[...]) if final_norm else out


def _mixer_out_and_moe(x, att, w_o, mod, g, w_router, router_bias, wg, wu, wd, layer, rows_per_cond,
                       final_g=None):
    n = x.shape[0]
    n_src = n // MOE_SRC
    n_dst = (n + N_GROUPS * n_src * (MOE_ALIGN - 1)) // MOE_DST + N_GROUPS
    n_rows = n_dst * MOE_DST
    ts = MOE_SORT_TILES * MOE_SRC
    last = n // ts - 1
    row = lambda i: (jnp.minimum(i, last), 0)
    full = lambda i: (0, 0)
    x, info, cum, srt = pl.pallas_call(
        _moe_sort_kernel,
        grid=(n // ts + 1,),
        in_specs=[
            pl.BlockSpec((ts, D_MODEL), row),
            pl.BlockSpec((ts, D_MODEL), row),
            pl.BlockSpec(w_o.shape, full),
            pl.BlockSpec((1, 6, D_MODEL), (lambda i: (0, 0, 0)) if rows_per_cond is None else
                         (lambda i: (1 + jnp.minimum(i, last) // (rows_per_cond // ts), 0, 0))),
            pl.BlockSpec((1, D_MODEL), full),
            pl.BlockSpec(w_router.shape, full),
            pl.BlockSpec(router_bias.shape, full),
        ],
        out_specs=[pl.BlockSpec((ts, D_MODEL), row),
                   pl.BlockSpec((ts, LANES), row),
                   pl.BlockSpec((MOE_SORT_TILES, 8, LANES), lambda i: (jnp.minimum(i, last), 0, 0)),
                   pl.BlockSpec(memory_space=pl.ANY)],
        out_shape=[jax.ShapeDtypeStruct((n, D_MODEL), F32),
                   jax.ShapeDtypeStruct((n, LANES), F32),
                   jax.ShapeDtypeStruct((n_src, 8, LANES), jnp.int32),
                   jax.ShapeDtypeStruct((n_rows, H_EXT), BF16)],
        scratch_shapes=[pltpu.VMEM((n_src, _LOC_ROWS, H_EXT), BF16),
                        pltpu.VMEM((MOE_DST, H_EXT), BF16),
                        pltpu.SMEM((n_src, N_GROUPS), jnp.int32),
                        pltpu.SMEM((N_GROUPS,), jnp.int32),
                        pltpu.SemaphoreType.DMA((1,))],
        compiler_params=_cparams("arbitrary"),
        name="moe_sort",
    )(x, att, w_o, mod, g, w_router, router_bias)
    cum = cum[:, :N_GROUPS, 0]

    blk = lambda width: pl.BlockSpec((MOE_DST, width), lambda d, c: (d, 0))
    expert = lambda shape: pl.BlockSpec(shape, lambda d, c: (layer, _dest_tile(d, c)[0], 0, 0))
    y = pl.pallas_call(
        _moe_ffn2_kernel,
        grid_spec=pltpu.PrefetchScalarGridSpec(
            num_scalar_prefetch=1,
            grid=(n_dst,),
            in_specs=[blk(H_EXT),
                      expert((None, EPG, D_MODEL, D_EXPERT)),
                      expert((None, EPG, D_MODEL, D_EXPERT)),
                      expert((None, EPG, D_EXPERT, D_MODEL))],
            out_specs=blk(D_MODEL),
        ),
        out_shape=jax.ShapeDtypeStruct((n_rows, D_MODEL), BF16),
        compiler_params=_cparams("arbitrary"),
        name="moe_experts",
    )(cum, srt, wg, wu, wd)

    tu = MOE_UNSORT_TILES * MOE_SRC
    in_specs = [pl.BlockSpec((tu, D_MODEL), lambda i, c: (i, 0)),
                _mod_spec(None if rows_per_cond is None else rows_per_cond // tu),
                pl.BlockSpec((tu, LANES), lambda i, c: (i, 0)),
                pl.BlockSpec(y.shape, lambda i, c: (0, 0), pipeline_mode=pl.Buffered(1))]
    args = [cum, x, mod, info, y]
    if final_g is not None:
        in_specs.append(pl.BlockSpec((1, D_MODEL), lambda i, c: (0, 0)))
        args.append(final_g)
    return pl.pallas_call(
        functools.partial(_moe_unsort2_kernel, final_norm=final_g is not None),
        grid_spec=pltpu.PrefetchScalarGridSpec(
            num_scalar_prefetch=1,
            grid=(n_src // MOE_UNSORT_TILES,),
            in_specs=in_specs,
            out_specs=pl.BlockSpec((tu, D_MODEL), lambda i, c: (i, 0)),
        ),
        out_shape=jax.ShapeDtypeStruct((n, D_MODEL), F32),
        compiler_params=_cparams("parallel"),
        name="moe_unsort",
    )(*args)


def _rope_tables(n_lat):
    t = np.arange(n_lat)
    n_freq = MLA_ROPE // 4
    inv_freq = jnp.asarray(ROPE_THETA, F32) ** (-jnp.arange(n_freq, dtype=F32) / n_freq)
    ar = jnp.asarray(t // GRID_W, F32)[:, None] * inv_freq
    ac = jnp.asarray(t % GRID_W, F32)[:, None] * inv_freq
    cos = jnp.concatenate([jnp.cos(ar), jnp.cos(ar), jnp.cos(ac), jnp.cos(ac)], axis=-1)
    sin = jnp.concatenate([-jnp.sin(ar), jnp.sin(ar), -jnp.sin(ac), jnp.sin(ac)], axis=-1)
    return (jnp.tile(cos, (1, MLA_HEADS)), jnp.tile(sin, (1, MLA_HEADS)),
            jnp.tile(cos, (1, LANES // MLA_ROPE)), jnp.tile(sin, (1, LANES // MLA_ROPE)))


_ROPE_SWAP = np.concatenate([np.arange(8, 16), np.arange(0, 8), np.arange(24, 32), np.arange(16, 24)])


def _mla_weights(w_in, w_uq, w_ukv):
    o = MLA_Q_LORA + MLA_KV_LORA
    kr = w_in[:, o:]
    rep = LANES // MLA_ROPE
    w_in_x = jnp.concatenate([w_in[:, :o], jnp.tile(kr, (1, rep)), jnp.tile(kr[:, _ROPE_SWAP], (1, rep))],
                             axis=-1).astype(BF16)
    uq = w_uq.reshape(MLA_Q_LORA, MLA_HEADS, MLA_NOPE + MLA_ROPE)
    q_nope = uq[:, :, :MLA_NOPE].reshape(MLA_Q_LORA, -1)
    q_rope = uq[:, :, MLA_NOPE:]
    w_uq_x = jnp.concatenate([q_nope, q_rope.reshape(MLA_Q_LORA, -1),
                              q_rope[:, :, _ROPE_SWAP].reshape(MLA_Q_LORA, -1)], axis=-1).astype(BF16)
    ukv = w_ukv.reshape(MLA_KV_LORA, MLA_HEADS, MLA_NOPE + MLA_V)
    w_uk = ukv[:, :, :MLA_NOPE].reshape(MLA_KV_LORA, -1).astype(BF16)
    w_uv = ukv[:, :, MLA_NOPE:].reshape(MLA_KV_LORA, -1).astype(BF16)
    return w_in_x, w_uq_x, w_uk, w_uv


def kernel(x_prompt, x_sample, cache_mla_ckv, cache_mla_krope, cache_nat_k, cache_nat_v, c, c_ctx,
           w_ada, b_ada, norm_mix, norm_ffn, norm_final, mla_w_in, mla_q_norm, mla_w_uq, mla_kv_norm,
           mla_w_ukv, mla_w_o, nat_w_qkv, nat_rpb, nat_w_o, w_router, router_bias,
           moe_w_gate, moe_w_up, moe_w_down):
    B, S, D = x_prompt.shape
    Bd, Sd, _ = x_sample.shape
    assert D == D_MODEL and Bd + 1 <= 8
    tm_c, tm_s = 512, 512

    xc = x_prompt.reshape(B * S, D)
    xs = x_sample.reshape(Bd * Sd, D)
    cond8 = jnp.concatenate([c_ctx[None, :], c, jnp.zeros((8 - 1 - Bd, D), F32)], axis=0)
    mod_all = _ada_modulation(cond8, w_ada, b_ada).reshape(DEPTH, 8, 6, D)

    rope_tabs = _rope_tables(Sd)
    wr = jnp.pad(w_router, ((0, 0), (0, LANES - N_EXPERTS)))
    wr_hi = wr.astype(BF16)
    wr = jnp.concatenate([wr_hi, (wr - wr_hi.astype(F32)).astype(BF16)], axis=1)
    rb = jnp.pad(router_bias, (0, LANES - N_EXPERTS)).reshape(1, LANES)
    wg = moe_w_gate.astype(BF16)
    wu = moe_w_up.astype(BF16)
    wd = moe_w_down.astype(BF16)

    ckv_buf = kr_buf = k_buf = v_buf = None
    for layer in range(DEPTH):
        mod = mod_all[layer]
        g_mix = norm_mix[layer][None, :]
        j = layer // 2
        if layer % 2 == 0:
            w_in_x, w_uq_x, w_uk, w_uv = _mla_weights(mla_w_in[j], mla_w_uq[j], mla_w_ukv[j])
            w_in_c = w_in_x[:, :MLA_Q_LORA + MLA_KV_LORA + LANES]
            w_uq_c = w_uq_x[:, :MLA_HEADS * (MLA_NOPE + MLA_ROPE)]
            qg = mla_q_norm[j][None, :]
            kvg = mla_kv_norm[j][None, :]
            w_o = mla_w_o[j].astype(BF16)
            qn, qr, ckv_buf, kr_buf, kr4 = _premix_mla(xc, mod, g_mix, w_in_c, qg, w_uq_c, kvg, None, None, tm_c,
                                                       cache=(ckv_buf, kr_buf, j, (DEPTH + 1) // 2, B))
            att_c = _mla_attention(qn, qr, ckv_buf, kr4, None, None, w_uk, w_uv, B, S, ckv_slot=j)
            qn, qr, ckv, kr, kr4 = _premix_mla(xs, mod, g_mix, w_in_x, qg, w_uq_x, kvg, rope_tabs, Sd, tm_s)
            cache_kr4 = jnp.tile(cache_mla_krope[:, j], (1, 1, LANES // MLA_ROPE))
            att_s = _mla_attention(qn, qr, ckv, kr4, cache_mla_ckv[:, j], cache_kr4, w_uk, w_uv, Bd, 512)
        else:
            w_qkv = nat_w_qkv[j].astype(BF16)
            w_o = nat_w_o[j].astype(BF16)
            q, k_buf, v_buf = _premix_nat_cache(xc, mod, g_mix, w_qkv, k_buf, v_buf, j, DEPTH // 2, B, tm_c)
            att_c = _dense_attention(q, k_buf, v_buf, j)
            q, k, v = _premix_nat(xs, mod, g_mix, w_qkv, Sd, tm_s, BF16)
            bias = _nat_bias_pairs(nat_rpb[j], Sd // GRID_W)
            att_s = _nat_attention(q, k, v,
                                   cache_nat_k[:, j].reshape(Bd, -1, D), cache_nat_v[:, j].reshape(Bd, -1, D),
                                   bias, Bd)
        g_ffn = norm_ffn[layer][None, :]
        final_g = norm_final[None, :] if layer == DEPTH - 1 else None
        xc = _mixer_out_and_moe(xc, att_c, w_o, mod, g_ffn, wr, rb, wg, wu, wd, layer, None, final_g)
        xs = _mixer_out_and_moe(xs, att_s, w_o, mod, g_ffn, wr, rb, wg, wu, wd, layer, Sd, final_g)

    y_prompt = xc.reshape(B, S, D)
    y_sample = xs.reshape(Bd, Sd, D)
    kv_shape = (B, DEPTH // 2, S, NAT_HEADS, NAT_DH)
    return (y_prompt, y_sample, ckv_buf, kr_buf, k_buf.reshape(kv_shape), v_buf.reshape(kv_shape))
```

```python
import functools

import numpy as np
import jax
import jax.numpy as jnp
from jax import lax
from jax.experimental import pallas as pl
from jax.experimental.pallas import tpu as pltpu

F32 = jnp.float32
BF16 = jnp.bfloat16

D_MODEL = 1024
DEPTH = 4
GRID_W = 64
LANES = 128
MLA_HEADS = 16
MLA_NOPE = 64
MLA_ROPE = 32
MLA_V = 64
MLA_Q_LORA = 384
MLA_KV_LORA = 256
MLA_SCALE = (MLA_NOPE + MLA_ROPE) ** -0.5
ROPE_THETA = 10000.0
NAT_HEADS = 16
NAT_DH = 64
NAT_SCALE = NAT_DH ** -0.5
WIN_H = 8
WIN_W = 16
NAT_QROWS = 4
NAT_KROWS = NAT_QROWS + WIN_H
N_EXPERTS = 16
N_GROUPS = 4
EPG = N_EXPERTS // N_GROUPS
D_EXPERT = 256
NORM_EPS = 1e-6
NEG_INF = -1e30

HEAD_PAIRS = MLA_HEADS // 2
VMEM_LIMIT = 56 * 1024 * 1024


def _cparams(*sem):
    return pltpu.CompilerParams(dimension_semantics=sem, vmem_limit_bytes=VMEM_LIMIT)


def _sigmoid(x):
    return 1.0 / (1.0 + jnp.exp(-x))


def _rms(x, g):
    ms = jnp.mean(x * x, axis=-1, keepdims=True)
    return x * lax.rsqrt(ms + NORM_EPS) * g


def _dot(a, b):
    return jnp.dot(a, b, preferred_element_type=F32)


def _dot_nt(a, b):
    return lax.dot_general(a, b, (((1,), (1,)), ((), ())), preferred_element_type=F32)


def _lane_mask(width, idx, dtype):
    lane = lax.broadcasted_iota(jnp.int32, (1, LANES), 1)
    return ((lane >= idx * width) & (lane < (idx + 1) * width)).astype(dtype)


def _ada_kernel(cond_ref, w_ref, b_ref, o_ref):
    c = cond_ref[...]
    s = (c * _sigmoid(c)).astype(BF16)
    o_ref[0] = _dot(s, w_ref[0].astype(BF16)) + b_ref[0]


def _ada_modulation(cond8, w_ada, b_ada):
    n_chunk = 6
    return pl.pallas_call(
        _ada_kernel,
        grid=(DEPTH, n_chunk),
        in_specs=[
            pl.BlockSpec((8, D_MODEL), lambda l, j: (0, 0)),
            pl.BlockSpec((1, D_MODEL, D_MODEL), lambda l, j: (l, 0, j)),
            pl.BlockSpec((1, 1, D_MODEL), lambda l, j: (l, 0, j)),
        ],
        out_specs=pl.BlockSpec((1, 8, D_MODEL), lambda l, j: (l, 0, j)),
        out_shape=jax.ShapeDtypeStruct((DEPTH, 8, 6 * D_MODEL), F32),
        compiler_params=_cparams("parallel", "parallel"),
        name="ada_modulation",
    )(cond8, w_ada, b_ada.reshape(DEPTH, 1, 6 * D_MODEL))


def _mod_spec(rows_per_cond):
    if rows_per_cond is None:
        return pl.BlockSpec((1, 6, D_MODEL), lambda i, *_: (0, 0, 0))
    return pl.BlockSpec((1, 6, D_MODEL), lambda i, *_: (1 + i // rows_per_cond, 0, 0))


def _premix_mla_kernel(*refs, rope, cache):
    if rope:
        (x_ref, mod_ref, g_ref, w_in_ref, qg_ref, w_uq_ref, kvg_ref,
         cosq_ref, sinq_ref, cosk_ref, sink_ref,
         qn_ref, qr_ref, ckv_ref, kr_ref, kr4_ref) = refs
    elif cache == "update":
        (x_ref, mod_ref, g_ref, w_in_ref, qg_ref, w_uq_ref, kvg_ref, _, _,
         qn_ref, qr_ref, ckv_ref, kr_ref, kr4_ref) = refs
    else:
        (x_ref, mod_ref, g_ref, w_in_ref, qg_ref, w_uq_ref, kvg_ref,
         qn_ref, qr_ref, ckv_ref, kr_ref, kr4_ref) = refs
    h = _rms(x_ref[...], g_ref[...]) * (1.0 + mod_ref[0, 1:2, :]) + mod_ref[0, 0:1, :]
    lat = _dot(h.astype(BF16), w_in_ref[...])
    c_q = lat[:, :MLA_Q_LORA]
    c_kv = lat[:, MLA_Q_LORA:MLA_Q_LORA + MLA_KV_LORA]
    o = MLA_Q_LORA + MLA_KV_LORA
    kr4 = lat[:, o:o + LANES]
    q = _dot(_rms(c_q, qg_ref[...]).astype(BF16), w_uq_ref[...])
    n_nope = MLA_HEADS * MLA_NOPE
    n_rope = MLA_HEADS * MLA_ROPE
    qr = q[:, n_nope:n_nope + n_rope]
    if rope:
        qr = qr * cosq_ref[...] + q[:, n_nope + n_rope:] * sinq_ref[...]
        kr4 = kr4 * cosk_ref[...] + lat[:, o + LANES:o + 2 * LANES] * sink_ref[...]
    qn_ref[...] = q[:, :n_nope].astype(BF16)
    qr_ref[...] = qr.astype(BF16)
    kr4_ref[...] = kr4.astype(BF16)
    ckv = _rms(c_kv, kvg_ref[...])
    if cache is None:
        ckv_ref[...] = ckv
        kr_ref[...] = kr4[:, :MLA_ROPE]
    else:
        nb, n_slots, s, _ = ckv_ref.shape
        ckv_ref[:, 0] = ckv.reshape(nb, s, MLA_KV_LORA)
        kr_ref[:, 0] = kr4[:, :MLA_ROPE].reshape(nb, s, MLA_ROPE)
        for slot in range(1, n_slots):
            ckv_ref[:, slot] = jnp.zeros((nb, s, MLA_KV_LORA), F32)
            kr_ref[:, slot] = jnp.zeros((nb, s, MLA_ROPE), F32)


def _premix_mla(x, mod, g, w_in, qg, w_uq, kvg, rope_tabs, rows_per_cond, tm, cache=None):
    n = x.shape[0]
    rope = rope_tabs is not None
    row = lambda i: (i, 0)
    full = lambda i: (0, 0)
    in_specs = [
        pl.BlockSpec((tm, D_MODEL), row),
        _mod_spec(None if rows_per_cond is None else rows_per_cond // tm),
        pl.BlockSpec((1, D_MODEL), full),
        pl.BlockSpec(w_in.shape, full),
        pl.BlockSpec((1, MLA_Q_LORA), full),
        pl.BlockSpec(w_uq.shape, full),
        pl.BlockSpec((1, MLA_KV_LORA), full),
    ]
    args = [x, mod, g, w_in, qg, w_uq, kvg]
    if rope:
        nblk = rope_tabs[0].shape[0] // tm
        pos = lambda i: (i % nblk, 0)
        for t in rope_tabs:
            in_specs.append(pl.BlockSpec((tm, t.shape[1]), pos))
            args.append(t)
    widths = (MLA_HEADS * MLA_NOPE, MLA_HEADS * MLA_ROPE, MLA_KV_LORA, MLA_ROPE, LANES)
    dtypes = (BF16, BF16, F32, F32, BF16)
    out_specs = [pl.BlockSpec((tm, w), row) for w in widths]
    out_shape = [jax.ShapeDtypeStruct((n, w), d) for w, d in zip(widths, dtypes)]
    mode, aliases = None, {}
    if cache is not None:
        assert not rope
        ckv_buf, kr_buf, slot, n_slots, n_batch = cache
        s = n // n_batch
        nb = tm // s
        mode = "first" if ckv_buf is None else "update"
        if mode == "first":
            assert slot == 0
            spec = lambda w: pl.BlockSpec((nb, n_slots, s, w), lambda i: (i, 0, 0, 0))
        else:
            spec = lambda w: pl.BlockSpec((nb, 1, s, w), lambda i: (i, slot, 0, 0))
            aliases = {len(args): 2, len(args) + 1: 3}
            in_specs += [pl.BlockSpec(memory_space=pl.ANY)] * 2
            args += [ckv_buf, kr_buf]
        for idx in (2, 3):
            out_specs[idx] = spec(widths[idx])
            out_shape[idx] = jax.ShapeDtypeStruct((n_batch, n_slots, s, widths[idx]), F32)
    return pl.pallas_call(
        functools.partial(_premix_mla_kernel, rope=rope, cache=mode),
        grid=(n // tm,),
        in_specs=in_specs,
        out_specs=out_specs,
        out_shape=out_shape,
        input_output_aliases=aliases,
        compiler_params=_cparams("parallel"),
        name="premix_mla_rope" if rope else "premix_mla",
    )(*args)


LOG2E = 1.4426950408889634


def _softmax_pv(s_list, v_list, scale=1.0):
    m = functools.reduce(jnp.maximum, [jnp.max(s, axis=-1, keepdims=True) for s in s_list])
    e_list = [jnp.exp2((s - m) * (scale * LOG2E)) for s in s_list]
    l = functools.reduce(lambda a, b: a + b, [jnp.sum(e, axis=-1, keepdims=True) for e in e_list])
    o = functools.reduce(lambda a, b: a + b,
                         [_dot(e.astype(BF16), v) for e, v in zip(e_list, v_list)])
    return o / l


def _stack_heads(q, masks):
    return jnp.concatenate([q * m for m in masks], axis=0)


def _unstack_heads(o2, width):
    tq = o2.shape[0] // 2
    lane = lax.broadcasted_iota(jnp.int32, (1, LANES), 1)
    return jnp.where(lane < width, o2[:tq], o2[tq:])


MLA_ATTN_BATCHES = 2


def _mla_attn_kernel(*refs, cached, nb):
    if cached:
        (qn_ref, qr_ref, ckv_ref, kr4_ref, cckv_ref, ckr4_ref, w_uk_ref, w_uv_ref,
         o_ref, kn_s, v_s, kr_s) = refs
    else:
        (qn_ref, qr_ref, ckv_ref, kr4_ref, w_uk_ref, w_uv_ref, o_ref, kn_s, v_s, kr_s) = refs
    from_slot = len(ckv_ref.shape) == 3
    t_own = ckv_ref.shape[1] if from_slot else ckv_ref.shape[0] // nb

    for b in range(nb):
        own = slice(b * t_own, (b + 1) * t_own)

        def fill_kv():
            c = (ckv_ref[b] if from_slot else ckv_ref[own, :]).astype(BF16)
            kn_s[0:t_own, :] = _dot(c, w_uk_ref[...]).astype(BF16)
            v_s[0:t_own, :] = _dot(c, w_uv_ref[...]).astype(BF16)
            kr_s[0:t_own, :] = kr4_ref[own, :]
            if cached:
                cc = cckv_ref[0].astype(BF16)
                kn_s[t_own:, :] = _dot(cc, w_uk_ref[...]).astype(BF16)
                v_s[t_own:, :] = _dot(cc, w_uv_ref[...]).astype(BF16)
                kr_s[t_own:, :] = ckr4_ref[0].astype(BF16)

        if cached:
            pl.when(pl.program_id(1) == 0)(fill_kv)
            qrows = slice(None)
        else:
            fill_kv()
            qrows = own

        kr4 = kr_s[...]
        for p in range(HEAD_PAIRS):
            sl = slice(p * LANES, (p + 1) * LANES)
            qn = qn_ref[qrows, sl]
            qr = qr_ref[qrows, (p // 2) * LANES:(p // 2 + 1) * LANES]
            k_cat = jnp.concatenate([kn_s[:, sl], kr4], axis=-1)
            q_cat = jnp.concatenate(
                [_stack_heads(qn, [_lane_mask(MLA_NOPE, i, BF16) for i in range(2)]),
                 _stack_heads(qr, [_lane_mask(MLA_ROPE, (2 * p + i) % 4, BF16) for i in range(2)])], axis=-1)
            o2 = _softmax_pv([_dot_nt(q_cat, k_cat)], [v_s[:, sl]], MLA_SCALE)
            o_ref[qrows, sl] = _unstack_heads(o2, MLA_V).astype(BF16)


def _mla_attention(qn, qr, ckv, kr4, cache_ckv, cache_kr4, w_uk, w_uv, n_batch, tq, ckv_slot=None):
    n = qn.shape[0]
    s_own = n // n_batch
    nq = s_own // tq
    cached = cache_ckv is not None
    t_all = s_own + (cache_ckv.shape[1] if cached else 0)
    nb = 1 if cached else MLA_ATTN_BATCHES
    assert cached or nq == 1
    tq = tq * nb
    qrow = lambda b, j: (b * nq + j, 0)
    own = lambda b, j: (b, 0)
    full = lambda b, j: (0, 0)
    in_specs = [
        pl.BlockSpec((tq, qn.shape[1]), qrow),
        pl.BlockSpec((tq, qr.shape[1]), qrow),
        pl.BlockSpec((nb * s_own, MLA_KV_LORA), own) if ckv_slot is None else
        pl.BlockSpec((nb, None, s_own, MLA_KV_LORA), lambda b, j: (b, ckv_slot, 0, 0)),
        pl.BlockSpec((nb * s_own, LANES), own),
    ]
    args = [qn, qr, ckv, kr4]
    if cached:
        in_specs += [pl.BlockSpec((1,) + cache_ckv.shape[1:], lambda b, j: (b, 0, 0)),
                     pl.BlockSpec((1,) + cache_kr4.shape[1:], lambda b, j: (b, 0, 0))]
        args += [cache_ckv, cache_kr4]
    in_specs += [pl.BlockSpec(w_uk.shape, full), pl.BlockSpec(w_uv.shape, full)]
    args += [w_uk, w_uv]
    return pl.pallas_call(
        functools.partial(_mla_attn_kernel, cached=cached, nb=nb),
        grid=(n_batch // nb, nq),
        in_specs=in_specs,
        out_specs=pl.BlockSpec((tq, D_MODEL), qrow),
        out_shape=jax.ShapeDtypeStruct((n, D_MODEL), BF16),
        scratch_shapes=[pltpu.VMEM((t_all, D_MODEL), BF16), pltpu.VMEM((t_all, D_MODEL), BF16),
                        pltpu.VMEM((t_all, LANES), BF16)],
        compiler_params=_cparams("parallel", "arbitrary"),
        name="mla_attention_cached" if cached else "mla_attention",
    )(*args)


def _premix_nat_kernel(x_ref, mod_ref, g_ref, w_ref, q_ref, k_ref, v_ref):
    h = _rms(x_ref[...], g_ref[...]) * (1.0 + mod_ref[0, 1:2, :]) + mod_ref[0, 0:1, :]
    qkv = _dot(h.astype(BF16), w_ref[...])
    q_ref[...] = qkv[:, :D_MODEL].astype(q_ref.dtype)
    k_ref[...] = qkv[:, D_MODEL:2 * D_MODEL].astype(k_ref.dtype)
    v_ref[...] = qkv[:, 2 * D_MODEL:].astype(v_ref.dtype)


def _premix_nat(x, mod, g, w_qkv, rows_per_cond, tm, kv_dtype):
    n = x.shape[0]
    row = lambda i: (i, 0)
    full = lambda i: (0, 0)
    return pl.pallas_call(
        _premix_nat_kernel,
        grid=(n // tm,),
        in_specs=[
            pl.BlockSpec((tm, D_MODEL), row),
            _mod_spec(None if rows_per_cond is None else rows_per_cond // tm),
            pl.BlockSpec((1, D_MODEL), full),
            pl.BlockSpec(w_qkv.shape, full),
        ],
        out_specs=[pl.BlockSpec((tm, D_MODEL), row)] * 3,
        out_shape=[jax.ShapeDtypeStruct((n, D_MODEL), BF16),
                   jax.ShapeDtypeStruct((n, D_MODEL), kv_dtype),
                   jax.ShapeDtypeStruct((n, D_MODEL), kv_dtype)],
        compiler_params=_cparams("parallel"),
        name="premix_nat",
    )(x, mod, g, w_qkv)


def _premix_nat_cache_kernel(*refs, first):
    if first:
        x_ref, mod_ref, g_ref, w_ref, q_ref, k_ref, v_ref = refs
    else:
        x_ref, mod_ref, g_ref, w_ref, _, _, q_ref, k_ref, v_ref = refs
    h = _rms(x_ref[...], g_ref[...]) * (1.0 + mod_ref[0, 1:2, :]) + mod_ref[0, 0:1, :]
    qkv = _dot(h.astype(BF16), w_ref[...])
    q_ref[...] = qkv[:, :D_MODEL].astype(BF16)
    nb, n_slots, s, _ = k_ref.shape
    k_ref[:, 0] = qkv[:, D_MODEL:2 * D_MODEL].reshape(nb, s, D_MODEL)
    v_ref[:, 0] = qkv[:, 2 * D_MODEL:].reshape(nb, s, D_MODEL)
    for slot in range(1, n_slots):
        k_ref[:, slot] = jnp.zeros((nb, s, D_MODEL), F32)
        v_ref[:, slot] = jnp.zeros((nb, s, D_MODEL), F32)


def _premix_nat_cache(x, mod, g, w_qkv, k_buf, v_buf, slot, n_slots, n_batch, tm):
    n = x.shape[0]
    s = n // n_batch
    nb = tm // s
    first = k_buf is None
    row = lambda i: (i, 0)
    full = lambda i: (0, 0)
    in_specs = [pl.BlockSpec((tm, D_MODEL), row), _mod_spec(None), pl.BlockSpec((1, D_MODEL), full),
                pl.BlockSpec(w_qkv.shape, full)]
    args = [x, mod, g, w_qkv]
    if first:
        assert slot == 0
        kv_spec = pl.BlockSpec((nb, n_slots, s, D_MODEL), lambda i: (i, 0, 0, 0))
        aliases = {}
    else:
        in_specs += [pl.BlockSpec(memory_space=pl.ANY)] * 2
        args += [k_buf, v_buf]
        kv_spec = pl.BlockSpec((nb, 1, s, D_MODEL), lambda i: (i, slot, 0, 0))
        aliases = {4: 1, 5: 2}
    kv_shape = jax.ShapeDtypeStruct((n_batch, n_slots, s, D_MODEL), F32)
    return pl.pallas_call(
        functools.partial(_premix_nat_cache_kernel, first=first),
        grid=(n // tm,),
        in_specs=in_specs,
        out_specs=[pl.BlockSpec((tm, D_MODEL), row), kv_spec, kv_spec],
        out_shape=[jax.ShapeDtypeStruct((n, D_MODEL), BF16), kv_shape, kv_shape],
        input_output_aliases=aliases,
        compiler_params=_cparams("parallel"),
        name="premix_nat_cache",
    )(*args)


DENSE_ATTN_BATCHES = 4


def _dense_attn_kernel(q_ref, k_ref, v_ref, o_ref):
    nb, s, _ = k_ref.shape
    for b in range(nb):
        rows = slice(b * s, (b + 1) * s)
        for p in range(HEAD_PAIRS):
            sl = slice(p * LANES, (p + 1) * LANES)
            q2 = _stack_heads(q_ref[rows, sl], [_lane_mask(NAT_DH, i, BF16) for i in range(2)])
            k = k_ref[b, :, sl].astype(BF16)
            v = v_ref[b, :, sl].astype(BF16)
            o2 = _softmax_pv([_dot_nt(q2, k)], [v], NAT_SCALE)
            o_ref[rows, sl] = _unstack_heads(o2, NAT_DH).astype(BF16)


def _dense_attention(q, k_buf, v_buf, slot):
    n = q.shape[0]
    n_batch, _, s, _ = k_buf.shape
    nb = DENSE_ATTN_BATCHES
    blk = pl.BlockSpec((nb * s, D_MODEL), lambda b: (b, 0))
    kv = pl.BlockSpec((nb, None, s, D_MODEL), lambda b: (b, slot, 0, 0))
    return pl.pallas_call(
        _dense_attn_kernel,
        grid=(n_batch // nb,),
        in_specs=[blk, kv, kv],
        out_specs=blk,
        out_shape=jax.ShapeDtypeStruct((n, D_MODEL), BF16),
        compiler_params=_cparams("parallel"),
        name="dense_attention",
    )(q, k_buf, v_buf)


_NAT_QBLK = NAT_QROWS * GRID_W
_NAT_KBLK = NAT_KROWS * GRID_W


def _nat_block_plan(rows):
    assert rows % NAT_QROWS == 0 and rows >= NAT_KROWS and NAT_KROWS % 2 == 0
    plan, variants = [], []
    for r0 in range(0, rows, NAT_QROWS):
        ks = min(max(r0 - WIN_H // 2, 0), rows - NAT_KROWS)
        r = r0 + np.arange(NAT_QROWS)
        kr = ks + np.arange(NAT_KROWS)
        rs = np.clip(r - WIN_H // 2, 0, rows - WIN_H)
        valid_row = (kr[None, :] >= rs[:, None]) & (kr[None, :] < rs[:, None] + WIN_H)
        d0 = ks - r + (WIN_H - 1)
        for vi, (d0_v, valid_v) in enumerate(variants):
            if np.array_equal(d0, d0_v) and np.array_equal(valid_row, valid_v):
                break
        else:
            vi = len(variants)
            variants.append((d0, valid_row))
        plan.append((ks, vi))
    return plan, variants


def _nat_bias_row_range(variants):
    lo = min(int(d0.min()) for d0, _ in variants)
    hi = max(int(d0.max()) for d0, _ in variants) + NAT_KROWS
    return lo, hi


def _nat_bias_pairs(rpb, rows):
    n_layers, n_heads, n_dr, n_dc = rpb.shape
    _, variants = _nat_block_plan(rows)
    lo, hi = _nat_bias_row_range(variants)
    c = np.arange(GRID_W)
    cs = np.clip(c - WIN_W // 2, 0, GRID_W - WIN_W)
    valid_col = (c[None, :] >= cs[:, None]) & (c[None, :] < cs[:, None] + WIN_W)
    d_col = c[None, :] - c[:, None] + (WIN_W - 1)
    sel = (d_col[None] == np.arange(n_dc)[:, None, None]) & valid_col[None]
    rows2 = (lo + np.arange(hi - lo))[:, None] + np.arange(2)[None, :]
    in_range = (rows2 >= 0) & (rows2 < n_dr)
    sel2 = np.zeros((2, n_dc, GRID_W, 2, GRID_W), np.float32)
    mask = np.zeros((hi - lo, GRID_W, 2, GRID_W), np.float32)
    for half in range(2):
        sel2[half, :, :, half, :] = sel
        mask[:, :, half, :] = np.where(in_range[:, half, None, None] & valid_col[None], 0.0, NEG_INF)
    picked = jnp.where(jnp.asarray(in_range)[None, None, :, :, None],
                       rpb[:, :, np.clip(rows2, 0, n_dr - 1), :], 0.0)
    tab = jnp.einsum('lhdm,mck->lhdck', picked.reshape(n_layers, n_heads, hi - lo, 2 * n_dc),
                     jnp.asarray(sel2.reshape(2 * n_dc, GRID_W, 2 * GRID_W)), precision=lax.Precision.HIGHEST)
    return tab + jnp.asarray(mask.reshape(hi - lo, GRID_W, 2 * GRID_W))


def _nat_block_bias(tp_ref, head, d0, valid_row, row_lo):
    neg = jnp.full((GRID_W, 2 * GRID_W), NEG_INF, F32)
    left = lax.broadcasted_iota(jnp.int32, (1, 2 * GRID_W), 1) < GRID_W
    rows_out = []
    for dr in range(NAT_QROWS):
        pieces = []
        for a in range(0, NAT_KROWS, 2):
            ok0, ok1 = bool(valid_row[dr, a]), bool(valid_row[dr, a + 1])
            if not (ok0 or ok1):
                pieces.append(neg)
                continue
            piece = tp_ref[head, int(d0[dr]) + a - row_lo]
            if ok0 and not ok1:
                piece = jnp.where(left, piece, NEG_INF)
            elif ok1 and not ok0:
                piece = jnp.where(left, NEG_INF, piece)
            pieces.append(piece)
        rows_out.append(jnp.concatenate(pieces, axis=-1))
    return jnp.concatenate(rows_out, axis=0)


NAT_ATTN_BATCHES = 2


def _nat_attn_kernel(q_ref, k_ref, v_ref, kc_ref, vc_ref, tp_ref, o_ref, *, plan, variants, row_lo):
    nb = kc_ref.shape[0]
    s = q_ref.shape[0] // nb
    assert NAT_SCALE == 2.0 ** round(np.log2(NAT_SCALE))
    masks = [_lane_mask(NAT_DH, i, BF16) * NAT_SCALE for i in range(2)]
    for bi, (ks, var) in enumerate(plan):
        bias2 = jnp.concatenate([_nat_block_bias(tp_ref, i, *variants[var], row_lo) for i in range(2)], axis=0)
        for b in range(nb):
            rows = slice(b * s + bi * _NAT_QBLK, b * s + (bi + 1) * _NAT_QBLK)
            keys = slice(b * s + ks * GRID_W, b * s + ks * GRID_W + _NAT_KBLK)
            q2 = _stack_heads(q_ref[rows, :], masks)
            kc = kc_ref[b].astype(BF16)
            vc = vc_ref[b].astype(BF16)
            o2 = _softmax_pv([_dot_nt(q2, k_ref[keys, :]) + bias2, _dot_nt(q2, kc)], [v_ref[keys, :], vc])
            o_ref[rows, :] = _unstack_heads(o2, NAT_DH).astype(BF16)


def _nat_attention(q, k, v, cache_k, cache_v, bias, layer_slot, n_batch):
    n = q.shape[0]
    s = n // n_batch
    plan, variants = _nat_block_plan(s // GRID_W)
    row_lo, _ = _nat_bias_row_range(variants)
    nb = NAT_ATTN_BATCHES
    own = pl.BlockSpec((nb * s, LANES), lambda p, b: (b, p))
    cache = pl.BlockSpec((nb, cache_k.shape[1], LANES), lambda p, b: (b, 0, p))
    return pl.pallas_call(
        functools.partial(_nat_attn_kernel, plan=plan, variants=variants, row_lo=row_lo),
        grid=(HEAD_PAIRS, n_batch // nb),
        in_specs=[own, own, own, cache, cache,
                  pl.BlockSpec((None, 2) + bias.shape[2:], lambda p, b: (layer_slot, p, 0, 0, 0))],
        out_specs=own,
        out_shape=jax.ShapeDtypeStruct((n, D_MODEL), BF16),
        compiler_params=_cparams("parallel", "parallel"),
        name="nat_attention",
    )(q, k, v, cache_k, cache_v, bias)


def _top2_sum(a, b, c, d):
    hi1, lo1 = jnp.maximum(a, b), jnp.minimum(a, b)
    hi2, lo2 = jnp.maximum(c, d), jnp.minimum(c, d)
    return jnp.maximum(hi1, hi2) + jnp.maximum(jnp.minimum(hi1, hi2), jnp.maximum(lo1, lo2))


def _route(scores, biased):
    sc = [scores[e:e + 1, :] for e in range(N_EXPERTS)]
    bs = [biased[e:e + 1, :] for e in range(N_EXPERTS)]
    gscore = [_top2_sum(*bs[EPG * g:EPG * (g + 1)]) for g in range(N_GROUPS)]
    best, gidx = gscore[0], jnp.zeros_like(gscore[0], dtype=jnp.int32)
    for g in range(1, N_GROUPS):
        better = gscore[g] > best
        gidx = jnp.where(better, g, gidx)
        best = jnp.where(better, gscore[g], best)
    cb = [functools.reduce(lambda a, b: a + b,
                           [jnp.where(gidx == g, bs[EPG * g + i], 0.0) for g in range(N_GROUPS)])
          for i in range(EPG)]
    cs = [functools.reduce(lambda a, b: a + b,
                           [jnp.where(gidx == g, sc[EPG * g + i], 0.0) for g in range(N_GROUPS)])
          for i in range(EPG)]
    b1, i1 = cb[0], jnp.zeros_like(gidx)
    for i in range(1, EPG):
        better = cb[i] > b1
        i1 = jnp.where(better, i, i1)
        b1 = jnp.where(better, cb[i], b1)
    b2, i2 = jnp.full_like(b1, -jnp.inf), jnp.full_like(i1, -1)
    for i in range(EPG):
        better = (i1 != i) & (cb[i] > b2)
        i2 = jnp.where(better, i, i2)
        b2 = jnp.where(better, cb[i], b2)
    sel = [(i1 == i) | (i2 == i) for i in range(EPG)]
    w = [jnp.where(sel[i], cs[i], 0.0) for i in range(EPG)]
    tot = w[0] + w[1] + w[2] + w[3]
    return gidx, [w[i] / tot for i in range(EPG)]


MOE_SRC = 256
MOE_DST = 512
MOE_SORT_TILES = 2
MOE_UNSORT_TILES = 4
_LOG2_DST = MOE_DST.bit_length() - 1
assert MOE_DST == 1 << _LOG2_DST and MOE_SRC <= MOE_DST <= 2 * MOE_SRC
_INFO_GID = EPG


def _group_tiles(cum_ref):
    n_src = cum_ref.shape[0]
    tot = [cum_ref[n_src - 1, g] for g in range(N_GROUPS)]
    ends, acc = [], 0
    for t in tot:
        acc = acc + lax.shift_right_logical(t + (MOE_DST - 1), _LOG2_DST)
        ends.append(acc)
    return tot, ends


def _dest_tile(d, cum_ref):
    tot, ends = _group_tiles(cum_ref)
    g = ((d >= ends[0]).astype(jnp.int32) + (d >= ends[1]).astype(jnp.int32)
         + (d >= ends[2]).astype(jnp.int32))
    first = jnp.where(g == 0, 0, jnp.where(g == 1, ends[0], jnp.where(g == 2, ends[1], ends[2])))
    tot_g = jnp.where(g == 0, tot[0], jnp.where(g == 1, tot[1], jnp.where(g == 2, tot[2], tot[3])))
    k0 = (d - first) * MOE_DST
    n_valid = jnp.clip(tot_g - k0, 0, MOE_DST)
    return g, k0, n_valid


def _counts_before(cum_ref, s, g):
    return jnp.where(s > 0, cum_ref[jnp.maximum(s - 1, 0), g], 0)


MOE_ALIGN = 16
_LOG2_ALIGN = MOE_ALIGN.bit_length() - 1
_LOC_ROWS = MOE_SRC + N_GROUPS * MOE_ALIGN
_INFO_LRANK = EPG + 1
H_EXT = D_MODEL + LANES
_PIECES = [MOE_SRC >> i for i in range((MOE_SRC // MOE_ALIGN).bit_length())]


def _moe_sort_tile(s, x_ref, a_ref, wo_ref, mod_ref, g_ref, wr_ref, rb_ref, xo_ref, info_ref, cum_ref,
                   lsort, seg_s, base_s):
    x = x_ref[...] + mod_ref[0, 2:3, :] * _dot(a_ref[...], wo_ref[...])
    xo_ref[...] = x
    h = _rms(x, g_ref[...]) * (1.0 + mod_ref[0, 4:5, :]) + mod_ref[0, 3:4, :]
    h_hi = h.astype(BF16)
    h_lo = (h - h_hi.astype(F32)).astype(BF16)
    hi_w = _dot(h_hi, wr_ref[...])
    logits = hi_w[:, :LANES] + (_dot(h_lo, wr_ref[:, :LANES]) + hi_w[:, LANES:])
    scores = _sigmoid(logits)
    gidx, gates = _route(scores.T[:N_EXPERTS], (scores + rb_ref[...]).T[:N_EXPERTS])
    tm = h.shape[0]
    sub = lax.broadcasted_iota(jnp.int32, (8, 1), 0)
    onehot = (sub == gidx).astype(F32)
    tri = (lax.broadcasted_iota(jnp.int32, (tm, tm), 0)
           <= lax.broadcasted_iota(jnp.int32, (tm, tm), 1)).astype(BF16)
    prefix = _dot(onehot.astype(BF16), tri)
    lrank = jnp.sum((prefix - 1.0) * onehot, axis=0, keepdims=True)

    count = [jnp.sum(onehot[g:g + 1, :]).astype(jnp.int32) for g in range(N_GROUPS)]
    padded = [lax.shift_left(lax.shift_right_logical(c + (MOE_ALIGN - 1), _LOG2_ALIGN), _LOG2_ALIGN)
              for c in count]
    loff, acc = [], 0
    for p in padded:
        loff.append(acc)
        acc = acc + p
    lpos = lrank + functools.reduce(
        lambda a, b: a + b, [jnp.where(gidx == g, jnp.asarray(v, jnp.int32).astype(F32), 0.0)
                             for g, v in enumerate(loff)])
    infot = jnp.concatenate(gates + [gidx.astype(F32), lrank, jnp.zeros((LANES - EPG - 2, tm), F32)], axis=0)
    info_ref[...] = infot.T
    g_hi = [gt.astype(BF16).astype(F32) for gt in gates]
    g_lo = [gt - gh for gt, gh in zip(gates, g_hi)]
    gext = jnp.concatenate(g_hi + g_lo + [jnp.zeros((LANES - 2 * EPG, tm), F32)], axis=0)
    h_ext = jnp.concatenate([h_hi, gext.T.astype(BF16)], axis=-1)
    perm = (lax.broadcasted_iota(jnp.int32, (_LOC_ROWS, 1), 0).astype(F32) == lpos).astype(BF16)
    lsort[s] = _dot(perm, h_ext).astype(BF16)

    new_base = []
    for g in range(N_GROUPS):
        seg_s[s, g] = padded[g]
        new_base.append(base_s[g] + padded[g])
        base_s[g] = new_base[g]
    cum = functools.reduce(lambda a, b: a + b,
                           [jnp.where(sub == g, new_base[g], 0) for g in range(N_GROUPS)])
    cum_ref[0] = jnp.broadcast_to(cum, (8, LANES))


def _copy_pieces(length, make_copy, wait):
    done = jnp.int32(0)
    for rows in _PIECES:
        take = (length & rows) != 0

        @pl.when(take)
        def _():
            cp = make_copy(pl.multiple_of(done, MOE_ALIGN), rows)
            cp.wait() if wait else cp.start()
        done = done + jnp.where(take, rows, 0)


def _moe_sort_kernel(x_ref, a_ref, wo_ref, mod_ref, g_ref, wr_ref, rb_ref, xo_ref, info_ref, cum_ref, srt_ref,
                     lsort, zeros_v, seg_s, base_s, sem):
    step = pl.program_id(0)
    n_src = lsort.shape[0]
    n_rows = srt_ref.shape[0]

    @pl.when(step == 0)
    def _():
        for g in range(N_GROUPS):
            base_s[g] = 0
        zeros_v[...] = jnp.zeros_like(zeros_v)

    @pl.when(step < n_src // MOE_SORT_TILES)
    def _():
        for j in range(MOE_SORT_TILES):
            rows = pl.ds(j * MOE_SRC, MOE_SRC)
            _moe_sort_tile(step * MOE_SORT_TILES + j, x_ref.at[rows], a_ref.at[rows], wo_ref, mod_ref, g_ref,
                           wr_ref, rb_ref, xo_ref.at[rows], info_ref.at[rows], cum_ref.at[pl.ds(j, 1)],
                           lsort, seg_s, base_s)

    @pl.when(step == n_src // MOE_SORT_TILES)
    def _():
        starts, acc = [], jnp.int32(0)
        for g in range(N_GROUPS):
            starts.append(acc)
            acc = acc + lax.shift_left(
                lax.shift_right_logical(base_s[g] + (MOE_DST - 1), _LOG2_DST), _LOG2_DST)
        end = acc

        def all_copies(wait):
            def tile_copies(t, dst):
                src = jnp.int32(0)
                new_dst = []
                for g in range(N_GROUPS):
                    n = seg_s[t, g]
                    _copy_pieces(n, lambda off, rows, src=src, g=g: pltpu.make_async_copy(
                        lsort.at[t, pl.ds(pl.multiple_of(src + off, MOE_ALIGN), rows)],
                        srt_ref.at[pl.ds(pl.multiple_of(dst[g] + off, MOE_ALIGN), rows)], sem.at[0]), wait)
                    src = src + n
                    new_dst.append(dst[g] + n)
                return tuple(new_dst)

            dst = lax.fori_loop(0, n_src, tile_copies, tuple(starts))
            for g in range(N_GROUPS):
                gap_end = starts[g + 1] if g + 1 < N_GROUPS else end
                _copy_pieces(gap_end - dst[g], lambda off, rows, g=g: pltpu.make_async_copy(
                    zeros_v.at[pl.ds(0, rows)],
                    srt_ref.at[pl.ds(pl.multiple_of(dst[g] + off, MOE_ALIGN), rows)], sem.at[0]), wait)

            def zero_tile(r, carry):
                cp = pltpu.make_async_copy(
                    zeros_v, srt_ref.at[pl.ds(pl.multiple_of(end + r * MOE_DST, MOE_DST), MOE_DST)], sem.at[0])
                cp.wait() if wait else cp.start()
                return carry

            lax.fori_loop(0, lax.shift_right_logical(n_rows - end, _LOG2_DST), zero_tile, 0)

        all_copies(False)
        all_copies(True)


def _moe_ffn2_kernel(cum_ref, xs_ref, wg_ref, wu_ref, wd_ref, y_ref):
    _, ends = _group_tiles(cum_ref)

    @pl.when(pl.program_id(0) >= ends[-1])
    def _():
        y_ref[...] = jnp.zeros_like(y_ref)

    @pl.when(pl.program_id(0) < ends[-1])
    def _():
        xs = xs_ref[:, :D_MODEL]
        gext = xs_ref[:, D_MODEL:].astype(F32)
        acts = []
        for i in range(EPG):
            a = _dot(xs, wg_ref[i])
            u = _dot(xs, wu_ref[i])
            gate = gext[:, i:i + 1] + gext[:, EPG + i:EPG + i + 1]
            acts.append(((a * _sigmoid(a)) * u * gate).astype(BF16))
        act = jnp.concatenate(acts, axis=-1)
        y_ref[...] = _dot(act, wd_ref[...].reshape(EPG * D_EXPERT, D_MODEL)).astype(BF16)


def _segment_row(cum_ref, s, g):
    _, ends = _group_tiles(cum_ref)
    first_tile = ends[g - 1] if g > 0 else 0
    return first_tile * MOE_DST + _counts_before(cum_ref, s, g)


def _unsort_window(cum_ref, s, g, n_rows):
    return pl.multiple_of(jnp.minimum(_segment_row(cum_ref, s, g), n_rows - MOE_SRC), MOE_ALIGN)


def _moe_unsort2_kernel(*refs, final_norm):
    if final_norm:
        cum_ref, x_ref, mod_ref, info_ref, y_ref, gf_ref, o_ref = refs
    else:
        cum_ref, x_ref, mod_ref, info_ref, y_ref, o_ref = refs
    n_rows = y_ref.shape[0]
    col = lax.broadcasted_iota(jnp.int32, (1, MOE_SRC), 1).astype(F32)
    for j in range(MOE_UNSORT_TILES):
        s = pl.program_id(0) * MOE_UNSORT_TILES + j
        rows = slice(j * MOE_SRC, (j + 1) * MOE_SRC)
        inf = info_ref[rows, :]
        gid = inf[:, _INFO_GID:_INFO_GID + 1]
        lrank = inf[:, _INFO_LRANK:_INFO_LRANK + 1]
        perms, wins = [], []
        for g in range(N_GROUPS):
            win = _unsort_window(cum_ref, s, g, n_rows)
            shift = _segment_row(cum_ref, s, g) - win
            rel = jnp.where(gid == float(g), lrank + shift.astype(F32), -1.0)
            perms.append((rel == col).astype(BF16))
            wins.append(y_ref[pl.ds(win, MOE_SRC), :])
        out = x_ref[rows, :] + mod_ref[0, 5:6, :] * _dot(jnp.concatenate(perms, axis=-1),
                                                         jnp.concatenate(wins, axis=0))
        o_ref[rows, :] = _rms(out, gf_ref[...]) if final_norm else out


def _mixer_out_and_moe(x, att, w_o, mod, g, w_router, router_bias, wg, wu, wd, layer, rows_per_cond,
                       final_g=None):
    n = x.shape[0]
    n_src = n // MOE_SRC
    n_dst = (n + N_GROUPS * n_src * (MOE_ALIGN - 1)) // MOE_DST + N_GROUPS
    n_rows = n_dst * MOE_DST
    ts = MOE_SORT_TILES * MOE_SRC
    last = n // ts - 1
    row = lambda i: (jnp.minimum(i, last), 0)
    full = lambda i: (0, 0)
    x, info, cum, srt = pl.pallas_call(
        _moe_sort_kernel,
        grid=(n // ts + 1,),
        in_specs=[
            pl.BlockSpec((ts, D_MODEL), row),
            pl.BlockSpec((ts, D_MODEL), row),
            pl.BlockSpec(w_o.shape, full),
            pl.BlockSpec((1, 6, D_MODEL), (lambda i: (0, 0, 0)) if rows_per_cond is None else
                         (lambda i: (1 + jnp.minimum(i, last) // (rows_per_cond // ts), 0, 0))),
            pl.BlockSpec((1, D_MODEL), full),
            pl.BlockSpec(w_router.shape, full),
            pl.BlockSpec(router_bias.shape, full),
        ],
        out_specs=[pl.BlockSpec((ts, D_MODEL), row),
                   pl.BlockSpec((ts, LANES), row),
                   pl.BlockSpec((MOE_SORT_TILES, 8, LANES), lambda i: (jnp.minimum(i, last), 0, 0)),
                   pl.BlockSpec(memory_space=pl.ANY)],
        out_shape=[jax.ShapeDtypeStruct((n, D_MODEL), F32),
                   jax.ShapeDtypeStruct((n, LANES), F32),
                   jax.ShapeDtypeStruct((n_src, 8, LANES), jnp.int32),
                   jax.ShapeDtypeStruct((n_rows, H_EXT), BF16)],
        scratch_shapes=[pltpu.VMEM((n_src, _LOC_ROWS, H_EXT), BF16),
                        pltpu.VMEM((MOE_DST, H_EXT), BF16),
                        pltpu.SMEM((n_src, N_GROUPS), jnp.int32),
                        pltpu.SMEM((N_GROUPS,), jnp.int32),
                        pltpu.SemaphoreType.DMA((1,))],
        compiler_params=_cparams("arbitrary"),
        name="moe_sort",
    )(x, att, w_o, mod, g, w_router, router_bias)
    cum = cum[:, :N_GROUPS, 0]

    blk = lambda width: pl.BlockSpec((MOE_DST, width), lambda d, c: (d, 0))
    expert = lambda shape: pl.BlockSpec(shape, lambda d, c: (layer, _dest_tile(d, c)[0], 0, 0))
    y = pl.pallas_call(
        _moe_ffn2_kernel,
        grid_spec=pltpu.PrefetchScalarGridSpec(
            num_scalar_prefetch=1,
            grid=(n_dst,),
            in_specs=[blk(H_EXT),
                      expert((None, EPG, D_MODEL, D_EXPERT)),
                      expert((None, EPG, D_MODEL, D_EXPERT)),
                      expert((None, EPG, D_EXPERT, D_MODEL))],
            out_specs=blk(D_MODEL),
        ),
        out_shape=jax.ShapeDtypeStruct((n_rows, D_MODEL), BF16),
        compiler_params=_cparams("arbitrary"),
        name="moe_experts",
    )(cum, srt, wg, wu, wd)

    tu = MOE_UNSORT_TILES * MOE_SRC
    in_specs = [pl.BlockSpec((tu, D_MODEL), lambda i, c: (i, 0)),
                _mod_spec(None if rows_per_cond is None else rows_per_cond // tu),
                pl.BlockSpec((tu, LANES), lambda i, c: (i, 0)),
                pl.BlockSpec(y.shape, lambda i, c: (0, 0), pipeline_mode=pl.Buffered(1))]
    args = [cum, x, mod, info, y]
    if final_g is not None:
        in_specs.append(pl.BlockSpec((1, D_MODEL), lambda i, c: (0, 0)))
        args.append(final_g)
    return pl.pallas_call(
        functools.partial(_moe_unsort2_kernel, final_norm=final_g is not None),
        grid_spec=pltpu.PrefetchScalarGridSpec(
            num_scalar_prefetch=1,
            grid=(n_src // MOE_UNSORT_TILES,),
            in_specs=in_specs,
            out_specs=pl.BlockSpec((tu, D_MODEL), lambda i, c: (i, 0)),
        ),
        out_shape=jax.ShapeDtypeStruct((n, D_MODEL), F32),
        compiler_params=_cparams("parallel"),
        name="moe_unsort",
    )(*args)


def _rope_tables(n_lat):
    t = np.arange(n_lat)
    n_freq = MLA_ROPE // 4
    inv_freq = jnp.asarray(ROPE_THETA, F32) ** (-jnp.arange(n_freq, dtype=F32) / n_freq)
    ar = jnp.asarray(t // GRID_W, F32)[:, None] * inv_freq
    ac = jnp.asarray(t % GRID_W, F32)[:, None] * inv_freq
    cos = jnp.concatenate([jnp.cos(ar), jnp.cos(ar), jnp.cos(ac), jnp.cos(ac)], axis=-1)
    sin = jnp.concatenate([-jnp.sin(ar), jnp.sin(ar), -jnp.sin(ac), jnp.sin(ac)], axis=-1)
    return (jnp.tile(cos, (1, MLA_HEADS)), jnp.tile(sin, (1, MLA_HEADS)),
            jnp.tile(cos, (1, LANES // MLA_ROPE)), jnp.tile(sin, (1, LANES // MLA_ROPE)))


_ROPE_SWAP = np.concatenate([np.arange(8, 16), np.arange(0, 8), np.arange(24, 32), np.arange(16, 24)])


def _mla_weights(w_in, w_uq, w_ukv):
    o = MLA_Q_LORA + MLA_KV_LORA
    kr = w_in[:, o:]
    rep = LANES // MLA_ROPE
    w_in_x = jnp.concatenate([w_in[:, :o], jnp.tile(kr, (1, rep)), jnp.tile(kr[:, _ROPE_SWAP], (1, rep))],
                             axis=-1).astype(BF16)
    uq = w_uq.reshape(MLA_Q_LORA, MLA_HEADS, MLA_NOPE + MLA_ROPE)
    q_nope = uq[:, :, :MLA_NOPE].reshape(MLA_Q_LORA, -1)
    q_rope = uq[:, :, MLA_NOPE:]
    w_uq_x = jnp.concatenate([q_nope, q_rope.reshape(MLA_Q_LORA, -1),
                              q_rope[:, :, _ROPE_SWAP].reshape(MLA_Q_LORA, -1)], axis=-1).astype(BF16)
    ukv = w_ukv.reshape(MLA_KV_LORA, MLA_HEADS, MLA_NOPE + MLA_V)
    w_uk = ukv[:, :, :MLA_NOPE].reshape(MLA_KV_LORA, -1).astype(BF16)
    w_uv = ukv[:, :, MLA_NOPE:].reshape(MLA_KV_LORA, -1).astype(BF16)
    return w_in_x, w_uq_x, w_uk, w_uv


def kernel(x_prompt, x_sample, cache_mla_ckv, cache_mla_krope, cache_nat_k, cache_nat_v, c, c_ctx,
           w_ada, b_ada, norm_mix, norm_ffn, norm_final, mla_w_in, mla_q_norm, mla_w_uq, mla_kv_norm,
           mla_w_ukv, mla_w_o, nat_w_qkv, nat_rpb, nat_w_o, w_router, router_bias,
           moe_w_gate, moe_w_up, moe_w_down):
    B, S, D = x_prompt.shape
    Bd, Sd, _ = x_sample.shape
    assert D == D_MODEL and Bd + 1 <= 8
    tm_c, tm_s = 512, 512

    xc = x_prompt.reshape(B * S, D)
    xs = x_sample.reshape(Bd * Sd, D)
    cond8 = jnp.concatenate([c_ctx[None, :], c, jnp.zeros((8 - 1 - Bd, D), F32)], axis=0)
    mod_all = _ada_modulation(cond8, w_ada, b_ada).reshape(DEPTH, 8, 6, D)

    rope_tabs = _rope_tables(Sd)
    wr = jnp.pad(w_router, ((0, 0), (0, LANES - N_EXPERTS)))
    wr_hi = wr.astype(BF16)
    wr = jnp.concatenate([wr_hi, (wr - wr_hi.astype(F32)).astype(BF16)], axis=1)
    rb = jnp.pad(router_bias, (0, LANES - N_EXPERTS)).reshape(1, LANES)
    wg = moe_w_gate.astype(BF16)
    wu = moe_w_up.astype(BF16)
    wd = moe_w_down.astype(BF16)

    nat_bias = _nat_bias_pairs(nat_rpb, Sd // GRID_W)
    ckv_buf = kr_buf = k_buf = v_buf = None
    for layer in range(DEPTH):
        mod = mod_all[layer]
        g_mix = norm_mix[layer][None, :]
        j = layer // 2
        if layer % 2 == 0:
            w_in_x, w_uq_x, w_uk, w_uv = _mla_weights(mla_w_in[j], mla_w_uq[j], mla_w_ukv[j])
            w_in_c = w_in_x[:, :MLA_Q_LORA + MLA_KV_LORA + LANES]
            w_uq_c = w_uq_x[:, :MLA_HEADS * (MLA_NOPE + MLA_ROPE)]
            qg = mla_q_norm[j][None, :]
            kvg = mla_kv_norm[j][None, :]
            w_o = mla_w_o[j].astype(BF16)
            qn, qr, ckv_buf, kr_buf, kr4 = _premix_mla(xc, mod, g_mix, w_in_c, qg, w_uq_c, kvg, None, None, tm_c,
                                                       cache=(ckv_buf, kr_buf, j, (DEPTH + 1) // 2, B))
            att_c = _mla_attention(qn, qr, ckv_buf, kr4, None, None, w_uk, w_uv, B, S, ckv_slot=j)
            qn, qr, ckv, kr, kr4 = _premix_mla(xs, mod, g_mix, w_in_x, qg, w_uq_x, kvg, rope_tabs, Sd, tm_s)
            cache_kr4 = jnp.tile(cache_mla_krope[:, j], (1, 1, LANES // MLA_ROPE))
            att_s = _mla_attention(qn, qr, ckv, kr4, cache_mla_ckv[:, j], cache_kr4, w_uk, w_uv, Bd, 512)
        else:
            w_qkv = nat_w_qkv[j].astype(BF16)
            w_o = nat_w_o[j].astype(BF16)
            q, k_buf, v_buf = _premix_nat_cache(xc, mod, g_mix, w_qkv, k_buf, v_buf, j, DEPTH // 2, B, tm_c)
            att_c = _dense_attention(q, k_buf, v_buf, j)
            q, k, v = _premix_nat(xs, mod, g_mix, w_qkv, Sd, tm_s, BF16)
            att_s = _nat_attention(q, k, v,
                                   cache_nat_k[:, j].reshape(Bd, -1, D), cache_nat_v[:, j].reshape(Bd, -1, D),
                                   nat_bias, j, Bd)
        g_ffn = norm_ffn[layer][None, :]
        final_g = norm_final[None, :] if layer == DEPTH - 1 else None
        xc = _mixer_out_and_moe(xc, att_c, w_o, mod, g_ffn, wr, rb, wg, wu, wd, layer, None, final_g)
        xs = _mixer_out_and_moe(xs, att_s, w_o, mod, g_ffn, wr, rb, wg, wu, wd, layer, Sd, final_g)

    y_prompt = xc.reshape(B, S, D)
    y_sample = xs.reshape(Bd, Sd, D)
    kv_shape = (B, DEPTH // 2, S, NAT_HEADS, NAT_DH)
    return (y_prompt, y_sample, ckv_buf, kr_buf, k_buf.reshape(kv_shape), v_buf.reshape(kv_shape))
```

```python
import functools

import numpy as np
import jax
import jax.numpy as jnp
from jax import lax
from jax.experimental import pallas as pl
from jax.experimental.pallas import tpu as pltpu

F32 = jnp.float32
BF16 = jnp.bfloat16

D_MODEL = 1024
DEPTH = 4
GRID_W = 64
LANES = 128
MLA_HEADS = 16
MLA_NOPE = 64
MLA_ROPE = 32
MLA_V = 64
MLA_Q_LORA = 384
MLA_KV_LORA = 256
MLA_SCALE = (MLA_NOPE + MLA_ROPE) ** -0.5
ROPE_THETA = 10000.0
NAT_HEADS = 16
NAT_DH = 64
NAT_SCALE = NAT_DH ** -0.5
WIN_H = 8
WIN_W = 16
NAT_QROWS = 4
NAT_KROWS = NAT_QROWS + WIN_H
N_EXPERTS = 16
N_GROUPS = 4
EPG = N_EXPERTS // N_GROUPS
D_EXPERT = 256
NORM_EPS = 1e-6
NEG_INF = -1e30

HEAD_PAIRS = MLA_HEADS // 2
VMEM_LIMIT = 56 * 1024 * 1024


def _cparams(*sem):
    return pltpu.CompilerParams(dimension_semantics=sem, vmem_limit_bytes=VMEM_LIMIT)


def _sigmoid(x):
    return 1.0 / (1.0 + jnp.exp(-x))


def _rms(x, g):
    ms = jnp.mean(x * x, axis=-1, keepdims=True)
    return x * lax.rsqrt(ms + NORM_EPS) * g


def _dot(a, b):
    return jnp.dot(a, b, preferred_element_type=F32)


def _dot_nt(a, b):
    return lax.dot_general(a, b, (((1,), (1,)), ((), ())), preferred_element_type=F32)


def _lane_mask(width, idx, dtype):
    lane = lax.broadcasted_iota(jnp.int32, (1, LANES), 1)
    return ((lane >= idx * width) & (lane < (idx + 1) * width)).astype(dtype)


def _ada_kernel(cond_ref, w_ref, b_ref, o_ref):
    c = cond_ref[...]
    s = (c * _sigmoid(c)).astype(BF16)
    o_ref[0] = _dot(s, w_ref[0].astype(BF16)) + b_ref[0]


def _ada_modulation(cond8, w_ada, b_ada):
    n_chunk = 6
    return pl.pallas_call(
        _ada_kernel,
        grid=(DEPTH, n_chunk),
        in_specs=[
            pl.BlockSpec((8, D_MODEL), lambda l, j: (0, 0)),
            pl.BlockSpec((1, D_MODEL, D_MODEL), lambda l, j: (l, 0, j)),
            pl.BlockSpec((1, 1, D_MODEL), lambda l, j: (l, 0, j)),
        ],
        out_specs=pl.BlockSpec((1, 8, D_MODEL), lambda l, j: (l, 0, j)),
        out_shape=jax.ShapeDtypeStruct((DEPTH, 8, 6 * D_MODEL), F32),
        compiler_params=_cparams("parallel", "parallel"),
        name="ada_modulation",
    )(cond8, w_ada, b_ada.reshape(DEPTH, 1, 6 * D_MODEL))


def _mod_spec(rows_per_cond):
    if rows_per_cond is None:
        return pl.BlockSpec((1, 6, D_MODEL), lambda i, *_: (0, 0, 0))
    return pl.BlockSpec((1, 6, D_MODEL), lambda i, *_: (1 + i // rows_per_cond, 0, 0))


def _premix_mla_kernel(*refs, rope, cache):
    if rope:
        (x_ref, mod_ref, g_ref, w_in_ref, qg_ref, w_uq_ref, kvg_ref,
         cosq_ref, sinq_ref, cosk_ref, sink_ref,
         qn_ref, qr_ref, ckv_ref, kr_ref, kr4_ref) = refs
    elif cache == "update":
        (x_ref, mod_ref, g_ref, w_in_ref, qg_ref, w_uq_ref, kvg_ref, _, _,
         qn_ref, qr_ref, ckv_ref, kr_ref, kr4_ref) = refs
    else:
        (x_ref, mod_ref, g_ref, w_in_ref, qg_ref, w_uq_ref, kvg_ref,
         qn_ref, qr_ref, ckv_ref, kr_ref, kr4_ref) = refs
    h = _rms(x_ref[...], g_ref[...]) * (1.0 + mod_ref[0, 1:2, :]) + mod_ref[0, 0:1, :]
    lat = _dot(h.astype(BF16), w_in_ref[...])
    c_q = lat[:, :MLA_Q_LORA]
    c_kv = lat[:, MLA_Q_LORA:MLA_Q_LORA + MLA_KV_LORA]
    o = MLA_Q_LORA + MLA_KV_LORA
    kr4 = lat[:, o:o + LANES]
    q = _dot(_rms(c_q, qg_ref[...]).astype(BF16), w_uq_ref[...])
    n_nope = MLA_HEADS * MLA_NOPE
    n_rope = MLA_HEADS * MLA_ROPE
    qr = q[:, n_nope:n_nope + n_rope]
    if rope:
        qr = qr * cosq_ref[...] + q[:, n_nope + n_rope:] * sinq_ref[...]
        kr4 = kr4 * cosk_ref[...] + lat[:, o + LANES:o + 2 * LANES] * sink_ref[...]
    qn_ref[...] = q[:, :n_nope].astype(BF16)
    qr_ref[...] = qr.astype(BF16)
    kr4_ref[...] = kr4.astype(BF16)
    ckv = _rms(c_kv, kvg_ref[...])
    if cache is None:
        ckv_ref[...] = ckv
        kr_ref[...] = kr4[:, :MLA_ROPE]
    else:
        nb, n_slots, s, _ = ckv_ref.shape
        ckv_ref[:, 0] = ckv.reshape(nb, s, MLA_KV_LORA)
        kr_ref[:, 0] = kr4[:, :MLA_ROPE].reshape(nb, s, MLA_ROPE)
        for slot in range(1, n_slots):
            ckv_ref[:, slot] = jnp.zeros((nb, s, MLA_KV_LORA), F32)
            kr_ref[:, slot] = jnp.zeros((nb, s, MLA_ROPE), F32)


def _premix_mla(x, mod, g, w_in, qg, w_uq, kvg, rope_tabs, rows_per_cond, tm, cache=None):
    n = x.shape[0]
    rope = rope_tabs is not None
    row = lambda i: (i, 0)
    full = lambda i: (0, 0)
    in_specs = [
        pl.BlockSpec((tm, D_MODEL), row),
        _mod_spec(None if rows_per_cond is None else rows_per_cond // tm),
        pl.BlockSpec((1, D_MODEL), full),
        pl.BlockSpec(w_in.shape, full),
        pl.BlockSpec((1, MLA_Q_LORA), full),
        pl.BlockSpec(w_uq.shape, full),
        pl.BlockSpec((1, MLA_KV_LORA), full),
    ]
    args = [x, mod, g, w_in, qg, w_uq, kvg]
    if rope:
        nblk = rope_tabs[0].shape[0] // tm
        pos = lambda i: (i % nblk, 0)
        for t in rope_tabs:
            in_specs.append(pl.BlockSpec((tm, t.shape[1]), pos))
            args.append(t)
    widths = (MLA_HEADS * MLA_NOPE, MLA_HEADS * MLA_ROPE, MLA_KV_LORA, MLA_ROPE, LANES)
    dtypes = (BF16, BF16, F32, F32, BF16)
    out_specs = [pl.BlockSpec((tm, w), row) for w in widths]
    out_shape = [jax.ShapeDtypeStruct((n, w), d) for w, d in zip(widths, dtypes)]
    mode, aliases = None, {}
    if cache is not None:
        assert not rope
        ckv_buf, kr_buf, slot, n_slots, n_batch = cache
        s = n // n_batch
        nb = tm // s
        mode = "first" if ckv_buf is None else "update"
        if mode == "first":
            assert slot == 0
            spec = lambda w: pl.BlockSpec((nb, n_slots, s, w), lambda i: (i, 0, 0, 0))
        else:
            spec = lambda w: pl.BlockSpec((nb, 1, s, w), lambda i: (i, slot, 0, 0))
            aliases = {len(args): 2, len(args) + 1: 3}
            in_specs += [pl.BlockSpec(memory_space=pl.ANY)] * 2
            args += [ckv_buf, kr_buf]
        for idx in (2, 3):
            out_specs[idx] = spec(widths[idx])
            out_shape[idx] = jax.ShapeDtypeStruct((n_batch, n_slots, s, widths[idx]), F32)
    return pl.pallas_call(
        functools.partial(_premix_mla_kernel, rope=rope, cache=mode),
        grid=(n // tm,),
        in_specs=in_specs,
        out_specs=out_specs,
        out_shape=out_shape,
        input_output_aliases=aliases,
        compiler_params=_cparams("parallel"),
        name="premix_mla_rope" if rope else "premix_mla",
    )(*args)


LOG2E = 1.4426950408889634


def _softmax_pv(s_list, v_list, scale=1.0):
    m = functools.reduce(jnp.maximum, [jnp.max(s, axis=-1, keepdims=True) for s in s_list])
    e_list = [jnp.exp2((s - m) * (scale * LOG2E)) for s in s_list]
    l = functools.reduce(lambda a, b: a + b, [jnp.sum(e, axis=-1, keepdims=True) for e in e_list])
    o = functools.reduce(lambda a, b: a + b,
                         [_dot(e.astype(BF16), v) for e, v in zip(e_list, v_list)])
    return o / l


def _stack_heads(q, masks):
    return jnp.concatenate([q * m for m in masks], axis=0)


def _unstack_heads(o2, width):
    tq = o2.shape[0] // 2
    lane = lax.broadcasted_iota(jnp.int32, (1, LANES), 1)
    return jnp.where(lane < width, o2[:tq], o2[tq:])


MLA_ATTN_BATCHES = 2


def _mla_attn_kernel(*refs, cached, nb):
    if cached:
        (qn_ref, qr_ref, ckv_ref, kr4_ref, cckv_ref, ckr4_ref, w_uk_ref, w_uv_ref,
         o_ref, kn_s, v_s, kr_s) = refs
    else:
        (qn_ref, qr_ref, ckv_ref, kr4_ref, w_uk_ref, w_uv_ref, o_ref, kn_s, v_s, kr_s) = refs
    from_slot = len(ckv_ref.shape) == 3
    t_own = ckv_ref.shape[1] if from_slot else ckv_ref.shape[0] // nb

    for b in range(nb):
        own = slice(b * t_own, (b + 1) * t_own)

        def fill_kv():
            c = (ckv_ref[b] if from_slot else ckv_ref[own, :]).astype(BF16)
            kn_s[0:t_own, :] = _dot(c, w_uk_ref[...]).astype(BF16)
            v_s[0:t_own, :] = _dot(c, w_uv_ref[...]).astype(BF16)
            kr_s[0:t_own, :] = kr4_ref[own, :]
            if cached:
                cc = cckv_ref[0].astype(BF16)
                kn_s[t_own:, :] = _dot(cc, w_uk_ref[...]).astype(BF16)
                v_s[t_own:, :] = _dot(cc, w_uv_ref[...]).astype(BF16)
                kr_s[t_own:, :] = ckr4_ref[0].astype(BF16)

        if cached:
            pl.when(pl.program_id(1) == 0)(fill_kv)
            qrows = slice(None)
        else:
            fill_kv()
            qrows = own

        kr4 = kr_s[...]
        for p in range(HEAD_PAIRS):
            sl = slice(p * LANES, (p + 1) * LANES)
            qn = qn_ref[qrows, sl]
            qr = qr_ref[qrows, (p // 2) * LANES:(p // 2 + 1) * LANES]
            k_cat = jnp.concatenate([kn_s[:, sl], kr4], axis=-1)
            q_cat = jnp.concatenate(
                [_stack_heads(qn, [_lane_mask(MLA_NOPE, i, BF16) for i in range(2)]),
                 _stack_heads(qr, [_lane_mask(MLA_ROPE, (2 * p + i) % 4, BF16) for i in range(2)])], axis=-1)
            o2 = _softmax_pv([_dot_nt(q_cat, k_cat)], [v_s[:, sl]], MLA_SCALE)
            o_ref[qrows, sl] = _unstack_heads(o2, MLA_V).astype(BF16)


def _mla_attention(qn, qr, ckv, kr4, cache_ckv, cache_kr4, w_uk, w_uv, n_batch, tq, ckv_slot=None):
    n = qn.shape[0]
    s_own = n // n_batch
    nq = s_own // tq
    cached = cache_ckv is not None
    t_all = s_own + (cache_ckv.shape[1] if cached else 0)
    nb = 1 if cached else MLA_ATTN_BATCHES
    assert cached or nq == 1
    tq = tq * nb
    qrow = lambda b, j: (b * nq + j, 0)
    own = lambda b, j: (b, 0)
    full = lambda b, j: (0, 0)
    in_specs = [
        pl.BlockSpec((tq, qn.shape[1]), qrow),
        pl.BlockSpec((tq, qr.shape[1]), qrow),
        pl.BlockSpec((nb * s_own, MLA_KV_LORA), own) if ckv_slot is None else
        pl.BlockSpec((nb, None, s_own, MLA_KV_LORA), lambda b, j: (b, ckv_slot, 0, 0)),
        pl.BlockSpec((nb * s_own, LANES), own),
    ]
    args = [qn, qr, ckv, kr4]
    if cached:
        in_specs += [pl.BlockSpec((1,) + cache_ckv.shape[1:], lambda b, j: (b, 0, 0)),
                     pl.BlockSpec((1,) + cache_kr4.shape[1:], lambda b, j: (b, 0, 0))]
        args += [cache_ckv, cache_kr4]
    in_specs += [pl.BlockSpec(w_uk.shape, full), pl.BlockSpec(w_uv.shape, full)]
    args += [w_uk, w_uv]
    return pl.pallas_call(
        functools.partial(_mla_attn_kernel, cached=cached, nb=nb),
        grid=(n_batch // nb, nq),
        in_specs=in_specs,
        out_specs=pl.BlockSpec((tq, D_MODEL), qrow),
        out_shape=jax.ShapeDtypeStruct((n, D_MODEL), BF16),
        scratch_shapes=[pltpu.VMEM((t_all, D_MODEL), BF16), pltpu.VMEM((t_all, D_MODEL), BF16),
                        pltpu.VMEM((t_all, LANES), BF16)],
        compiler_params=_cparams("parallel", "arbitrary"),
        name="mla_attention_cached" if cached else "mla_attention",
    )(*args)


def _premix_nat_kernel(x_ref, mod_ref, g_ref, w_ref, q_ref, k_ref, v_ref):
    h = _rms(x_ref[...], g_ref[...]) * (1.0 + mod_ref[0, 1:2, :]) + mod_ref[0, 0:1, :]
    qkv = _dot(h.astype(BF16), w_ref[...])
    q_ref[...] = qkv[:, :D_MODEL].astype(q_ref.dtype)
    k_ref[...] = qkv[:, D_MODEL:2 * D_MODEL].astype(k_ref.dtype)
    v_ref[...] = qkv[:, 2 * D_MODEL:].astype(v_ref.dtype)


def _premix_nat(x, mod, g, w_qkv, rows_per_cond, tm, kv_dtype):
    n = x.shape[0]
    row = lambda i: (i, 0)
    full = lambda i: (0, 0)
    return pl.pallas_call(
        _premix_nat_kernel,
        grid=(n // tm,),
        in_specs=[
            pl.BlockSpec((tm, D_MODEL), row),
            _mod_spec(None if rows_per_cond is None else rows_per_cond // tm),
            pl.BlockSpec((1, D_MODEL), full),
            pl.BlockSpec(w_qkv.shape, full),
        ],
        out_specs=[pl.BlockSpec((tm, D_MODEL), row)] * 3,
        out_shape=[jax.ShapeDtypeStruct((n, D_MODEL), BF16),
                   jax.ShapeDtypeStruct((n, D_MODEL), kv_dtype),
                   jax.ShapeDtypeStruct((n, D_MODEL), kv_dtype)],
        compiler_params=_cparams("parallel"),
        name="premix_nat",
    )(x, mod, g, w_qkv)


def _premix_nat_cache_kernel(*refs, first):
    if first:
        x_ref, mod_ref, g_ref, w_ref, q_ref, k_ref, v_ref = refs
    else:
        x_ref, mod_ref, g_ref, w_ref, _, _, q_ref, k_ref, v_ref = refs
    h = _rms(x_ref[...], g_ref[...]) * (1.0 + mod_ref[0, 1:2, :]) + mod_ref[0, 0:1, :]
    qkv = _dot(h.astype(BF16), w_ref[...])
    q_ref[...] = qkv[:, :D_MODEL].astype(BF16)
    nb, n_slots, s, _ = k_ref.shape
    k_ref[:, 0] = qkv[:, D_MODEL:2 * D_MODEL].reshape(nb, s, D_MODEL)
    v_ref[:, 0] = qkv[:, 2 * D_MODEL:].reshape(nb, s, D_MODEL)
    for slot in range(1, n_slots):
        k_ref[:, slot] = jnp.zeros((nb, s, D_MODEL), F32)
        v_ref[:, slot] = jnp.zeros((nb, s, D_MODEL), F32)


def _premix_nat_cache(x, mod, g, w_qkv, k_buf, v_buf, slot, n_slots, n_batch, tm):
    n = x.shape[0]
    s = n // n_batch
    nb = tm // s
    first = k_buf is None
    row = lambda i: (i, 0)
    full = lambda i: (0, 0)
    in_specs = [pl.BlockSpec((tm, D_MODEL), row), _mod_spec(None), pl.BlockSpec((1, D_MODEL), full),
                pl.BlockSpec(w_qkv.shape, full)]
    args = [x, mod, g, w_qkv]
    if first:
        assert slot == 0
        kv_spec = pl.BlockSpec((nb, n_slots, s, D_MODEL), lambda i: (i, 0, 0, 0))
        aliases = {}
    else:
        in_specs += [pl.BlockSpec(memory_space=pl.ANY)] * 2
        args += [k_buf, v_buf]
        kv_spec = pl.BlockSpec((nb, 1, s, D_MODEL), lambda i: (i, slot, 0, 0))
        aliases = {4: 1, 5: 2}
    kv_shape = jax.ShapeDtypeStruct((n_batch, n_slots, s, D_MODEL), F32)
    return pl.pallas_call(
        functools.partial(_premix_nat_cache_kernel, first=first),
        grid=(n // tm,),
        in_specs=in_specs,
        out_specs=[pl.BlockSpec((tm, D_MODEL), row), kv_spec, kv_spec],
        out_shape=[jax.ShapeDtypeStruct((n, D_MODEL), BF16), kv_shape, kv_shape],
        input_output_aliases=aliases,
        compiler_params=_cparams("parallel"),
        name="premix_nat_cache",
    )(*args)


DENSE_ATTN_BATCHES = 4


def _dense_attn_kernel(q_ref, k_ref, v_ref, o_ref):
    nb, s, _ = k_ref.shape
    for b in range(nb):
        rows = slice(b * s, (b + 1) * s)
        for p in range(HEAD_PAIRS):
            sl = slice(p * LANES, (p + 1) * LANES)
            q2 = _stack_heads(q_ref[rows, sl], [_lane_mask(NAT_DH, i, BF16) for i in range(2)])
            k = k_ref[b, :, sl].astype(BF16)
            v = v_ref[b, :, sl].astype(BF16)
            o2 = _softmax_pv([_dot_nt(q2, k)], [v], NAT_SCALE)
            o_ref[rows, sl] = _unstack_heads(o2, NAT_DH).astype(BF16)


def _dense_attention(q, k_buf, v_buf, slot):
    n = q.shape[0]
    n_batch, _, s, _ = k_buf.shape
    nb = DENSE_ATTN_BATCHES
    blk = pl.BlockSpec((nb * s, D_MODEL), lambda b: (b, 0))
    kv = pl.BlockSpec((nb, None, s, D_MODEL), lambda b: (b, slot, 0, 0))
    return pl.pallas_call(
        _dense_attn_kernel,
        grid=(n_batch // nb,),
        in_specs=[blk, kv, kv],
        out_specs=blk,
        out_shape=jax.ShapeDtypeStruct((n, D_MODEL), BF16),
        compiler_params=_cparams("parallel"),
        name="dense_attention",
    )(q, k_buf, v_buf)


_NAT_QBLK = NAT_QROWS * GRID_W
_NAT_KBLK = NAT_KROWS * GRID_W


def _nat_block_plan(rows):
    assert rows % NAT_QROWS == 0 and rows >= NAT_KROWS and NAT_KROWS % 2 == 0
    plan, variants = [], []
    for r0 in range(0, rows, NAT_QROWS):
        ks = min(max(r0 - WIN_H // 2, 0), rows - NAT_KROWS)
        r = r0 + np.arange(NAT_QROWS)
        kr = ks + np.arange(NAT_KROWS)
        rs = np.clip(r - WIN_H // 2, 0, rows - WIN_H)
        valid_row = (kr[None, :] >= rs[:, None]) & (kr[None, :] < rs[:, None] + WIN_H)
        d0 = ks - r + (WIN_H - 1)
        for vi, (d0_v, valid_v) in enumerate(variants):
            if np.array_equal(d0, d0_v) and np.array_equal(valid_row, valid_v):
                break
        else:
            vi = len(variants)
            variants.append((d0, valid_row))
        plan.append((ks, vi))
    return plan, variants


def _nat_bias_row_range(variants):
    lo = min(int(d0.min()) for d0, _ in variants)
    hi = max(int(d0.max()) for d0, _ in variants) + NAT_KROWS
    return lo, hi


def _nat_bias_pairs(rpb, rows):
    n_layers, n_heads, n_dr, n_dc = rpb.shape
    _, variants = _nat_block_plan(rows)
    lo, hi = _nat_bias_row_range(variants)
    c = np.arange(GRID_W)
    cs = np.clip(c - WIN_W // 2, 0, GRID_W - WIN_W)
    valid_col = (c[None, :] >= cs[:, None]) & (c[None, :] < cs[:, None] + WIN_W)
    d_col = c[None, :] - c[:, None] + (WIN_W - 1)
    sel = (d_col[None] == np.arange(n_dc)[:, None, None]) & valid_col[None]
    rows2 = (lo + np.arange(hi - lo))[:, None] + np.arange(2)[None, :]
    in_range = (rows2 >= 0) & (rows2 < n_dr)
    sel2 = np.zeros((2, n_dc, GRID_W, 2, GRID_W), np.float32)
    mask = np.zeros((hi - lo, GRID_W, 2, GRID_W), np.float32)
    for half in range(2):
        sel2[half, :, :, half, :] = sel
        mask[:, :, half, :] = np.where(in_range[:, half, None, None] & valid_col[None], 0.0, NEG_INF)
    picked = jnp.where(jnp.asarray(in_range)[None, None, :, :, None],
                       rpb[:, :, np.clip(rows2, 0, n_dr - 1), :], 0.0)
    tab = jnp.einsum('lhdm,mck->lhdck', picked.reshape(n_layers, n_heads, hi - lo, 2 * n_dc),
                     jnp.asarray(sel2.reshape(2 * n_dc, GRID_W, 2 * GRID_W)), precision=lax.Precision.HIGHEST)
    return tab + jnp.asarray(mask.reshape(hi - lo, GRID_W, 2 * GRID_W))


def _nat_block_bias(tp_ref, head, d0, valid_row, row_lo):
    neg = jnp.full((GRID_W, 2 * GRID_W), NEG_INF, F32)
    left = lax.broadcasted_iota(jnp.int32, (1, 2 * GRID_W), 1) < GRID_W
    rows_out = []
    for dr in range(NAT_QROWS):
        pieces = []
        for a in range(0, NAT_KROWS, 2):
            ok0, ok1 = bool(valid_row[dr, a]), bool(valid_row[dr, a + 1])
            if not (ok0 or ok1):
                pieces.append(neg)
                continue
            piece = tp_ref[head, int(d0[dr]) + a - row_lo]
            if ok0 and not ok1:
                piece = jnp.where(left, piece, NEG_INF)
            elif ok1 and not ok0:
                piece = jnp.where(left, NEG_INF, piece)
            pieces.append(piece)
        rows_out.append(jnp.concatenate(pieces, axis=-1))
    return jnp.concatenate(rows_out, axis=0)


NAT_ATTN_BATCHES = 2


def _nat_attn_kernel(q_ref, k_ref, v_ref, kc_ref, vc_ref, tp_ref, o_ref, *, plan, variants, row_lo):
    nb = kc_ref.shape[0]
    s = q_ref.shape[0] // nb
    assert NAT_SCALE == 2.0 ** round(np.log2(NAT_SCALE))
    masks = [_lane_mask(NAT_DH, i, BF16) * NAT_SCALE for i in range(2)]
    for bi, (ks, var) in enumerate(plan):
        bias2 = jnp.concatenate([_nat_block_bias(tp_ref, i, *variants[var], row_lo) for i in range(2)], axis=0)
        for b in range(nb):
            rows = slice(b * s + bi * _NAT_QBLK, b * s + (bi + 1) * _NAT_QBLK)
            keys = slice(b * s + ks * GRID_W, b * s + ks * GRID_W + _NAT_KBLK)
            q2 = _stack_heads(q_ref[rows, :], masks)
            kc = kc_ref[b].astype(BF16)
            vc = vc_ref[b].astype(BF16)
            o2 = _softmax_pv([_dot_nt(q2, k_ref[keys, :]) + bias2, _dot_nt(q2, kc)], [v_ref[keys, :], vc])
            o_ref[rows, :] = _unstack_heads(o2, NAT_DH).astype(BF16)


def _nat_attention(q, k, v, cache_k, cache_v, bias, layer_slot, n_batch):
    n = q.shape[0]
    s = n // n_batch
    plan, variants = _nat_block_plan(s // GRID_W)
    row_lo, _ = _nat_bias_row_range(variants)
    nb = NAT_ATTN_BATCHES
    own = pl.BlockSpec((nb * s, LANES), lambda p, b: (b, p))
    cache = pl.BlockSpec((nb, cache_k.shape[1], LANES), lambda p, b: (b, 0, p))
    return pl.pallas_call(
        functools.partial(_nat_attn_kernel, plan=plan, variants=variants, row_lo=row_lo),
        grid=(HEAD_PAIRS, n_batch // nb),
        in_specs=[own, own, own, cache, cache,
                  pl.BlockSpec((None, 2) + bias.shape[2:], lambda p, b: (layer_slot, p, 0, 0, 0))],
        out_specs=own,
        out_shape=jax.ShapeDtypeStruct((n, D_MODEL), BF16),
        compiler_params=_cparams("parallel", "parallel"),
        name="nat_attention",
    )(q, k, v, cache_k, cache_v, bias)


def _top2_sum(a, b, c, d):
    hi1, lo1 = jnp.maximum(a, b), jnp.minimum(a, b)
    hi2, lo2 = jnp.maximum(c, d), jnp.minimum(c, d)
    return jnp.maximum(hi1, hi2) + jnp.maximum(jnp.minimum(hi1, hi2), jnp.maximum(lo1, lo2))


def _route(scores, biased):
    sc = [scores[e:e + 1, :] for e in range(N_EXPERTS)]
    bs = [biased[e:e + 1, :] for e in range(N_EXPERTS)]
    gscore = [_top2_sum(*bs[EPG * g:EPG * (g + 1)]) for g in range(N_GROUPS)]
    best, gidx = gscore[0], jnp.zeros_like(gscore[0], dtype=jnp.int32)
    for g in range(1, N_GROUPS):
        better = gscore[g] > best
        gidx = jnp.where(better, g, gidx)
        best = jnp.where(better, gscore[g], best)
    cb = [functools.reduce(lambda a, b: a + b,
                           [jnp.where(gidx == g, bs[EPG * g + i], 0.0) for g in range(N_GROUPS)])
          for i in range(EPG)]
    cs = [functools.reduce(lambda a, b: a + b,
                           [jnp.where(gidx == g, sc[EPG * g + i], 0.0) for g in range(N_GROUPS)])
          for i in range(EPG)]
    b1, i1 = cb[0], jnp.zeros_like(gidx)
    for i in range(1, EPG):
        better = cb[i] > b1
        i1 = jnp.where(better, i, i1)
        b1 = jnp.where(better, cb[i], b1)
    b2, i2 = jnp.full_like(b1, -jnp.inf), jnp.full_like(i1, -1)
    for i in range(EPG):
        better = (i1 != i) & (cb[i] > b2)
        i2 = jnp.where(better, i, i2)
        b2 = jnp.where(better, cb[i], b2)
    sel = [(i1 == i) | (i2 == i) for i in range(EPG)]
    w = [jnp.where(sel[i], cs[i], 0.0) for i in range(EPG)]
    tot = w[0] + w[1] + w[2] + w[3]
    return gidx, [w[i] / tot for i in range(EPG)]


MOE_SRC = 256
MOE_DST = 512
MOE_SORT_TILES = 2
MOE_UNSORT_TILES = 4
_LOG2_DST = MOE_DST.bit_length() - 1
assert MOE_DST == 1 << _LOG2_DST and MOE_SRC <= MOE_DST <= 2 * MOE_SRC
_INFO_GID = EPG


def _group_tiles(cum_ref):
    n_src = cum_ref.shape[0]
    tot = [cum_ref[n_src - 1, g] for g in range(N_GROUPS)]
    ends, acc = [], 0
    for t in tot:
        acc = acc + lax.shift_right_logical(t + (MOE_DST - 1), _LOG2_DST)
        ends.append(acc)
    return tot, ends


def _dest_tile(d, cum_ref):
    tot, ends = _group_tiles(cum_ref)
    g = ((d >= ends[0]).astype(jnp.int32) + (d >= ends[1]).astype(jnp.int32)
         + (d >= ends[2]).astype(jnp.int32))
    first = jnp.where(g == 0, 0, jnp.where(g == 1, ends[0], jnp.where(g == 2, ends[1], ends[2])))
    tot_g = jnp.where(g == 0, tot[0], jnp.where(g == 1, tot[1], jnp.where(g == 2, tot[2], tot[3])))
    k0 = (d - first) * MOE_DST
    n_valid = jnp.clip(tot_g - k0, 0, MOE_DST)
    return g, k0, n_valid


def _counts_before(cum_ref, s, g):
    return jnp.where(s > 0, cum_ref[jnp.maximum(s - 1, 0), g], 0)


MOE_ALIGN = 16
_LOG2_ALIGN = MOE_ALIGN.bit_length() - 1
_LOC_ROWS = MOE_SRC + N_GROUPS * MOE_ALIGN
_INFO_LRANK = EPG + 1
H_EXT = D_MODEL + LANES
_PIECES = [MOE_SRC >> i for i in range((MOE_SRC // MOE_ALIGN).bit_length())]


def _moe_sort_tile(s, x_ref, a_ref, wo_ref, mod_ref, g_ref, wr_ref, rb_ref, xo_ref, info_ref, cum_ref,
                   lsort, seg_s, base_s):
    x = x_ref[...] + mod_ref[0, 2:3, :] * _dot(a_ref[...], wo_ref[...])
    xo_ref[...] = x
    h = _rms(x, g_ref[...]) * (1.0 + mod_ref[0, 4:5, :]) + mod_ref[0, 3:4, :]
    h_hi = h.astype(BF16)
    h_lo = (h - h_hi.astype(F32)).astype(BF16)
    hi_w = _dot(h_hi, wr_ref[...])
    logits = hi_w[:, :LANES] + (_dot(h_lo, wr_ref[:, :LANES]) + hi_w[:, LANES:])
    scores = _sigmoid(logits)
    gidx, gates = _route(scores.T[:N_EXPERTS], (scores + rb_ref[...]).T[:N_EXPERTS])
    tm = h.shape[0]
    sub = lax.broadcasted_iota(jnp.int32, (8, 1), 0)
    onehot = (sub == gidx).astype(F32)
    tri = (lax.broadcasted_iota(jnp.int32, (tm, tm), 0)
           <= lax.broadcasted_iota(jnp.int32, (tm, tm), 1)).astype(BF16)
    prefix = _dot(onehot.astype(BF16), tri)
    lrank = jnp.sum((prefix - 1.0) * onehot, axis=0, keepdims=True)

    count = [jnp.sum(onehot[g:g + 1, :]).astype(jnp.int32) for g in range(N_GROUPS)]
    padded = [lax.shift_left(lax.shift_right_logical(c + (MOE_ALIGN - 1), _LOG2_ALIGN), _LOG2_ALIGN)
              for c in count]
    loff, acc = [], 0
    for p in padded:
        loff.append(acc)
        acc = acc + p
    lpos = lrank + functools.reduce(
        lambda a, b: a + b, [jnp.where(gidx == g, jnp.asarray(v, jnp.int32).astype(F32), 0.0)
                             for g, v in enumerate(loff)])
    infot = jnp.concatenate(gates + [gidx.astype(F32), lrank, jnp.zeros((LANES - EPG - 2, tm), F32)], axis=0)
    info_ref[...] = infot.T
    g_hi = [gt.astype(BF16).astype(F32) for gt in gates]
    g_lo = [gt - gh for gt, gh in zip(gates, g_hi)]
    gext = jnp.concatenate(g_hi + g_lo + [jnp.zeros((LANES - 2 * EPG, tm), F32)], axis=0)
    h_ext = jnp.concatenate([h_hi, gext.T.astype(BF16)], axis=-1)
    perm = (lax.broadcasted_iota(jnp.int32, (_LOC_ROWS, 1), 0).astype(F32) == lpos).astype(BF16)
    lsort[s] = _dot(perm, h_ext).astype(BF16)

    new_base = []
    for g in range(N_GROUPS):
        seg_s[s, g] = padded[g]
        new_base.append(base_s[g] + padded[g])
        base_s[g] = new_base[g]
    cum = functools.reduce(lambda a, b: a + b,
                           [jnp.where(sub == g, new_base[g], 0) for g in range(N_GROUPS)])
    cum_ref[0] = jnp.broadcast_to(cum, (8, LANES))


def _copy_pieces(length, make_copy, wait):
    done = jnp.int32(0)
    for rows in _PIECES:
        take = (length & rows) != 0

        @pl.when(take)
        def _():
            cp = make_copy(pl.multiple_of(done, MOE_ALIGN), rows)
            cp.wait() if wait else cp.start()
        done = done + jnp.where(take, rows, 0)


def _moe_sort_kernel(x_ref, a_ref, wo_ref, mod_ref, g_ref, wr_ref, rb_ref, xo_ref, info_ref, cum_ref, srt_ref,
                     lsort, zeros_v, seg_s, base_s, sem):
    step = pl.program_id(0)
    n_src = lsort.shape[0]
    n_rows = srt_ref.shape[0]

    @pl.when(step == 0)
    def _():
        for g in range(N_GROUPS):
            base_s[g] = 0
        zeros_v[...] = jnp.zeros_like(zeros_v)

    @pl.when(step < n_src // MOE_SORT_TILES)
    def _():
        for j in range(MOE_SORT_TILES):
            rows = pl.ds(j * MOE_SRC, MOE_SRC)
            _moe_sort_tile(step * MOE_SORT_TILES + j, x_ref.at[rows], a_ref.at[rows], wo_ref, mod_ref, g_ref,
                           wr_ref, rb_ref, xo_ref.at[rows], info_ref.at[rows], cum_ref.at[pl.ds(j, 1)],
                           lsort, seg_s, base_s)

    @pl.when(step == n_src // MOE_SORT_TILES)
    def _():
        starts, acc = [], jnp.int32(0)
        for g in range(N_GROUPS):
            starts.append(acc)
            acc = acc + lax.shift_left(
                lax.shift_right_logical(base_s[g] + (MOE_DST - 1), _LOG2_DST), _LOG2_DST)
        end = acc

        def all_copies(wait):
            def tile_copies(t, dst):
                src = jnp.int32(0)
                new_dst = []
                for g in range(N_GROUPS):
                    n = seg_s[t, g]
                    _copy_pieces(n, lambda off, rows, src=src, g=g: pltpu.make_async_copy(
                        lsort.at[t, pl.ds(pl.multiple_of(src + off, MOE_ALIGN), rows)],
                        srt_ref.at[pl.ds(pl.multiple_of(dst[g] + off, MOE_ALIGN), rows)], sem.at[0]), wait)
                    src = src + n
                    new_dst.append(dst[g] + n)
                return tuple(new_dst)

            dst = lax.fori_loop(0, n_src, tile_copies, tuple(starts))
            for g in range(N_GROUPS):
                gap_end = starts[g + 1] if g + 1 < N_GROUPS else end
                _copy_pieces(gap_end - dst[g], lambda off, rows, g=g: pltpu.make_async_copy(
                    zeros_v.at[pl.ds(0, rows)],
                    srt_ref.at[pl.ds(pl.multiple_of(dst[g] + off, MOE_ALIGN), rows)], sem.at[0]), wait)

            def zero_tile(r, carry):
                cp = pltpu.make_async_copy(
                    zeros_v, srt_ref.at[pl.ds(pl.multiple_of(end + r * MOE_DST, MOE_DST), MOE_DST)], sem.at[0])
                cp.wait() if wait else cp.start()
                return carry

            lax.fori_loop(0, lax.shift_right_logical(n_rows - end, _LOG2_DST), zero_tile, 0)

        all_copies(False)
        all_copies(True)


def _moe_ffn2_kernel(cum_ref, xs_ref, wg_ref, wu_ref, wd_ref, y_ref):
    _, _, n_valid = _dest_tile(pl.program_id(0), cum_ref)
    half = MOE_DST // 2

    def experts(rows):
        xs = xs_ref[rows, :D_MODEL]
        gext = xs_ref[rows, D_MODEL:].astype(F32)
        acts = []
        for i in range(EPG):
            a = _dot(xs, wg_ref[i])
            u = _dot(xs, wu_ref[i])
            gate = gext[:, i:i + 1] + gext[:, EPG + i:EPG + i + 1]
            acts.append(((a * _sigmoid(a)) * u * gate).astype(BF16))
        act = jnp.concatenate(acts, axis=-1)
        y_ref[rows, :] = _dot(act, wd_ref[...].reshape(EPG * D_EXPERT, D_MODEL)).astype(BF16)

    @pl.when(n_valid == 0)
    def _():
        y_ref[...] = jnp.zeros_like(y_ref)

    @pl.when((n_valid > 0) & (n_valid <= half))
    def _():
        experts(slice(0, half))
        y_ref[half:, :] = jnp.zeros((MOE_DST - half, D_MODEL), BF16)

    @pl.when(n_valid > half)
    def _():
        experts(slice(None))


def _segment_row(cum_ref, s, g):
    _, ends = _group_tiles(cum_ref)
    first_tile = ends[g - 1] if g > 0 else 0
    return first_tile * MOE_DST + _counts_before(cum_ref, s, g)


def _unsort_window(cum_ref, s, g, n_rows):
    return pl.multiple_of(jnp.minimum(_segment_row(cum_ref, s, g), n_rows - MOE_SRC), MOE_ALIGN)


def _moe_unsort2_kernel(*refs, final_norm):
    if final_norm:
        cum_ref, x_ref, mod_ref, info_ref, y_ref, gf_ref, o_ref = refs
    else:
        cum_ref, x_ref, mod_ref, info_ref, y_ref, o_ref = refs
    n_rows = y_ref.shape[0]
    col = lax.broadcasted_iota(jnp.int32, (1, MOE_SRC), 1).astype(F32)
    for j in range(MOE_UNSORT_TILES):
        s = pl.program_id(0) * MOE_UNSORT_TILES + j
        rows = slice(j * MOE_SRC, (j + 1) * MOE_SRC)
        inf = info_ref[rows, :]
        gid = inf[:, _INFO_GID:_INFO_GID + 1]
        lrank = inf[:, _INFO_LRANK:_INFO_LRANK + 1]
        perms, wins = [], []
        for g in range(N_GROUPS):
            win = _unsort_window(cum_ref, s, g, n_rows)
            shift = _segment_row(cum_ref, s, g) - win
            rel = jnp.where(gid == float(g), lrank + shift.astype(F32), -1.0)
            perms.append((rel == col).astype(BF16))
            wins.append(y_ref[pl.ds(win, MOE_SRC), :])
        out = x_ref[rows, :] + mod_ref[0, 5:6, :] * _dot(jnp.concatenate(perms, axis=-1),
                                                         jnp.concatenate(wins, axis=0))
        o_ref[rows, :] = _rms(out, gf_ref[...]) if final_norm else out


def _mixer_out_and_moe(x, att, w_o, mod, g, w_router, router_bias, wg, wu, wd, layer, rows_per_cond,
                       final_g=None):
    n = x.shape[0]
    n_src = n // MOE_SRC
    n_dst = (n + N_GROUPS * n_src * (MOE_ALIGN - 1)) // MOE_DST + N_GROUPS
    n_rows = n_dst * MOE_DST
    ts = MOE_SORT_TILES * MOE_SRC
    last = n // ts - 1
    row = lambda i: (jnp.minimum(i, last), 0)
    full = lambda i: (0, 0)
    x, info, cum, srt = pl.pallas_call(
        _moe_sort_kernel,
        grid=(n // ts + 1,),
        in_specs=[
            pl.BlockSpec((ts, D_MODEL), row),
            pl.BlockSpec((ts, D_MODEL), row),
            pl.BlockSpec(w_o.shape, full),
            pl.BlockSpec((1, 6, D_MODEL), (lambda i: (0, 0, 0)) if rows_per_cond is None else
                         (lambda i: (1 + jnp.minimum(i, last) // (rows_per_cond // ts), 0, 0))),
            pl.BlockSpec((1, D_MODEL), full),
            pl.BlockSpec(w_router.shape, full),
            pl.BlockSpec(router_bias.shape, full),
        ],
        out_specs=[pl.BlockSpec((ts, D_MODEL), row),
                   pl.BlockSpec((ts, LANES), row),
                   pl.BlockSpec((MOE_SORT_TILES, 8, LANES), lambda i: (jnp.minimum(i, last), 0, 0)),
                   pl.BlockSpec(memory_space=pl.ANY)],
        out_shape=[jax.ShapeDtypeStruct((n, D_MODEL), F32),
                   jax.ShapeDtypeStruct((n, LANES), F32),
                   jax.ShapeDtypeStruct((n_src, 8, LANES), jnp.int32),
                   jax.ShapeDtypeStruct((n_rows, H_EXT), BF16)],
        scratch_shapes=[pltpu.VMEM((n_src, _LOC_ROWS, H_EXT), BF16),
                        pltpu.VMEM((MOE_DST, H_EXT), BF16),
                        pltpu.SMEM((n_src, N_GROUPS), jnp.int32),
                        pltpu.SMEM((N_GROUPS,), jnp.int32),
                        pltpu.SemaphoreType.DMA((1,))],
        compiler_params=_cparams("arbitrary"),
        name="moe_sort",
    )(x, att, w_o, mod, g, w_router, router_bias)
    cum = cum[:, :N_GROUPS, 0]

    blk = lambda width: pl.BlockSpec((MOE_DST, width), lambda d, c: (d, 0))
    expert = lambda shape: pl.BlockSpec(shape, lambda d, c: (layer, _dest_tile(d, c)[0], 0, 0))
    y = pl.pallas_call(
        _moe_ffn2_kernel,
        grid_spec=pltpu.PrefetchScalarGridSpec(
            num_scalar_prefetch=1,
            grid=(n_dst,),
            in_specs=[blk(H_EXT),
                      expert((None, EPG, D_MODEL, D_EXPERT)),
                      expert((None, EPG, D_MODEL, D_EXPERT)),
                      expert((None, EPG, D_EXPERT, D_MODEL))],
            out_specs=blk(D_MODEL),
        ),
        out_shape=jax.ShapeDtypeStruct((n_rows, D_MODEL), BF16),
        compiler_params=_cparams("arbitrary"),
        name="moe_experts",
    )(cum, srt, wg, wu, wd)

    tu = MOE_UNSORT_TILES * MOE_SRC
    in_specs = [pl.BlockSpec((tu, D_MODEL), lambda i, c: (i, 0)),
                _mod_spec(None if rows_per_cond is None else rows_per_cond // tu),
                pl.BlockSpec((tu, LANES), lambda i, c: (i, 0)),
                pl.BlockSpec(y.shape, lambda i, c: (0, 0), pipeline_mode=pl.Buffered(1))]
    args = [cum, x, mod, info, y]
    if final_g is not None:
        in_specs.append(pl.BlockSpec((1, D_MODEL), lambda i, c: (0, 0)))
        args.append(final_g)
    return pl.pallas_call(
        functools.partial(_moe_unsort2_kernel, final_norm=final_g is not None),
        grid_spec=pltpu.PrefetchScalarGridSpec(
            num_scalar_prefetch=1,
            grid=(n_src // MOE_UNSORT_TILES,),
            in_specs=in_specs,
            out_specs=pl.BlockSpec((tu, D_MODEL), lambda i, c: (i, 0)),
        ),
        out_shape=jax.ShapeDtypeStruct((n, D_MODEL), F32),
        compiler_params=_cparams("parallel"),
        name="moe_unsort",
    )(*args)


def _rope_tables(n_lat):
    t = np.arange(n_lat)
    n_freq = MLA_ROPE // 4
    inv_freq = jnp.asarray(ROPE_THETA, F32) ** (-jnp.arange(n_freq, dtype=F32) / n_freq)
    ar = jnp.asarray(t // GRID_W, F32)[:, None] * inv_freq
    ac = jnp.asarray(t % GRID_W, F32)[:, None] * inv_freq
    cos = jnp.concatenate([jnp.cos(ar), jnp.cos(ar), jnp.cos(ac), jnp.cos(ac)], axis=-1)
    sin = jnp.concatenate([-jnp.sin(ar), jnp.sin(ar), -jnp.sin(ac), jnp.sin(ac)], axis=-1)
    return (jnp.tile(cos, (1, MLA_HEADS)), jnp.tile(sin, (1, MLA_HEADS)),
            jnp.tile(cos, (1, LANES // MLA_ROPE)), jnp.tile(sin, (1, LANES // MLA_ROPE)))


_ROPE_SWAP = np.concatenate([np.arange(8, 16), np.arange(0, 8), np.arange(24, 32), np.arange(16, 24)])


def _mla_weights(w_in, w_uq, w_ukv):
    w_in, w_uq, w_ukv = w_in.astype(BF16), w_uq.astype(BF16), w_ukv.astype(BF16)
    o = MLA_Q_LORA + MLA_KV_LORA
    kr = w_in[:, o:]
    rep = LANES // MLA_ROPE
    w_in_x = jnp.concatenate([w_in[:, :o], jnp.tile(kr, (1, rep)), jnp.tile(kr[:, _ROPE_SWAP], (1, rep))],
                             axis=-1)
    uq = w_uq.reshape(MLA_Q_LORA, MLA_HEADS, MLA_NOPE + MLA_ROPE)
    q_nope = uq[:, :, :MLA_NOPE].reshape(MLA_Q_LORA, -1)
    q_rope = uq[:, :, MLA_NOPE:]
    w_uq_x = jnp.concatenate([q_nope, q_rope.reshape(MLA_Q_LORA, -1),
                              q_rope[:, :, _ROPE_SWAP].reshape(MLA_Q_LORA, -1)], axis=-1)
    ukv = w_ukv.reshape(MLA_KV_LORA, MLA_HEADS, MLA_NOPE + MLA_V)
    w_uk = ukv[:, :, :MLA_NOPE].reshape(MLA_KV_LORA, -1)
    w_uv = ukv[:, :, MLA_NOPE:].reshape(MLA_KV_LORA, -1)
    return w_in_x, w_uq_x, w_uk, w_uv


def kernel(x_prompt, x_sample, cache_mla_ckv, cache_mla_krope, cache_nat_k, cache_nat_v, c, c_ctx,
           w_ada, b_ada, norm_mix, norm_ffn, norm_final, mla_w_in, mla_q_norm, mla_w_uq, mla_kv_norm,
           mla_w_ukv, mla_w_o, nat_w_qkv, nat_rpb, nat_w_o, w_router, router_bias,
           moe_w_gate, moe_w_up, moe_w_down):
    B, S, D = x_prompt.shape
    Bd, Sd, _ = x_sample.shape
    assert D == D_MODEL and Bd + 1 <= 8
    tm_c, tm_s = 512, 512

    xc = x_prompt.reshape(B * S, D)
    xs = x_sample.reshape(Bd * Sd, D)
    cond8 = jnp.concatenate([c_ctx[None, :], c, jnp.zeros((8 - 1 - Bd, D), F32)], axis=0)
    mod_all = _ada_modulation(cond8, w_ada, b_ada).reshape(DEPTH, 8, 6, D)

    rope_tabs = _rope_tables(Sd)
    wr = jnp.pad(w_router, ((0, 0), (0, LANES - N_EXPERTS)))
    wr_hi = wr.astype(BF16)
    wr = jnp.concatenate([wr_hi, (wr - wr_hi.astype(F32)).astype(BF16)], axis=1)
    rb = jnp.pad(router_bias, (0, LANES - N_EXPERTS)).reshape(1, LANES)
    wg = moe_w_gate.astype(BF16)
    wu = moe_w_up.astype(BF16)
    wd = moe_w_down.astype(BF16)

    nat_bias = _nat_bias_pairs(nat_rpb, Sd // GRID_W)
    ckv_buf = kr_buf = k_buf = v_buf = None
    for layer in range(DEPTH):
        mod = mod_all[layer]
        g_mix = norm_mix[layer][None, :]
        j = layer // 2
        if layer % 2 == 0:
            w_in_x, w_uq_x, w_uk, w_uv = _mla_weights(mla_w_in[j], mla_w_uq[j], mla_w_ukv[j])
            w_in_c = w_in_x[:, :MLA_Q_LORA + MLA_KV_LORA + LANES]
            w_uq_c = w_uq_x[:, :MLA_HEADS * (MLA_NOPE + MLA_ROPE)]
            qg = mla_q_norm[j][None, :]
            kvg = mla_kv_norm[j][None, :]
            w_o = mla_w_o[j].astype(BF16)
            qn, qr, ckv_buf, kr_buf, kr4 = _premix_mla(xc, mod, g_mix, w_in_c, qg, w_uq_c, kvg, None, None, tm_c,
                                                       cache=(ckv_buf, kr_buf, j, (DEPTH + 1) // 2, B))
            att_c = _mla_attention(qn, qr, ckv_buf, kr4, None, None, w_uk, w_uv, B, S, ckv_slot=j)
            qn, qr, ckv, kr, kr4 = _premix_mla(xs, mod, g_mix, w_in_x, qg, w_uq_x, kvg, rope_tabs, Sd, tm_s)
            cache_kr4 = jnp.tile(cache_mla_krope[:, j], (1, 1, LANES // MLA_ROPE))
            att_s = _mla_attention(qn, qr, ckv, kr4, cache_mla_ckv[:, j], cache_kr4, w_uk, w_uv, Bd, 512)
        else:
            w_qkv = nat_w_qkv[j].astype(BF16)
            w_o = nat_w_o[j].astype(BF16)
            q, k_buf, v_buf = _premix_nat_cache(xc, mod, g_mix, w_qkv, k_buf, v_buf, j, DEPTH // 2, B, tm_c)
            att_c = _dense_attention(q, k_buf, v_buf, j)
            q, k, v = _premix_nat(xs, mod, g_mix, w_qkv, Sd, tm_s, BF16)
            att_s = _nat_attention(q, k, v,
                                   cache_nat_k[:, j].reshape(Bd, -1, D), cache_nat_v[:, j].reshape(Bd, -1, D),
                                   nat_bias, j, Bd)
        g_ffn = norm_ffn[layer][None, :]
        final_g = norm_final[None, :] if layer == DEPTH - 1 else None
        xc = _mixer_out_and_moe(xc, att_c, w_o, mod, g_ffn, wr, rb, wg, wu, wd, layer, None, final_g)
        xs = _mixer_out_and_moe(xs, att_s, w_o, mod, g_ffn, wr, rb, wg, wu, wd, layer, Sd, final_g)

    y_prompt = xc.reshape(B, S, D)
    y_sample = xs.reshape(Bd, Sd, D)
    kv_shape = (B, DEPTH // 2, S, NAT_HEADS, NAT_DH)
    return (y_prompt, y_sample, ckv_buf, kr_buf, k_buf.reshape(kv_shape), v_buf.reshape(kv_shape))
```

```python
import functools

import numpy as np
import jax
import jax.numpy as jnp
from jax import lax
from jax.experimental import pallas as pl
from jax.experimental.pallas import tpu as pltpu

F32 = jnp.float32
BF16 = jnp.bfloat16

D_MODEL = 1024
DEPTH = 4
GRID_W = 64
LANES = 128
MLA_HEADS = 16
MLA_NOPE = 64
MLA_ROPE = 32
MLA_V = 64
MLA_Q_LORA = 384
MLA_KV_LORA = 256
MLA_SCALE = (MLA_NOPE + MLA_ROPE) ** -0.5
ROPE_THETA = 10000.0
NAT_HEADS = 16
NAT_DH = 64
NAT_SCALE = NAT_DH ** -0.5
WIN_H = 8
WIN_W = 16
NAT_QROWS = 4
NAT_KROWS = NAT_QROWS + WIN_H
N_EXPERTS = 16
N_GROUPS = 4
EPG = N_EXPERTS // N_GROUPS
D_EXPERT = 256
NORM_EPS = 1e-6
NEG_INF = -1e30

HEAD_PAIRS = MLA_HEADS // 2
VMEM_LIMIT = 56 * 1024 * 1024


def _cparams(*sem):
    return pltpu.CompilerParams(dimension_semantics=sem, vmem_limit_bytes=VMEM_LIMIT)


def _sigmoid(x):
    return 1.0 / (1.0 + jnp.exp(-x))


def _rms(x, g):
    ms = jnp.mean(x * x, axis=-1, keepdims=True)
    return x * lax.rsqrt(ms + NORM_EPS) * g


def _dot(a, b):
    return jnp.dot(a, b, preferred_element_type=F32)


def _dot_nt(a, b):
    return lax.dot_general(a, b, (((1,), (1,)), ((), ())), preferred_element_type=F32)


def _lane_mask(width, idx, dtype):
    lane = lax.broadcasted_iota(jnp.int32, (1, LANES), 1)
    return ((lane >= idx * width) & (lane < (idx + 1) * width)).astype(dtype)


def _ada_kernel(cond_ref, w_ref, b_ref, o_ref):
    c = cond_ref[...]
    s = (c * _sigmoid(c)).astype(BF16)
    o_ref[0] = _dot(s, w_ref[0].astype(BF16)) + b_ref[0]


def _ada_modulation(cond8, w_ada, b_ada):
    n_chunk = 6
    return pl.pallas_call(
        _ada_kernel,
        grid=(DEPTH, n_chunk),
        in_specs=[
            pl.BlockSpec((8, D_MODEL), lambda l, j: (0, 0)),
            pl.BlockSpec((1, D_MODEL, D_MODEL), lambda l, j: (l, 0, j)),
            pl.BlockSpec((1, 1, D_MODEL), lambda l, j: (l, 0, j)),
        ],
        out_specs=pl.BlockSpec((1, 8, D_MODEL), lambda l, j: (l, 0, j)),
        out_shape=jax.ShapeDtypeStruct((DEPTH, 8, 6 * D_MODEL), F32),
        compiler_params=_cparams("parallel", "parallel"),
        name="ada_modulation",
    )(cond8, w_ada, b_ada.reshape(DEPTH, 1, 6 * D_MODEL))


def _mod_spec(rows_per_cond):
    if rows_per_cond is None:
        return pl.BlockSpec((1, 6, D_MODEL), lambda i, *_: (0, 0, 0))
    return pl.BlockSpec((1, 6, D_MODEL), lambda i, *_: (1 + i // rows_per_cond, 0, 0))


def _premix_mla_kernel(*refs, rope, cache):
    if rope:
        (x_ref, mod_ref, g_ref, w_in_ref, qg_ref, w_uq_ref, kvg_ref,
         cosq_ref, sinq_ref, cosk_ref, sink_ref,
         qn_ref, qr_ref, ckv_ref, kr_ref, kr4_ref) = refs
    elif cache == "update":
        (x_ref, mod_ref, g_ref, w_in_ref, qg_ref, w_uq_ref, kvg_ref, _, _,
         qn_ref, qr_ref, ckv_ref, kr_ref, kr4_ref) = refs
    else:
        (x_ref, mod_ref, g_ref, w_in_ref, qg_ref, w_uq_ref, kvg_ref,
         qn_ref, qr_ref, ckv_ref, kr_ref, kr4_ref) = refs
    h = _rms(x_ref[...], g_ref[...]) * (1.0 + mod_ref[0, 1:2, :]) + mod_ref[0, 0:1, :]
    lat = _dot(h.astype(BF16), w_in_ref[...])
    c_q = lat[:, :MLA_Q_LORA]
    c_kv = lat[:, MLA_Q_LORA:MLA_Q_LORA + MLA_KV_LORA]
    o = MLA_Q_LORA + MLA_KV_LORA
    kr4 = lat[:, o:o + LANES]
    q = _dot(_rms(c_q, qg_ref[...]).astype(BF16), w_uq_ref[...])
    n_nope = MLA_HEADS * MLA_NOPE
    n_rope = MLA_HEADS * MLA_ROPE
    qr = q[:, n_nope:n_nope + n_rope]
    if rope:
        qr = qr * cosq_ref[...] + q[:, n_nope + n_rope:] * sinq_ref[...]
        kr4 = kr4 * cosk_ref[...] + lat[:, o + LANES:o + 2 * LANES] * sink_ref[...]
    qn_ref[...] = q[:, :n_nope].astype(BF16)
    qr_ref[...] = qr.astype(BF16)
    kr4_ref[...] = kr4.astype(BF16)
    ckv = _rms(c_kv, kvg_ref[...])
    if cache is None:
        ckv_ref[...] = ckv
        kr_ref[...] = kr4[:, :MLA_ROPE]
    else:
        nb, n_slots, s, _ = ckv_ref.shape
        ckv_ref[:, 0] = ckv.reshape(nb, s, MLA_KV_LORA)
        kr_ref[:, 0] = kr4[:, :MLA_ROPE].reshape(nb, s, MLA_ROPE)
        for slot in range(1, n_slots):
            ckv_ref[:, slot] = jnp.zeros((nb, s, MLA_KV_LORA), F32)
            kr_ref[:, slot] = jnp.zeros((nb, s, MLA_ROPE), F32)


def _premix_mla(x, mod, g, w_in, qg, w_uq, kvg, rope_tabs, rows_per_cond, tm, cache=None):
    n = x.shape[0]
    rope = rope_tabs is not None
    row = lambda i: (i, 0)
    full = lambda i: (0, 0)
    in_specs = [
        pl.BlockSpec((tm, D_MODEL), row),
        _mod_spec(None if rows_per_cond is None else rows_per_cond // tm),
        pl.BlockSpec((1, D_MODEL), full),
        pl.BlockSpec(w_in.shape, full),
        pl.BlockSpec((1, MLA_Q_LORA), full),
        pl.BlockSpec(w_uq.shape, full),
        pl.BlockSpec((1, MLA_KV_LORA), full),
    ]
    args = [x, mod, g, w_in, qg, w_uq, kvg]
    if rope:
        nblk = rope_tabs[0].shape[0] // tm
        pos = lambda i: (i % nblk, 0)
        for t in rope_tabs:
            in_specs.append(pl.BlockSpec((tm, t.shape[1]), pos))
            args.append(t)
    widths = (MLA_HEADS * MLA_NOPE, MLA_HEADS * MLA_ROPE, MLA_KV_LORA, MLA_ROPE, LANES)
    dtypes = (BF16, BF16, F32, F32, BF16)
    out_specs = [pl.BlockSpec((tm, w), row) for w in widths]
    out_shape = [jax.ShapeDtypeStruct((n, w), d) for w, d in zip(widths, dtypes)]
    mode, aliases = None, {}
    if cache is not None:
        assert not rope
        ckv_buf, kr_buf, slot, n_slots, n_batch = cache
        s = n // n_batch
        nb = tm // s
        mode = "first" if ckv_buf is None else "update"
        if mode == "first":
            assert slot == 0
            spec = lambda w: pl.BlockSpec((nb, n_slots, s, w), lambda i: (i, 0, 0, 0))
        else:
            spec = lambda w: pl.BlockSpec((nb, 1, s, w), lambda i: (i, slot, 0, 0))
            aliases = {len(args): 2, len(args) + 1: 3}
            in_specs += [pl.BlockSpec(memory_space=pl.ANY)] * 2
            args += [ckv_buf, kr_buf]
        for idx in (2, 3):
            out_specs[idx] = spec(widths[idx])
            out_shape[idx] = jax.ShapeDtypeStruct((n_batch, n_slots, s, widths[idx]), F32)
    return pl.pallas_call(
        functools.partial(_premix_mla_kernel, rope=rope, cache=mode),
        grid=(n // tm,),
        in_specs=in_specs,
        out_specs=out_specs,
        out_shape=out_shape,
        input_output_aliases=aliases,
        compiler_params=_cparams("parallel"),
        name="premix_mla_rope" if rope else "premix_mla",
    )(*args)


LOG2E = 1.4426950408889634


def _softmax_pv(s_list, v_list, scale=1.0):
    m = functools.reduce(jnp.maximum, [jnp.max(s, axis=-1, keepdims=True) for s in s_list])
    e_list = [jnp.exp2((s - m) * (scale * LOG2E)) for s in s_list]
    l = functools.reduce(lambda a, b: a + b, [jnp.sum(e, axis=-1, keepdims=True) for e in e_list])
    o = functools.reduce(lambda a, b: a + b,
                         [_dot(e.astype(BF16), v) for e, v in zip(e_list, v_list)])
    return o / l


def _stack_heads(q, masks):
    return jnp.concatenate([q * m for m in masks], axis=0)


def _unstack_heads(o2, width):
    tq = o2.shape[0] // 2
    lane = lax.broadcasted_iota(jnp.int32, (1, LANES), 1)
    return jnp.where(lane < width, o2[:tq], o2[tq:])


MLA_ATTN_BATCHES = 2


def _mla_attn_kernel(*refs, cached, nb):
    if cached:
        (qn_ref, qr_ref, ckv_ref, kr4_ref, cckv_ref, ckr4_ref, w_uk_ref, w_uv_ref,
         o_ref, kn_s, v_s, kr_s) = refs
    else:
        (qn_ref, qr_ref, ckv_ref, kr4_ref, w_uk_ref, w_uv_ref, o_ref, kn_s, v_s, kr_s) = refs
    from_slot = len(ckv_ref.shape) == 3
    t_own = ckv_ref.shape[1] if from_slot else ckv_ref.shape[0] // nb

    for b in range(nb):
        own = slice(b * t_own, (b + 1) * t_own)

        def fill_kv():
            c = (ckv_ref[b] if from_slot else ckv_ref[own, :]).astype(BF16)
            kn_s[0:t_own, :] = _dot(c, w_uk_ref[...]).astype(BF16)
            v_s[0:t_own, :] = _dot(c, w_uv_ref[...]).astype(BF16)
            kr_s[0:t_own, :] = kr4_ref[own, :]
            if cached:
                cc = cckv_ref[0].astype(BF16)
                kn_s[t_own:, :] = _dot(cc, w_uk_ref[...]).astype(BF16)
                v_s[t_own:, :] = _dot(cc, w_uv_ref[...]).astype(BF16)
                kr_s[t_own:, :] = ckr4_ref[0].astype(BF16)

        if cached:
            pl.when(pl.program_id(1) == 0)(fill_kv)
            qrows = slice(None)
        else:
            fill_kv()
            qrows = own

        kr4 = kr_s[...]
        for p in range(HEAD_PAIRS):
            sl = slice(p * LANES, (p + 1) * LANES)
            qn = qn_ref[qrows, sl]
            qr = qr_ref[qrows, (p // 2) * LANES:(p // 2 + 1) * LANES]
            k_cat = jnp.concatenate([kn_s[:, sl], kr4], axis=-1)
            q_cat = jnp.concatenate(
                [_stack_heads(qn, [_lane_mask(MLA_NOPE, i, BF16) for i in range(2)]),
                 _stack_heads(qr, [_lane_mask(MLA_ROPE, (2 * p + i) % 4, BF16) for i in range(2)])], axis=-1)
            o2 = _softmax_pv([_dot_nt(q_cat, k_cat)], [v_s[:, sl]], MLA_SCALE)
            o_ref[qrows, sl] = _unstack_heads(o2, MLA_V).astype(BF16)


def _mla_attention(qn, qr, ckv, kr4, cache_ckv, cache_kr4, w_uk, w_uv, n_batch, tq, ckv_slot=None):
    n = qn.shape[0]
    s_own = n // n_batch
    nq = s_own // tq
    cached = cache_ckv is not None
    t_all = s_own + (cache_ckv.shape[1] if cached else 0)
    nb = 1 if cached else MLA_ATTN_BATCHES
    assert cached or nq == 1
    tq = tq * nb
    qrow = lambda b, j: (b * nq + j, 0)
    own = lambda b, j: (b, 0)
    full = lambda b, j: (0, 0)
    in_specs = [
        pl.BlockSpec((tq, qn.shape[1]), qrow),
        pl.BlockSpec((tq, qr.shape[1]), qrow),
        pl.BlockSpec((nb * s_own, MLA_KV_LORA), own) if ckv_slot is None else
        pl.BlockSpec((nb, None, s_own, MLA_KV_LORA), lambda b, j: (b, ckv_slot, 0, 0)),
        pl.BlockSpec((nb * s_own, LANES), own),
    ]
    args = [qn, qr, ckv, kr4]
    if cached:
        in_specs += [pl.BlockSpec((1,) + cache_ckv.shape[1:], lambda b, j: (b, 0, 0)),
                     pl.BlockSpec((1,) + cache_kr4.shape[1:], lambda b, j: (b, 0, 0))]
        args += [cache_ckv, cache_kr4]
    in_specs += [pl.BlockSpec(w_uk.shape, full), pl.BlockSpec(w_uv.shape, full)]
    args += [w_uk, w_uv]
    return pl.pallas_call(
        functools.partial(_mla_attn_kernel, cached=cached, nb=nb),
        grid=(n_batch // nb, nq),
        in_specs=in_specs,
        out_specs=pl.BlockSpec((tq, D_MODEL), qrow),
        out_shape=jax.ShapeDtypeStruct((n, D_MODEL), BF16),
        scratch_shapes=[pltpu.VMEM((t_all, D_MODEL), BF16), pltpu.VMEM((t_all, D_MODEL), BF16),
                        pltpu.VMEM((t_all, LANES), BF16)],
        compiler_params=_cparams("parallel", "arbitrary"),
        name="mla_attention_cached" if cached else "mla_attention",
    )(*args)


def _premix_nat_kernel(x_ref, mod_ref, g_ref, w_ref, q_ref, k_ref, v_ref):
    h = _rms(x_ref[...], g_ref[...]) * (1.0 + mod_ref[0, 1:2, :]) + mod_ref[0, 0:1, :]
    qkv = _dot(h.astype(BF16), w_ref[...])
    q_ref[...] = qkv[:, :D_MODEL].astype(q_ref.dtype)
    k_ref[...] = qkv[:, D_MODEL:2 * D_MODEL].astype(k_ref.dtype)
    v_ref[...] = qkv[:, 2 * D_MODEL:].astype(v_ref.dtype)


def _premix_nat(x, mod, g, w_qkv, rows_per_cond, tm, kv_dtype):
    n = x.shape[0]
    row = lambda i: (i, 0)
    full = lambda i: (0, 0)
    return pl.pallas_call(
        _premix_nat_kernel,
        grid=(n // tm,),
        in_specs=[
            pl.BlockSpec((tm, D_MODEL), row),
            _mod_spec(None if rows_per_cond is None else rows_per_cond // tm),
            pl.BlockSpec((1, D_MODEL), full),
            pl.BlockSpec(w_qkv.shape, full),
        ],
        out_specs=[pl.BlockSpec((tm, D_MODEL), row)] * 3,
        out_shape=[jax.ShapeDtypeStruct((n, D_MODEL), BF16),
                   jax.ShapeDtypeStruct((n, D_MODEL), kv_dtype),
                   jax.ShapeDtypeStruct((n, D_MODEL), kv_dtype)],
        compiler_params=_cparams("parallel"),
        name="premix_nat",
    )(x, mod, g, w_qkv)


def _premix_nat_cache_kernel(*refs, first):
    if first:
        x_ref, mod_ref, g_ref, w_ref, q_ref, k_ref, v_ref = refs
    else:
        x_ref, mod_ref, g_ref, w_ref, _, _, q_ref, k_ref, v_ref = refs
    h = _rms(x_ref[...], g_ref[...]) * (1.0 + mod_ref[0, 1:2, :]) + mod_ref[0, 0:1, :]
    qkv = _dot(h.astype(BF16), w_ref[...])
    q_ref[...] = qkv[:, :D_MODEL].astype(BF16)
    nb, n_slots, s, _ = k_ref.shape
    k_ref[:, 0] = qkv[:, D_MODEL:2 * D_MODEL].reshape(nb, s, D_MODEL)
    v_ref[:, 0] = qkv[:, 2 * D_MODEL:].reshape(nb, s, D_MODEL)
    for slot in range(1, n_slots):
        k_ref[:, slot] = jnp.zeros((nb, s, D_MODEL), F32)
        v_ref[:, slot] = jnp.zeros((nb, s, D_MODEL), F32)


def _premix_nat_cache(x, mod, g, w_qkv, k_buf, v_buf, slot, n_slots, n_batch, tm):
    n = x.shape[0]
    s = n // n_batch
    nb = tm // s
    first = k_buf is None
    row = lambda i: (i, 0)
    full = lambda i: (0, 0)
    in_specs = [pl.BlockSpec((tm, D_MODEL), row), _mod_spec(None), pl.BlockSpec((1, D_MODEL), full),
                pl.BlockSpec(w_qkv.shape, full)]
    args = [x, mod, g, w_qkv]
    if first:
        assert slot == 0
        kv_spec = pl.BlockSpec((nb, n_slots, s, D_MODEL), lambda i: (i, 0, 0, 0))
        aliases = {}
    else:
        in_specs += [pl.BlockSpec(memory_space=pl.ANY)] * 2
        args += [k_buf, v_buf]
        kv_spec = pl.BlockSpec((nb, 1, s, D_MODEL), lambda i: (i, slot, 0, 0))
        aliases = {4: 1, 5: 2}
    kv_shape = jax.ShapeDtypeStruct((n_batch, n_slots, s, D_MODEL), F32)
    return pl.pallas_call(
        functools.partial(_premix_nat_cache_kernel, first=first),
        grid=(n // tm,),
        in_specs=in_specs,
        out_specs=[pl.BlockSpec((tm, D_MODEL), row), kv_spec, kv_spec],
        out_shape=[jax.ShapeDtypeStruct((n, D_MODEL), BF16), kv_shape, kv_shape],
        input_output_aliases=aliases,
        compiler_params=_cparams("parallel"),
        name="premix_nat_cache",
    )(*args)


DENSE_ATTN_BATCHES = 4


def _dense_attn_kernel(q_ref, k_ref, v_ref, o_ref):
    nb, s, _ = k_ref.shape
    for b in range(nb):
        rows = slice(b * s, (b + 1) * s)
        for p in range(HEAD_PAIRS):
            sl = slice(p * LANES, (p + 1) * LANES)
            q2 = _stack_heads(q_ref[rows, sl], [_lane_mask(NAT_DH, i, BF16) for i in range(2)])
            k = k_ref[b, :, sl].astype(BF16)
            v = v_ref[b, :, sl].astype(BF16)
            o2 = _softmax_pv([_dot_nt(q2, k)], [v], NAT_SCALE)
            o_ref[rows, sl] = _unstack_heads(o2, NAT_DH).astype(BF16)


def _dense_attention(q, k_buf, v_buf, slot):
    n = q.shape[0]
    n_batch, _, s, _ = k_buf.shape
    nb = DENSE_ATTN_BATCHES
    blk = pl.BlockSpec((nb * s, D_MODEL), lambda b: (b, 0))
    kv = pl.BlockSpec((nb, None, s, D_MODEL), lambda b: (b, slot, 0, 0))
    return pl.pallas_call(
        _dense_attn_kernel,
        grid=(n_batch // nb,),
        in_specs=[blk, kv, kv],
        out_specs=blk,
        out_shape=jax.ShapeDtypeStruct((n, D_MODEL), BF16),
        compiler_params=_cparams("parallel"),
        name="dense_attention",
    )(q, k_buf, v_buf)


_NAT_QBLK = NAT_QROWS * GRID_W
_NAT_KBLK = NAT_KROWS * GRID_W


def _nat_block_plan(rows):
    assert rows % NAT_QROWS == 0 and rows >= NAT_KROWS and NAT_KROWS % 2 == 0
    plan, variants = [], []
    for r0 in range(0, rows, NAT_QROWS):
        ks = min(max(r0 - WIN_H // 2, 0), rows - NAT_KROWS)
        r = r0 + np.arange(NAT_QROWS)
        kr = ks + np.arange(NAT_KROWS)
        rs = np.clip(r - WIN_H // 2, 0, rows - WIN_H)
        valid_row = (kr[None, :] >= rs[:, None]) & (kr[None, :] < rs[:, None] + WIN_H)
        d0 = ks - r + (WIN_H - 1)
        for vi, (d0_v, valid_v) in enumerate(variants):
            if np.array_equal(d0, d0_v) and np.array_equal(valid_row, valid_v):
                break
        else:
            vi = len(variants)
            variants.append((d0, valid_row))
        plan.append((ks, vi))
    return plan, variants


def _nat_bias_row_range(variants):
    lo = min(int(d0.min()) for d0, _ in variants)
    hi = max(int(d0.max()) for d0, _ in variants) + NAT_KROWS
    return lo, hi


def _nat_bias_pairs(rpb, rows):
    n_layers, n_heads, n_dr, n_dc = rpb.shape
    _, variants = _nat_block_plan(rows)
    lo, hi = _nat_bias_row_range(variants)
    c = np.arange(GRID_W)
    cs = np.clip(c - WIN_W // 2, 0, GRID_W - WIN_W)
    valid_col = (c[None, :] >= cs[:, None]) & (c[None, :] < cs[:, None] + WIN_W)
    d_col = c[None, :] - c[:, None] + (WIN_W - 1)
    sel = (d_col[None] == np.arange(n_dc)[:, None, None]) & valid_col[None]
    rows2 = (lo + np.arange(hi - lo))[:, None] + np.arange(2)[None, :]
    in_range = (rows2 >= 0) & (rows2 < n_dr)
    sel2 = np.zeros((2, n_dc, GRID_W, 2, GRID_W), np.float32)
    mask = np.zeros((hi - lo, GRID_W, 2, GRID_W), np.float32)
    for half in range(2):
        sel2[half, :, :, half, :] = sel
        mask[:, :, half, :] = np.where(in_range[:, half, None, None] & valid_col[None], 0.0, NEG_INF)
    picked = jnp.where(jnp.asarray(in_range)[None, None, :, :, None],
                       rpb[:, :, np.clip(rows2, 0, n_dr - 1), :], 0.0)
    tab = jnp.einsum('lhdm,mck->lhdck', picked.reshape(n_layers, n_heads, hi - lo, 2 * n_dc),
                     jnp.asarray(sel2.reshape(2 * n_dc, GRID_W, 2 * GRID_W)), precision=lax.Precision.HIGHEST)
    return tab + jnp.asarray(mask.reshape(hi - lo, GRID_W, 2 * GRID_W))


def _nat_block_bias(tp_ref, head, d0, valid_row, row_lo):
    neg = jnp.full((GRID_W, 2 * GRID_W), NEG_INF, F32)
    left = lax.broadcasted_iota(jnp.int32, (1, 2 * GRID_W), 1) < GRID_W
    rows_out = []
    for dr in range(NAT_QROWS):
        pieces = []
        for a in range(0, NAT_KROWS, 2):
            ok0, ok1 = bool(valid_row[dr, a]), bool(valid_row[dr, a + 1])
            if not (ok0 or ok1):
                pieces.append(neg)
                continue
            piece = tp_ref[head, int(d0[dr]) + a - row_lo]
            if ok0 and not ok1:
                piece = jnp.where(left, piece, NEG_INF)
            elif ok1 and not ok0:
                piece = jnp.where(left, NEG_INF, piece)
            pieces.append(piece)
        rows_out.append(jnp.concatenate(pieces, axis=-1))
    return jnp.concatenate(rows_out, axis=0)


NAT_ATTN_BATCHES = 2


def _nat_attn_kernel(q_ref, k_ref, v_ref, kc_ref, vc_ref, tp_ref, o_ref, *, plan, variants, row_lo):
    nb = kc_ref.shape[0]
    s = q_ref.shape[0] // nb
    assert NAT_SCALE == 2.0 ** round(np.log2(NAT_SCALE))
    masks = [_lane_mask(NAT_DH, i, BF16) * NAT_SCALE for i in range(2)]
    for bi, (ks, var) in enumerate(plan):
        bias2 = jnp.concatenate([_nat_block_bias(tp_ref, i, *variants[var], row_lo) for i in range(2)], axis=0)
        for b in range(nb):
            rows = slice(b * s + bi * _NAT_QBLK, b * s + (bi + 1) * _NAT_QBLK)
            keys = slice(b * s + ks * GRID_W, b * s + ks * GRID_W + _NAT_KBLK)
            q2 = _stack_heads(q_ref[rows, :], masks)
            kc = kc_ref[b].astype(BF16)
            vc = vc_ref[b].astype(BF16)
            o2 = _softmax_pv([_dot_nt(q2, k_ref[keys, :]) + bias2, _dot_nt(q2, kc)], [v_ref[keys, :], vc])
            o_ref[rows, :] = _unstack_heads(o2, NAT_DH).astype(BF16)


def _nat_attention(q, k, v, cache_k, cache_v, bias, layer_slot, n_batch):
    n = q.shape[0]
    s = n // n_batch
    plan, variants = _nat_block_plan(s // GRID_W)
    row_lo, _ = _nat_bias_row_range(variants)
    nb = NAT_ATTN_BATCHES
    own = pl.BlockSpec((nb * s, LANES), lambda p, b: (b, p))
    cache = pl.BlockSpec((nb, cache_k.shape[1], LANES), lambda p, b: (b, 0, p))
    return pl.pallas_call(
        functools.partial(_nat_attn_kernel, plan=plan, variants=variants, row_lo=row_lo),
        grid=(HEAD_PAIRS, n_batch // nb),
        in_specs=[own, own, own, cache, cache,
                  pl.BlockSpec((None, 2) + bias.shape[2:], lambda p, b: (layer_slot, p, 0, 0, 0))],
        out_specs=own,
        out_shape=jax.ShapeDtypeStruct((n, D_MODEL), BF16),
        compiler_params=_cparams("parallel", "parallel"),
        name="nat_attention",
    )(q, k, v, cache_k, cache_v, bias)


def _top2_sum(a, b, c, d):
    hi1, lo1 = jnp.maximum(a, b), jnp.minimum(a, b)
    hi2, lo2 = jnp.maximum(c, d), jnp.minimum(c, d)
    return jnp.maximum(hi1, hi2) + jnp.maximum(jnp.minimum(hi1, hi2), jnp.maximum(lo1, lo2))


def _route(scores, biased):
    sc = [scores[e:e + 1, :] for e in range(N_EXPERTS)]
    bs = [biased[e:e + 1, :] for e in range(N_EXPERTS)]
    gscore = [_top2_sum(*bs[EPG * g:EPG * (g + 1)]) for g in range(N_GROUPS)]
    best, gidx = gscore[0], jnp.zeros_like(gscore[0], dtype=jnp.int32)
    for g in range(1, N_GROUPS):
        better = gscore[g] > best
        gidx = jnp.where(better, g, gidx)
        best = jnp.where(better, gscore[g], best)
    cb = [functools.reduce(lambda a, b: a + b,
                           [jnp.where(gidx == g, bs[EPG * g + i], 0.0) for g in range(N_GROUPS)])
          for i in range(EPG)]
    cs = [functools.reduce(lambda a, b: a + b,
                           [jnp.where(gidx == g, sc[EPG * g + i], 0.0) for g in range(N_GROUPS)])
          for i in range(EPG)]
    b1, i1 = cb[0], jnp.zeros_like(gidx)
    for i in range(1, EPG):
        better = cb[i] > b1
        i1 = jnp.where(better, i, i1)
        b1 = jnp.where(better, cb[i], b1)
    b2, i2 = jnp.full_like(b1, -jnp.inf), jnp.full_like(i1, -1)
    for i in range(EPG):
        better = (i1 != i) & (cb[i] > b2)
        i2 = jnp.where(better, i, i2)
        b2 = jnp.where(better, cb[i], b2)
    sel = [(i1 == i) | (i2 == i) for i in range(EPG)]
    w = [jnp.where(sel[i], cs[i], 0.0) for i in range(EPG)]
    tot = w[0] + w[1] + w[2] + w[3]
    return gidx, [w[i] / tot for i in range(EPG)]


MOE_SRC = 256
MOE_DST = 512
MOE_SORT_TILES = 2
MOE_UNSORT_TILES = 4
_LOG2_DST = MOE_DST.bit_length() - 1
assert MOE_DST == 1 << _LOG2_DST and MOE_SRC <= MOE_DST <= 2 * MOE_SRC
_INFO_GID = EPG


def _group_tiles(cum_ref):
    n_src = cum_ref.shape[0]
    tot = [cum_ref[n_src - 1, g] for g in range(N_GROUPS)]
    ends, acc = [], 0
    for t in tot:
        acc = acc + lax.shift_right_logical(t + (MOE_DST - 1), _LOG2_DST)
        ends.append(acc)
    return tot, ends


def _dest_tile(d, cum_ref):
    tot, ends = _group_tiles(cum_ref)
    g = ((d >= ends[0]).astype(jnp.int32) + (d >= ends[1]).astype(jnp.int32)
         + (d >= ends[2]).astype(jnp.int32))
    first = jnp.where(g == 0, 0, jnp.where(g == 1, ends[0], jnp.where(g == 2, ends[1], ends[2])))
    tot_g = jnp.where(g == 0, tot[0], jnp.where(g == 1, tot[1], jnp.where(g == 2, tot[2], tot[3])))
    k0 = (d - first) * MOE_DST
    n_valid = jnp.clip(tot_g - k0, 0, MOE_DST)
    return g, k0, n_valid


def _counts_before(cum_ref, s, g):
    return jnp.where(s > 0, cum_ref[jnp.maximum(s - 1, 0), g], 0)


MOE_ALIGN = 16
_LOG2_ALIGN = MOE_ALIGN.bit_length() - 1
_LOC_ROWS = MOE_SRC + N_GROUPS * MOE_ALIGN
_INFO_LRANK = EPG + 1
H_EXT = D_MODEL + LANES
_PIECES = [MOE_SRC >> i for i in range((MOE_SRC // MOE_ALIGN).bit_length())]


def _moe_sort_tile(s, x_ref, a_ref, wo_ref, mod_ref, g_ref, wr_ref, rb_ref, xo_ref, info_ref, cum_ref,
                   lsort, seg_s, base_s):
    x = x_ref[...] + mod_ref[0, 2:3, :] * _dot(a_ref[...], wo_ref[...])
    xo_ref[...] = x
    h = _rms(x, g_ref[...]) * (1.0 + mod_ref[0, 4:5, :]) + mod_ref[0, 3:4, :]
    h_hi = h.astype(BF16)
    h_lo = (h - h_hi.astype(F32)).astype(BF16)
    hi_w = _dot(h_hi, wr_ref[...])
    logits = hi_w[:, :LANES] + (_dot(h_lo, wr_ref[:, :LANES]) + hi_w[:, LANES:])
    scores = _sigmoid(logits)
    gidx, gates = _route(scores.T[:N_EXPERTS], (scores + rb_ref[...]).T[:N_EXPERTS])
    tm = h.shape[0]
    sub = lax.broadcasted_iota(jnp.int32, (8, 1), 0)
    onehot = (sub == gidx).astype(F32)
    tri = (lax.broadcasted_iota(jnp.int32, (tm, tm), 0)
           <= lax.broadcasted_iota(jnp.int32, (tm, tm), 1)).astype(BF16)
    prefix = _dot(onehot.astype(BF16), tri)
    lrank = jnp.sum((prefix - 1.0) * onehot, axis=0, keepdims=True)

    count = [jnp.sum(onehot[g:g + 1, :]).astype(jnp.int32) for g in range(N_GROUPS)]
    padded = [lax.shift_left(lax.shift_right_logical(c + (MOE_ALIGN - 1), _LOG2_ALIGN), _LOG2_ALIGN)
              for c in count]
    loff, acc = [], 0
    for p in padded:
        loff.append(acc)
        acc = acc + p
    lpos = lrank + functools.reduce(
        lambda a, b: a + b, [jnp.where(gidx == g, jnp.asarray(v, jnp.int32).astype(F32), 0.0)
                             for g, v in enumerate(loff)])
    infot = jnp.concatenate(gates + [gidx.astype(F32), lrank, jnp.zeros((LANES - EPG - 2, tm), F32)], axis=0)
    info_ref[...] = infot.T
    g_hi = [gt.astype(BF16).astype(F32) for gt in gates]
    g_lo = [gt - gh for gt, gh in zip(gates, g_hi)]
    gext = jnp.concatenate(g_hi + g_lo + [jnp.zeros((LANES - 2 * EPG, tm), F32)], axis=0)
    h_ext = jnp.concatenate([h_hi, gext.T.astype(BF16)], axis=-1)
    perm = (lax.broadcasted_iota(jnp.int32, (_LOC_ROWS, 1), 0).astype(F32) == lpos).astype(BF16)
    lsort[s] = _dot(perm, h_ext).astype(BF16)

    new_base = []
    for g in range(N_GROUPS):
        seg_s[s, g] = padded[g]
        new_base.append(base_s[g] + padded[g])
        base_s[g] = new_base[g]
    cum = functools.reduce(lambda a, b: a + b,
                           [jnp.where(sub == g, new_base[g], 0) for g in range(N_GROUPS)])
    cum_ref[0] = jnp.broadcast_to(cum, (8, LANES))


def _copy_pieces(length, make_copy, wait):
    done = jnp.int32(0)
    for i, rows in enumerate(_PIECES):
        take = (length & rows) != 0

        @pl.when(take)
        def _():
            cp = make_copy(pl.multiple_of(done, MOE_ALIGN), rows)
            cp.wait() if wait else cp.start(priority=i % 2)
        done = done + jnp.where(take, rows, 0)


def _moe_sort_kernel(x_ref, a_ref, wo_ref, mod_ref, g_ref, wr_ref, rb_ref, xo_ref, info_ref, cum_ref, srt_ref,
                     lsort, zeros_v, seg_s, base_s, sem):
    step = pl.program_id(0)
    n_src = lsort.shape[0]
    n_rows = srt_ref.shape[0]

    @pl.when(step == 0)
    def _():
        for g in range(N_GROUPS):
            base_s[g] = 0
        zeros_v[...] = jnp.zeros_like(zeros_v)

    @pl.when(step < n_src // MOE_SORT_TILES)
    def _():
        for j in range(MOE_SORT_TILES):
            rows = pl.ds(j * MOE_SRC, MOE_SRC)
            _moe_sort_tile(step * MOE_SORT_TILES + j, x_ref.at[rows], a_ref.at[rows], wo_ref, mod_ref, g_ref,
                           wr_ref, rb_ref, xo_ref.at[rows], info_ref.at[rows], cum_ref.at[pl.ds(j, 1)],
                           lsort, seg_s, base_s)

    @pl.when(step == n_src // MOE_SORT_TILES)
    def _():
        starts, acc = [], jnp.int32(0)
        for g in range(N_GROUPS):
            starts.append(acc)
            acc = acc + lax.shift_left(
                lax.shift_right_logical(base_s[g] + (MOE_DST - 1), _LOG2_DST), _LOG2_DST)
        end = acc

        def all_copies(wait):
            def tile_copies(t, dst):
                src = jnp.int32(0)
                new_dst = []
                for g in range(N_GROUPS):
                    n = seg_s[t, g]
                    _copy_pieces(n, lambda off, rows, src=src, g=g: pltpu.make_async_copy(
                        lsort.at[t, pl.ds(pl.multiple_of(src + off, MOE_ALIGN), rows)],
                        srt_ref.at[pl.ds(pl.multiple_of(dst[g] + off, MOE_ALIGN), rows)], sem.at[0]), wait)
                    src = src + n
                    new_dst.append(dst[g] + n)
                return tuple(new_dst)

            dst = lax.fori_loop(0, n_src, tile_copies, tuple(starts))
            for g in range(N_GROUPS):
                gap_end = starts[g + 1] if g + 1 < N_GROUPS else end
                _copy_pieces(gap_end - dst[g], lambda off, rows, g=g: pltpu.make_async_copy(
                    zeros_v.at[pl.ds(0, rows)],
                    srt_ref.at[pl.ds(pl.multiple_of(dst[g] + off, MOE_ALIGN), rows)], sem.at[0]), wait)

            def zero_tile(r, carry):
                cp = pltpu.make_async_copy(
                    zeros_v, srt_ref.at[pl.ds(pl.multiple_of(end + r * MOE_DST, MOE_DST), MOE_DST)], sem.at[0])
                cp.wait() if wait else cp.start()
                return carry

            lax.fori_loop(0, lax.shift_right_logical(n_rows - end, _LOG2_DST), zero_tile, 0)

        all_copies(False)
        all_copies(True)


def _moe_ffn2_kernel(cum_ref, xs_ref, wg_ref, wu_ref, wd_ref, y_ref):
    _, _, n_valid = _dest_tile(pl.program_id(0), cum_ref)
    half = MOE_DST // 2

    def experts(rows):
        xs = xs_ref[rows, :D_MODEL]
        gext = xs_ref[rows, D_MODEL:].astype(F32)
        acts = []
        for i in range(EPG):
            a = _dot(xs, wg_ref[i])
            u = _dot(xs, wu_ref[i])
            gate = gext[:, i:i + 1] + gext[:, EPG + i:EPG + i + 1]
            acts.append(((a * _sigmoid(a)) * u * gate).astype(BF16))
        act = jnp.concatenate(acts, axis=-1)
        y_ref[rows, :] = _dot(act, wd_ref[...].reshape(EPG * D_EXPERT, D_MODEL)).astype(BF16)

    @pl.when(n_valid == 0)
    def _():
        y_ref[...] = jnp.zeros_like(y_ref)

    @pl.when((n_valid > 0) & (n_valid <= half))
    def _():
        experts(slice(0, half))
        y_ref[half:, :] = jnp.zeros((MOE_DST - half, D_MODEL), BF16)

    @pl.when(n_valid > half)
    def _():
        experts(slice(None))


def _segment_row(cum_ref, s, g):
    _, ends = _group_tiles(cum_ref)
    first_tile = ends[g - 1] if g > 0 else 0
    return first_tile * MOE_DST + _counts_before(cum_ref, s, g)


def _unsort_window(cum_ref, s, g, n_rows):
    return pl.multiple_of(jnp.minimum(_segment_row(cum_ref, s, g), n_rows - MOE_SRC), MOE_ALIGN)


def _moe_unsort2_kernel(*refs, final_norm):
    if final_norm:
        cum_ref, x_ref, mod_ref, info_ref, y_ref, gf_ref, o_ref = refs
    else:
        cum_ref, x_ref, mod_ref, info_ref, y_ref, o_ref = refs
    n_rows = y_ref.shape[0]
    col = lax.broadcasted_iota(jnp.int32, (1, MOE_SRC), 1).astype(F32)
    for j in range(MOE_UNSORT_TILES):
        s = pl.program_id(0) * MOE_UNSORT_TILES + j
        rows = slice(j * MOE_SRC, (j + 1) * MOE_SRC)
        inf = info_ref[rows, :]
        gid = inf[:, _INFO_GID:_INFO_GID + 1]
        lrank = inf[:, _INFO_LRANK:_INFO_LRANK + 1]
        perms, wins = [], []
        for g in range(N_GROUPS):
            win = _unsort_window(cum_ref, s, g, n_rows)
            shift = _segment_row(cum_ref, s, g) - win
            rel = jnp.where(gid == float(g), lrank + shift.astype(F32), -1.0)
            perms.append((rel == col).astype(BF16))
            wins.append(y_ref[pl.ds(win, MOE_SRC), :])
        out = x_ref[rows, :] + mod_ref[0, 5:6, :] * _dot(jnp.concatenate(perms, axis=-1),
                                                         jnp.concatenate(wins, axis=0))
        o_ref[rows, :] = _rms(out, gf_ref[...]) if final_norm else out


def _mixer_out_and_moe(x, att, w_o, mod, g, w_router, router_bias, wg, wu, wd, layer, rows_per_cond,
                       final_g=None):
    n = x.shape[0]
    n_src = n // MOE_SRC
    n_dst = (n + N_GROUPS * n_src * (MOE_ALIGN - 1)) // MOE_DST + N_GROUPS
    n_rows = n_dst * MOE_DST
    ts = MOE_SORT_TILES * MOE_SRC
    last = n // ts - 1
    row = lambda i: (jnp.minimum(i, last), 0)
    full = lambda i: (0, 0)
    x, info, cum, srt = pl.pallas_call(
        _moe_sort_kernel,
        grid=(n // ts + 1,),
        in_specs=[
            pl.BlockSpec((ts, D_MODEL), row),
            pl.BlockSpec((ts, D_MODEL), row),
            pl.BlockSpec(w_o.shape, full),
            pl.BlockSpec((1, 6, D_MODEL), (lambda i: (0, 0, 0)) if rows_per_cond is None else
                         (lambda i: (1 + jnp.minimum(i, last) // (rows_per_cond // ts), 0, 0))),
            pl.BlockSpec((1, D_MODEL), full),
            pl.BlockSpec(w_router.shape, full),
            pl.BlockSpec(router_bias.shape, full),
        ],
        out_specs=[pl.BlockSpec((ts, D_MODEL), row),
                   pl.BlockSpec((ts, LANES), row),
                   pl.BlockSpec((MOE_SORT_TILES, 8, LANES), lambda i: (jnp.minimum(i, last), 0, 0)),
                   pl.BlockSpec(memory_space=pl.ANY)],
        out_shape=[jax.ShapeDtypeStruct((n, D_MODEL), F32),
                   jax.ShapeDtypeStruct((n, LANES), F32),
                   jax.ShapeDtypeStruct((n_src, 8, LANES), jnp.int32),
                   jax.ShapeDtypeStruct((n_rows, H_EXT), BF16)],
        scratch_shapes=[pltpu.VMEM((n_src, _LOC_ROWS, H_EXT), BF16),
                        pltpu.VMEM((MOE_DST, H_EXT), BF16),
                        pltpu.SMEM((n_src, N_GROUPS), jnp.int32),
                        pltpu.SMEM((N_GROUPS,), jnp.int32),
                        pltpu.SemaphoreType.DMA((1,))],
        compiler_params=_cparams("arbitrary"),
        name="moe_sort",
    )(x, att, w_o, mod, g, w_router, router_bias)
    cum = cum[:, :N_GROUPS, 0]

    blk = lambda width: pl.BlockSpec((MOE_DST, width), lambda d, c: (d, 0))
    expert = lambda shape: pl.BlockSpec(shape, lambda d, c: (layer, _dest_tile(d, c)[0], 0, 0))
    y = pl.pallas_call(
        _moe_ffn2_kernel,
        grid_spec=pltpu.PrefetchScalarGridSpec(
            num_scalar_prefetch=1,
            grid=(n_dst,),
            in_specs=[blk(H_EXT),
                      expert((None, EPG, D_MODEL, D_EXPERT)),
                      expert((None, EPG, D_MODEL, D_EXPERT)),
                      expert((None, EPG, D_EXPERT, D_MODEL))],
            out_specs=blk(D_MODEL),
        ),
        out_shape=jax.ShapeDtypeStruct((n_rows, D_MODEL), BF16),
        compiler_params=_cparams("arbitrary"),
        name="moe_experts",
    )(cum, srt, wg, wu, wd)

    tu = MOE_UNSORT_TILES * MOE_SRC
    in_specs = [pl.BlockSpec((tu, D_MODEL), lambda i, c: (i, 0)),
                _mod_spec(None if rows_per_cond is None else rows_per_cond // tu),
                pl.BlockSpec((tu, LANES), lambda i, c: (i, 0)),
                pl.BlockSpec(y.shape, lambda i, c: (0, 0), pipeline_mode=pl.Buffered(1))]
    args = [cum, x, mod, info, y]
    if final_g is not None:
        in_specs.append(pl.BlockSpec((1, D_MODEL), lambda i, c: (0, 0)))
        args.append(final_g)
    return pl.pallas_call(
        functools.partial(_moe_unsort2_kernel, final_norm=final_g is not None),
        grid_spec=pltpu.PrefetchScalarGridSpec(
            num_scalar_prefetch=1,
            grid=(n_src // MOE_UNSORT_TILES,),
            in_specs=in_specs,
            out_specs=pl.BlockSpec((tu, D_MODEL), lambda i, c: (i, 0)),
        ),
        out_shape=jax.ShapeDtypeStruct((n, D_MODEL), F32),
        compiler_params=_cparams("parallel"),
        name="moe_unsort",
    )(*args)


def _rope_tables(n_lat):
    t = np.arange(n_lat)
    n_freq = MLA_ROPE // 4
    inv_freq = jnp.asarray(ROPE_THETA, F32) ** (-jnp.arange(n_freq, dtype=F32) / n_freq)
    ar = jnp.asarray(t // GRID_W, F32)[:, None] * inv_freq
    ac = jnp.asarray(t % GRID_W, F32)[:, None] * inv_freq
    cos = jnp.concatenate([jnp.cos(ar), jnp.cos(ar), jnp.cos(ac), jnp.cos(ac)], axis=-1)
    sin = jnp.concatenate([-jnp.sin(ar), jnp.sin(ar), -jnp.sin(ac), jnp.sin(ac)], axis=-1)
    return (jnp.tile(cos, (1, MLA_HEADS)), jnp.tile(sin, (1, MLA_HEADS)),
            jnp.tile(cos, (1, LANES // MLA_ROPE)), jnp.tile(sin, (1, LANES // MLA_ROPE)))


_ROPE_SWAP = np.concatenate([np.arange(8, 16), np.arange(0, 8), np.arange(24, 32), np.arange(16, 24)])


def _mla_weights(w_in, w_uq, w_ukv):
    w_in, w_uq, w_ukv = w_in.astype(BF16), w_uq.astype(BF16), w_ukv.astype(BF16)
    o = MLA_Q_LORA + MLA_KV_LORA
    kr = w_in[:, o:]
    rep = LANES // MLA_ROPE
    w_in_x = jnp.concatenate([w_in[:, :o], jnp.tile(kr, (1, rep)), jnp.tile(kr[:, _ROPE_SWAP], (1, rep))],
                             axis=-1)
    uq = w_uq.reshape(MLA_Q_LORA, MLA_HEADS, MLA_NOPE + MLA_ROPE)
    q_nope = uq[:, :, :MLA_NOPE].reshape(MLA_Q_LORA, -1)
    q_rope = uq[:, :, MLA_NOPE:]
    w_uq_x = jnp.concatenate([q_nope, q_rope.reshape(MLA_Q_LORA, -1),
                              q_rope[:, :, _ROPE_SWAP].reshape(MLA_Q_LORA, -1)], axis=-1)
    ukv = w_ukv.reshape(MLA_KV_LORA, MLA_HEADS, MLA_NOPE + MLA_V)
    w_uk = ukv[:, :, :MLA_NOPE].reshape(MLA_KV_LORA, -1)
    w_uv = ukv[:, :, MLA_NOPE:].reshape(MLA_KV_LORA, -1)
    return w_in_x, w_uq_x, w_uk, w_uv


def kernel(x_prompt, x_sample, cache_mla_ckv, cache_mla_krope, cache_nat_k, cache_nat_v, c, c_ctx,
           w_ada, b_ada, norm_mix, norm_ffn, norm_final, mla_w_in, mla_q_norm, mla_w_uq, mla_kv_norm,
           mla_w_ukv, mla_w_o, nat_w_qkv, nat_rpb, nat_w_o, w_router, router_bias,
           moe_w_gate, moe_w_up, moe_w_down):
    B, S, D = x_prompt.shape
    Bd, Sd, _ = x_sample.shape
    assert D == D_MODEL and Bd + 1 <= 8
    tm_c, tm_s = 512, 512

    xc = x_prompt.reshape(B * S, D)
    xs = x_sample.reshape(Bd * Sd, D)
    cond8 = jnp.concatenate([c_ctx[None, :], c, jnp.zeros((8 - 1 - Bd, D), F32)], axis=0)
    mod_all = _ada_modulation(cond8, w_ada, b_ada).reshape(DEPTH, 8, 6, D)

    rope_tabs = _rope_tables(Sd)
    wr = jnp.pad(w_router, ((0, 0), (0, LANES - N_EXPERTS)))
    wr_hi = wr.astype(BF16)
    wr = jnp.concatenate([wr_hi, (wr - wr_hi.astype(F32)).astype(BF16)], axis=1)
    rb = jnp.pad(router_bias, (0, LANES - N_EXPERTS)).reshape(1, LANES)
    wg = moe_w_gate.astype(BF16)
    wu = moe_w_up.astype(BF16)
    wd = moe_w_down.astype(BF16)

    nat_bias = _nat_bias_pairs(nat_rpb, Sd // GRID_W)
    ckv_buf = kr_buf = k_buf = v_buf = None
    for layer in range(DEPTH):
        mod = mod_all[layer]
        g_mix = norm_mix[layer][None, :]
        j = layer // 2
        if layer % 2 == 0:
            w_in_x, w_uq_x, w_uk, w_uv = _mla_weights(mla_w_in[j], mla_w_uq[j], mla_w_ukv[j])
            w_in_c = w_in_x[:, :MLA_Q_LORA + MLA_KV_LORA + LANES]
            w_uq_c = w_uq_x[:, :MLA_HEADS * (MLA_NOPE + MLA_ROPE)]
            qg = mla_q_norm[j][None, :]
            kvg = mla_kv_norm[j][None, :]
            w_o = mla_w_o[j].astype(BF16)
            qn, qr, ckv_buf, kr_buf, kr4 = _premix_mla(xc, mod, g_mix, w_in_c, qg, w_uq_c, kvg, None, None, tm_c,
                                                       cache=(ckv_buf, kr_buf, j, (DEPTH + 1) // 2, B))
            att_c = _mla_attention(qn, qr, ckv_buf, kr4, None, None, w_uk, w_uv, B, S, ckv_slot=j)
            qn, qr, ckv, kr, kr4 = _premix_mla(xs, mod, g_mix, w_in_x, qg, w_uq_x, kvg, rope_tabs, Sd, tm_s)
            cache_kr4 = jnp.tile(cache_mla_krope[:, j], (1, 1, LANES // MLA_ROPE))
            att_s = _mla_attention(qn, qr, ckv, kr4, cache_mla_ckv[:, j], cache_kr4, w_uk, w_uv, Bd, 512)
        else:
            w_qkv = nat_w_qkv[j].astype(BF16)
            w_o = nat_w_o[j].astype(BF16)
            q, k_buf, v_buf = _premix_nat_cache(xc, mod, g_mix, w_qkv, k_buf, v_buf, j, DEPTH // 2, B, tm_c)
            att_c = _dense_attention(q, k_buf, v_buf, j)
            q, k, v = _premix_nat(xs, mod, g_mix, w_qkv, Sd, tm_s, BF16)
            att_s = _nat_attention(q, k, v,
                                   cache_nat_k[:, j].reshape(Bd, -1, D), cache_nat_v[:, j].reshape(Bd, -1, D),
                                   nat_bias, j, Bd)
        g_ffn = norm_ffn[layer][None, :]
        final_g = norm_final[None, :] if layer == DEPTH - 1 else None
        xc = _mixer_out_and_moe(xc, att_c, w_o, mod, g_ffn, wr, rb, wg, wu, wd, layer, None, final_g)
        xs = _mixer_out_and_moe(xs, att_s, w_o, mod, g_ffn, wr, rb, wg, wu, wd, layer, Sd, final_g)

    y_prompt = xc.reshape(B, S, D)
    y_sample = xs.reshape(Bd, Sd, D)
    kv_shape = (B, DEPTH // 2, S, NAT_HEADS, NAT_DH)
    return (y_prompt, y_sample, ckv_buf, kr_buf, k_buf.reshape(kv_shape), v_buf.reshape(kv_shape))
```
